```python
import math
import jax
import jax.numpy as jnp
from jax import lax
import numpy as np

D_MODEL = 1024
BATCH = 8
SEQ = 2048
DEPTH = 2
DEC_BATCH = 128
DEC_SEQ = 8
PAST_LEN = 16384
PAGE_SIZE = 128

D_PLE = 256
N_BRANCH = 4
BRANCH_W = D_MODEL // N_BRANCH
CONV_W = 4
CHUNK = 64
EPS = 1e-6

GDN_H = 4
GDN_DK = 64
GDN_DV = BRANCH_W // GDN_H
GDN_QK = GDN_H * GDN_DK
GDN_CONV_CH = 2 * GDN_QK + BRANCH_W
HG_H = 4
HG_DK = 64
HG_DV = BRANCH_W // HG_H
HG_QK = HG_H * HG_DK
SSD_H = 4
SSD_P = BRANCH_W // SSD_H
SSD_G = 2
SSD_N = 128
SSD_CONV_CH = BRANCH_W + 2 * SSD_G * SSD_N
ML_H = 4
ML_DK = 32
ML_DV = BRANCH_W // ML_H
ML_QK = ML_H * ML_DK
D_FF = ((8 * D_MODEL // 3 + 255) // 256) * 256
N_EXPERTS = 8
TOP_K = 2
N_DENSE = (DEPTH + 1) // 2
N_MOE = DEPTH // 2

IN_SPLITS = (GDN_CONV_CH, GDN_H, GDN_H, BRANCH_W,
             HG_QK, HG_QK, BRANCH_W, BRANCH_W,
             BRANCH_W, SSD_CONV_CH, SSD_H,
             ML_QK, ML_QK, BRANCH_W, ML_H, ML_H, BRANCH_W,
             N_BRANCH * D_MODEL)
N_IN = sum(IN_SPLITS)

STATE_KEYS = ('gdn_conv', 'gdn', 'hgrn', 'ssd_conv', 'ssd', 'ml_c', 'ml_n', 'ml_m')
MIXER_KEYS = ('w_in', 'gdn_conv_w', 'gdn_a_log', 'gdn_dt_bias', 'gdn_norm', 'hgrn_norm',
              'ssd_conv_w', 'ssd_conv_b', 'ssd_a_log', 'ssd_dt_bias', 'ssd_d', 'ssd_norm',
              'ml_ig_b', 'ml_fg_b', 'ml_norm', 'w_branch', 'w_out')

kernel_name = 'hybrid_gdn_hgrn2_ssd_mlstm_step'

F32 = jnp.float32


def rmsnorm(x, g):
    xf = x.astype(F32)
    y = xf * lax.rsqrt(jnp.mean(xf * xf, axis=-1, keepdims=True) + EPS)
    return (y * g.astype(F32)).astype(x.dtype)


def l2norm(x):
    x = x.astype(F32)
    return x * lax.rsqrt(jnp.sum(x * x, axis=-1, keepdims=True) + EPS)


def masked_exp(d, mask):
    return jnp.where(mask, jnp.exp(jnp.where(mask, d, 0.0)), 0.0)


def split_cols(t):
    offs, acc = [], 0
    for s in IN_SPLITS[:-1]:
        acc += s
        offs.append(acc)
    return jnp.split(t, offs, axis=-1)


def to_heads(t, h):
    b, l, _ = t.shape
    return t.reshape(b, l, h, -1).transpose(0, 2, 1, 3)


def from_heads(t):
    b, h, l, d = t.shape
    return t.transpose(0, 2, 1, 3).reshape(b, l, h * d)


def gate_heads(t):
    return t.astype(F32).transpose(0, 2, 1)


def chunk_len(l):
    return l if l <= CHUNK else math.gcd(l, CHUNK)


def to_chunks(t, c):
    b, h, l = t.shape[:3]
    t = t.reshape((b, h, l // c, c) + t.shape[3:])
    return jnp.moveaxis(t, 2, 0)


def from_chunks(t):
    n, b, h, c, d = t.shape
    return jnp.moveaxis(t, 0, 2).reshape(b, h, n * c, d)


def causal_conv(x, buf, w, b=None):
    l = x.shape[1]
    xp = jnp.concatenate([buf.astype(x.dtype), x], axis=1)
    y = w[0] * xp[:, :l]
    for j in range(1, CONV_W):
        y = y + w[j] * xp[:, j:j + l]
    if b is not None:
        y = y + b
    return jax.nn.silu(y), xp[:, l:]


def gdn_scan(q, k, v, g, beta, s0):
    c = chunk_len(q.shape[2])
    tri = jnp.tril(jnp.ones((c, c), dtype=bool))
    strict = jnp.tril(jnp.ones((c, c), dtype=bool), -1)
    eye = jnp.eye(c, dtype=F32)
    dv = v.shape[-1]

    def step(s, blk):
        qc, kc, vc, gc, bc = blk
        gam = jnp.cumsum(gc, axis=-1)
        dec = masked_exp(gam[..., :, None] - gam[..., None, :], tri)
        a = jnp.where(strict, bc[..., :, None] * jnp.einsum('bhtk,bhsk->bhts', kc, kc) * dec, 0.0)
        rhs = jnp.concatenate([bc[..., None] * vc, (bc * jnp.exp(gam))[..., None] * kc], axis=-1)
        sol = lax.linalg.triangular_solve(eye + a, rhs, left_side=True, lower=True, unit_diagonal=True)
        u = sol[..., :dv] - jnp.einsum('bhtk,bhkv->bhtv', sol[..., dv:], s)
        att = jnp.einsum('bhtk,bhsk->bhts', qc, kc) * dec
        o = (jnp.exp(gam)[..., None] * jnp.einsum('bhtk,bhkv->bhtv', qc, s)
             + jnp.einsum('bhts,bhsv->bhtv', att, u))
        w_end = jnp.exp(gam[..., -1:] - gam)
        s = (jnp.exp(gam[..., -1])[..., None, None] * s
             + jnp.einsum('bhsk,bhsv->bhkv', kc * w_end[..., None], u))
        return s, o

    xs = tuple(to_chunks(t.astype(F32), c) for t in (q, k, v, g, beta))
    s, o = lax.scan(step, s0.astype(F32), xs)
    return from_chunks(o), s


def hgrn_scan(q, k, v, logf, s0):
    c = chunk_len(q.shape[2])
    tri = jnp.tril(jnp.ones((c, c), dtype=bool))

    def step(s, blk):
        qc, kc, vc, lf = blk
        b = jnp.cumsum(lf, axis=2)
        dec = masked_exp(b[:, :, :, None, :] - b[:, :, None, :, :], tri[:, :, None])
        att = jnp.sum(qc[:, :, :, None, :] * kc[:, :, None, :, :] * dec, axis=-1)
        o = (jnp.einsum('bhtk,bhkv->bhtv', qc * jnp.exp(b), s)
             + jnp.einsum('bhts,bhsv->bhtv', att, vc))
        s = (jnp.exp(b[:, :, -1, :])[..., None] * s
             + jnp.einsum('bhsk,bhsv->bhkv', kc * jnp.exp(b[:, :, -1:, :] - b), vc))
        return s, o

    xs = tuple(to_chunks(t.astype(F32), c) for t in (q, k, v, logf))
    s, o = lax.scan(step, s0.astype(F32), xs)
    return from_chunks(o), s


def ssd_scan(x, dt, da, bm, cm, h0):
    c = chunk_len(x.shape[2])
    tri = jnp.tril(jnp.ones((c, c), dtype=bool))

    def step(h, blk):
        xc, dtc, dac, bc, cc = blk
        cum = jnp.cumsum(dac, axis=-1)
        dec = masked_exp(cum[..., :, None] - cum[..., None, :], tri)
        att = jnp.einsum('bhtn,bhsn->bhts', cc, bc) * dec * dtc[..., None, :]
        y = (jnp.exp(cum)[..., None] * jnp.einsum('bhtn,bhpn->bhtp', cc, h)
             + jnp.einsum('bhts,bhsp->bhtp', att, xc))
        w_end = dtc * jnp.exp(cum[..., -1:] - cum)
        h = (jnp.exp(cum[..., -1])[..., None, None] * h
             + jnp.einsum('bhsp,bhsn->bhpn', xc * w_end[..., None], bc))
        return h, y

    xs = tuple(to_chunks(t.astype(F32), c) for t in (x, dt, da, bm, cm))
    h, y = lax.scan(step, h0.astype(F32), xs)
    return from_chunks(y), h


def mlstm_scan(q, k, v, ig, logf, c0, n0, m0):
    c = chunk_len(q.shape[2])
    tri = jnp.tril(jnp.ones((c, c), dtype=bool))

    def step(carry, blk):
        cm, nm, mm = carry
        qc, kc, vc, igc, lfc = blk
        b = jnp.cumsum(lfc, axis=-1)
        m = b + jnp.maximum(mm[..., None], lax.cummax(igc - b, axis=2))
        w_in = jnp.exp(b + mm[..., None] - m)
        logw = b[..., :, None] - b[..., None, :] + igc[..., None, :] - m[..., :, None]
        wts = masked_exp(logw, tri) * jnp.einsum('bhtk,bhsk->bhts', qc, kc)
        num = (w_in[..., None] * jnp.einsum('bhtk,bhkv->bhtv', qc, cm)
               + jnp.einsum('bhts,bhsv->bhtv', wts, vc))
        nq = w_in * jnp.einsum('bhtk,bhk->bht', qc, nm) + jnp.sum(wts, axis=-1)
        h = num / jnp.maximum(jnp.abs(nq), jnp.exp(-m))[..., None]
        m_end = m[..., -1]
        w_end = jnp.exp(b[..., -1:] - b + igc - m_end[..., None])
        a_end = jnp.exp(b[..., -1] + mm - m_end)
        cm = a_end[..., None, None] * cm + jnp.einsum('bhsk,bhsv->bhkv', kc * w_end[..., None], vc)
        nm = a_end[..., None] * nm + jnp.einsum('bhsk,bhs->bhk', kc, w_end)
        return (cm, nm, m_end), h

    xs = tuple(to_chunks(t.astype(F32), c) for t in (q, k, v, ig, logf))
    (cm, nm, mm), h = lax.scan(step, (c0.astype(F32), n0.astype(F32), m0.astype(F32)), xs)
    return from_chunks(h), cm, nm, mm


def mixer_block(u, st, w, lb):
    dty = u.dtype
    (gdn_in, gdn_b, gdn_a, gdn_z, hg_q, hg_f, hg_v, hg_g,
     ssd_z, ssd_in, ssd_dt, ml_q, ml_k, ml_v, ml_i, ml_f, ml_o, gates) = split_cols(u @ w['w_in'])

    qkv, gdn_conv_new = causal_conv(gdn_in, st['gdn_conv'], w['gdn_conv_w'])
    gq, gk, gv = jnp.split(qkv, [GDN_QK, 2 * GDN_QK], axis=-1)
    gq = l2norm(to_heads(gq, GDN_H)) * (GDN_DK ** -0.5)
    gk = l2norm(to_heads(gk, GDN_H))
    beta = jax.nn.sigmoid(gate_heads(gdn_b))
    g_log = (-jnp.exp(w['gdn_a_log'].astype(F32))[:, None]
             * jax.nn.softplus(gate_heads(gdn_a) + w['gdn_dt_bias'].astype(F32)[:, None]))
    o_a, s_gdn = gdn_scan(gq, gk, to_heads(gv, GDN_H), g_log, beta, st['gdn'])
    o_a = rmsnorm(o_a.astype(dty), w['gdn_norm']) * jax.nn.silu(to_heads(gdn_z, GDN_H))

    f_pre = hg_f.astype(F32)
    lbf = lb.astype(F32)
    log_f = jnp.log(lbf + (1.0 - lbf) * jax.nn.sigmoid(f_pre))
    k_in = (1.0 - lbf) * jax.nn.sigmoid(-f_pre)
    o_b, s_hg = hgrn_scan(to_heads(hg_q, HG_H), to_heads(k_in, HG_H), to_heads(hg_v, HG_H),
                          to_heads(log_f, HG_H), st['hgrn'])
    o_b = rmsnorm(o_b.astype(dty), w['hgrn_norm']) * jax.nn.silu(to_heads(hg_g, HG_H))

    xbc, ssd_conv_new = causal_conv(ssd_in, st['ssd_conv'], w['ssd_conv_w'], w['ssd_conv_b'])
    sx, sb, sc = jnp.split(xbc, [BRANCH_W, BRANCH_W + SSD_G * SSD_N], axis=-1)
    sx = to_heads(sx, SSD_H)
    sb = jnp.repeat(to_heads(sb, SSD_G), SSD_H // SSD_G, axis=1)
    sc = jnp.repeat(to_heads(sc, SSD_G), SSD_H // SSD_G, axis=1)
    dt = jax.nn.softplus(gate_heads(ssd_dt) + w['ssd_dt_bias'].astype(F32)[:, None])
    da = -jnp.exp(w['ssd_a_log'].astype(F32))[:, None] * dt
    y_c, h_ssd = ssd_scan(sx, dt, da, sb, sc, st['ssd'])
    y_c = from_heads(y_c.astype(dty) + w['ssd_d'][:, None, None] * sx)
    o_c = rmsnorm(y_c * jax.nn.silu(ssd_z), w['ssd_norm'])

    ig = gate_heads(ml_i) + w['ml_ig_b'].astype(F32)[:, None]
    lf = jax.nn.log_sigmoid(gate_heads(ml_f) + w['ml_fg_b'].astype(F32)[:, None])
    h_d, c_ml, n_ml, m_ml = mlstm_scan(to_heads(ml_q, ML_H), to_heads(ml_k, ML_H) * (ML_DK ** -0.5),
                                       to_heads(ml_v, ML_H), ig, lf, st['ml_c'], st['ml_n'], st['ml_m'])
    o_d = rmsnorm(h_d.astype(dty), w['ml_norm']) * jax.nn.sigmoid(to_heads(ml_o, ML_H))

    br = jnp.stack([from_heads(o_a), from_heads(o_b), o_c, from_heads(o_d)], axis=2)
    gate = jax.nn.sigmoid(gates.reshape(gates.shape[:2] + (N_BRANCH, D_MODEL)))
    merged = jnp.sum(gate * jnp.einsum('blnw,nwd->blnd', br, w['w_branch']), axis=2)
    out = merged @ w['w_out']
    new = {'gdn_conv': gdn_conv_new.astype(dty), 'gdn': s_gdn.astype(dty), 'hgrn': s_hg.astype(dty),
           'ssd_conv': ssd_conv_new.astype(dty), 'ssd': h_ssd.astype(dty), 'ml_c': c_ml.astype(dty),
           'ml_n': n_ml.astype(dty), 'ml_m': m_ml.astype(dty)}
    return out, new


def swiglu(x, wg, wu, wd):
    return (jax.nn.silu(x @ wg) * (x @ wu)) @ wd


def moe_swiglu(x, w_router, wg, wu, wd):
    logits = (x @ w_router).astype(F32)
    top_v, top_i = lax.top_k(logits, TOP_K)
    top_w = jax.nn.softmax(top_v, axis=-1)
    combine = jnp.einsum('blk,blke->ble', top_w, jax.nn.one_hot(top_i, N_EXPERTS, dtype=F32)).astype(x.dtype)
    y = jnp.zeros_like(x)
    for e in range(N_EXPERTS):
        y = y + combine[..., e:e + 1] * swiglu(x, wg[e], wu[e], wd[e])
    return y


def trunk(x, p, states, prm):
    sm = jax.nn.softmax(prm['hgrn_lb'].astype(F32), axis=0)
    lb_all = jnp.cumsum(sm, axis=0) - sm[0]
    h = x
    outs = {k: [] for k in STATE_KEYS}
    for l in range(DEPTH):
        st = {k: states[k][l] for k in STATE_KEYS}
        wl = {k: prm[k][l] for k in MIXER_KEYS}
        mix, new = mixer_block(rmsnorm(h, prm['g_mix'][l]), st, wl, lb_all[l])
        h = h + mix
        u = rmsnorm(h, prm['g_ffn'][l])
        j = l // 2
        if l % 2 == 0:
            h = h + swiglu(u, prm['w_ff_gate'][j], prm['w_ff_up'][j], prm['w_ff_down'][j])
        else:
            h = h + moe_swiglu(u, prm['w_router'][j], prm['w_ex_gate'][j], prm['w_ex_up'][j], prm['w_ex_down'][j])
        pg = jax.nn.sigmoid(rmsnorm(h, prm['g_ple'][l]) @ prm['w_ple_gate'][l])
        h = h + pg * (p[l] @ prm['w_ple'][l])
        for k in STATE_KEYS:
            outs[k].append(new[k])
    y = rmsnorm(h, prm['g_final'])
    return y, {k: jnp.stack(outs[k]) for k in STATE_KEYS}


def setup_inputs(seed: int = 0) -> dict:
    key = jax.random.key(seed)
    keys = iter(jax.random.split(key, 96))

    def nrm(shape, scale=1.0):
        return scale * jax.random.normal(next(keys), shape, jnp.float32)

    def unif(shape, lo, hi):
        return jax.random.uniform(next(keys), shape, jnp.float32, lo, hi)

    def dt_bias(shape):
        dt = jnp.exp(unif(shape, math.log(1e-3), math.log(1e-1)))
        return dt + jnp.log(-jnp.expm1(-dt))

    def gain(shape):
        return 1.0 + nrm(shape, 0.05)

    return {
        'x_prompt': nrm((BATCH, SEQ, D_MODEL)),
        'x_sample': nrm((DEC_BATCH, DEC_SEQ, D_MODEL)),
        'state_gdn_conv': nrm((DEPTH, DEC_BATCH, CONV_W - 1, GDN_CONV_CH)),
        'state_gdn': nrm((DEPTH, DEC_BATCH, GDN_H, GDN_DK, GDN_DV), 0.1),
        'state_hgrn': nrm((DEPTH, DEC_BATCH, HG_H, HG_DK, HG_DV), 0.5),
        'state_ssd_conv': nrm((DEPTH, DEC_BATCH, CONV_W - 1, SSD_CONV_CH)),
        'state_ssd': nrm((DEPTH, DEC_BATCH, SSD_H, SSD_P, SSD_N), 0.1),
        'state_mlstm_c': nrm((DEPTH, DEC_BATCH, ML_H, ML_DK, ML_DV)),
        'state_mlstm_n': nrm((DEPTH, DEC_BATCH, ML_H, ML_DK)),
        'state_mlstm_m': nrm((DEPTH, DEC_BATCH, ML_H)),
        'p_prompt': nrm((DEPTH, BATCH, SEQ, D_PLE)),
        'p_sample': nrm((DEPTH, DEC_BATCH, DEC_SEQ, D_PLE)),
        'g_mix': gain((DEPTH, D_MODEL)),
        'w_in': nrm((DEPTH, D_MODEL, N_IN), D_MODEL ** -0.5),
        'gdn_conv_w': nrm((DEPTH, CONV_W, GDN_CONV_CH), CONV_W ** -0.5),
        'gdn_a_log': jnp.log(unif((DEPTH, GDN_H), 1.0, 16.0)),
        'gdn_dt_bias': dt_bias((DEPTH, GDN_H)),
        'gdn_norm': gain((DEPTH, GDN_DV)),
        'hgrn_lb': nrm((DEPTH, HG_QK), 0.5),
        'hgrn_norm': gain((DEPTH, HG_DV)),
        'ssd_conv_w': nrm((DEPTH, CONV_W, SSD_CONV_CH), CONV_W ** -0.5),
        'ssd_conv_b': nrm((DEPTH, SSD_CONV_CH), 0.02),
        'ssd_a_log': jnp.log(unif((DEPTH, SSD_H), 1.0, 16.0)),
        'ssd_dt_bias': dt_bias((DEPTH, SSD_H)),
        'ssd_d': gain((DEPTH, SSD_H)),
        'ssd_norm': gain((DEPTH, BRANCH_W)),
        'ml_ig_b': nrm((DEPTH, ML_H), 0.1),
        'ml_fg_b': 3.0 + nrm((DEPTH, ML_H), 0.1),
        'ml_norm': gain((DEPTH, ML_DV)),
        'w_branch': nrm((DEPTH, N_BRANCH, BRANCH_W, D_MODEL), BRANCH_W ** -0.5),
        'w_out': nrm((DEPTH, D_MODEL, D_MODEL), D_MODEL ** -0.5),
        'g_ffn': gain((DEPTH, D_MODEL)),
        'w_ff_gate': nrm((N_DENSE, D_MODEL, D_FF), D_MODEL ** -0.5),
        'w_ff_up': nrm((N_DENSE, D_MODEL, D_FF), D_MODEL ** -0.5),
        'w_ff_down': nrm((N_DENSE, D_FF, D_MODEL), D_FF ** -0.5),
        'w_router': nrm((N_MOE, D_MODEL, N_EXPERTS), D_MODEL ** -0.5),
        'w_ex_gate': nrm((N_MOE, N_EXPERTS, D_MODEL, D_FF), D_MODEL ** -0.5),
        'w_ex_up': nrm((N_MOE, N_EXPERTS, D_MODEL, D_FF), D_MODEL ** -0.5),
        'w_ex_down': nrm((N_MOE, N_EXPERTS, D_FF, D_MODEL), D_FF ** -0.5),
        'w_ple': nrm((DEPTH, D_PLE, D_MODEL), D_PLE ** -0.5),
        'w_ple_gate': nrm((DEPTH, D_MODEL, D_MODEL), D_MODEL ** -0.5),
        'g_ple': gain((DEPTH, D_MODEL)),
        'g_final': gain((D_MODEL,)),
    }


def reference(x_prompt, x_sample, state_gdn_conv, state_gdn, state_hgrn, state_ssd_conv, state_ssd,
              state_mlstm_c, state_mlstm_n, state_mlstm_m, p_prompt, p_sample,
              g_mix, w_in, gdn_conv_w, gdn_a_log, gdn_dt_bias, gdn_norm, hgrn_lb, hgrn_norm,
              ssd_conv_w, ssd_conv_b, ssd_a_log, ssd_dt_bias, ssd_d, ssd_norm,
              ml_ig_b, ml_fg_b, ml_norm, w_branch, w_out, g_ffn,
              w_ff_gate, w_ff_up, w_ff_down, w_router, w_ex_gate, w_ex_up, w_ex_down,
              w_ple, w_ple_gate, g_ple, g_final):
    prm = {'g_mix': g_mix, 'w_in': w_in, 'gdn_conv_w': gdn_conv_w, 'gdn_a_log': gdn_a_log,
           'gdn_dt_bias': gdn_dt_bias, 'gdn_norm': gdn_norm, 'hgrn_lb': hgrn_lb, 'hgrn_norm': hgrn_norm,
           'ssd_conv_w': ssd_conv_w, 'ssd_conv_b': ssd_conv_b, 'ssd_a_log': ssd_a_log,
           'ssd_dt_bias': ssd_dt_bias, 'ssd_d': ssd_d, 'ssd_norm': ssd_norm,
           'ml_ig_b': ml_ig_b, 'ml_fg_b': ml_fg_b, 'ml_norm': ml_norm,
           'w_branch': w_branch, 'w_out': w_out, 'g_ffn': g_ffn,
           'w_ff_gate': w_ff_gate, 'w_ff_up': w_ff_up, 'w_ff_down': w_ff_down,
           'w_router': w_router, 'w_ex_gate': w_ex_gate, 'w_ex_up': w_ex_up, 'w_ex_down': w_ex_down,
           'w_ple': w_ple, 'w_ple_gate': w_ple_gate, 'g_ple': g_ple, 'g_final': g_final}
    bp = x_prompt.shape[0]
    dty = x_prompt.dtype
    st_p = {'gdn_conv': jnp.zeros((DEPTH, bp, CONV_W - 1, GDN_CONV_CH), dty),
            'gdn': jnp.zeros((DEPTH, bp, GDN_H, GDN_DK, GDN_DV), dty),
            'hgrn': jnp.zeros((DEPTH, bp, HG_H, HG_DK, HG_DV), dty),
            'ssd_conv': jnp.zeros((DEPTH, bp, CONV_W - 1, SSD_CONV_CH), dty),
            'ssd': jnp.zeros((DEPTH, bp, SSD_H, SSD_P, SSD_N), dty),
            'ml_c': jnp.zeros((DEPTH, bp, ML_H, ML_DK, ML_DV), dty),
            'ml_n': jnp.zeros((DEPTH, bp, ML_H, ML_DK), dty),
            'ml_m': jnp.zeros((DEPTH, bp, ML_H), dty)}
    st_s = {'gdn_conv': state_gdn_conv, 'gdn': state_gdn, 'hgrn': state_hgrn,
            'ssd_conv': state_ssd_conv, 'ssd': state_ssd, 'ml_c': state_mlstm_c,
            'ml_n': state_mlstm_n, 'ml_m': state_mlstm_m}
    y_prompt, npr = trunk(x_prompt, p_prompt, st_p, prm)
    y_sample, nsm = trunk(x_sample, p_sample, st_s, prm)
    return (y_prompt, y_sample,
            npr['gdn_conv'], npr['gdn'], npr['hgrn'], npr['ssd_conv'], npr['ssd'],
            npr['ml_c'], npr['ml_n'], npr['ml_m'],
            nsm['gdn_conv'], nsm['gdn'], nsm['hgrn'], nsm['ssd_conv'], nsm['ssd'],
            nsm['ml_c'], nsm['ml_n'], nsm['ml_m'])
```

```python
import functools
import math

import jax
import jax.numpy as jnp
from jax import lax
from jax.experimental import pallas as pl
from jax.experimental.pallas import tpu as pltpu

F32 = jnp.float32
BF16 = jnp.bfloat16

D_MODEL = 1024
DEPTH = 2
D_PLE = 256
N_BRANCH = 4
BRANCH_W = D_MODEL // N_BRANCH
CONV_W = 4
CHUNK = 64
EPS = 1e-6

GDN_H = 4
GDN_DK = 64
GDN_QK = GDN_H * GDN_DK
GDN_CONV_CH = 2 * GDN_QK + BRANCH_W
HG_H = 4
HG_QK = 256
SSD_H = 4
SSD_G = 2
SSD_N = 128
SSD_CONV_CH = BRANCH_W + 2 * SSD_G * SSD_N
ML_H = 4
ML_DK = 32
ML_QK = ML_H * ML_DK
D_FF = ((8 * D_MODEL // 3 + 255) // 256) * 256
N_EXPERTS = 8
TOP_K = 2

_REF_SPLITS = (('gdn_in', GDN_CONV_CH), ('gdn_b', GDN_H), ('gdn_a', GDN_H), ('gdn_z', BRANCH_W),
               ('hg_q', HG_QK), ('hg_f', HG_QK), ('hg_v', BRANCH_W), ('hg_g', BRANCH_W),
               ('ssd_z', BRANCH_W), ('ssd_in', SSD_CONV_CH), ('ssd_dt', SSD_H),
               ('ml_q', ML_QK), ('ml_k', ML_QK), ('ml_v', BRANCH_W), ('ml_i', ML_H), ('ml_f', ML_H),
               ('ml_o', BRANCH_W), ('gates', N_BRANCH * D_MODEL))
_MY_ORDER = ('gates', 'gdn_z', 'ssd_z', 'gdn_in', 'ssd_in', 'hg_q', 'hg_f', 'hg_v', 'hg_g',
             'ml_v', 'ml_o', 'ml_q', 'ml_k', 'gdn_b', 'gdn_a', 'ssd_dt', 'ml_i', 'ml_f')
LANES = 128


def _layout():
    widths = dict(_REF_SPLITS)
    off, out = 0, {}
    for name in _MY_ORDER:
        out[name] = (off, widths[name])
        off += widths[name]
    return out, -(-off // LANES) * LANES


COLS, N_PROJ = _layout()
VMEM_LIMIT = 56 * 1024 * 1024


def _permute_w_in(w):
    ref_off, acc = {}, 0
    for name, wd in _REF_SPLITS:
        ref_off[name] = acc
        acc += wd
    pieces = [w[:, ref_off[n]:ref_off[n] + COLS[n][1]] for n in _MY_ORDER]
    used = sum(COLS[n][1] for n in _MY_ORDER)
    pieces.append(jnp.zeros((w.shape[0], N_PROJ - used), w.dtype))
    return jnp.concatenate(pieces, axis=1).astype(BF16)


def _col(t, name):
    o, wd = COLS[name]
    return t[..., o:o + wd]


def _rms(x, g):
    return x * lax.rsqrt(jnp.mean(x * x, axis=-1, keepdims=True) + EPS) * g


def _norm_matmul_kernel(x_ref, g_ref, w_ref, o_ref, xn_ref):
    @pl.when(pl.program_id(1) == 0)
    def _():
        xn_ref[...] = _rms(x_ref[...], g_ref[...]).astype(BF16)

    o_ref[...] = jnp.dot(xn_ref[...], w_ref[...], preferred_element_type=F32)


def norm_matmul(x, g, w, tm, tn):
    m, d = x.shape
    n = w.shape[1]
    return pl.pallas_call(
        _norm_matmul_kernel,
        grid=(m // tm, n // tn),
        in_specs=[pl.BlockSpec((tm, d), lambda i, j: (i, 0)),
                  pl.BlockSpec((1, d), lambda i, j: (0, 0)),
                  pl.BlockSpec((d, tn), lambda i, j: (0, j))],
        out_specs=pl.BlockSpec((tm, tn), lambda i, j: (i, j)),
        out_shape=jax.ShapeDtypeStruct((m, n), F32),
        scratch_shapes=[pltpu.VMEM((tm, d), BF16)],
        compiler_params=pltpu.CompilerParams(
            dimension_semantics=("parallel", "arbitrary"), vmem_limit_bytes=VMEM_LIMIT),
        name="norm_matmul",
    )(x, g.reshape(1, d), w)


def _merge_kernel(h_ref, gates_ref, br_ref, wb_ref, wo_ref, o_ref):
    merged = None
    for n in range(N_BRANCH):
        y = jnp.dot(br_ref[:, n * BRANCH_W:(n + 1) * BRANCH_W], wb_ref[n], preferred_element_type=F32)
        t = jax.nn.sigmoid(gates_ref[:, n * D_MODEL:(n + 1) * D_MODEL]) * y
        merged = t if merged is None else merged + t
    o_ref[...] = h_ref[...] + jnp.dot(merged.astype(BF16), wo_ref[...], preferred_element_type=F32)


def merge(h, proj, br, w_branch, w_out, tm):
    m, d = h.shape
    assert COLS['gates'] == (0, N_BRANCH * D_MODEL)
    return pl.pallas_call(
        _merge_kernel,
        grid=(m // tm,),
        in_specs=[pl.BlockSpec((tm, d), lambda i: (i, 0)),
                  pl.BlockSpec((tm, N_BRANCH * d), lambda i: (i, 0)),
                  pl.BlockSpec((tm, N_BRANCH * BRANCH_W), lambda i: (i, 0)),
                  pl.BlockSpec((N_BRANCH, BRANCH_W, d), lambda i: (0, 0, 0)),
                  pl.BlockSpec((d, d), lambda i: (0, 0))],
        out_specs=pl.BlockSpec((tm, d), lambda i: (i, 0)),
        out_shape=jax.ShapeDtypeStruct((m, d), F32),
        compiler_params=pltpu.CompilerParams(
            dimension_semantics=("parallel",), vmem_limit_bytes=VMEM_LIMIT),
        name="merge",
    )(h, proj, br, w_branch, w_out)


def _ffn_kernel(h_ref, g_ref, wg_ref, wu_ref, wd_ref, o_ref, u_ref):
    f = pl.program_id(1)

    @pl.when(f == 0)
    def _():
        u_ref[...] = _rms(h_ref[...], g_ref[...]).astype(BF16)

    u = u_ref[...]
    a = jax.nn.silu(jnp.dot(u, wg_ref[...], preferred_element_type=F32))
    b = jnp.dot(u, wu_ref[...], preferred_element_type=F32)
    y = jnp.dot((a * b).astype(BF16), wd_ref[...], preferred_element_type=F32)

    @pl.when(f == 0)
    def _():
        o_ref[...] = h_ref[...] + y

    @pl.when(f != 0)
    def _():
        o_ref[...] += y


def ffn(h, g, wg, wu, wd, tm, tf):
    m, d = h.shape
    ff = wg.shape[1]
    return pl.pallas_call(
        _ffn_kernel,
        grid=(m // tm, ff // tf),
        in_specs=[pl.BlockSpec((tm, d), lambda i, f: (i, 0)),
                  pl.BlockSpec((1, d), lambda i, f: (0, 0)),
                  pl.BlockSpec((d, tf), lambda i, f: (0, f)),
                  pl.BlockSpec((d, tf), lambda i, f: (0, f)),
                  pl.BlockSpec((tf, d), lambda i, f: (f, 0))],
        out_specs=pl.BlockSpec((tm, d), lambda i, f: (i, 0)),
        out_shape=jax.ShapeDtypeStruct((m, d), F32),
        scratch_shapes=[pltpu.VMEM((tm, d), BF16)],
        compiler_params=pltpu.CompilerParams(
            dimension_semantics=("parallel", "arbitrary"), vmem_limit_bytes=VMEM_LIMIT),
        name="ffn",
    )(h, g.reshape(1, d), wg, wu, wd)


def _router_kernel(h_ref, g_ref, wr_ref, u_ref, w_ref, i_ref):
    u = _rms(h_ref[...], g_ref[...])
    u_ref[...] = u.astype(BF16)
    logits = jnp.dot(u, wr_ref[...], preferred_element_type=F32, precision=lax.Precision.HIGHEST)
    lane = lax.broadcasted_iota(jnp.int32, logits.shape, 1)
    neg = jnp.float32(-jnp.inf)
    logits = jnp.where(lane < N_EXPERTS, logits, neg)
    m1 = jnp.max(logits, axis=-1, keepdims=True)
    i1 = jnp.min(jnp.where(logits == m1, lane, LANES), axis=-1, keepdims=True)
    rest = jnp.where(lane == i1, neg, logits)
    m2 = jnp.max(rest, axis=-1, keepdims=True)
    i2 = jnp.min(jnp.where(rest == m2, lane, LANES), axis=-1, keepdims=True)
    e = jnp.exp(m2 - m1)
    den = 1.0 + e
    w_ref[...] = jnp.where(lane == 0, 1.0 / den, jnp.where(lane == 1, e / den, 0.0))
    i_ref[...] = jnp.where(lane == 0, i1, jnp.where(lane == 1, i2, 0))


def router(h, g, w_router, tm):
    m, d = h.shape
    wr = jnp.pad(w_router, ((0, 0), (0, LANES - N_EXPERTS)))
    return pl.pallas_call(
        _router_kernel,
        grid=(m // tm,),
        in_specs=[pl.BlockSpec((tm, d), lambda i: (i, 0)),
                  pl.BlockSpec((1, d), lambda i: (0, 0)),
                  pl.BlockSpec((d, LANES), lambda i: (0, 0))],
        out_specs=[pl.BlockSpec((tm, d), lambda i: (i, 0)),
                   pl.BlockSpec((tm, LANES), lambda i: (i, 0)),
                   pl.BlockSpec((tm, LANES), lambda i: (i, 0))],
        out_shape=[jax.ShapeDtypeStruct((m, d), BF16),
                   jax.ShapeDtypeStruct((m, LANES), F32),
                   jax.ShapeDtypeStruct((m, LANES), jnp.int32)],
        compiler_params=pltpu.CompilerParams(
            dimension_semantics=("parallel",), vmem_limit_bytes=VMEM_LIMIT),
        name="router",
    )(h, g.reshape(1, d), wr)


def _expert_kernel(te_ref, nt_ref, x_ref, wg_ref, wu_ref, wd_ref, o_ref):
    i = pl.program_id(0)
    f = pl.program_id(1)

    @pl.when(i < nt_ref[0])
    def _():
        x = x_ref[...]
        a = jax.nn.silu(jnp.dot(x, wg_ref[0], preferred_element_type=F32))
        b = jnp.dot(x, wu_ref[0], preferred_element_type=F32)
        y = jnp.dot((a * b).astype(BF16), wd_ref[0], preferred_element_type=F32)

        @pl.when(f == 0)
        def _():
            o_ref[...] = y

        @pl.when(f != 0)
        def _():
            o_ref[...] += y

    @pl.when(jnp.logical_and(i >= nt_ref[0], f == 0))
    def _():
        o_ref[...] = jnp.zeros_like(o_ref)


def experts(xs, tile_expert, n_tiles, wg, wu, wd, tm, tf):
    r, d = xs.shape
    ff = wg.shape[2]
    grid_spec = pltpu.PrefetchScalarGridSpec(
        num_scalar_prefetch=2,
        grid=(r // tm, ff // tf),
        in_specs=[pl.BlockSpec((tm, d), lambda i, f, te, nt: (i, 0)),
                  pl.BlockSpec((1, d, tf), lambda i, f, te, nt: (te[i], 0, f)),
                  pl.BlockSpec((1, d, tf), lambda i, f, te, nt: (te[i], 0, f)),
                  pl.BlockSpec((1, tf, d), lambda i, f, te, nt: (te[i], f, 0))],
        out_specs=pl.BlockSpec((tm, d), lambda i, f, te, nt: (i, 0)),
    )
    return pl.pallas_call(
        _expert_kernel,
        grid_spec=grid_spec,
        out_shape=jax.ShapeDtypeStruct((r, d), F32),
        compiler_params=pltpu.CompilerParams(
            dimension_semantics=("arbitrary", "arbitrary"), vmem_limit_bytes=VMEM_LIMIT),
        name="experts",
    )(tile_expert, n_tiles, xs, wg, wu, wd)


def moe(h, g, w_router, wg, wu, wd, tm_route, tm_e, tf):
    m, d = h.shape
    u, top_w, top_i = router(h, g, w_router, tm_route)
    top_w = top_w[:, :TOP_K]
    top_i = top_i[:, :TOP_K]
    flat_e = top_i.reshape(-1)
    onehot = (flat_e[:, None] == jnp.arange(N_EXPERTS, dtype=jnp.int32)[None, :]).astype(jnp.int32)
    rank = jnp.sum((jnp.cumsum(onehot, axis=0) - 1) * onehot, axis=1)
    counts = jnp.sum(onehot, axis=0)
    tiles_per = (counts + tm_e - 1) // tm_e
    tile_end = jnp.cumsum(tiles_per)
    tile_start = tile_end - tiles_per
    dest = tile_start[flat_e] * tm_e + rank
    n_rows = TOP_K * m + N_EXPERTS * tm_e
    n_row_tiles = n_rows // tm_e
    src_tok = jnp.zeros((n_rows,), jnp.int32).at[dest].set(jnp.arange(TOP_K * m, dtype=jnp.int32) // TOP_K)
    tile_ids = jnp.arange(n_row_tiles, dtype=jnp.int32)
    tile_expert = jnp.minimum(jnp.sum((tile_ids[:, None] >= tile_end[None, :]).astype(jnp.int32), axis=1),
                              N_EXPERTS - 1).astype(jnp.int32)
    n_tiles = tile_end[-1:].astype(jnp.int32)
    last_e = tile_expert[jnp.maximum(n_tiles[0] - 1, 0)]
    tile_expert = jnp.where(tile_ids < n_tiles[0], tile_expert, last_e)
    xs = jnp.take(u, src_tok, axis=0)
    ys = experts(xs, tile_expert, n_tiles, wg, wu, wd, tm_e, tf)
    picked = jnp.take(ys, dest, axis=0).reshape(m, TOP_K, d)
    return h + jnp.sum(top_w[:, :, None] * picked, axis=1)


def _ple_kernel(h_ref, p_ref, g_ref, wg_ref, wp_ref, gf_ref, o_ref, *, final):
    h = h_ref[...]
    v = _rms(h, g_ref[...]).astype(BF16)
    pg = jax.nn.sigmoid(jnp.dot(v, wg_ref[...], preferred_element_type=F32))
    e = jnp.dot(p_ref[...].astype(BF16), wp_ref[...], preferred_element_type=F32)
    out = h + pg * e
    if final:
        out = _rms(out, gf_ref[...])
    o_ref[...] = out


def ple(h, p, g, w_gate, w_p, g_final, tm, final):
    m, d = h.shape
    dp = p.shape[1]
    return pl.pallas_call(
        functools.partial(_ple_kernel, final=final),
        grid=(m // tm,),
        in_specs=[pl.BlockSpec((tm, d), lambda i: (i, 0)),
                  pl.BlockSpec((tm, dp), lambda i: (i, 0)),
                  pl.BlockSpec((1, d), lambda i: (0, 0)),
                  pl.BlockSpec((d, d), lambda i: (0, 0)),
                  pl.BlockSpec((dp, d), lambda i: (0, 0)),
                  pl.BlockSpec((1, d), lambda i: (0, 0))],
        out_specs=pl.BlockSpec((tm, d), lambda i: (i, 0)),
        out_shape=jax.ShapeDtypeStruct((m, d), F32),
        compiler_params=pltpu.CompilerParams(
            dimension_semantics=("parallel",), vmem_limit_bytes=VMEM_LIMIT),
        name="ple",
    )(h, p, g.reshape(1, d), w_gate, w_p, g_final.reshape(1, d))


def _l2norm(x):
    return x * lax.rsqrt(jnp.sum(x * x, axis=-1, keepdims=True) + EPS)


def _masked_exp(d, mask):
    return jnp.where(mask, jnp.exp(jnp.where(mask, d, 0.0)), 0.0)


def _to_heads(t, h):
    b, l, _ = t.shape
    return t.reshape(b, l, h, -1).transpose(0, 2, 1, 3)


def _from_heads(t):
    b, h, l, d = t.shape
    return t.transpose(0, 2, 1, 3).reshape(b, l, h * d)


def _gate_heads(t):
    return t.transpose(0, 2, 1)


def _chunk_len(l):
    return l if l <= CHUNK else math.gcd(l, CHUNK)


def _to_chunks(t, c):
    b, h, l = t.shape[:3]
    t = t.reshape((b, h, l // c, c) + t.shape[3:])
    return jnp.moveaxis(t, 2, 0)


def _from_chunks(t):
    n, b, h, c, d = t.shape
    return jnp.moveaxis(t, 0, 2).reshape(b, h, n * c, d)


def _causal_conv(x, buf, w, b=None):
    l = x.shape[1]
    xp = jnp.concatenate([buf, x], axis=1)
    y = w[0] * xp[:, :l]
    for j in range(1, CONV_W):
        y = y + w[j] * xp[:, j:j + l]
    if b is not None:
        y = y + b
    return jax.nn.silu(y), xp[:, l:]


def _gdn_scan(q, k, v, g, beta, s0):
    c = _chunk_len(q.shape[2])
    tri = jnp.tril(jnp.ones((c, c), dtype=bool))
    strict = jnp.tril(jnp.ones((c, c), dtype=bool), -1)
    eye = jnp.eye(c, dtype=F32)
    dv = v.shape[-1]

    def step(s, blk):
        qc, kc, vc, gc, bc = blk
        gam = jnp.cumsum(gc, axis=-1)
        dec = _masked_exp(gam[..., :, None] - gam[..., None, :], tri)
        a = jnp.where(strict, bc[..., :, None] * jnp.einsum('bhtk,bhsk->bhts', kc, kc) * dec, 0.0)
        rhs = jnp.concatenate([bc[..., None] * vc, (bc * jnp.exp(gam))[..., None] * kc], axis=-1)
        sol = lax.linalg.triangular_solve(eye + a, rhs, left_side=True, lower=True, unit_diagonal=True)
        u = sol[..., :dv] - jnp.einsum('bhtk,bhkv->bhtv', sol[..., dv:], s)
        att = jnp.einsum('bhtk,bhsk->bhts', qc, kc) * dec
        o = (jnp.exp(gam)[..., None] * jnp.einsum('bhtk,bhkv->bhtv', qc, s)
             + jnp.einsum('bhts,bhsv->bhtv', att, u))
        w_end = jnp.exp(gam[..., -1:] - gam)
        s = (jnp.exp(gam[..., -1])[..., None, None] * s
             + jnp.einsum('bhsk,bhsv->bhkv', kc * w_end[..., None], u))
        return s, o

    xs = tuple(_to_chunks(t, c) for t in (q, k, v, g, beta))
    s, o = lax.scan(step, s0, xs)
    return _from_chunks(o), s


def _hgrn_scan(q, k, v, logf, s0):
    c = _chunk_len(q.shape[2])
    tri = jnp.tril(jnp.ones((c, c), dtype=bool))

    def step(s, blk):
        qc, kc, vc, lf = blk
        b = jnp.cumsum(lf, axis=2)
        dec = _masked_exp(b[:, :, :, None, :] - b[:, :, None, :, :], tri[:, :, None])
        att = jnp.sum(qc[:, :, :, None, :] * kc[:, :, None, :, :] * dec, axis=-1)
        o = (jnp.einsum('bhtk,bhkv->bhtv', qc * jnp.exp(b), s)
             + jnp.einsum('bhts,bhsv->bhtv', att, vc))
        s = (jnp.exp(b[:, :, -1, :])[..., None] * s
             + jnp.einsum('bhsk,bhsv->bhkv', kc * jnp.exp(b[:, :, -1:, :] - b), vc))
        return s, o

    xs = tuple(_to_chunks(t, c) for t in (q, k, v, logf))
    s, o = lax.scan(step, s0, xs)
    return _from_chunks(o), s


def _ssd_scan(x, dt, da, bm, cm, h0):
    c = _chunk_len(x.shape[2])
    tri = jnp.tril(jnp.ones((c, c), dtype=bool))

    def step(h, blk):
        xc, dtc, dac, bc, cc = blk
        cum = jnp.cumsum(dac, axis=-1)
        dec = _masked_exp(cum[..., :, None] - cum[..., None, :], tri)
        att = jnp.einsum('bhtn,bhsn->bhts', cc, bc) * dec * dtc[..., None, :]
        y = (jnp.exp(cum)[..., None] * jnp.einsum('bhtn,bhpn->bhtp', cc, h)
             + jnp.einsum('bhts,bhsp->bhtp', att, xc))
        w_end = dtc * jnp.exp(cum[..., -1:] - cum)
        h = (jnp.exp(cum[..., -1])[..., None, None] * h
             + jnp.einsum('bhsp,bhsn->bhpn', xc * w_end[..., None], bc))
        return h, y

    xs = tuple(_to_chunks(t, c) for t in (x, dt, da, bm, cm))
    h, y = lax.scan(step, h0, xs)
    return _from_chunks(y), h


def _mlstm_scan(q, k, v, ig, logf, c0, n0, m0):
    c = _chunk_len(q.shape[2])
    tri = jnp.tril(jnp.ones((c, c), dtype=bool))

    def step(carry, blk):
        cm, nm, mm = carry
        qc, kc, vc, igc, lfc = blk
        b = jnp.cumsum(lfc, axis=-1)
        m = b + jnp.maximum(mm[..., None], lax.cummax(igc - b, axis=2))
        w_in = jnp.exp(b + mm[..., None] - m)
        logw = b[..., :, None] - b[..., None, :] + igc[..., None, :] - m[..., :, None]
        wts = _masked_exp(logw, tri) * jnp.einsum('bhtk,bhsk->bhts', qc, kc)
        num = (w_in[..., None] * jnp.einsum('bhtk,bhkv->bhtv', qc, cm)
               + jnp.einsum('bhts,bhsv->bhtv', wts, vc))
        nq = w_in * jnp.einsum('bhtk,bhk->bht', qc, nm) + jnp.sum(wts, axis=-1)
        h = num / jnp.maximum(jnp.abs(nq), jnp.exp(-m))[..., None]
        m_end = m[..., -1]
        w_end = jnp.exp(b[..., -1:] - b + igc - m_end[..., None])
        a_end = jnp.exp(b[..., -1] + mm - m_end)
        cm = a_end[..., None, None] * cm + jnp.einsum('bhsk,bhsv->bhkv', kc * w_end[..., None], vc)
        nm = a_end[..., None] * nm + jnp.einsum('bhsk,bhs->bhk', kc, w_end)
        return (cm, nm, m_end), h

    xs = tuple(_to_chunks(t, c) for t in (q, k, v, ig, logf))
    (cm, nm, mm), h = lax.scan(step, (c0, n0, m0), xs)
    return _from_chunks(h), cm, nm, mm


def _mixers(proj, st, w, lb):
    gdn_in, ssd_in = _col(proj, 'gdn_in'), _col(proj, 'ssd_in')

    qkv, gdn_conv_new = _causal_conv(gdn_in, st['gdn_conv'], w['gdn_conv_w'])
    gq, gk, gv = jnp.split(qkv, [GDN_QK, 2 * GDN_QK], axis=-1)
    gq = _l2norm(_to_heads(gq, GDN_H)) * (GDN_DK ** -0.5)
    gk = _l2norm(_to_heads(gk, GDN_H))
    beta = jax.nn.sigmoid(_gate_heads(_col(proj, 'gdn_b')))
    g_log = (-jnp.exp(w['gdn_a_log'])[:, None]
             * jax.nn.softplus(_gate_heads(_col(proj, 'gdn_a')) + w['gdn_dt_bias'][:, None]))
    o_a, s_gdn = _gdn_scan(gq, gk, _to_heads(gv, GDN_H), g_log, beta, st['gdn'])
    o_a = _rms(o_a, w['gdn_norm']) * jax.nn.silu(_to_heads(_col(proj, 'gdn_z'), GDN_H))

    f_pre = _col(proj, 'hg_f')
    log_f = jnp.log(lb + (1.0 - lb) * jax.nn.sigmoid(f_pre))
    k_in = (1.0 - lb) * jax.nn.sigmoid(-f_pre)
    o_b, s_hg = _hgrn_scan(_to_heads(_col(proj, 'hg_q'), HG_H), _to_heads(k_in, HG_H),
                           _to_heads(_col(proj, 'hg_v'), HG_H), _to_heads(log_f, HG_H), st['hgrn'])
    o_b = _rms(o_b, w['hgrn_norm']) * jax.nn.silu(_to_heads(_col(proj, 'hg_g'), HG_H))

    xbc, ssd_conv_new = _causal_conv(ssd_in, st['ssd_conv'], w['ssd_conv_w'], w['ssd_conv_b'])
    sx, sb, sc = jnp.split(xbc, [BRANCH_W, BRANCH_W + SSD_G * SSD_N], axis=-1)
    sx = _to_heads(sx, SSD_H)
    sb = jnp.repeat(_to_heads(sb, SSD_G), SSD_H // SSD_G, axis=1)
    sc = jnp.repeat(_to_heads(sc, SSD_G), SSD_H // SSD_G, axis=1)
    dt = jax.nn.softplus(_gate_heads(_col(proj, 'ssd_dt')) + w['ssd_dt_bias'][:, None])
    da = -jnp.exp(w['ssd_a_log'])[:, None] * dt
    y_c, h_ssd = _ssd_scan(sx, dt, da, sb, sc, st['ssd'])
    y_c = _from_heads(y_c + w['ssd_d'][:, None, None] * sx)
    o_c = _rms(y_c * jax.nn.silu(_col(proj, 'ssd_z')), w['ssd_norm'])

    ig = _gate_heads(_col(proj, 'ml_i')) + w['ml_ig_b'][:, None]
    lf = jax.nn.log_sigmoid(_gate_heads(_col(proj, 'ml_f')) + w['ml_fg_b'][:, None])
    h_d, c_ml, n_ml, m_ml = _mlstm_scan(_to_heads(_col(proj, 'ml_q'), ML_H),
                                        _to_heads(_col(proj, 'ml_k'), ML_H) * (ML_DK ** -0.5),
                                        _to_heads(_col(proj, 'ml_v'), ML_H), ig, lf,
                                        st['ml_c'], st['ml_n'], st['ml_m'])
    o_d = _rms(h_d, w['ml_norm']) * jax.nn.sigmoid(_to_heads(_col(proj, 'ml_o'), ML_H))

    br = jnp.concatenate([_from_heads(o_a), _from_heads(o_b), o_c, _from_heads(o_d)], axis=-1)
    new = {'gdn_conv': gdn_conv_new, 'gdn': s_gdn, 'hgrn': s_hg, 'ssd_conv': ssd_conv_new,
           'ssd': h_ssd, 'ml_c': c_ml, 'ml_n': n_ml, 'ml_m': m_ml}
    return br.astype(BF16), new


STATE_KEYS = ('gdn_conv', 'gdn', 'hgrn', 'ssd_conv', 'ssd', 'ml_c', 'ml_n', 'ml_m')
MIXER_KEYS = ('gdn_conv_w', 'gdn_a_log', 'gdn_dt_bias', 'gdn_norm', 'hgrn_norm',
              'ssd_conv_w', 'ssd_conv_b', 'ssd_a_log', 'ssd_dt_bias', 'ssd_d', 'ssd_norm',
              'ml_ig_b', 'ml_fg_b', 'ml_norm')


def kernel(x_prompt, x_sample, state_gdn_conv, state_gdn, state_hgrn, state_ssd_conv, state_ssd, state_mlstm_c, state_mlstm_n, state_mlstm_m, p_prompt, p_sample, g_mix, w_in, gdn_conv_w, gdn_a_log, gdn_dt_bias, gdn_norm, hgrn_lb, hgrn_norm, ssd_conv_w, ssd_conv_b, ssd_a_log, ssd_dt_bias, ssd_d, ssd_norm, ml_ig_b, ml_fg_b, ml_norm, w_branch, w_out, g_ffn, w_ff_gate, w_ff_up, w_ff_down, w_router, w_ex_gate, w_ex_up, w_ex_down, w_ple, w_ple_gate, g_ple, g_final):
    prm = {'gdn_conv_w': gdn_conv_w, 'gdn_a_log': gdn_a_log, 'gdn_dt_bias': gdn_dt_bias,
           'gdn_norm': gdn_norm, 'hgrn_norm': hgrn_norm, 'ssd_conv_w': ssd_conv_w,
           'ssd_conv_b': ssd_conv_b, 'ssd_a_log': ssd_a_log, 'ssd_dt_bias': ssd_dt_bias,
           'ssd_d': ssd_d, 'ssd_norm': ssd_norm, 'ml_ig_b': ml_ig_b, 'ml_fg_b': ml_fg_b,
           'ml_norm': ml_norm}
    bp, lp, d = x_prompt.shape
    bs, ls, _ = x_sample.shape
    mp, ms = bp * lp, bs * ls
    z = lambda *s: jnp.zeros(s, F32)
    st_p = {'gdn_conv': z(DEPTH, bp, CONV_W - 1, GDN_CONV_CH), 'gdn': z(DEPTH, bp, GDN_H, GDN_DK, BRANCH_W // GDN_H),
            'hgrn': z(DEPTH, bp, HG_H, HG_QK // HG_H, BRANCH_W // HG_H),
            'ssd_conv': z(DEPTH, bp, CONV_W - 1, SSD_CONV_CH), 'ssd': z(DEPTH, bp, SSD_H, BRANCH_W // SSD_H, SSD_N),
            'ml_c': z(DEPTH, bp, ML_H, ML_DK, BRANCH_W // ML_H), 'ml_n': z(DEPTH, bp, ML_H, ML_DK),
            'ml_m': z(DEPTH, bp, ML_H)}
    st_s = {'gdn_conv': state_gdn_conv, 'gdn': state_gdn, 'hgrn': state_hgrn, 'ssd_conv': state_ssd_conv,
            'ssd': state_ssd, 'ml_c': state_mlstm_c, 'ml_n': state_mlstm_n, 'ml_m': state_mlstm_m}

    sm = jax.nn.softmax(hgrn_lb, axis=0)
    lb_all = jnp.cumsum(sm, axis=0) - sm[0]

    h = jnp.concatenate([x_prompt.reshape(mp, d), x_sample.reshape(ms, d)], axis=0)
    new_p = {k: [] for k in STATE_KEYS}
    new_s = {k: [] for k in STATE_KEYS}
    for l in range(DEPTH):
        wl = {k: prm[k][l] for k in MIXER_KEYS}
        proj = norm_matmul(h, g_mix[l], _permute_w_in(w_in[l]), tm=1024, tn=1152)
        br_p, np_ = _mixers(proj[:mp].reshape(bp, lp, N_PROJ), {k: st_p[k][l] for k in STATE_KEYS}, wl, lb_all[l])
        br_s, ns_ = _mixers(proj[mp:].reshape(bs, ls, N_PROJ), {k: st_s[k][l] for k in STATE_KEYS}, wl, lb_all[l])
        br = jnp.concatenate([br_p.reshape(mp, -1), br_s.reshape(ms, -1)], axis=0)
        h = merge(h, proj, br, w_branch[l].astype(BF16), w_out[l].astype(BF16), tm=512)
        j = l // 2
        if l % 2 == 0:
            h = ffn(h, g_ffn[l], w_ff_gate[j].astype(BF16), w_ff_up[j].astype(BF16),
                    w_ff_down[j].astype(BF16), tm=512, tf=D_FF // 2)
        else:
            h = moe(h, g_ffn[l], w_router[j], w_ex_gate[j].astype(BF16), w_ex_up[j].astype(BF16),
                    w_ex_down[j].astype(BF16), tm_route=512, tm_e=512, tf=D_FF // 2)
        p = jnp.concatenate([p_prompt[l].reshape(mp, D_PLE), p_sample[l].reshape(ms, D_PLE)], axis=0)
        h = ple(h, p, g_ple[l], w_ple_gate[l].astype(BF16), w_ple[l].astype(BF16), g_final,
                tm=512, final=(l == DEPTH - 1))
        for k in STATE_KEYS:
            new_p[k].append(np_[k])
            new_s[k].append(ns_[k])
    y_prompt = h[:mp].reshape(bp, lp, d)
    y_sample = h[mp:].reshape(bs, ls, d)
    sp = {k: jnp.stack(v) for k, v in new_p.items()}
    ss = {k: jnp.stack(v) for k, v in new_s.items()}
    return (y_prompt, y_sample,
            sp['gdn_conv'], sp['gdn'], sp['hgrn'], sp['ssd_conv'], sp['ssd'], sp['ml_c'], sp['ml_n'], sp['ml_m'],
            ss['gdn_conv'], ss['gdn'], ss['hgrn'], ss['ssd_conv'], ss['ssd'], ss['ml_c'], ss['ml_n'], ss['ml_m'])
```

```python
import functools

import numpy as np
import jax
import jax.numpy as jnp
from jax import lax
from jax.experimental import pallas as pl
from jax.experimental.pallas import tpu as pltpu

F32 = jnp.float32
BF16 = jnp.bfloat16
HI = lax.Precision.HIGHEST

D_MODEL = 1024
DEPTH = 2
D_PLE = 256
N_BRANCH = 4
BRANCH_W = D_MODEL // N_BRANCH
CONV_W = 4
EPS = 1e-6

GDN_H = 4
GDN_DK = 64
GDN_QK = GDN_H * GDN_DK
GDN_CONV_CH = 2 * GDN_QK + BRANCH_W
HG_H = 4
HG_QK = 256
SSD_H = 4
SSD_G = 2
SSD_N = 128
SSD_CONV_CH = BRANCH_W + 2 * SSD_G * SSD_N
ML_H = 4
ML_DK = 32
ML_QK = ML_H * ML_DK
D_FF = ((8 * D_MODEL // 3 + 255) // 256) * 256
N_EXPERTS = 8
TOP_K = 2

_REF_SPLITS = (('gdn_in', GDN_CONV_CH), ('gdn_b', GDN_H), ('gdn_a', GDN_H), ('gdn_z', BRANCH_W),
               ('hg_q', HG_QK), ('hg_f', HG_QK), ('hg_v', BRANCH_W), ('hg_g', BRANCH_W),
               ('ssd_z', BRANCH_W), ('ssd_in', SSD_CONV_CH), ('ssd_dt', SSD_H),
               ('ml_q', ML_QK), ('ml_k', ML_QK), ('ml_v', BRANCH_W), ('ml_i', ML_H), ('ml_f', ML_H),
               ('ml_o', BRANCH_W), ('gates', N_BRANCH * D_MODEL))
_MY_ORDER = ('gates', 'gdn_z', 'ssd_z', 'gdn_in', 'ssd_in', 'hg_q', 'hg_f', 'hg_v', 'hg_g',
             'ml_v', 'ml_o', 'ml_q', 'ml_k', 'gdn_b', 'gdn_a', 'ssd_dt', 'ml_i', 'ml_f')
LANES = 128
CH = 64


def _layout():
    widths = dict(_REF_SPLITS)
    off, out = 0, {}
    for name in _MY_ORDER:
        out[name] = (off, widths[name])
        off += widths[name]
    return out, -(-off // LANES) * LANES


COLS, N_PROJ = _layout()
GATE_COL0 = COLS['gdn_b'][0]
L_GDN_B, L_GDN_A, L_SSD_DT, L_ML_I, L_ML_F = (COLS[n][0] - GATE_COL0 for n in ('gdn_b', 'gdn_a', 'ssd_dt', 'ml_i', 'ml_f'))
VMEM_LIMIT = 56 * 1024 * 1024


def _permute_w_in(w):
    ref_off, acc = {}, 0
    for name, wd in _REF_SPLITS:
        ref_off[name] = acc
        acc += wd
    pieces = [w[:, ref_off[n]:ref_off[n] + COLS[n][1]] for n in _MY_ORDER]
    used = sum(COLS[n][1] for n in _MY_ORDER)
    pieces.append(jnp.zeros((w.shape[0], N_PROJ - used), w.dtype))
    return jnp.concatenate(pieces, axis=1).astype(BF16)


def _rms(x, g):
    return x * lax.rsqrt(jnp.mean(x * x, axis=-1, keepdims=True) + EPS) * g


def _dot(a, b, **kw):
    return jnp.dot(a, b, preferred_element_type=F32, **kw)


def _dot_nt(a, b, **kw):
    return lax.dot_general(a, b, (((1,), (1,)), ((), ())), preferred_element_type=F32, **kw)


def _dot_tn(a, b, **kw):
    return lax.dot_general(a, b, (((0,), (0,)), ((), ())), preferred_element_type=F32, **kw)


def _norm_matmul_kernel(x_ref, g_ref, w_ref, o_ref, xn_ref):
    @pl.when(pl.program_id(1) == 0)
    def _():
        xn_ref[...] = _rms(x_ref[...], g_ref[...]).astype(BF16)

    o_ref[...] = _dot(xn_ref[...], w_ref[...])


def norm_matmul(x, g, w, tm, tn):
    m, d = x.shape
    n = w.shape[1]
    return pl.pallas_call(
        _norm_matmul_kernel,
        grid=(m // tm, n // tn),
        in_specs=[pl.BlockSpec((tm, d), lambda i, j: (i, 0)),
                  pl.BlockSpec((1, d), lambda i, j: (0, 0)),
                  pl.BlockSpec((d, tn), lambda i, j: (0, j))],
        out_specs=pl.BlockSpec((tm, tn), lambda i, j: (i, j)),
        out_shape=jax.ShapeDtypeStruct((m, n), F32),
        scratch_shapes=[pltpu.VMEM((tm, d), BF16)],
        compiler_params=pltpu.CompilerParams(
            dimension_semantics=("parallel", "arbitrary"), vmem_limit_bytes=VMEM_LIMIT),
        name="norm_matmul",
    )(x, g.reshape(1, d), w)


def _seg_masks(t_len):
    row = lax.broadcasted_iota(jnp.int32, (CH, CH), 0)
    col = lax.broadcasted_iota(jnp.int32, (CH, CH), 1)
    same = (row // t_len) == (col // t_len)
    tri = jnp.logical_and(same, col <= row)
    strict = jnp.logical_and(same, col < row)
    return same, tri, strict


def _row_forms(x, n_rows=24):
    r = lax.broadcasted_iota(jnp.int32, (n_rows, LANES), 0)
    l = lax.broadcasted_iota(jnp.int32, (n_rows, LANES), 1)
    return _dot_nt((r == l).astype(F32), x, precision=HI)


def _softplus(x):
    return jnp.maximum(x, 0.0) + jnp.log1p(jnp.exp(-jnp.abs(x)))


def _masked_exp(d, mask):
    return jnp.where(mask, jnp.exp(jnp.where(mask, d, 0.0)), 0.0)


def _conv_silu(x, ext_scr, cw_ref, bias, t_len, n_seg):
    w = x.shape[-1]
    ext_scr[:, 8:8 + t_len, :] = x.reshape(n_seg, t_len, w)
    y = cw_ref[3:4, :] * x
    for j in range(1, CONV_W):
        y = y + cw_ref[3 - j:4 - j, :] * ext_scr[:, 8 - j:8 - j + t_len, :].reshape(CH, w)
    if bias is not None:
        y = y + bias
    tail = ext_scr[:, 5 + t_len:8 + t_len, :]
    ext_scr[:, 5:8, :] = tail
    return jax.nn.silu(y), tail


def _halves(xp, lo):
    s_lo = jnp.sum(jnp.where(lo, xp, 0.0), axis=-1, keepdims=True)
    s_hi = jnp.sum(jnp.where(lo, 0.0, xp), axis=-1, keepdims=True)
    return jnp.where(lo, s_lo, s_hi)


def _head_rmsnorm(xp, lo):
    return xp * lax.rsqrt(_halves(xp * xp, lo) * (1.0 / 64) + EPS)


def _head_l2norm(xp, lo):
    return xp * lax.rsqrt(_halves(xp * xp, lo) + EPS)


def _seg_sum(parts, rowi, t_len):
    if len(parts) == 1:
        return parts[0]
    acc = jnp.where(rowi // t_len == 0, parts[0], 0.0)
    for s in range(1, len(parts)):
        acc = acc + jnp.where(rowi // t_len == s, parts[s], 0.0)
    return acc


def _seg_rows(x, rowi, t_len, s, n_seg):
    return x if n_seg == 1 else jnp.where(rowi // t_len == s, x, 0.0)


def _quarter_sel(idx, width, vals):
    out = vals[3]
    for h in (2, 1, 0):
        out = jnp.where(idx < (h + 1) * width, vals[h], out)
    return out


def _gate_rows(pairs):
    t = jnp.zeros((8, LANES), F32)
    for r, (off, v) in enumerate(pairs):
        t = t.at[r, off:off + v.shape[0]].set(v.astype(F32))
    return t


def _to_bd(s):
    b, _, a, c = s.shape
    s = s.reshape(b, 2, 2, a, c)
    z = jnp.zeros_like(s[:, :, 0])
    top = jnp.concatenate([s[:, :, 0], z], axis=-1)
    bot = jnp.concatenate([z, s[:, :, 1]], axis=-1)
    return jnp.concatenate([top, bot], axis=-2)


def _from_bd(s):
    b, _, a2, c2 = s.shape
    a, c = a2 // 2, c2 // 2
    return jnp.stack([s[:, :, :a, :c], s[:, :, a:, c:]], axis=2).reshape(b, 4, a, c)


def _mixer_call(kernel_fn, name, proj, row_blk0, in_blocks, state_ins, params, state_outs, scratch,
                *, n_outer, n_chunks, n_seg):
    rows = n_outer * n_chunks * CH
    rmap = lambda blk: (lambda i, c: (row_blk0 + i * n_chunks + c, blk))
    full = lambda a: pl.BlockSpec(a.shape, lambda i, c: (0,) * a.ndim)
    sblk = lambda a: pl.BlockSpec((a.shape[0] // n_outer,) + a.shape[1:], lambda i, c: (i,) + (0,) * (a.ndim - 1))
    for off, wd in in_blocks:
        assert off % wd == 0
    in_specs = ([pl.BlockSpec((CH, wd), rmap(off // wd)) for off, wd in in_blocks]
                + [sblk(a) for a in state_ins] + [full(a) for a in params])
    out_specs = ([pl.BlockSpec((CH, BRANCH_W), lambda i, c: (i * n_chunks + c, 0))]
                 + [sblk(a) for a in state_outs])
    out_shape = ([jax.ShapeDtypeStruct((rows, BRANCH_W), BF16)]
                 + [jax.ShapeDtypeStruct(a.shape, F32) for a in state_outs])
    return pl.pallas_call(
        functools.partial(kernel_fn, n_seg=n_seg),
        grid=(n_outer, n_chunks),
        in_specs=in_specs, out_specs=out_specs, out_shape=out_shape,
        scratch_shapes=scratch,
        compiler_params=pltpu.CompilerParams(
            dimension_semantics=("parallel", "arbitrary"), vmem_limit_bytes=VMEM_LIMIT),
        name=name,
    )(*([proj] * len(in_blocks)), *state_ins, *params)


def _gdn_kernel(xin_ref, z_ref, sm_ref, conv0_ref, s0_ref, cw_ref, gp_ref, ng_ref,
                o_ref, convn_ref, sn_ref, ext_scr, s_scr, *, n_seg):
    t_len = CH // n_seg
    c = pl.program_id(1)
    last = pl.num_programs(1) - 1

    @pl.when(c == 0)
    def _():
        s_scr[...] = s0_ref[...]
        ext_scr[:, 5:8, :] = conv0_ref[...]

    xc, tail = _conv_silu(xin_ref[...], ext_scr, cw_ref, None, t_len, n_seg)

    @pl.when(c == last)
    def _():
        convn_ref[...] = tail

    same, tri, strict = _seg_masks(t_len)
    lane = lax.broadcasted_iota(jnp.int32, (CH, LANES), 1)
    rowi = lax.broadcasted_iota(jnp.int32, (CH, 1), 0)
    lo = lane < 64
    sm = sm_ref[...]
    beta = jax.nn.sigmoid(sm)
    gl = jnp.logical_and(lane >= L_GDN_A, lane < L_GDN_A + GDN_H)
    g = jnp.where(gl, -jnp.exp(gp_ref[0:1, :]) * _softplus(sm + gp_ref[1:2, :]), 0.0)
    gam = _dot(tri.astype(F32), g, precision=HI)
    gam_end = _dot(same.astype(F32), g, precision=HI)
    gam_r = _row_forms(gam, 8)
    r128 = lax.broadcasted_iota(jnp.int32, (LANES, LANES), 0)
    l128 = lax.broadcasted_iota(jnp.int32, (LANES, LANES), 1)
    bd = (r128 < 64) == (l128 < 64)
    rsel = lax.broadcasted_iota(jnp.int32, (LANES, 1), 0) < 64
    outs = []
    for p in range(2):
        q_p = _head_l2norm(xc[:, 128 * p:128 * (p + 1)], lo) * (GDN_DK ** -0.5)
        k_p = _head_l2norm(xc[:, GDN_QK + 128 * p:GDN_QK + 128 * (p + 1)], lo)
        v_p = xc[:, 2 * GDN_QK + 128 * p:2 * GDN_QK + 128 * (p + 1)]
        kb = k_p.astype(BF16)
        solv = jnp.zeros((CH, LANES), F32)
        solk = jnp.zeros((CH, LANES), F32)
        atts, gcs = [], []
        for j in range(2):
            h = 2 * p + j
            mj = lo if j == 0 else jnp.logical_not(lo)
            kk = _dot_nt(jnp.where(mj, k_p, 0.0).astype(BF16), kb)
            qk = _dot_nt(jnp.where(mj, q_p, 0.0).astype(BF16), kb)
            gc = gam[:, L_GDN_A + h:L_GDN_A + h + 1]
            bc = beta[:, L_GDN_B + h:L_GDN_B + h + 1]
            dec = _masked_exp(gc - gam_r[L_GDN_A + h:L_GDN_A + h + 1, :], tri)
            a = jnp.where(strict, bc * kk * dec, 0.0)
            x = jnp.concatenate([jnp.where(mj, bc * v_p, 0.0),
                                 jnp.where(mj, (bc * jnp.exp(gc)) * k_p, 0.0)], axis=-1)
            x = x - _dot(a, x, precision=HI)
            pw = a
            n = 2
            while n < t_len:
                pw = _dot(pw, pw, precision=HI)
                x = x + _dot(pw, x, precision=HI)
                n *= 2
            solv = solv + x[:, :LANES]
            solk = solk + x[:, LANES:]
            atts.append(qk * dec)
            gcs.append(gc)
        solk_b = solk.astype(BF16)
        qb = q_p.astype(BF16)
        u = solv - _seg_sum([_dot(solk_b, s_scr[s, p].astype(BF16)) for s in range(n_seg)], rowi, t_len)
        qs = _seg_sum([_dot(qb, s_scr[s, p].astype(BF16)) for s in range(n_seg)], rowi, t_len)
        o = jnp.where(lo, jnp.exp(gcs[0]), jnp.exp(gcs[1])) * qs
        for j in range(2):
            mj = lo if j == 0 else jnp.logical_not(lo)
            o = o + _dot(atts[j].astype(BF16), jnp.where(mj, u, 0.0).astype(BF16))
        ge0 = gam_end[:, L_GDN_A + 2 * p:L_GDN_A + 2 * p + 1]
        ge1 = gam_end[:, L_GDN_A + 2 * p + 1:L_GDN_A + 2 * p + 2]
        kw = k_p * jnp.where(lo, jnp.exp(ge0 - gcs[0]), jnp.exp(ge1 - gcs[1]))
        ub = u.astype(BF16)
        for s in range(n_seg):
            r0 = s * t_len
            dec_s = jnp.where(rsel, jnp.exp(ge0[r0:r0 + 1, :]), jnp.exp(ge1[r0:r0 + 1, :]))
            upd = _dot_tn(_seg_rows(kw, rowi, t_len, s, n_seg).astype(BF16), ub)
            s_scr[s, p] = dec_s * s_scr[s, p] + jnp.where(bd, upd, 0.0)
        outs.append(_head_rmsnorm(o, lo))
    o_all = jnp.concatenate(outs, axis=-1) * ng_ref[...] * jax.nn.silu(z_ref[...])
    o_ref[...] = o_all.astype(BF16)

    @pl.when(c == last)
    def _():
        sn_ref[...] = s_scr[...]


def gdn_mixer(proj, row_blk0, conv0, s0, w, **grid):
    t_len = CH // grid['n_seg']
    gp = _gate_rows([(L_GDN_A, w['gdn_a_log']), (L_GDN_A, w['gdn_dt_bias'])])
    ng = jnp.tile(w['gdn_norm'], GDN_H).reshape(1, BRANCH_W)
    s0bd = _to_bd(s0)
    o, convn, sn = _mixer_call(
        _gdn_kernel, "gdn_mixer", proj, row_blk0,
        [COLS['gdn_in'], COLS['gdn_z'], (GATE_COL0, LANES)],
        [conv0, s0bd], [w['gdn_conv_w'], gp, ng], [conv0, s0bd],
        [pltpu.VMEM((grid['n_seg'], 8 + t_len, GDN_CONV_CH), F32), pltpu.VMEM((grid['n_seg'], 2, 128, 128), F32)],
        **grid)
    return o, convn, _from_bd(sn)


def _hgrn_levels(t_len):
    lv, n = [], t_len
    while n >= 2:
        lv.append(n)
        n //= 2
    return lv


def _hgrn_cmat(t_len):
    t = np.arange(CH)[:, None]
    j = np.arange(CH)[None, :]
    same = (t // t_len) == (j // t_len)
    mats = [same & (j <= t), same]
    for n in _hgrn_levels(t_len):
        mid = (t // n) * n + n // 2
        mats.append((t % n >= n // 2) & (j >= mid) & (j <= t))
        mats.append((t % n < n // 2) & (j > t) & (j <= mid - 1))
    return jnp.asarray(np.concatenate(mats, axis=0).astype(np.float32), dtype=BF16)


def _split3(x):
    hi = x.astype(BF16)
    r = x - hi.astype(F32)
    mid = r.astype(BF16)
    return hi, mid, (r - mid.astype(F32)).astype(BF16)


def _hgrn_kernel(x_ref, s0_ref, cm_ref, lb_ref, ng_ref, o_ref, sn_ref, s_scr, *, n_seg):
    t_len = CH // n_seg
    levels = _hgrn_levels(t_len)
    c = pl.program_id(1)
    last = pl.num_programs(1) - 1

    @pl.when(c == 0)
    def _():
        s_scr[...] = s0_ref[...]

    lane = lax.broadcasted_iota(jnp.int32, (CH, LANES), 1)
    rowi = lax.broadcasted_iota(jnp.int32, (CH, 1), 0)
    row = lax.broadcasted_iota(jnp.int32, (CH, CH), 0)
    col = lax.broadcasted_iota(jnp.int32, (CH, CH), 1)
    lo = lane < 64
    r128 = lax.broadcasted_iota(jnp.int32, (LANES, LANES), 0)
    l128 = lax.broadcasted_iota(jnp.int32, (LANES, LANES), 1)
    bd = (r128 < 64) == (l128 < 64)

    lb = lb_ref[...]
    f_pre = x_ref[:, HG_QK:2 * HG_QK]
    log_f = jnp.log(lb + (1.0 - lb) * jax.nn.sigmoid(f_pre))
    k_in = (1.0 - lb) * jax.nn.sigmoid(-f_pre)
    cm = cm_ref[...]
    ex = None
    for part in _split3(log_f):
        t = _dot(cm, part)
        ex = t if ex is None else ex + t
    b = ex[0:CH]
    b_end = ex[CH:2 * CH]
    outs = []
    for p in range(2):
        ls = slice(128 * p, 128 * (p + 1))
        q_p = x_ref[:, ls]
        k_p = k_in[:, ls]
        v_p = x_ref[:, 2 * HG_QK + 128 * p:2 * HG_QK + 128 * (p + 1)]
        qk = q_p * k_p
        qe = (q_p * jnp.exp(b[:, ls])).astype(BF16)
        o = _seg_sum([_dot_nt(qe, s_scr[s, p].astype(BF16)) for s in range(n_seg)], rowi, t_len)
        for j in range(2):
            mj = lo if j == 0 else jnp.logical_not(lo)
            diag = jnp.sum(jnp.where(mj, qk, 0.0), axis=-1, keepdims=True)
            att = jnp.where(row == col, diag, 0.0)
            for li, n in enumerate(levels):
                base = 2 * CH + 2 * CH * li
                eq = ex[base:base + CH, ls]
                ek = ex[base + CH:base + 2 * CH, ls]
                tq = (rowi % n) >= (n // 2)
                qt = jnp.where(jnp.logical_and(mj, tq), q_p * jnp.exp(eq), 0.0)
                kt = jnp.where(tq, 0.0, k_p * jnp.exp(ek))
                att = att + jnp.where((row // n) == (col // n), _dot_nt(qt.astype(BF16), kt.astype(BF16)), 0.0)
            o = o + _dot(att.astype(BF16), jnp.where(mj, v_p, 0.0).astype(BF16))
        kw = (k_p * jnp.exp(b_end[:, ls] - b[:, ls])).astype(BF16)
        for s in range(n_seg):
            r0 = s * t_len
            upd = _dot_tn(_seg_rows(v_p, rowi, t_len, s, n_seg).astype(BF16), kw)
            s_scr[s, p] = jnp.exp(b_end[r0:r0 + 1, ls]) * s_scr[s, p] + jnp.where(bd, upd, 0.0)
        outs.append(_head_rmsnorm(o, lo))
    o_all = jnp.concatenate(outs, axis=-1) * ng_ref[...] * jax.nn.silu(x_ref[:, 3 * HG_QK:4 * HG_QK])
    o_ref[...] = o_all.astype(BF16)

    @pl.when(c == last)
    def _():
        sn_ref[...] = s_scr[...]


def hgrn_mixer(proj, row_blk0, s0, w, lb, **grid):
    t_len = CH // grid['n_seg']
    ng = jnp.tile(w['hgrn_norm'], HG_H).reshape(1, BRANCH_W)
    st = _to_bd(jnp.swapaxes(s0, -1, -2))
    assert COLS['hg_f'][0] == COLS['hg_q'][0] + HG_QK and COLS['hg_g'][0] == COLS['hg_q'][0] + 3 * HG_QK
    o, sn = _mixer_call(
        _hgrn_kernel, "hgrn_mixer", proj, row_blk0, [(COLS['hg_q'][0], 4 * HG_QK)],
        [st], [_hgrn_cmat(t_len), lb.reshape(1, HG_QK), ng], [st],
        [pltpu.VMEM((grid['n_seg'], 2, 128, 128), F32)], **grid)
    return o, jnp.swapaxes(_from_bd(sn), -1, -2)


def _ssd_kernel(xin_ref, z_ref, sm_ref, conv0_ref, h0_ref, cw_ref, cb_ref, gp_ref, dvec_ref, ng_ref,
                o_ref, convn_ref, hn_ref, ext_scr, h_scr, *, n_seg):
    t_len = CH // n_seg
    c = pl.program_id(1)
    last = pl.num_programs(1) - 1

    @pl.when(c == 0)
    def _():
        h_scr[...] = h0_ref[...].reshape(n_seg, 2, 128, SSD_N)
        ext_scr[:, 5:8, :] = conv0_ref[...]

    xc, tail = _conv_silu(xin_ref[...], ext_scr, cw_ref, cb_ref[...], t_len, n_seg)

    @pl.when(c == last)
    def _():
        convn_ref[...] = tail

    sx, bm, cm = xc[:, :256], xc[:, 256:512], xc[:, 512:768]
    same, tri, _ = _seg_masks(t_len)
    lane = lax.broadcasted_iota(jnp.int32, (CH, LANES), 1)
    rowi = lax.broadcasted_iota(jnp.int32, (CH, 1), 0)
    gl = jnp.logical_and(lane >= L_SSD_DT, lane < L_SSD_DT + SSD_H)
    dt = jnp.where(gl, _softplus(sm_ref[...] + gp_ref[1:2, :]), 0.0)
    da = -jnp.exp(gp_ref[0:1, :]) * dt
    cum = _dot(tri.astype(F32), da, precision=HI)
    cum_end = _dot(same.astype(F32), da, precision=HI)
    cum_r = _row_forms(cum, 16)
    dt_r = _row_forms(dt, 16)
    lo = lane < 64
    rsel = lax.broadcasted_iota(jnp.int32, (LANES, 1), 0) < 64
    ys = []
    for g in range(SSD_G):
        cg = cm[:, 128 * g:128 * (g + 1)].astype(BF16)
        bg = bm[:, 128 * g:128 * (g + 1)].astype(BF16)
        sxp = sx[:, 128 * g:128 * (g + 1)]
        cb = _dot_nt(cg, bg)
        yst = _seg_sum([_dot_nt(cg, h_scr[s, g].astype(BF16)) for s in range(n_seg)], rowi, t_len)
        yatt = jnp.zeros((CH, LANES), F32)
        cols = []
        for j in range(2):
            l = L_SSD_DT + 2 * g + j
            cc = cum[:, l:l + 1]
            dec = _masked_exp(cc - cum_r[l:l + 1, :], tri)
            att = cb * dec * dt_r[l:l + 1, :]
            xm = jnp.where(lo if j == 0 else jnp.logical_not(lo), sxp, 0.0)
            yatt = yatt + _dot(att.astype(BF16), xm.astype(BF16))
            cols.append((cc, dt[:, l:l + 1] * jnp.exp(cum_end[:, l:l + 1] - cc)))
        ys.append(jnp.where(lo, jnp.exp(cols[0][0]), jnp.exp(cols[1][0])) * yst + yatt)
        xw = sxp * jnp.where(lo, cols[0][1], cols[1][1])
        for s in range(n_seg):
            r0 = s * t_len
            l = L_SSD_DT + 2 * g
            e0 = jnp.exp(cum_end[r0:r0 + 1, l:l + 1])
            e1 = jnp.exp(cum_end[r0:r0 + 1, l + 1:l + 2])
            upd = _dot_tn(_seg_rows(xw, rowi, t_len, s, n_seg).astype(BF16), bg)
            h_scr[s, g] = jnp.where(rsel, e0, e1) * h_scr[s, g] + upd
    y_all = jnp.concatenate(ys, axis=-1) + dvec_ref[...] * sx
    o_ref[...] = _rms(y_all * jax.nn.silu(z_ref[...]), ng_ref[...]).astype(BF16)

    @pl.when(c == last)
    def _():
        hn_ref[...] = h_scr[...].reshape(n_seg, SSD_H, 64, SSD_N)


def ssd_mixer(proj, row_blk0, conv0, h0, w, **grid):
    t_len = CH // grid['n_seg']
    gp = _gate_rows([(L_SSD_DT, w['ssd_a_log']), (L_SSD_DT, w['ssd_dt_bias'])])
    dvec = jnp.repeat(w['ssd_d'], BRANCH_W // SSD_H).reshape(1, BRANCH_W)
    return _mixer_call(
        _ssd_kernel, "ssd_mixer", proj, row_blk0,
        [COLS['ssd_in'], COLS['ssd_z'], (GATE_COL0, LANES)],
        [conv0, h0], [w['ssd_conv_w'], w['ssd_conv_b'].reshape(1, SSD_CONV_CH), gp, dvec,
                      w['ssd_norm'].reshape(1, BRANCH_W)], [conv0, h0],
        [pltpu.VMEM((grid['n_seg'], 8 + t_len, SSD_CONV_CH), F32), pltpu.VMEM((grid['n_seg'], 2, 128, SSD_N), F32)],
        **grid)


def _mlstm_kernel(x_ref, c0_ref, n0_ref, m0_ref, gp_ref, ng_ref, o_ref, cn_ref, nn_ref, mn_ref,
                  c_scr, n_scr, m_scr, *, n_seg):
    t_len = CH // n_seg
    c = pl.program_id(1)
    last = pl.num_programs(1) - 1

    @pl.when(c == 0)
    def _():
        c_scr[...] = c0_ref[...]
        n_scr[...] = n0_ref[...]
        m_scr[...] = m0_ref[...]

    same, tri, _ = _seg_masks(t_len)
    lane = lax.broadcasted_iota(jnp.int32, (CH, LANES), 1)
    lane256 = lax.broadcasted_iota(jnp.int32, (CH, BRANCH_W), 1)
    rowi = lax.broadcasted_iota(jnp.int32, (CH, 1), 0)
    r128 = lax.broadcasted_iota(jnp.int32, (LANES, 1), 0)
    neg = jnp.float32(-jnp.inf)

    v_all = x_ref[:, 0:256]
    q_all = x_ref[:, 512:640]
    k_all = x_ref[:, 640:768] * (ML_DK ** -0.5)
    sm = x_ref[:, 768:896]
    ig = sm + gp_ref[0:1, :]
    fl = jnp.logical_and(lane >= L_ML_F, lane < L_ML_F + ML_H)
    lf = jnp.where(fl, -_softplus(-(sm + gp_ref[1:2, :])), 0.0)
    b = _dot(tri.astype(F32), lf, precision=HI)
    b_end = _dot(same.astype(F32), lf, precision=HI)
    b_r = _row_forms(b)
    ig_r = _row_forms(ig)
    mm = m_scr[...]
    qb = q_all.astype(BF16)
    kb = k_all.astype(BF16)
    qn = _seg_sum([_dot(qb, n_scr[s].astype(BF16)) for s in range(n_seg)], rowi, t_len)
    qc = _seg_sum([_dot(qb, c_scr[s].astype(BF16)) for s in range(n_seg)], rowi, t_len)
    num_att = jnp.zeros((CH, BRANCH_W), F32)
    w_ins, dens, w_ends, a_ends, m_ends = [], [], [], [], []
    for h in range(ML_H):
        li, lf_ = L_ML_I + h, L_ML_F + h
        bc = b[:, lf_:lf_ + 1]
        bec = b_end[:, lf_:lf_ + 1]
        igc = ig[:, li:li + 1]
        mmc = mm[:, li:li + 1]
        br = b_r[lf_:lf_ + 1, :]
        igr = ig_r[li:li + 1, :]
        diff = igr - br
        cmx = jnp.max(jnp.where(tri, diff, neg), axis=-1, keepdims=True)
        smx = jnp.max(jnp.where(same, diff, neg), axis=-1, keepdims=True)
        m_c = bc + jnp.maximum(mmc, cmx)
        m_end = bec + jnp.maximum(mmc, smx)
        w_in = jnp.exp(bc + mmc - m_c)
        logw = bc - br + igr - m_c
        mq = jnp.logical_and(lane >= ML_DK * h, lane < ML_DK * (h + 1))
        qk = _dot_nt(jnp.where(mq, q_all, 0.0).astype(BF16), kb)
        wts = _masked_exp(logw, tri) * qk
        mv = jnp.logical_and(lane256 >= 64 * h, lane256 < 64 * (h + 1))
        num_att = num_att + _dot(wts.astype(BF16), jnp.where(mv, v_all, 0.0).astype(BF16))
        nq = w_in * qn[:, li:li + 1] + jnp.sum(wts, axis=-1, keepdims=True)
        w_ins.append(w_in)
        dens.append(jnp.maximum(jnp.abs(nq), jnp.exp(-m_c)))
        w_ends.append(jnp.exp(bec - bc + igc - m_end))
        a_ends.append(jnp.exp(bec + mmc - m_end))
        m_ends.append(m_end)
    num = _quarter_sel(lane256, 64, w_ins) * qc + num_att
    hout = num / _quarter_sel(lane256, 64, dens)
    outs = [_head_rmsnorm(hout[:, 128 * p:128 * (p + 1)], lane < 64) for p in range(2)]
    o_all = jnp.concatenate(outs, axis=-1) * ng_ref[...] * jax.nn.sigmoid(x_ref[:, 256:512])
    o_ref[...] = o_all.astype(BF16)

    kw = k_all * _quarter_sel(lane, ML_DK, w_ends)
    wend_tile = jnp.zeros((CH, LANES), F32)
    m_tile = jnp.zeros((CH, LANES), F32)
    for h in range(ML_H):
        wend_tile = jnp.where(lane == L_ML_I + h, w_ends[h], wend_tile)
        m_tile = jnp.where(lane == L_ML_I + h, m_ends[h], m_tile)
    m_scr[...] = m_tile
    vb = v_all.astype(BF16)
    wb = wend_tile.astype(BF16)
    rc = lax.broadcasted_iota(jnp.int32, (LANES, BRANCH_W), 0)
    lc = lax.broadcasted_iota(jnp.int32, (LANES, BRANCH_W), 1)
    bd_c = (rc // ML_DK) == (lc // 64)
    rn = lax.broadcasted_iota(jnp.int32, (LANES, LANES), 0)
    ln = lax.broadcasted_iota(jnp.int32, (LANES, LANES), 1)
    bd_n = ln == (rn // ML_DK) + L_ML_I
    for s in range(n_seg):
        r0 = s * t_len
        a_sel = _quarter_sel(r128, ML_DK, [a[r0:r0 + 1, :] for a in a_ends])
        upd_c = _dot_tn(_seg_rows(kw, rowi, t_len, s, n_seg).astype(BF16), vb)
        upd_n = _dot_tn(_seg_rows(k_all, rowi, t_len, s, n_seg).astype(BF16), wb)
        c_scr[s] = a_sel * c_scr[s] + jnp.where(bd_c, upd_c, 0.0)
        n_scr[s] = a_sel * n_scr[s] + jnp.where(bd_n, upd_n, 0.0)

    @pl.when(c == last)
    def _():
        cn_ref[...] = c_scr[...]
        nn_ref[...] = n_scr[...]
        mn_ref[...] = m_tile


def mlstm_mixer(proj, row_blk0, c0, n0, m0, w, **grid):
    t_len = CH // grid['n_seg']
    bsz = c0.shape[0]
    gp = _gate_rows([(L_ML_I, w['ml_ig_b']), (L_ML_F, w['ml_fg_b'])])
    ng = jnp.tile(w['ml_norm'], ML_H).reshape(1, BRANCH_W)
    eye = jnp.eye(ML_H, dtype=F32)
    pad = ((0, 0), (L_ML_I, LANES - L_ML_I - ML_H))
    c_bd = jnp.einsum('bhkv,hg->bhkgv', c0, eye).reshape(bsz, ML_QK, BRANCH_W)
    n_bd = jnp.pad(jnp.einsum('bhk,hg->bhkg', n0, eye).reshape(bsz, ML_QK, ML_H), ((0, 0),) + pad)
    m_exp = jnp.pad(jnp.repeat(m0, t_len, axis=0), pad)
    assert (COLS['ml_o'][0], COLS['ml_q'][0], COLS['ml_k'][0], GATE_COL0) == tuple(
        COLS['ml_v'][0] + o for o in (256, 512, 640, 768))
    o, cn, nn, mn = _mixer_call(
        _mlstm_kernel, "mlstm_mixer", proj, row_blk0, [(COLS['ml_v'][0], 896)],
        [c_bd, n_bd, m_exp], [gp, ng], [c_bd, n_bd, m_exp],
        [pltpu.VMEM((grid['n_seg'], ML_QK, BRANCH_W), F32), pltpu.VMEM((grid['n_seg'], ML_QK, LANES), F32),
         pltpu.VMEM((CH, LANES), F32)], **grid)
    c_new = jnp.einsum('bhkgv,hg->bhkv', cn.reshape(bsz, ML_H, ML_DK, ML_H, 64), eye)
    n_new = jnp.einsum('bhkg,hg->bhk', nn[:, :, L_ML_I:L_ML_I + ML_H].reshape(bsz, ML_H, ML_DK, ML_H), eye)
    m_new = mn[::t_len, L_ML_I:L_ML_I + ML_H]
    return o, c_new, n_new, m_new


def _merge_kernel(h_ref, gates_ref, br_ref, wb_ref, wo_ref, o_ref):
    merged = None
    for n in range(N_BRANCH):
        y = _dot(br_ref[:, n * BRANCH_W:(n + 1) * BRANCH_W], wb_ref[n])
        t = jax.nn.sigmoid(gates_ref[:, n * D_MODEL:(n + 1) * D_MODEL]) * y
        merged = t if merged is None else merged + t
    o_ref[...] = h_ref[...] + _dot(merged.astype(BF16), wo_ref[...])


def merge(h, proj, br, w_branch, w_out, tm):
    m, d = h.shape
    assert COLS['gates'] == (0, N_BRANCH * D_MODEL)
    return pl.pallas_call(
        _merge_kernel,
        grid=(m // tm,),
        in_specs=[pl.BlockSpec((tm, d), lambda i: (i, 0)),
                  pl.BlockSpec((tm, N_BRANCH * d), lambda i: (i, 0)),
                  pl.BlockSpec((tm, N_BRANCH * BRANCH_W), lambda i: (i, 0)),
                  pl.BlockSpec((N_BRANCH, BRANCH_W, d), lambda i: (0, 0, 0)),
                  pl.BlockSpec((d, d), lambda i: (0, 0))],
        out_specs=pl.BlockSpec((tm, d), lambda i: (i, 0)),
        out_shape=jax.ShapeDtypeStruct((m, d), F32),
        compiler_params=pltpu.CompilerParams(
            dimension_semantics=("parallel",), vmem_limit_bytes=VMEM_LIMIT),
        name="merge",
    )(h, proj, br, w_branch, w_out)


def _ffn_kernel(h_ref, g_ref, wg_ref, wu_ref, wd_ref, o_ref, u_ref):
    f = pl.program_id(1)

    @pl.when(f == 0)
    def _():
        u_ref[...] = _rms(h_ref[...], g_ref[...]).astype(BF16)

    u = u_ref[...]
    a = jax.nn.silu(_dot(u, wg_ref[...]))
    b = _dot(u, wu_ref[...])
    y = _dot((a * b).astype(BF16), wd_ref[...])

    @pl.when(f == 0)
    def _():
        o_ref[...] = h_ref[...] + y

    @pl.when(f != 0)
    def _():
        o_ref[...] += y


def ffn(h, g, wg, wu, wd, tm, tf):
    m, d = h.shape
    ff = wg.shape[1]
    return pl.pallas_call(
        _ffn_kernel,
        grid=(m // tm, ff // tf),
        in_specs=[pl.BlockSpec((tm, d), lambda i, f: (i, 0)),
                  pl.BlockSpec((1, d), lambda i, f: (0, 0)),
                  pl.BlockSpec((d, tf), lambda i, f: (0, f)),
                  pl.BlockSpec((d, tf), lambda i, f: (0, f)),
                  pl.BlockSpec((tf, d), lambda i, f: (f, 0))],
        out_specs=pl.BlockSpec((tm, d), lambda i, f: (i, 0)),
        out_shape=jax.ShapeDtypeStruct((m, d), F32),
        scratch_shapes=[pltpu.VMEM((tm, d), BF16)],
        compiler_params=pltpu.CompilerParams(
            dimension_semantics=("parallel", "arbitrary"), vmem_limit_bytes=VMEM_LIMIT),
        name="ffn",
    )(h, g.reshape(1, d), wg, wu, wd)


def _router_kernel(h_ref, g_ref, wr_ref, u_ref, w_ref, i_ref):
    u = _rms(h_ref[...], g_ref[...])
    u_ref[...] = u.astype(BF16)
    logits = _dot(u, wr_ref[...], precision=HI)
    lane = lax.broadcasted_iota(jnp.int32, logits.shape, 1)
    neg = jnp.float32(-jnp.inf)
    logits = jnp.where(lane < N_EXPERTS, logits, neg)
    m1 = jnp.max(logits, axis=-1, keepdims=True)
    i1 = jnp.min(jnp.where(logits == m1, lane, LANES), axis=-1, keepdims=True)
    rest = jnp.where(lane == i1, neg, logits)
    m2 = jnp.max(rest, axis=-1, keepdims=True)
    i2 = jnp.min(jnp.where(rest == m2, lane, LANES), axis=-1, keepdims=True)
    e = jnp.exp(m2 - m1)
    den = 1.0 + e
    w_ref[...] = jnp.where(lane == 0, 1.0 / den, jnp.where(lane == 1, e / den, 0.0))
    i_ref[...] = jnp.where(lane == 0, i1, jnp.where(lane == 1, i2, 0))


def router(h, g, w_router, tm):
    m, d = h.shape
    wr = jnp.pad(w_router, ((0, 0), (0, LANES - N_EXPERTS)))
    return pl.pallas_call(
        _router_kernel,
        grid=(m // tm,),
        in_specs=[pl.BlockSpec((tm, d), lambda i: (i, 0)),
                  pl.BlockSpec((1, d), lambda i: (0, 0)),
                  pl.BlockSpec((d, LANES), lambda i: (0, 0))],
        out_specs=[pl.BlockSpec((tm, d), lambda i: (i, 0)),
                   pl.BlockSpec((tm, LANES), lambda i: (i, 0)),
                   pl.BlockSpec((tm, LANES), lambda i: (i, 0))],
        out_shape=[jax.ShapeDtypeStruct((m, d), BF16),
                   jax.ShapeDtypeStruct((m, LANES), F32),
                   jax.ShapeDtypeStruct((m, LANES), jnp.int32)],
        compiler_params=pltpu.CompilerParams(
            dimension_semantics=("parallel",), vmem_limit_bytes=VMEM_LIMIT),
        name="router",
    )(h, g.reshape(1, d), wr)


def _expert_kernel(te_ref, nt_ref, x_ref, wg_ref, wu_ref, wd_ref, o_ref):
    i = pl.program_id(0)
    f = pl.program_id(1)

    @pl.when(i < nt_ref[0])
    def _():
        x = x_ref[...]
        a = jax.nn.silu(_dot(x, wg_ref[0]))
        b = _dot(x, wu_ref[0])
        y = _dot((a * b).astype(BF16), wd_ref[0])

        @pl.when(f == 0)
        def _():
            o_ref[...] = y

        @pl.when(f != 0)
        def _():
            o_ref[...] += y

    @pl.when(jnp.logical_and(i >= nt_ref[0], f == 0))
    def _():
        o_ref[...] = jnp.zeros_like(o_ref)


def experts(xs, tile_expert, n_tiles, wg, wu, wd, tm, tf):
    r, d = xs.shape
    ff = wg.shape[2]
    grid_spec = pltpu.PrefetchScalarGridSpec(
        num_scalar_prefetch=2,
        grid=(r // tm, ff // tf),
        in_specs=[pl.BlockSpec((tm, d), lambda i, f, te, nt: (i, 0)),
                  pl.BlockSpec((1, d, tf), lambda i, f, te, nt: (te[i], 0, f)),
                  pl.BlockSpec((1, d, tf), lambda i, f, te, nt: (te[i], 0, f)),
                  pl.BlockSpec((1, tf, d), lambda i, f, te, nt: (te[i], f, 0))],
        out_specs=pl.BlockSpec((tm, d), lambda i, f, te, nt: (i, 0)),
    )
    return pl.pallas_call(
        _expert_kernel,
        grid_spec=grid_spec,
        out_shape=jax.ShapeDtypeStruct((r, d), F32),
        compiler_params=pltpu.CompilerParams(
            dimension_semantics=("arbitrary", "arbitrary"), vmem_limit_bytes=VMEM_LIMIT),
        name="experts",
    )(tile_expert, n_tiles, xs, wg, wu, wd)


def moe(h, g, w_router, wg, wu, wd, tm_route, tm_e, tf):
    m, d = h.shape
    u, top_w, top_i = router(h, g, w_router, tm_route)
    top_w = top_w[:, :TOP_K]
    top_i = top_i[:, :TOP_K]
    flat_e = top_i.reshape(-1)
    onehot = (flat_e[:, None] == jnp.arange(N_EXPERTS, dtype=jnp.int32)[None, :]).astype(jnp.int32)
    rank = jnp.sum((jnp.cumsum(onehot, axis=0) - 1) * onehot, axis=1)
    counts = jnp.sum(onehot, axis=0)
    tiles_per = (counts + tm_e - 1) // tm_e
    tile_end = jnp.cumsum(tiles_per)
    tile_start = tile_end - tiles_per
    dest = tile_start[flat_e] * tm_e + rank
    n_rows = TOP_K * m + N_EXPERTS * tm_e
    n_row_tiles = n_rows // tm_e
    src_tok = jnp.zeros((n_rows,), jnp.int32).at[dest].set(jnp.arange(TOP_K * m, dtype=jnp.int32) // TOP_K)
    tile_ids = jnp.arange(n_row_tiles, dtype=jnp.int32)
    tile_expert = jnp.minimum(jnp.sum((tile_ids[:, None] >= tile_end[None, :]).astype(jnp.int32), axis=1),
                              N_EXPERTS - 1).astype(jnp.int32)
    n_tiles = tile_end[-1:].astype(jnp.int32)
    last_e = tile_expert[jnp.maximum(n_tiles[0] - 1, 0)]
    tile_expert = jnp.where(tile_ids < n_tiles[0], tile_expert, last_e)
    xs = jnp.take(u, src_tok, axis=0)
    ys = experts(xs, tile_expert, n_tiles, wg, wu, wd, tm_e, tf)
    picked = jnp.take(ys, dest, axis=0).reshape(m, TOP_K, d)
    return h + jnp.sum(top_w[:, :, None] * picked, axis=1)


def _ple_kernel(h_ref, p_ref, g_ref, wg_ref, wp_ref, gf_ref, o_ref, *, final):
    h = h_ref[...]
    v = _rms(h, g_ref[...]).astype(BF16)
    pg = jax.nn.sigmoid(_dot(v, wg_ref[...]))
    e = _dot(p_ref[...].astype(BF16), wp_ref[...])
    out = h + pg * e
    if final:
        out = _rms(out, gf_ref[...])
    o_ref[...] = out


def ple(h, p, g, w_gate, w_p, g_final, tm, final):
    m, d = h.shape
    dp = p.shape[1]
    return pl.pallas_call(
        functools.partial(_ple_kernel, final=final),
        grid=(m // tm,),
        in_specs=[pl.BlockSpec((tm, d), lambda i: (i, 0)),
                  pl.BlockSpec((tm, dp), lambda i: (i, 0)),
                  pl.BlockSpec((1, d), lambda i: (0, 0)),
                  pl.BlockSpec((d, d), lambda i: (0, 0)),
                  pl.BlockSpec((dp, d), lambda i: (0, 0)),
                  pl.BlockSpec((1, d), lambda i: (0, 0))],
        out_specs=pl.BlockSpec((tm, d), lambda i: (i, 0)),
        out_shape=jax.ShapeDtypeStruct((m, d), F32),
        compiler_params=pltpu.CompilerParams(
            dimension_semantics=("parallel",), vmem_limit_bytes=VMEM_LIMIT),
        name="ple",
    )(h, p, g.reshape(1, d), w_gate, w_p, g_final.reshape(1, d))


STATE_KEYS = ('gdn_conv', 'gdn', 'hgrn', 'ssd_conv', 'ssd', 'ml_c', 'ml_n', 'ml_m')
MIXER_KEYS = ('gdn_conv_w', 'gdn_a_log', 'gdn_dt_bias', 'gdn_norm', 'hgrn_norm',
              'ssd_conv_w', 'ssd_conv_b', 'ssd_a_log', 'ssd_dt_bias', 'ssd_d', 'ssd_norm',
              'ml_ig_b', 'ml_fg_b', 'ml_norm')


def _mixers(proj, row_blk0, st, w, lb, **grid):
    o_a, gdn_conv, s_gdn = gdn_mixer(proj, row_blk0, st['gdn_conv'], st['gdn'], w, **grid)
    o_b, s_hg = hgrn_mixer(proj, row_blk0, st['hgrn'], w, lb, **grid)
    o_c, ssd_conv, h_ssd = ssd_mixer(proj, row_blk0, st['ssd_conv'], st['ssd'], w, **grid)
    o_d, c_ml, n_ml, m_ml = mlstm_mixer(proj, row_blk0, st['ml_c'], st['ml_n'], st['ml_m'], w, **grid)
    new = {'gdn_conv': gdn_conv, 'gdn': s_gdn, 'hgrn': s_hg, 'ssd_conv': ssd_conv, 'ssd': h_ssd,
           'ml_c': c_ml, 'ml_n': n_ml, 'ml_m': m_ml}
    return jnp.concatenate([o_a, o_b, o_c, o_d], axis=-1), new


def kernel(x_prompt, x_sample, state_gdn_conv, state_gdn, state_hgrn, state_ssd_conv, state_ssd, state_mlstm_c, state_mlstm_n, state_mlstm_m, p_prompt, p_sample, g_mix, w_in, gdn_conv_w, gdn_a_log, gdn_dt_bias, gdn_norm, hgrn_lb, hgrn_norm, ssd_conv_w, ssd_conv_b, ssd_a_log, ssd_dt_bias, ssd_d, ssd_norm, ml_ig_b, ml_fg_b, ml_norm, w_branch, w_out, g_ffn, w_ff_gate, w_ff_up, w_ff_down, w_router, w_ex_gate, w_ex_up, w_ex_down, w_ple, w_ple_gate, g_ple, g_final):
    prm = {'gdn_conv_w': gdn_conv_w, 'gdn_a_log': gdn_a_log, 'gdn_dt_bias': gdn_dt_bias,
           'gdn_norm': gdn_norm, 'hgrn_norm': hgrn_norm, 'ssd_conv_w': ssd_conv_w,
           'ssd_conv_b': ssd_conv_b, 'ssd_a_log': ssd_a_log, 'ssd_dt_bias': ssd_dt_bias,
           'ssd_d': ssd_d, 'ssd_norm': ssd_norm, 'ml_ig_b': ml_ig_b, 'ml_fg_b': ml_fg_b,
           'ml_norm': ml_norm}
    bp, lp, d = x_prompt.shape
    bs, ls, _ = x_sample.shape
    mp, ms = bp * lp, bs * ls
    assert lp % CH == 0 and CH % ls == 0 and ms % CH == 0 and mp % CH == 0
    st_s = {'gdn_conv': state_gdn_conv, 'gdn': state_gdn, 'hgrn': state_hgrn, 'ssd_conv': state_ssd_conv,
            'ssd': state_ssd, 'ml_c': state_mlstm_c, 'ml_n': state_mlstm_n, 'ml_m': state_mlstm_m}
    st_p = {k: jnp.zeros((bp,) + v.shape[2:], F32) for k, v in st_s.items()}
    grid_p = dict(n_outer=bp, n_chunks=lp // CH, n_seg=1)
    grid_s = dict(n_outer=ms // CH, n_chunks=1, n_seg=CH // ls)

    sm = jax.nn.softmax(hgrn_lb, axis=0)
    lb_all = jnp.cumsum(sm, axis=0) - sm[0]

    h = jnp.concatenate([x_prompt.reshape(mp, d), x_sample.reshape(ms, d)], axis=0)
    new_p = {k: [] for k in STATE_KEYS}
    new_s = {k: [] for k in STATE_KEYS}
    for l in range(DEPTH):
        wl = {k: prm[k][l] for k in MIXER_KEYS}
        proj = norm_matmul(h, g_mix[l], _permute_w_in(w_in[l]), tm=1024, tn=1152)
        br_p, np_ = _mixers(proj, 0, st_p, wl, lb_all[l], **grid_p)
        br_s, ns_ = _mixers(proj, mp // CH, {k: v[l] for k, v in st_s.items()}, wl, lb_all[l], **grid_s)
        br = jnp.concatenate([br_p, br_s], axis=0)
        h = merge(h, proj, br, w_branch[l].astype(BF16), w_out[l].astype(BF16), tm=512)
        j = l // 2
        if l % 2 == 0:
            h = ffn(h, g_ffn[l], w_ff_gate[j].astype(BF16), w_ff_up[j].astype(BF16),
                    w_ff_down[j].astype(BF16), tm=512, tf=D_FF // 2)
        else:
            h = moe(h, g_ffn[l], w_router[j], w_ex_gate[j].astype(BF16), w_ex_up[j].astype(BF16),
                    w_ex_down[j].astype(BF16), tm_route=512, tm_e=512, tf=D_FF // 2)
        p = jnp.concatenate([p_prompt[l].reshape(mp, D_PLE), p_sample[l].reshape(ms, D_PLE)], axis=0)
        h = ple(h, p, g_ple[l], w_ple_gate[l].astype(BF16), w_ple[l].astype(BF16), g_final,
                tm=512, final=(l == DEPTH - 1))
        for k in STATE_KEYS:
            new_p[k].append(np_[k])
            new_s[k].append(ns_[k])
    y_prompt = h[:mp].reshape(bp, lp, d)
    y_sample = h[mp:].reshape(bs, ls, d)
    sp = {k: jnp.stack(v) for k, v in new_p.items()}
    ss = {k: jnp.stack(v) for k, v in new_s.items()}
    return (y_prompt, y_sample,
            sp['gdn_conv'], sp['gdn'], sp['hgrn'], sp['ssd_conv'], sp['ssd'], sp['ml_c'], sp['ml_n'], sp['ml_m'],
            ss['gdn_conv'], ss['gdn'], ss['hgrn'], ss['ssd_conv'], ss['ssd'], ss['ml_c'], ss['ml_n'], ss['ml_m'])
```

```python
import functools

import numpy as np
import jax
import jax.numpy as jnp
from jax import lax
from jax.experimental import pallas as pl
from jax.experimental.pallas import tpu as pltpu

F32 = jnp.float32
BF16 = jnp.bfloat16
HI = lax.Precision.HIGHEST

D_MODEL = 1024
DEPTH = 2
D_PLE = 256
N_BRANCH = 4
BRANCH_W = D_MODEL // N_BRANCH
CONV_W = 4
EPS = 1e-6

GDN_H = 4
GDN_DK = 64
GDN_QK = GDN_H * GDN_DK
GDN_CONV_CH = 2 * GDN_QK + BRANCH_W
HG_H = 4
HG_QK = 256
SSD_H = 4
SSD_G = 2
SSD_N = 128
SSD_CONV_CH = BRANCH_W + 2 * SSD_G * SSD_N
ML_H = 4
ML_DK = 32
ML_QK = ML_H * ML_DK
D_FF = ((8 * D_MODEL // 3 + 255) // 256) * 256
N_EXPERTS = 8
TOP_K = 2

_REF_SPLITS = (('gdn_in', GDN_CONV_CH), ('gdn_b', GDN_H), ('gdn_a', GDN_H), ('gdn_z', BRANCH_W),
               ('hg_q', HG_QK), ('hg_f', HG_QK), ('hg_v', BRANCH_W), ('hg_g', BRANCH_W),
               ('ssd_z', BRANCH_W), ('ssd_in', SSD_CONV_CH), ('ssd_dt', SSD_H),
               ('ml_q', ML_QK), ('ml_k', ML_QK), ('ml_v', BRANCH_W), ('ml_i', ML_H), ('ml_f', ML_H),
               ('ml_o', BRANCH_W), ('gates', N_BRANCH * D_MODEL))
_MY_ORDER = ('gates', 'gdn_z', 'ssd_z', 'gdn_in', 'ssd_in', 'hg_q', 'hg_f', 'hg_v', 'hg_g',
             'ml_v', 'ml_o', 'ml_q', 'ml_k', 'gdn_b', 'gdn_a', 'ssd_dt', 'ml_i', 'ml_f')
LANES = 128
CH = 64


def _layout():
    widths = dict(_REF_SPLITS)
    off, out = 0, {}
    for name in _MY_ORDER:
        out[name] = (off, widths[name])
        off += widths[name]
    return out, -(-off // LANES) * LANES


COLS, N_PROJ = _layout()
GATE_COL0 = COLS['gdn_b'][0]
L_GDN_B, L_GDN_A, L_SSD_DT, L_ML_I, L_ML_F = (COLS[n][0] - GATE_COL0 for n in ('gdn_b', 'gdn_a', 'ssd_dt', 'ml_i', 'ml_f'))
VMEM_LIMIT = 56 * 1024 * 1024


def _permute_w_in(w):
    ref_off, acc = {}, 0
    for name, wd in _REF_SPLITS:
        ref_off[name] = acc
        acc += wd
    pieces = [w[:, ref_off[n]:ref_off[n] + COLS[n][1]] for n in _MY_ORDER]
    used = sum(COLS[n][1] for n in _MY_ORDER)
    pieces.append(jnp.zeros((w.shape[0], N_PROJ - used), w.dtype))
    return jnp.concatenate(pieces, axis=1).astype(BF16)


def _rms(x, g):
    return x * lax.rsqrt(jnp.mean(x * x, axis=-1, keepdims=True) + EPS) * g


def _dot(a, b, **kw):
    return jnp.dot(a, b, preferred_element_type=F32, **kw)


def _dot_nt(a, b, **kw):
    return lax.dot_general(a, b, (((1,), (1,)), ((), ())), preferred_element_type=F32, **kw)


def _dot_tn(a, b, **kw):
    return lax.dot_general(a, b, (((0,), (0,)), ((), ())), preferred_element_type=F32, **kw)


def _norm_matmul_kernel(x_ref, g_ref, w_ref, o_ref, xn_ref):
    @pl.when(pl.program_id(1) == 0)
    def _():
        xn_ref[...] = _rms(x_ref[...], g_ref[...]).astype(BF16)

    o_ref[...] = _dot(xn_ref[...], w_ref[...])


def norm_matmul(x, g, w, tm, tn):
    m, d = x.shape
    n = w.shape[1]
    return pl.pallas_call(
        _norm_matmul_kernel,
        grid=(m // tm, n // tn),
        in_specs=[pl.BlockSpec((tm, d), lambda i, j: (i, 0)),
                  pl.BlockSpec((1, d), lambda i, j: (0, 0)),
                  pl.BlockSpec((d, tn), lambda i, j: (0, j))],
        out_specs=pl.BlockSpec((tm, tn), lambda i, j: (i, j)),
        out_shape=jax.ShapeDtypeStruct((m, n), F32),
        scratch_shapes=[pltpu.VMEM((tm, d), BF16)],
        compiler_params=pltpu.CompilerParams(
            dimension_semantics=("parallel", "arbitrary"), vmem_limit_bytes=VMEM_LIMIT),
        name="norm_matmul",
    )(x, g.reshape(1, d), w)


def _seg_masks(t_len):
    row = lax.broadcasted_iota(jnp.int32, (CH, CH), 0)
    col = lax.broadcasted_iota(jnp.int32, (CH, CH), 1)
    same = (row // t_len) == (col // t_len)
    tri = jnp.logical_and(same, col <= row)
    strict = jnp.logical_and(same, col < row)
    return same, tri, strict


def _row_forms(x, n_rows=24):
    r = lax.broadcasted_iota(jnp.int32, (n_rows, LANES), 0)
    l = lax.broadcasted_iota(jnp.int32, (n_rows, LANES), 1)
    return _dot_nt((r == l).astype(F32), x, precision=HI)


def _softplus(x):
    return jnp.maximum(x, 0.0) + jnp.log1p(jnp.exp(-jnp.abs(x)))


def _split2(x):
    hi = x.astype(BF16)
    return hi, (x - hi.astype(F32)).astype(BF16)


def _dot3(a, b):
    return _dot(a[0], b[0]) + (_dot(a[0], b[1]) + _dot(a[1], b[0]))


def _masked_exp(d, mask):
    return jnp.where(mask, jnp.exp(jnp.where(mask, d, 0.0)), 0.0)


def _conv_silu(x, ext_scr, cw_ref, bias, t_len, n_seg):
    w = x.shape[-1]
    ext_scr[:, 8:8 + t_len, :] = x.reshape(n_seg, t_len, w)
    y = cw_ref[3:4, :] * x
    for j in range(1, CONV_W):
        y = y + cw_ref[3 - j:4 - j, :] * ext_scr[:, 8 - j:8 - j + t_len, :].reshape(CH, w)
    if bias is not None:
        y = y + bias
    tail = ext_scr[:, 5 + t_len:8 + t_len, :]
    ext_scr[:, 5:8, :] = tail
    return jax.nn.silu(y), tail


def _halves(xp, lo):
    s_lo = jnp.sum(jnp.where(lo, xp, 0.0), axis=-1, keepdims=True)
    s_hi = jnp.sum(jnp.where(lo, 0.0, xp), axis=-1, keepdims=True)
    return jnp.where(lo, s_lo, s_hi)


def _head_rmsnorm(xp, lo):
    return xp * lax.rsqrt(_halves(xp * xp, lo) * (1.0 / 64) + EPS)


def _head_l2norm(xp, lo):
    return xp * lax.rsqrt(_halves(xp * xp, lo) + EPS)


def _seg_sum(parts, rowi, t_len):
    if len(parts) == 1:
        return parts[0]
    acc = jnp.where(rowi // t_len == 0, parts[0], 0.0)
    for s in range(1, len(parts)):
        acc = acc + jnp.where(rowi // t_len == s, parts[s], 0.0)
    return acc


def _seg_rows(x, rowi, t_len, s, n_seg):
    return x if n_seg == 1 else jnp.where(rowi // t_len == s, x, 0.0)


def _quarter_sel(idx, width, vals):
    out = vals[3]
    for h in (2, 1, 0):
        out = jnp.where(idx < (h + 1) * width, vals[h], out)
    return out


def _gate_rows(pairs):
    t = jnp.zeros((8, LANES), F32)
    for r, (off, v) in enumerate(pairs):
        t = t.at[r, off:off + v.shape[0]].set(v.astype(F32))
    return t


def _to_bd(s):
    b, _, a, c = s.shape
    s = s.reshape(b, 2, 2, a, c)
    z = jnp.zeros_like(s[:, :, 0])
    top = jnp.concatenate([s[:, :, 0], z], axis=-1)
    bot = jnp.concatenate([z, s[:, :, 1]], axis=-1)
    return jnp.concatenate([top, bot], axis=-2)


def _from_bd(s):
    b, _, a2, c2 = s.shape
    a, c = a2 // 2, c2 // 2
    return jnp.stack([s[:, :, :a, :c], s[:, :, a:, c:]], axis=2).reshape(b, 4, a, c)


def _mixer_call(kernel_fn, name, proj, row_blk0, in_blocks, state_ins, params, state_outs, scratch,
                *, n_outer, n_chunks, n_seg):
    rows = n_outer * n_chunks * CH
    rmap = lambda blk: (lambda i, c: (row_blk0 + i * n_chunks + c, blk))
    full = lambda a: pl.BlockSpec(a.shape, lambda i, c: (0,) * a.ndim)
    sblk = lambda a: pl.BlockSpec((a.shape[0] // n_outer,) + a.shape[1:], lambda i, c: (i,) + (0,) * (a.ndim - 1))
    for off, wd in in_blocks:
        assert off % wd == 0
    in_specs = ([pl.BlockSpec((CH, wd), rmap(off // wd)) for off, wd in in_blocks]
                + [sblk(a) for a in state_ins] + [full(a) for a in params])
    out_specs = ([pl.BlockSpec((CH, BRANCH_W), lambda i, c: (i * n_chunks + c, 0))]
                 + [sblk(a) for a in state_outs])
    out_shape = ([jax.ShapeDtypeStruct((rows, BRANCH_W), BF16)]
                 + [jax.ShapeDtypeStruct(a.shape, F32) for a in state_outs])
    return pl.pallas_call(
        functools.partial(kernel_fn, n_seg=n_seg),
        grid=(n_outer, n_chunks),
        in_specs=in_specs, out_specs=out_specs, out_shape=out_shape,
        scratch_shapes=scratch,
        compiler_params=pltpu.CompilerParams(
            dimension_semantics=("parallel", "arbitrary"), vmem_limit_bytes=VMEM_LIMIT),
        name=name,
    )(*([proj] * len(in_blocks)), *state_ins, *params)


def _gdn_kernel(xin_ref, z_ref, sm_ref, conv0_ref, s0_ref, cw_ref, gp_ref, ng_ref,
                o_ref, convn_ref, sn_ref, ext_scr, s_scr, *, n_seg):
    t_len = CH // n_seg
    c = pl.program_id(1)
    last = pl.num_programs(1) - 1

    @pl.when(c == 0)
    def _():
        s_scr[...] = s0_ref[...]
        ext_scr[:, 5:8, :] = conv0_ref[...]

    xc, tail = _conv_silu(xin_ref[...], ext_scr, cw_ref, None, t_len, n_seg)

    @pl.when(c == last)
    def _():
        convn_ref[...] = tail

    same, tri, strict = _seg_masks(t_len)
    lane = lax.broadcasted_iota(jnp.int32, (CH, LANES), 1)
    rowi = lax.broadcasted_iota(jnp.int32, (CH, 1), 0)
    lo = lane < 64
    sm = sm_ref[...]
    beta = jax.nn.sigmoid(sm)
    gl = jnp.logical_and(lane >= L_GDN_A, lane < L_GDN_A + GDN_H)
    g = jnp.where(gl, -jnp.exp(gp_ref[0:1, :]) * _softplus(sm + gp_ref[1:2, :]), 0.0)
    gam = _dot(tri.astype(F32), g, precision=HI)
    gam_end = _dot(same.astype(F32), g, precision=HI)
    gam_r = _row_forms(gam, 8)
    r128 = lax.broadcasted_iota(jnp.int32, (LANES, LANES), 0)
    l128 = lax.broadcasted_iota(jnp.int32, (LANES, LANES), 1)
    bd = (r128 < 64) == (l128 < 64)
    rsel = lax.broadcasted_iota(jnp.int32, (LANES, 1), 0) < 64
    qs_, ks_, atts_, gcs_, a_, x_ = [], [], [], [], [], []
    for p in range(2):
        q_p = _head_l2norm(xc[:, 128 * p:128 * (p + 1)], lo) * (GDN_DK ** -0.5)
        k_p = _head_l2norm(xc[:, GDN_QK + 128 * p:GDN_QK + 128 * (p + 1)], lo)
        v_p = xc[:, 2 * GDN_QK + 128 * p:2 * GDN_QK + 128 * (p + 1)]
        kb = k_p.astype(BF16)
        qs_.append(q_p)
        ks_.append(k_p)
        for j in range(2):
            h = 2 * p + j
            mj = lo if j == 0 else jnp.logical_not(lo)
            kk = _dot_nt(jnp.where(mj, k_p, 0.0).astype(BF16), kb)
            qk = _dot_nt(jnp.where(mj, q_p, 0.0).astype(BF16), kb)
            gc = gam[:, L_GDN_A + h:L_GDN_A + h + 1]
            bc = beta[:, L_GDN_B + h:L_GDN_B + h + 1]
            dec = _masked_exp(gc - gam_r[L_GDN_A + h:L_GDN_A + h + 1, :], tri)
            a_.append(jnp.where(strict, bc * kk * dec, 0.0))
            x_.append(jnp.concatenate([jnp.where(mj, bc * v_p, 0.0),
                                       jnp.where(mj, (bc * jnp.exp(gc)) * k_p, 0.0)], axis=-1))
            atts_.append(qk * dec)
            gcs_.append(gc)
    sa = [_split2(a) for a in a_]
    x_ = [x - _dot3(s, _split2(x)) for s, x in zip(sa, x_)]
    n = 2
    while n < t_len:
        sa = [_split2(_dot3(s, s)) for s in sa]
        x_ = [x + _dot3(s, _split2(x)) for s, x in zip(sa, x_)]
        n *= 2
    outs = []
    for p in range(2):
        q_p, k_p = qs_[p], ks_[p]
        solv = x_[2 * p][:, :LANES] + x_[2 * p + 1][:, :LANES]
        solk = x_[2 * p][:, LANES:] + x_[2 * p + 1][:, LANES:]
        atts, gcs = atts_[2 * p:2 * p + 2], gcs_[2 * p:2 * p + 2]
        solk_b = solk.astype(BF16)
        qb = q_p.astype(BF16)
        u = solv - _seg_sum([_dot(solk_b, s_scr[s, p].astype(BF16)) for s in range(n_seg)], rowi, t_len)
        qs = _seg_sum([_dot(qb, s_scr[s, p].astype(BF16)) for s in range(n_seg)], rowi, t_len)
        o = jnp.where(lo, jnp.exp(gcs[0]), jnp.exp(gcs[1])) * qs
        for j in range(2):
            mj = lo if j == 0 else jnp.logical_not(lo)
            o = o + _dot(atts[j].astype(BF16), jnp.where(mj, u, 0.0).astype(BF16))
        ge0 = gam_end[:, L_GDN_A + 2 * p:L_GDN_A + 2 * p + 1]
        ge1 = gam_end[:, L_GDN_A + 2 * p + 1:L_GDN_A + 2 * p + 2]
        kw = k_p * jnp.where(lo, jnp.exp(ge0 - gcs[0]), jnp.exp(ge1 - gcs[1]))
        ub = u.astype(BF16)
        for s in range(n_seg):
            r0 = s * t_len
            dec_s = jnp.where(rsel, jnp.exp(ge0[r0:r0 + 1, :]), jnp.exp(ge1[r0:r0 + 1, :]))
            upd = _dot_tn(_seg_rows(kw, rowi, t_len, s, n_seg).astype(BF16), ub)
            s_scr[s, p] = dec_s * s_scr[s, p] + jnp.where(bd, upd, 0.0)
        outs.append(_head_rmsnorm(o, lo))
    o_all = jnp.concatenate(outs, axis=-1) * ng_ref[...] * jax.nn.silu(z_ref[...])
    o_ref[...] = o_all.astype(BF16)

    @pl.when(c == last)
    def _():
        sn_ref[...] = s_scr[...]


def gdn_mixer(proj, row_blk0, conv0, s0, w, **grid):
    t_len = CH // grid['n_seg']
    gp = _gate_rows([(L_GDN_A, w['gdn_a_log']), (L_GDN_A, w['gdn_dt_bias'])])
    ng = jnp.tile(w['gdn_norm'], GDN_H).reshape(1, BRANCH_W)
    s0bd = _to_bd(s0)
    o, convn, sn = _mixer_call(
        _gdn_kernel, "gdn_mixer", proj, row_blk0,
        [COLS['gdn_in'], COLS['gdn_z'], (GATE_COL0, LANES)],
        [conv0, s0bd], [w['gdn_conv_w'], gp, ng], [conv0, s0bd],
        [pltpu.VMEM((grid['n_seg'], 8 + t_len, GDN_CONV_CH), F32), pltpu.VMEM((grid['n_seg'], 2, 128, 128), F32)],
        **grid)
    return o, convn, _from_bd(sn)


def _hgrn_levels(t_len):
    lv, n = [], t_len
    while n >= 2:
        lv.append(n)
        n //= 2
    return lv


def _hgrn_cmat(t_len):
    t = np.arange(CH)[:, None]
    j = np.arange(CH)[None, :]
    same = (t // t_len) == (j // t_len)
    mats = [same & (j <= t), same]
    for n in _hgrn_levels(t_len):
        mid = (t // n) * n + n // 2
        mats.append((t % n >= n // 2) & (j >= mid) & (j <= t))
        mats.append((t % n < n // 2) & (j > t) & (j <= mid - 1))
    return jnp.asarray(np.concatenate(mats, axis=0).astype(np.float32), dtype=BF16)


def _split3(x):
    hi = x.astype(BF16)
    r = x - hi.astype(F32)
    mid = r.astype(BF16)
    return hi, mid, (r - mid.astype(F32)).astype(BF16)


def _hgrn_kernel(x_ref, s0_ref, cm_ref, lb_ref, ng_ref, o_ref, sn_ref, s_scr, *, n_seg):
    t_len = CH // n_seg
    levels = _hgrn_levels(t_len)
    c = pl.program_id(1)
    last = pl.num_programs(1) - 1

    @pl.when(c == 0)
    def _():
        s_scr[...] = s0_ref[...]

    lane = lax.broadcasted_iota(jnp.int32, (CH, LANES), 1)
    rowi = lax.broadcasted_iota(jnp.int32, (CH, 1), 0)
    row = lax.broadcasted_iota(jnp.int32, (CH, CH), 0)
    col = lax.broadcasted_iota(jnp.int32, (CH, CH), 1)
    lo = lane < 64
    r128 = lax.broadcasted_iota(jnp.int32, (LANES, LANES), 0)
    l128 = lax.broadcasted_iota(jnp.int32, (LANES, LANES), 1)
    bd = (r128 < 64) == (l128 < 64)

    lb = lb_ref[...]
    f_pre = x_ref[:, HG_QK:2 * HG_QK]
    log_f = jnp.log(lb + (1.0 - lb) * jax.nn.sigmoid(f_pre))
    k_in = (1.0 - lb) * jax.nn.sigmoid(-f_pre)
    cm = cm_ref[...]
    ex = None
    for part in _split3(log_f):
        t = _dot(cm, part)
        ex = t if ex is None else ex + t
    b = ex[0:CH]
    b_end = ex[CH:2 * CH]
    outs = []
    for p in range(2):
        ls = slice(128 * p, 128 * (p + 1))
        q_p = x_ref[:, ls]
        k_p = k_in[:, ls]
        v_p = x_ref[:, 2 * HG_QK + 128 * p:2 * HG_QK + 128 * (p + 1)]
        qk = q_p * k_p
        qe = (q_p * jnp.exp(b[:, ls])).astype(BF16)
        o = _seg_sum([_dot_nt(qe, s_scr[s, p].astype(BF16)) for s in range(n_seg)], rowi, t_len)
        for j in range(2):
            mj = lo if j == 0 else jnp.logical_not(lo)
            diag = jnp.sum(jnp.where(mj, qk, 0.0), axis=-1, keepdims=True)
            att = jnp.where(row == col, diag, 0.0)
            for li, n in enumerate(levels):
                base = 2 * CH + 2 * CH * li
                eq = ex[base:base + CH, ls]
                ek = ex[base + CH:base + 2 * CH, ls]
                tq = (rowi % n) >= (n // 2)
                qt = jnp.where(jnp.logical_and(mj, tq), q_p * jnp.exp(eq), 0.0)
                kt = jnp.where(tq, 0.0, k_p * jnp.exp(ek))
                att = att + jnp.where((row // n) == (col // n), _dot_nt(qt.astype(BF16), kt.astype(BF16)), 0.0)
            o = o + _dot(att.astype(BF16), jnp.where(mj, v_p, 0.0).astype(BF16))
        kw = (k_p * jnp.exp(b_end[:, ls] - b[:, ls])).astype(BF16)
        for s in range(n_seg):
            r0 = s * t_len
            upd = _dot_tn(_seg_rows(v_p, rowi, t_len, s, n_seg).astype(BF16), kw)
            s_scr[s, p] = jnp.exp(b_end[r0:r0 + 1, ls]) * s_scr[s, p] + jnp.where(bd, upd, 0.0)
        outs.append(_head_rmsnorm(o, lo))
    o_all = jnp.concatenate(outs, axis=-1) * ng_ref[...] * jax.nn.silu(x_ref[:, 3 * HG_QK:4 * HG_QK])
    o_ref[...] = o_all.astype(BF16)

    @pl.when(c == last)
    def _():
        sn_ref[...] = s_scr[...]


def hgrn_mixer(proj, row_blk0, s0, w, lb, **grid):
    t_len = CH // grid['n_seg']
    ng = jnp.tile(w['hgrn_norm'], HG_H).reshape(1, BRANCH_W)
    st = _to_bd(jnp.swapaxes(s0, -1, -2))
    assert COLS['hg_f'][0] == COLS['hg_q'][0] + HG_QK and COLS['hg_g'][0] == COLS['hg_q'][0] + 3 * HG_QK
    o, sn = _mixer_call(
        _hgrn_kernel, "hgrn_mixer", proj, row_blk0, [(COLS['hg_q'][0], 4 * HG_QK)],
        [st], [_hgrn_cmat(t_len), lb.reshape(1, HG_QK), ng], [st],
        [pltpu.VMEM((grid['n_seg'], 2, 128, 128), F32)], **grid)
    return o, jnp.swapaxes(_from_bd(sn), -1, -2)


def _ssd_kernel(xin_ref, z_ref, sm_ref, conv0_ref, h0_ref, cw_ref, cb_ref, gp_ref, dvec_ref, ng_ref,
                o_ref, convn_ref, hn_ref, ext_scr, h_scr, *, n_seg):
    t_len = CH // n_seg
    c = pl.program_id(1)
    last = pl.num_programs(1) - 1

    @pl.when(c == 0)
    def _():
        h_scr[...] = h0_ref[...].reshape(n_seg, 2, 128, SSD_N)
        ext_scr[:, 5:8, :] = conv0_ref[...]

    xc, tail = _conv_silu(xin_ref[...], ext_scr, cw_ref, cb_ref[...], t_len, n_seg)

    @pl.when(c == last)
    def _():
        convn_ref[...] = tail

    sx, bm, cm = xc[:, :256], xc[:, 256:512], xc[:, 512:768]
    same, tri, _ = _seg_masks(t_len)
    lane = lax.broadcasted_iota(jnp.int32, (CH, LANES), 1)
    rowi = lax.broadcasted_iota(jnp.int32, (CH, 1), 0)
    gl = jnp.logical_and(lane >= L_SSD_DT, lane < L_SSD_DT + SSD_H)
    dt = jnp.where(gl, _softplus(sm_ref[...] + gp_ref[1:2, :]), 0.0)
    da = -jnp.exp(gp_ref[0:1, :]) * dt
    cum = _dot(tri.astype(F32), da, precision=HI)
    cum_end = _dot(same.astype(F32), da, precision=HI)
    cum_r = _row_forms(cum, 16)
    dt_r = _row_forms(dt, 16)
    lo = lane < 64
    rsel = lax.broadcasted_iota(jnp.int32, (LANES, 1), 0) < 64
    ys = []
    for g in range(SSD_G):
        cg = cm[:, 128 * g:128 * (g + 1)].astype(BF16)
        bg = bm[:, 128 * g:128 * (g + 1)].astype(BF16)
        sxp = sx[:, 128 * g:128 * (g + 1)]
        cb = _dot_nt(cg, bg)
        yst = _seg_sum([_dot_nt(cg, h_scr[s, g].astype(BF16)) for s in range(n_seg)], rowi, t_len)
        yatt = jnp.zeros((CH, LANES), F32)
        cols = []
        for j in range(2):
            l = L_SSD_DT + 2 * g + j
            cc = cum[:, l:l + 1]
            dec = _masked_exp(cc - cum_r[l:l + 1, :], tri)
            att = cb * dec * dt_r[l:l + 1, :]
            xm = jnp.where(lo if j == 0 else jnp.logical_not(lo), sxp, 0.0)
            yatt = yatt + _dot(att.astype(BF16), xm.astype(BF16))
            cols.append((cc, dt[:, l:l + 1] * jnp.exp(cum_end[:, l:l + 1] - cc)))
        ys.append(jnp.where(lo, jnp.exp(cols[0][0]), jnp.exp(cols[1][0])) * yst + yatt)
        xw = sxp * jnp.where(lo, cols[0][1], cols[1][1])
        for s in range(n_seg):
            r0 = s * t_len
            l = L_SSD_DT + 2 * g
            e0 = jnp.exp(cum_end[r0:r0 + 1, l:l + 1])
            e1 = jnp.exp(cum_end[r0:r0 + 1, l + 1:l + 2])
            upd = _dot_tn(_seg_rows(xw, rowi, t_len, s, n_seg).astype(BF16), bg)
            h_scr[s, g] = jnp.where(rsel, e0, e1) * h_scr[s, g] + upd
    y_all = jnp.concatenate(ys, axis=-1) + dvec_ref[...] * sx
    o_ref[...] = _rms(y_all * jax.nn.silu(z_ref[...]), ng_ref[...]).astype(BF16)

    @pl.when(c == last)
    def _():
        hn_ref[...] = h_scr[...].reshape(n_seg, SSD_H, 64, SSD_N)


def ssd_mixer(proj, row_blk0, conv0, h0, w, **grid):
    t_len = CH // grid['n_seg']
    gp = _gate_rows([(L_SSD_DT, w['ssd_a_log']), (L_SSD_DT, w['ssd_dt_bias'])])
    dvec = jnp.repeat(w['ssd_d'], BRANCH_W // SSD_H).reshape(1, BRANCH_W)
    return _mixer_call(
        _ssd_kernel, "ssd_mixer", proj, row_blk0,
        [COLS['ssd_in'], COLS['ssd_z'], (GATE_COL0, LANES)],
        [conv0, h0], [w['ssd_conv_w'], w['ssd_conv_b'].reshape(1, SSD_CONV_CH), gp, dvec,
                      w['ssd_norm'].reshape(1, BRANCH_W)], [conv0, h0],
        [pltpu.VMEM((grid['n_seg'], 8 + t_len, SSD_CONV_CH), F32), pltpu.VMEM((grid['n_seg'], 2, 128, SSD_N), F32)],
        **grid)


def _mlstm_kernel(x_ref, c0_ref, n0_ref, m0_ref, gp_ref, ng_ref, o_ref, cn_ref, nn_ref, mn_ref,
                  c_scr, n_scr, m_scr, *, n_seg):
    t_len = CH // n_seg
    c = pl.program_id(1)
    last = pl.num_programs(1) - 1

    @pl.when(c == 0)
    def _():
        c_scr[...] = c0_ref[...]
        n_scr[...] = n0_ref[...]
        m_scr[...] = m0_ref[...]

    same, tri, _ = _seg_masks(t_len)
    lane = lax.broadcasted_iota(jnp.int32, (CH, LANES), 1)
    lane256 = lax.broadcasted_iota(jnp.int32, (CH, BRANCH_W), 1)
    rowi = lax.broadcasted_iota(jnp.int32, (CH, 1), 0)
    r128 = lax.broadcasted_iota(jnp.int32, (LANES, 1), 0)
    neg = jnp.float32(-jnp.inf)

    v_all = x_ref[:, 0:256]
    q_all = x_ref[:, 512:640]
    k_all = x_ref[:, 640:768] * (ML_DK ** -0.5)
    sm = x_ref[:, 768:896]
    ig = sm + gp_ref[0:1, :]
    fl = jnp.logical_and(lane >= L_ML_F, lane < L_ML_F + ML_H)
    lf = jnp.where(fl, -_softplus(-(sm + gp_ref[1:2, :])), 0.0)
    b = _dot(tri.astype(F32), lf, precision=HI)
    b_end = _dot(same.astype(F32), lf, precision=HI)
    b_r = _row_forms(b)
    ig_r = _row_forms(ig)
    mm = m_scr[...]
    qb = q_all.astype(BF16)
    kb = k_all.astype(BF16)
    qn = _seg_sum([_dot(qb, n_scr[s].astype(BF16)) for s in range(n_seg)], rowi, t_len)
    qc = _seg_sum([_dot(qb, c_scr[s].astype(BF16)) for s in range(n_seg)], rowi, t_len)
    num_att = jnp.zeros((CH, BRANCH_W), F32)
    w_ins, dens, w_ends, a_ends, m_ends = [], [], [], [], []
    for h in range(ML_H):
        li, lf_ = L_ML_I + h, L_ML_F + h
        bc = b[:, lf_:lf_ + 1]
        bec = b_end[:, lf_:lf_ + 1]
        igc = ig[:, li:li + 1]
        mmc = mm[:, li:li + 1]
        br = b_r[lf_:lf_ + 1, :]
        igr = ig_r[li:li + 1, :]
        diff = igr - br
        cmx = jnp.max(jnp.where(tri, diff, neg), axis=-1, keepdims=True)
        smx = jnp.max(jnp.where(same, diff, neg), axis=-1, keepdims=True)
        m_c = bc + jnp.maximum(mmc, cmx)
        m_end = bec + jnp.maximum(mmc, smx)
        w_in = jnp.exp(bc + mmc - m_c)
        logw = bc - br + igr - m_c
        mq = jnp.logical_and(lane >= ML_DK * h, lane < ML_DK * (h + 1))
        qk = _dot_nt(jnp.where(mq, q_all, 0.0).astype(BF16), kb)
        wts = _masked_exp(logw, tri) * qk
        mv = jnp.logical_and(lane256 >= 64 * h, lane256 < 64 * (h + 1))
        num_att = num_att + _dot(wts.astype(BF16), jnp.where(mv, v_all, 0.0).astype(BF16))
        nq = w_in * qn[:, li:li + 1] + jnp.sum(wts, axis=-1, keepdims=True)
        w_ins.append(w_in)
        dens.append(jnp.maximum(jnp.abs(nq), jnp.exp(-m_c)))
        w_ends.append(jnp.exp(bec - bc + igc - m_end))
        a_ends.append(jnp.exp(bec + mmc - m_end))
        m_ends.append(m_end)
    num = _quarter_sel(lane256, 64, w_ins) * qc + num_att
    hout = num / _quarter_sel(lane256, 64, dens)
    outs = [_head_rmsnorm(hout[:, 128 * p:128 * (p + 1)], lane < 64) for p in range(2)]
    o_all = jnp.concatenate(outs, axis=-1) * ng_ref[...] * jax.nn.sigmoid(x_ref[:, 256:512])
    o_ref[...] = o_all.astype(BF16)

    kw = k_all * _quarter_sel(lane, ML_DK, w_ends)
    wend_tile = jnp.zeros((CH, LANES), F32)
    m_tile = jnp.zeros((CH, LANES), F32)
    for h in range(ML_H):
        wend_tile = jnp.where(lane == L_ML_I + h, w_ends[h], wend_tile)
        m_tile = jnp.where(lane == L_ML_I + h, m_ends[h], m_tile)
    m_scr[...] = m_tile
    vb = v_all.astype(BF16)
    wb = wend_tile.astype(BF16)
    rc = lax.broadcasted_iota(jnp.int32, (LANES, BRANCH_W), 0)
    lc = lax.broadcasted_iota(jnp.int32, (LANES, BRANCH_W), 1)
    bd_c = (rc // ML_DK) == (lc // 64)
    rn = lax.broadcasted_iota(jnp.int32, (LANES, LANES), 0)
    ln = lax.broadcasted_iota(jnp.int32, (LANES, LANES), 1)
    bd_n = ln == (rn // ML_DK) + L_ML_I
    for s in range(n_seg):
        r0 = s * t_len
        a_sel = _quarter_sel(r128, ML_DK, [a[r0:r0 + 1, :] for a in a_ends])
        upd_c = _dot_tn(_seg_rows(kw, rowi, t_len, s, n_seg).astype(BF16), vb)
        upd_n = _dot_tn(_seg_rows(k_all, rowi, t_len, s, n_seg).astype(BF16), wb)
        c_scr[s] = a_sel * c_scr[s] + jnp.where(bd_c, upd_c, 0.0)
        n_scr[s] = a_sel * n_scr[s] + jnp.where(bd_n, upd_n, 0.0)

    @pl.when(c == last)
    def _():
        cn_ref[...] = c_scr[...]
        nn_ref[...] = n_scr[...]
        mn_ref[...] = m_tile


def mlstm_mixer(proj, row_blk0, c0, n0, m0, w, **grid):
    t_len = CH // grid['n_seg']
    bsz = c0.shape[0]
    gp = _gate_rows([(L_ML_I, w['ml_ig_b']), (L_ML_F, w['ml_fg_b'])])
    ng = jnp.tile(w['ml_norm'], ML_H).reshape(1, BRANCH_W)
    eye = jnp.eye(ML_H, dtype=F32)
    pad = ((0, 0), (L_ML_I, LANES - L_ML_I - ML_H))
    c_bd = jnp.einsum('bhkv,hg->bhkgv', c0, eye).reshape(bsz, ML_QK, BRANCH_W)
    n_bd = jnp.pad(jnp.einsum('bhk,hg->bhkg', n0, eye).reshape(bsz, ML_QK, ML_H), ((0, 0),) + pad)
    m_exp = jnp.pad(jnp.repeat(m0, t_len, axis=0), pad)
    assert (COLS['ml_o'][0], COLS['ml_q'][0], COLS['ml_k'][0], GATE_COL0) == tuple(
        COLS['ml_v'][0] + o for o in (256, 512, 640, 768))
    o, cn, nn, mn = _mixer_call(
        _mlstm_kernel, "mlstm_mixer", proj, row_blk0, [(COLS['ml_v'][0], 896)],
        [c_bd, n_bd, m_exp], [gp, ng], [c_bd, n_bd, m_exp],
        [pltpu.VMEM((grid['n_seg'], ML_QK, BRANCH_W), F32), pltpu.VMEM((grid['n_seg'], ML_QK, LANES), F32),
         pltpu.VMEM((CH, LANES), F32)], **grid)
    c_new = jnp.einsum('bhkgv,hg->bhkv', cn.reshape(bsz, ML_H, ML_DK, ML_H, 64), eye)
    n_new = jnp.einsum('bhkg,hg->bhk', nn[:, :, L_ML_I:L_ML_I + ML_H].reshape(bsz, ML_H, ML_DK, ML_H), eye)
    m_new = mn[::t_len, L_ML_I:L_ML_I + ML_H]
    return o, c_new, n_new, m_new


def _merge_kernel(h_ref, gates_ref, br_ref, wb_ref, wo_ref, o_ref):
    merged = None
    for n in range(N_BRANCH):
        y = _dot(br_ref[:, n * BRANCH_W:(n + 1) * BRANCH_W], wb_ref[n])
        t = jax.nn.sigmoid(gates_ref[:, n * D_MODEL:(n + 1) * D_MODEL]) * y
        merged = t if merged is None else merged + t
    o_ref[...] = h_ref[...] + _dot(merged.astype(BF16), wo_ref[...])


def merge(h, proj, br, w_branch, w_out, tm):
    m, d = h.shape
    assert COLS['gates'] == (0, N_BRANCH * D_MODEL)
    return pl.pallas_call(
        _merge_kernel,
        grid=(m // tm,),
        in_specs=[pl.BlockSpec((tm, d), lambda i: (i, 0)),
                  pl.BlockSpec((tm, N_BRANCH * d), lambda i: (i, 0)),
                  pl.BlockSpec((tm, N_BRANCH * BRANCH_W), lambda i: (i, 0)),
                  pl.BlockSpec((N_BRANCH, BRANCH_W, d), lambda i: (0, 0, 0)),
                  pl.BlockSpec((d, d), lambda i: (0, 0))],
        out_specs=pl.BlockSpec((tm, d), lambda i: (i, 0)),
        out_shape=jax.ShapeDtypeStruct((m, d), F32),
        compiler_params=pltpu.CompilerParams(
            dimension_semantics=("parallel",), vmem_limit_bytes=VMEM_LIMIT),
        name="merge",
    )(h, proj, br, w_branch, w_out)


def _ffn_kernel(h_ref, g_ref, wg_ref, wu_ref, wd_ref, o_ref, u_ref):
    f = pl.program_id(1)

    @pl.when(f == 0)
    def _():
        u_ref[...] = _rms(h_ref[...], g_ref[...]).astype(BF16)

    u = u_ref[...]
    a = jax.nn.silu(_dot(u, wg_ref[...]))
    b = _dot(u, wu_ref[...])
    y = _dot((a * b).astype(BF16), wd_ref[...])

    @pl.when(f == 0)
    def _():
        o_ref[...] = h_ref[...] + y

    @pl.when(f != 0)
    def _():
        o_ref[...] += y


def ffn(h, g, wg, wu, wd, tm, tf):
    m, d = h.shape
    ff = wg.shape[1]
    return pl.pallas_call(
        _ffn_kernel,
        grid=(m // tm, ff // tf),
        in_specs=[pl.BlockSpec((tm, d), lambda i, f: (i, 0)),
                  pl.BlockSpec((1, d), lambda i, f: (0, 0)),
                  pl.BlockSpec((d, tf), lambda i, f: (0, f)),
                  pl.BlockSpec((d, tf), lambda i, f: (0, f)),
                  pl.BlockSpec((tf, d), lambda i, f: (f, 0))],
        out_specs=pl.BlockSpec((tm, d), lambda i, f: (i, 0)),
        out_shape=jax.ShapeDtypeStruct((m, d), F32),
        scratch_shapes=[pltpu.VMEM((tm, d), BF16)],
        compiler_params=pltpu.CompilerParams(
            dimension_semantics=("parallel", "arbitrary"), vmem_limit_bytes=VMEM_LIMIT),
        name="ffn",
    )(h, g.reshape(1, d), wg, wu, wd)


def _router_kernel(h_ref, g_ref, wr_ref, u_ref, w_ref, i_ref):
    u = _rms(h_ref[...], g_ref[...])
    u_ref[...] = u
    logits = _dot(u, wr_ref[...], precision=HI)
    lane = lax.broadcasted_iota(jnp.int32, logits.shape, 1)
    neg = jnp.float32(-jnp.inf)
    logits = jnp.where(lane < N_EXPERTS, logits, neg)
    m1 = jnp.max(logits, axis=-1, keepdims=True)
    i1 = jnp.min(jnp.where(logits == m1, lane, LANES), axis=-1, keepdims=True)
    rest = jnp.where(lane == i1, neg, logits)
    m2 = jnp.max(rest, axis=-1, keepdims=True)
    i2 = jnp.min(jnp.where(rest == m2, lane, LANES), axis=-1, keepdims=True)
    e = jnp.exp(m2 - m1)
    den = 1.0 + e
    w_ref[...] = jnp.where(lane == 0, 1.0 / den, jnp.where(lane == 1, e / den, 0.0))
    i_ref[...] = jnp.where(lane == 0, i1, jnp.where(lane == 1, i2, 0))


def router(h, g, w_router, tm):
    m, d = h.shape
    wr = jnp.pad(w_router, ((0, 0), (0, LANES - N_EXPERTS)))
    return pl.pallas_call(
        _router_kernel,
        grid=(m // tm,),
        in_specs=[pl.BlockSpec((tm, d), lambda i: (i, 0)),
                  pl.BlockSpec((1, d), lambda i: (0, 0)),
                  pl.BlockSpec((d, LANES), lambda i: (0, 0))],
        out_specs=[pl.BlockSpec((tm, d), lambda i: (i, 0)),
                   pl.BlockSpec((tm, LANES), lambda i: (i, 0)),
                   pl.BlockSpec((tm, LANES), lambda i: (i, 0))],
        out_shape=[jax.ShapeDtypeStruct((m, d), F32),
                   jax.ShapeDtypeStruct((m, LANES), F32),
                   jax.ShapeDtypeStruct((m, LANES), jnp.int32)],
        compiler_params=pltpu.CompilerParams(
            dimension_semantics=("parallel",), vmem_limit_bytes=VMEM_LIMIT),
        name="router",
    )(h, g.reshape(1, d), wr)


def _row_copies(src_hbm, dst_hbm, idx_ref, base, buf, sem, n_rows, gather, wait):
    def body(r, carry):
        row = idx_ref[base + r]
        if gather:
            cp = pltpu.make_async_copy(src_hbm.at[pl.ds(row, 1)], buf.at[pl.ds(r, 1)], sem)
        else:
            cp = pltpu.make_async_copy(buf.at[pl.ds(r, 1)], dst_hbm.at[pl.ds(row, 1)], sem)
        if wait:
            cp.wait()
        else:
            cp.start()
        return carry
    lax.fori_loop(0, n_rows, body, 0, unroll=8)


def _expert_kernel(te_ref, nt_ref, src_ref, dst_ref, u_hbm, wg_ref, wu_ref, wd_ref, y_hbm,
                   xbuf, xb_ref, acc_ref, obuf, gsem, ssem, *, tm):
    i = pl.program_id(0)
    f = pl.program_id(1)
    last_f = pl.num_programs(1) - 1
    nt = nt_ref[0]
    gather = functools.partial(_row_copies, u_hbm, None, src_ref, gather=True, n_rows=tm)
    scatter = functools.partial(_row_copies, None, y_hbm, dst_ref, buf=obuf, sem=ssem.at[0], gather=False, n_rows=tm)

    @pl.when(jnp.logical_and(i < nt, f == 0))
    def _():
        slot = i % 2

        @pl.when(i == 0)
        def _():
            gather(base=0, buf=xbuf.at[0], sem=gsem.at[0], wait=False)
            obuf[...] = jnp.zeros_like(obuf)
            fill = pltpu.make_async_copy(obuf, y_hbm.at[pl.ds(y_hbm.shape[0] - tm, tm)], ssem.at[0])
            fill.start()
            fill.wait()

        gather(base=i * tm, buf=xbuf.at[slot], sem=gsem.at[slot], wait=True)

        @pl.when(i + 1 < nt)
        def _():
            gather(base=(i + 1) * tm, buf=xbuf.at[1 - slot], sem=gsem.at[1 - slot], wait=False)

        xb_ref[...] = xbuf[slot].astype(BF16)

    @pl.when(i < nt)
    def _():
        x = xb_ref[...]
        a = jax.nn.silu(_dot(x, wg_ref[0]))
        b = _dot(x, wu_ref[0])
        y = _dot((a * b).astype(BF16), wd_ref[0])

        @pl.when(f == 0)
        def _():
            acc_ref[...] = y

        @pl.when(f != 0)
        def _():
            acc_ref[...] += y

        @pl.when(f == last_f)
        def _():
            @pl.when(i > 0)
            def _():
                scatter(base=(i - 1) * tm, wait=True)

            obuf[...] = acc_ref[...]
            scatter(base=i * tm, wait=False)

            @pl.when(i == nt - 1)
            def _():
                scatter(base=i * tm, wait=True)


def experts(u, tile_expert, n_tiles, src_tok, dst_row, n_out_rows, wg, wu, wd, tm, tf):
    d = u.shape[1]
    ff = wg.shape[2]
    n_row_tiles = tile_expert.shape[0]
    wmap = lambda i, f, te, nt, src, dst: (te[i], 0, f)
    grid_spec = pltpu.PrefetchScalarGridSpec(
        num_scalar_prefetch=4,
        grid=(n_row_tiles, ff // tf),
        in_specs=[pl.BlockSpec(memory_space=pl.ANY),
                  pl.BlockSpec((1, d, tf), wmap),
                  pl.BlockSpec((1, d, tf), wmap),
                  pl.BlockSpec((1, tf, d), lambda i, f, te, nt, src, dst: (te[i], f, 0))],
        out_specs=pl.BlockSpec(memory_space=pl.ANY),
        scratch_shapes=[pltpu.VMEM((2, tm, d), F32), pltpu.VMEM((tm, d), BF16), pltpu.VMEM((tm, d), F32),
                        pltpu.VMEM((tm, d), F32), pltpu.SemaphoreType.DMA((2,)), pltpu.SemaphoreType.DMA((1,))],
    )
    return pl.pallas_call(
        functools.partial(_expert_kernel, tm=tm),
        grid_spec=grid_spec,
        out_shape=jax.ShapeDtypeStruct((n_out_rows, d), F32),
        compiler_params=pltpu.CompilerParams(
            dimension_semantics=("arbitrary", "arbitrary"), vmem_limit_bytes=VMEM_LIMIT,
            disable_bounds_checks=True),
        name="experts",
    )(tile_expert, n_tiles, src_tok, dst_row, u, wg, wu, wd)


def moe(h, g, w_router, wg, wu, wd, tm_route, tm_e, tf):
    m, d = h.shape
    u, top_w, top_i = router(h, g, w_router, tm_route)
    n_pairs = TOP_K * m
    flat_e = top_i[:, :TOP_K].reshape(-1)
    onehot = (flat_e[:, None] == jnp.arange(N_EXPERTS, dtype=jnp.int32)[None, :]).astype(jnp.int32)
    rank = jnp.sum((jnp.cumsum(onehot, axis=0) - 1) * onehot, axis=1)
    counts = jnp.sum(onehot, axis=0)
    tiles_per = (counts + tm_e - 1) // tm_e
    tile_end = jnp.cumsum(tiles_per)
    tile_start = tile_end - tiles_per
    grouped_row = tile_start[flat_e] * tm_e + rank
    n_rows = n_pairs + N_EXPERTS * tm_e
    n_row_tiles = n_rows // tm_e
    pair_ids = jnp.arange(n_pairs, dtype=jnp.int32)
    src_tok = jnp.zeros((n_rows,), jnp.int32).at[grouped_row].set(pair_ids // TOP_K)
    spare = n_pairs + jnp.arange(n_rows, dtype=jnp.int32) % tm_e
    dst_row = spare.at[grouped_row].set(pair_ids)
    tile_ids = jnp.arange(n_row_tiles, dtype=jnp.int32)
    tile_expert = jnp.minimum(jnp.sum((tile_ids[:, None] >= tile_end[None, :]).astype(jnp.int32), axis=1),
                              N_EXPERTS - 1).astype(jnp.int32)
    n_tiles = tile_end[-1:].astype(jnp.int32)
    last_e = tile_expert[jnp.maximum(n_tiles[0] - 1, 0)]
    tile_expert = jnp.where(tile_ids < n_tiles[0], tile_expert, last_e)
    y = experts(u, tile_expert, n_tiles, src_tok, dst_row, n_pairs + tm_e, wg, wu, wd, tm_e, tf)
    return y.reshape((n_pairs + tm_e) // TOP_K, TOP_K * d), top_w


def _ple_kernel(*refs, final, combine):
    if combine:
        h_ref, y_ref, tw_ref, p_ref, g_ref, wg_ref, wp_ref, gf_ref, o_ref = refs
        d = h_ref.shape[1]
        h = h_ref[...] + (tw_ref[:, 0:1] * y_ref[:, :d] + tw_ref[:, 1:2] * y_ref[:, d:])
    else:
        h_ref, p_ref, g_ref, wg_ref, wp_ref, gf_ref, o_ref = refs
        h = h_ref[...]
    v = _rms(h, g_ref[...]).astype(BF16)
    pg = jax.nn.sigmoid(_dot(v, wg_ref[...]))
    e = _dot(p_ref[...].astype(BF16), wp_ref[...])
    out = h + pg * e
    if final:
        out = _rms(out, gf_ref[...])
    o_ref[...] = out


def ple(h, p, g, w_gate, w_p, g_final, tm, final, expert_out=None):
    m, d = h.shape
    dp = p.shape[1]
    rows = lambda wd: pl.BlockSpec((tm, wd), lambda i: (i, 0))
    whole = lambda a, b: pl.BlockSpec((a, b), lambda i: (0, 0))
    extra, extra_specs = (), []
    if expert_out is not None:
        extra = expert_out
        extra_specs = [rows(TOP_K * d), rows(LANES)]
    return pl.pallas_call(
        functools.partial(_ple_kernel, final=final, combine=expert_out is not None),
        grid=(m // tm,),
        in_specs=[rows(d)] + extra_specs + [rows(dp), whole(1, d), whole(d, d), whole(dp, d), whole(1, d)],
        out_specs=rows(d),
        out_shape=jax.ShapeDtypeStruct((m, d), F32),
        compiler_params=pltpu.CompilerParams(
            dimension_semantics=("parallel",), vmem_limit_bytes=VMEM_LIMIT),
        name="ple",
    )(h, *extra, p, g.reshape(1, d), w_gate, w_p, g_final.reshape(1, d))


STATE_KEYS = ('gdn_conv', 'gdn', 'hgrn', 'ssd_conv', 'ssd', 'ml_c', 'ml_n', 'ml_m')
MIXER_KEYS = ('gdn_conv_w', 'gdn_a_log', 'gdn_dt_bias', 'gdn_norm', 'hgrn_norm',
              'ssd_conv_w', 'ssd_conv_b', 'ssd_a_log', 'ssd_dt_bias', 'ssd_d', 'ssd_norm',
              'ml_ig_b', 'ml_fg_b', 'ml_norm')


def _mixers(proj, row_blk0, st, w, lb, **grid):
    o_a, gdn_conv, s_gdn = gdn_mixer(proj, row_blk0, st['gdn_conv'], st['gdn'], w, **grid)
    o_b, s_hg = hgrn_mixer(proj, row_blk0, st['hgrn'], w, lb, **grid)
    o_c, ssd_conv, h_ssd = ssd_mixer(proj, row_blk0, st['ssd_conv'], st['ssd'], w, **grid)
    o_d, c_ml, n_ml, m_ml = mlstm_mixer(proj, row_blk0, st['ml_c'], st['ml_n'], st['ml_m'], w, **grid)
    new = {'gdn_conv': gdn_conv, 'gdn': s_gdn, 'hgrn': s_hg, 'ssd_conv': ssd_conv, 'ssd': h_ssd,
           'ml_c': c_ml, 'ml_n': n_ml, 'ml_m': m_ml}
    return jnp.concatenate([o_a, o_b, o_c, o_d], axis=-1), new


def kernel(x_prompt, x_sample, state_gdn_conv, state_gdn, state_hgrn, state_ssd_conv, state_ssd, state_mlstm_c, state_mlstm_n, state_mlstm_m, p_prompt, p_sample, g_mix, w_in, gdn_conv_w, gdn_a_log, gdn_dt_bias, gdn_norm, hgrn_lb, hgrn_norm, ssd_conv_w, ssd_conv_b, ssd_a_log, ssd_dt_bias, ssd_d, ssd_norm, ml_ig_b, ml_fg_b, ml_norm, w_branch, w_out, g_ffn, w_ff_gate, w_ff_up, w_ff_down, w_router, w_ex_gate, w_ex_up, w_ex_down, w_ple, w_ple_gate, g_ple, g_final):
    prm = {'gdn_conv_w': gdn_conv_w, 'gdn_a_log': gdn_a_log, 'gdn_dt_bias': gdn_dt_bias,
           'gdn_norm': gdn_norm, 'hgrn_norm': hgrn_norm, 'ssd_conv_w': ssd_conv_w,
           'ssd_conv_b': ssd_conv_b, 'ssd_a_log': ssd_a_log, 'ssd_dt_bias': ssd_dt_bias,
           'ssd_d': ssd_d, 'ssd_norm': ssd_norm, 'ml_ig_b': ml_ig_b, 'ml_fg_b': ml_fg_b,
           'ml_norm': ml_norm}
    bp, lp, d = x_prompt.shape
    bs, ls, _ = x_sample.shape
    mp, ms = bp * lp, bs * ls
    assert lp % CH == 0 and CH % ls == 0 and ms % CH == 0 and mp % CH == 0
    st_s = {'gdn_conv': state_gdn_conv, 'gdn': state_gdn, 'hgrn': state_hgrn, 'ssd_conv': state_ssd_conv,
            'ssd': state_ssd, 'ml_c': state_mlstm_c, 'ml_n': state_mlstm_n, 'ml_m': state_mlstm_m}
    st_p = {k: jnp.zeros((bp,) + v.shape[2:], F32) for k, v in st_s.items()}
    grid_p = dict(n_outer=bp, n_chunks=lp // CH, n_seg=1)
    grid_s = dict(n_outer=ms // CH, n_chunks=1, n_seg=CH // ls)

    sm = jax.nn.softmax(hgrn_lb, axis=0)
    lb_all = jnp.cumsum(sm, axis=0) - sm[0]

    h = jnp.concatenate([x_prompt.reshape(mp, d), x_sample.reshape(ms, d)], axis=0)
    new_p = {k: [] for k in STATE_KEYS}
    new_s = {k: [] for k in STATE_KEYS}
    for l in range(DEPTH):
        wl = {k: prm[k][l] for k in MIXER_KEYS}
        proj = norm_matmul(h, g_mix[l], _permute_w_in(w_in[l]), tm=1024, tn=1152)
        br_p, np_ = _mixers(proj, 0, st_p, wl, lb_all[l], **grid_p)
        br_s, ns_ = _mixers(proj, mp // CH, {k: v[l] for k, v in st_s.items()}, wl, lb_all[l], **grid_s)
        br = jnp.concatenate([br_p, br_s], axis=0)
        h = merge(h, proj, br, w_branch[l].astype(BF16), w_out[l].astype(BF16), tm=512)
        j = l // 2
        expert_out = None
        if l % 2 == 0:
            h = ffn(h, g_ffn[l], w_ff_gate[j].astype(BF16), w_ff_up[j].astype(BF16),
                    w_ff_down[j].astype(BF16), tm=512, tf=D_FF // 2)
        else:
            expert_out = moe(h, g_ffn[l], w_router[j], w_ex_gate[j].astype(BF16), w_ex_up[j].astype(BF16),
                             w_ex_down[j].astype(BF16), tm_route=512, tm_e=512, tf=D_FF // 2)
        p = jnp.concatenate([p_prompt[l].reshape(mp, D_PLE), p_sample[l].reshape(ms, D_PLE)], axis=0)
        h = ple(h, p, g_ple[l], w_ple_gate[l].astype(BF16), w_ple[l].astype(BF16), g_final,
                tm=512, final=(l == DEPTH - 1), expert_out=expert_out)
        for k in STATE_KEYS:
            new_p[k].append(np_[k])
            new_s[k].append(ns_[k])
    y_prompt = h[:mp].reshape(bp, lp, d)
    y_sample = h[mp:].reshape(bs, ls, d)
    sp = {k: jnp.stack(v) for k, v in new_p.items()}
    ss = {k: jnp.stack(v) for k, v in new_s.items()}
    return (y_prompt, y_sample,
            sp['gdn_conv'], sp['gdn'], sp['hgrn'], sp['ssd_conv'], sp['ssd'], sp['ml_c'], sp['ml_n'], sp['ml_m'],
            ss['gdn_conv'], ss['gdn'], ss['hgrn'], ss['ssd_conv'], ss['ssd'], ss['ml_c'], ss['ml_n'], ss['ml_m'])
```

```python
import functools

import numpy as np
import jax
import jax.numpy as jnp
from jax import lax
from jax.experimental import pallas as pl
from jax.experimental.pallas import tpu as pltpu

F32 = jnp.float32
BF16 = jnp.bfloat16
HI = lax.Precision.HIGHEST

D_MODEL = 1024
DEPTH = 2
D_PLE = 256
N_BRANCH = 4
BRANCH_W = D_MODEL // N_BRANCH
CONV_W = 4
EPS = 1e-6

GDN_H = 4
GDN_DK = 64
GDN_QK = GDN_H * GDN_DK
GDN_CONV_CH = 2 * GDN_QK + BRANCH_W
HG_H = 4
HG_QK = 256
SSD_H = 4
SSD_G = 2
SSD_N = 128
SSD_CONV_CH = BRANCH_W + 2 * SSD_G * SSD_N
ML_H = 4
ML_DK = 32
ML_QK = ML_H * ML_DK
D_FF = ((8 * D_MODEL // 3 + 255) // 256) * 256
N_EXPERTS = 8
TOP_K = 2

_REF_SPLITS = (('gdn_in', GDN_CONV_CH), ('gdn_b', GDN_H), ('gdn_a', GDN_H), ('gdn_z', BRANCH_W),
               ('hg_q', HG_QK), ('hg_f', HG_QK), ('hg_v', BRANCH_W), ('hg_g', BRANCH_W),
               ('ssd_z', BRANCH_W), ('ssd_in', SSD_CONV_CH), ('ssd_dt', SSD_H),
               ('ml_q', ML_QK), ('ml_k', ML_QK), ('ml_v', BRANCH_W), ('ml_i', ML_H), ('ml_f', ML_H),
               ('ml_o', BRANCH_W), ('gates', N_BRANCH * D_MODEL))
_MY_ORDER = ('gates', 'gdn_z', 'ssd_z', 'gdn_in', 'ssd_in', 'hg_q', 'hg_f', 'hg_v', 'hg_g',
             'ml_v', 'ml_o', 'ml_q', 'ml_k', 'gdn_b', 'gdn_a', 'ssd_dt', 'ml_i', 'ml_f')
LANES = 128
CH = 64


def _layout():
    widths = dict(_REF_SPLITS)
    off, out = 0, {}
    for name in _MY_ORDER:
        out[name] = (off, widths[name])
        off += widths[name]
    return out, -(-off // LANES) * LANES


COLS, N_PROJ = _layout()
GATE_COL0 = COLS['gdn_b'][0]
L_GDN_B, L_GDN_A, L_SSD_DT, L_ML_I, L_ML_F = (COLS[n][0] - GATE_COL0 for n in ('gdn_b', 'gdn_a', 'ssd_dt', 'ml_i', 'ml_f'))
VMEM_LIMIT = 56 * 1024 * 1024


def _ref_offsets():
    off, acc = {}, 0
    for name, wd in _REF_SPLITS:
        off[name] = acc
        acc += wd
    return off, acc


def _permute_kernel(w_ref, ws_ref, o_ref):
    ref_off, _ = _ref_offsets()
    for name in _MY_ORDER:
        dst, wd = COLS[name]
        if wd >= LANES:
            o_ref[:, dst:dst + wd] = w_ref[:, ref_off[name]:ref_off[name] + wd].astype(BF16)
    o_ref[:, GATE_COL0:GATE_COL0 + LANES] = ws_ref[...].astype(BF16)


def _permute_w_in(w, tk=256):
    ref_off, n_in = _ref_offsets()
    small = [n for n in _MY_ORDER if COLS[n][1] < LANES]
    ws = jnp.concatenate([w[:, ref_off[n]:ref_off[n] + COLS[n][1]] for n in small], axis=1)
    ws = jnp.pad(ws, ((0, 0), (0, LANES - ws.shape[1])))
    d = w.shape[0]
    return pl.pallas_call(
        _permute_kernel, grid=(d // tk,),
        in_specs=[pl.BlockSpec((tk, n_in), lambda i: (i, 0)), pl.BlockSpec((tk, LANES), lambda i: (i, 0))],
        out_specs=pl.BlockSpec((tk, N_PROJ), lambda i: (i, 0)),
        out_shape=jax.ShapeDtypeStruct((d, N_PROJ), BF16),
        compiler_params=pltpu.CompilerParams(dimension_semantics=("parallel",), vmem_limit_bytes=VMEM_LIMIT),
        name="permute_w_in",
    )(w, ws)


def _rms(x, g):
    return x * lax.rsqrt(jnp.mean(x * x, axis=-1, keepdims=True) + EPS) * g


def _dot(a, b, **kw):
    return jnp.dot(a, b, preferred_element_type=F32, **kw)


def _dot_nt(a, b, **kw):
    return lax.dot_general(a, b, (((1,), (1,)), ((), ())), preferred_element_type=F32, **kw)


def _dot_tn(a, b, **kw):
    return lax.dot_general(a, b, (((0,), (0,)), ((), ())), preferred_element_type=F32, **kw)


def _norm_matmul_kernel(x_ref, g_ref, w_ref, o_ref, xn_ref):
    @pl.when(pl.program_id(1) == 0)
    def _():
        xn_ref[...] = _rms(x_ref[...], g_ref[...]).astype(BF16)

    o_ref[...] = _dot(xn_ref[...], w_ref[...])


def norm_matmul(x, g, w, tm, tn):
    m, d = x.shape
    n = w.shape[1]
    return pl.pallas_call(
        _norm_matmul_kernel,
        grid=(m // tm, n // tn),
        in_specs=[pl.BlockSpec((tm, d), lambda i, j: (i, 0)),
                  pl.BlockSpec((1, d), lambda i, j: (0, 0)),
                  pl.BlockSpec((d, tn), lambda i, j: (0, j))],
        out_specs=pl.BlockSpec((tm, tn), lambda i, j: (i, j)),
        out_shape=jax.ShapeDtypeStruct((m, n), F32),
        scratch_shapes=[pltpu.VMEM((tm, d), BF16)],
        compiler_params=pltpu.CompilerParams(
            dimension_semantics=("parallel", "arbitrary"), vmem_limit_bytes=VMEM_LIMIT),
        name="norm_matmul",
    )(x, g.reshape(1, d), w)


def _seg_masks(t_len):
    row = lax.broadcasted_iota(jnp.int32, (CH, CH), 0)
    col = lax.broadcasted_iota(jnp.int32, (CH, CH), 1)
    same = (row // t_len) == (col // t_len)
    tri = jnp.logical_and(same, col <= row)
    strict = jnp.logical_and(same, col < row)
    return same, tri, strict


def _row_forms(x, n_rows=24):
    r = lax.broadcasted_iota(jnp.int32, (n_rows, LANES), 0)
    l = lax.broadcasted_iota(jnp.int32, (n_rows, LANES), 1)
    return _dot_nt((r == l).astype(F32), x, precision=HI)


def _softplus(x):
    return jnp.maximum(x, 0.0) + jnp.log1p(jnp.exp(-jnp.abs(x)))


def _split2(x):
    hi = x.astype(BF16)
    return hi, (x - hi.astype(F32)).astype(BF16)


def _dot3(a, b):
    return _dot(a[0], b[0]) + (_dot(a[0], b[1]) + _dot(a[1], b[0]))


def _masked_exp(d, mask):
    return jnp.where(mask, jnp.exp(jnp.where(mask, d, 0.0)), 0.0)


def _conv_silu(x, ext_scr, cw_ref, bias, t_len, n_seg):
    w = x.shape[-1]
    ext_scr[:, 8:8 + t_len, :] = x.reshape(n_seg, t_len, w)
    y = cw_ref[3:4, :] * x
    for j in range(1, CONV_W):
        y = y + cw_ref[3 - j:4 - j, :] * ext_scr[:, 8 - j:8 - j + t_len, :].reshape(CH, w)
    if bias is not None:
        y = y + bias
    tail = ext_scr[:, 5 + t_len:8 + t_len, :]
    ext_scr[:, 5:8, :] = tail
    return jax.nn.silu(y), tail


def _halves(xp, lo):
    s_lo = jnp.sum(jnp.where(lo, xp, 0.0), axis=-1, keepdims=True)
    s_hi = jnp.sum(jnp.where(lo, 0.0, xp), axis=-1, keepdims=True)
    return jnp.where(lo, s_lo, s_hi)


def _head_rmsnorm(xp, lo):
    return xp * lax.rsqrt(_halves(xp * xp, lo) * (1.0 / 64) + EPS)


def _head_l2norm(xp, lo):
    return xp * lax.rsqrt(_halves(xp * xp, lo) + EPS)


def _seg_sum(parts, rowi, t_len):
    if len(parts) == 1:
        return parts[0]
    acc = jnp.where(rowi // t_len == 0, parts[0], 0.0)
    for s in range(1, len(parts)):
        acc = acc + jnp.where(rowi // t_len == s, parts[s], 0.0)
    return acc


def _seg_rows(x, rowi, t_len, s, n_seg):
    return x if n_seg == 1 else jnp.where(rowi // t_len == s, x, 0.0)


def _quarter_sel(idx, width, vals):
    out = vals[3]
    for h in (2, 1, 0):
        out = jnp.where(idx < (h + 1) * width, vals[h], out)
    return out


def _gate_rows(pairs):
    t = jnp.zeros((8, LANES), F32)
    for r, (off, v) in enumerate(pairs):
        t = t.at[r, off:off + v.shape[0]].set(v.astype(F32))
    return t


def _to_bd(s):
    b, _, a, c = s.shape
    s = s.reshape(b, 2, 2, a, c)
    z = jnp.zeros_like(s[:, :, 0])
    top = jnp.concatenate([s[:, :, 0], z], axis=-1)
    bot = jnp.concatenate([z, s[:, :, 1]], axis=-1)
    return jnp.concatenate([top, bot], axis=-2)


def _from_bd(s):
    b, _, a2, c2 = s.shape
    a, c = a2 // 2, c2 // 2
    return jnp.stack([s[:, :, :a, :c], s[:, :, a:, c:]], axis=2).reshape(b, 4, a, c)


def _fused_mixer_kernel(*refs, n_seg, parts):
    tot = [sum(p[j] for p in parts) for j in range(1, 5)]
    ins, rest = refs[:tot[0]], refs[tot[0]:]
    sts, rest = rest[:tot[1]], rest[tot[1]:]
    prs, rest = rest[:tot[2]], rest[tot[2]:]
    o_ref, rest = rest[0], rest[1:]
    outs, scr = rest[:tot[1]], rest[tot[1]:]
    at = [0, 0, 0, 0]
    calls = []
    for k, (body, n_in, n_st, n_pr, n_scr) in enumerate(parts):
        take = lambda seq, j, n: seq[at[j]:at[j] + n]
        calls.append(functools.partial(
            body, *take(ins, 0, n_in), *take(sts, 1, n_st), *take(prs, 2, n_pr),
            o_ref.at[:, k * BRANCH_W:(k + 1) * BRANCH_W],
            *take(outs, 1, n_st), *take(scr, 3, n_scr), n_seg=n_seg))
        for j, n in enumerate((n_in, n_st, n_pr, n_scr)):
            at[j] += n

    @pl.when(pl.program_id(1) == 0)
    def _():
        for call in calls:
            call(init=True)

    finishers = [call(init=False) for call in calls]

    @pl.when(pl.program_id(1) == pl.num_programs(1) - 1)
    def _():
        for fin in finishers:
            fin()


def _mixer_call(specs, proj, row_blk0, *, n_outer, n_chunks, n_seg):
    rows = n_outer * n_chunks * CH
    rmap = lambda blk: (lambda i, c: (row_blk0 + i * n_chunks + c, blk))
    full = lambda a: pl.BlockSpec(a.shape, lambda i, c: (0,) * a.ndim)
    sblk = lambda a: pl.BlockSpec((a.shape[0] // n_outer,) + a.shape[1:], lambda i, c: (i,) + (0,) * (a.ndim - 1))
    in_blocks = [b for s in specs for b in s[1]]
    state_ins = [a for s in specs for a in s[2]]
    params = [a for s in specs for a in s[3]]
    scratch = [a for s in specs for a in s[4]]
    for off, wd in in_blocks:
        assert off % wd == 0
    parts = tuple((s[0], len(s[1]), len(s[2]), len(s[3]), len(s[4])) for s in specs)
    width = len(specs) * BRANCH_W
    return pl.pallas_call(
        functools.partial(_fused_mixer_kernel, n_seg=n_seg, parts=parts),
        grid=(n_outer, n_chunks),
        in_specs=([pl.BlockSpec((CH, wd), rmap(off // wd)) for off, wd in in_blocks]
                  + [sblk(a) for a in state_ins] + [full(a) for a in params]),
        out_specs=[pl.BlockSpec((CH, width), lambda i, c: (i * n_chunks + c, 0))] + [sblk(a) for a in state_ins],
        out_shape=([jax.ShapeDtypeStruct((rows, width), BF16)]
                   + [jax.ShapeDtypeStruct(a.shape, F32) for a in state_ins]),
        scratch_shapes=scratch,
        compiler_params=pltpu.CompilerParams(
            dimension_semantics=("parallel", "arbitrary"), vmem_limit_bytes=VMEM_LIMIT),
        name="token_mixers",
    )(*([proj] * len(in_blocks)), *state_ins, *params)


def _gdn_kernel(xin_ref, z_ref, sm_ref, conv0_ref, s0_ref, cw_ref, gp_ref, ng_ref,
                o_ref, convn_ref, sn_ref, ext_scr, s_scr, *, n_seg, init):
    t_len = CH // n_seg
    if init:
        s_scr[...] = s0_ref[...]
        ext_scr[:, 5:8, :] = conv0_ref[...]
        return None

    xc, tail = _conv_silu(xin_ref[...], ext_scr, cw_ref, None, t_len, n_seg)
    same, tri, strict = _seg_masks(t_len)
    lane = lax.broadcasted_iota(jnp.int32, (CH, LANES), 1)
    rowi = lax.broadcasted_iota(jnp.int32, (CH, 1), 0)
    lo = lane < 64
    sm = sm_ref[...]
    beta = jax.nn.sigmoid(sm)
    gl = jnp.logical_and(lane >= L_GDN_A, lane < L_GDN_A + GDN_H)
    g = jnp.where(gl, -jnp.exp(gp_ref[0:1, :]) * _softplus(sm + gp_ref[1:2, :]), 0.0)
    gam = _dot(tri.astype(F32), g, precision=HI)
    gam_end = _dot(same.astype(F32), g, precision=HI)
    gam_r = _row_forms(gam, 8)
    r128 = lax.broadcasted_iota(jnp.int32, (LANES, LANES), 0)
    l128 = lax.broadcasted_iota(jnp.int32, (LANES, LANES), 1)
    bd = (r128 < 64) == (l128 < 64)
    rsel = lax.broadcasted_iota(jnp.int32, (LANES, 1), 0) < 64
    qs_, ks_, atts_, gcs_, a_, x_ = [], [], [], [], [], []
    for p in range(2):
        q_p = _head_l2norm(xc[:, 128 * p:128 * (p + 1)], lo) * (GDN_DK ** -0.5)
        k_p = _head_l2norm(xc[:, GDN_QK + 128 * p:GDN_QK + 128 * (p + 1)], lo)
        v_p = xc[:, 2 * GDN_QK + 128 * p:2 * GDN_QK + 128 * (p + 1)]
        kb = k_p.astype(BF16)
        qs_.append(q_p)
        ks_.append(k_p)
        for j in range(2):
            h = 2 * p + j
            mj = lo if j == 0 else jnp.logical_not(lo)
            kk = _dot_nt(jnp.where(mj, k_p, 0.0).astype(BF16), kb)
            qk = _dot_nt(jnp.where(mj, q_p, 0.0).astype(BF16), kb)
            gc = gam[:, L_GDN_A + h:L_GDN_A + h + 1]
            bc = beta[:, L_GDN_B + h:L_GDN_B + h + 1]
            dec = _masked_exp(gc - gam_r[L_GDN_A + h:L_GDN_A + h + 1, :], tri)
            a_.append(jnp.where(strict, bc * kk * dec, 0.0))
            x_.append(jnp.concatenate([jnp.where(mj, bc * v_p, 0.0),
                                       jnp.where(mj, (bc * jnp.exp(gc)) * k_p, 0.0)], axis=-1))
            atts_.append(qk * dec)
            gcs_.append(gc)
    sa = [_split2(a) for a in a_]
    x_ = [x - _dot3(s, _split2(x)) for s, x in zip(sa, x_)]
    n = 2
    while n < t_len:
        sa = [_split2(_dot3(s, s)) for s in sa]
        x_ = [x + _dot3(s, _split2(x)) for s, x in zip(sa, x_)]
        n *= 2
    outs = []
    for p in range(2):
        q_p, k_p = qs_[p], ks_[p]
        solv = x_[2 * p][:, :LANES] + x_[2 * p + 1][:, :LANES]
        solk = x_[2 * p][:, LANES:] + x_[2 * p + 1][:, LANES:]
        atts, gcs = atts_[2 * p:2 * p + 2], gcs_[2 * p:2 * p + 2]
        solk_b = solk.astype(BF16)
        qb = q_p.astype(BF16)
        u = solv - _seg_sum([_dot(solk_b, s_scr[s, p].astype(BF16)) for s in range(n_seg)], rowi, t_len)
        qs = _seg_sum([_dot(qb, s_scr[s, p].astype(BF16)) for s in range(n_seg)], rowi, t_len)
        o = jnp.where(lo, jnp.exp(gcs[0]), jnp.exp(gcs[1])) * qs
        for j in range(2):
            mj = lo if j == 0 else jnp.logical_not(lo)
            o = o + _dot(atts[j].astype(BF16), jnp.where(mj, u, 0.0).astype(BF16))
        ge0 = gam_end[:, L_GDN_A + 2 * p:L_GDN_A + 2 * p + 1]
        ge1 = gam_end[:, L_GDN_A + 2 * p + 1:L_GDN_A + 2 * p + 2]
        kw = k_p * jnp.where(lo, jnp.exp(ge0 - gcs[0]), jnp.exp(ge1 - gcs[1]))
        ub = u.astype(BF16)
        for s in range(n_seg):
            r0 = s * t_len
            dec_s = jnp.where(rsel, jnp.exp(ge0[r0:r0 + 1, :]), jnp.exp(ge1[r0:r0 + 1, :]))
            upd = _dot_tn(_seg_rows(kw, rowi, t_len, s, n_seg).astype(BF16), ub)
            s_scr[s, p] = dec_s * s_scr[s, p] + jnp.where(bd, upd, 0.0)
        outs.append(_head_rmsnorm(o, lo))
    o_all = jnp.concatenate(outs, axis=-1) * ng_ref[...] * jax.nn.silu(z_ref[...])
    o_ref[...] = o_all.astype(BF16)

    def finish():
        convn_ref[...] = tail
        sn_ref[...] = s_scr[...]
    return finish


def gdn_spec(st, w, lb, n_seg):
    t_len = CH // n_seg
    gp = _gate_rows([(L_GDN_A, w['gdn_a_log']), (L_GDN_A, w['gdn_dt_bias'])])
    ng = jnp.tile(w['gdn_norm'], GDN_H).reshape(1, BRANCH_W)
    spec = (_gdn_kernel, [COLS['gdn_in'], COLS['gdn_z'], (GATE_COL0, LANES)],
            [st['gdn_conv'], _to_bd(st['gdn'])], [w['gdn_conv_w'], gp, ng],
            [pltpu.VMEM((n_seg, 8 + t_len, GDN_CONV_CH), F32), pltpu.VMEM((n_seg, 2, 128, 128), F32)])
    return spec, lambda convn, sn: {'gdn_conv': convn, 'gdn': _from_bd(sn)}


def _hgrn_levels(t_len):
    lv, n = [], t_len
    while n >= 2:
        lv.append(n)
        n //= 2
    return lv


def _hgrn_cmat(t_len):
    t = np.arange(CH)[:, None]
    j = np.arange(CH)[None, :]
    same = (t // t_len) == (j // t_len)
    mats = [same & (j <= t), same]
    for n in _hgrn_levels(t_len):
        mid = (t // n) * n + n // 2
        mats.append((t % n >= n // 2) & (j >= mid) & (j <= t))
        mats.append((t % n < n // 2) & (j > t) & (j <= mid - 1))
    return jnp.asarray(np.concatenate(mats, axis=0).astype(np.float32), dtype=BF16)


def _split3(x):
    hi = x.astype(BF16)
    r = x - hi.astype(F32)
    mid = r.astype(BF16)
    return hi, mid, (r - mid.astype(F32)).astype(BF16)


def _hgrn_kernel(x_ref, s0_ref, cm_ref, lb_ref, ng_ref, o_ref, sn_ref, s_scr, *, n_seg, init):
    t_len = CH // n_seg
    levels = _hgrn_levels(t_len)
    if init:
        s_scr[...] = s0_ref[...]
        return None

    lane = lax.broadcasted_iota(jnp.int32, (CH, LANES), 1)
    rowi = lax.broadcasted_iota(jnp.int32, (CH, 1), 0)
    row = lax.broadcasted_iota(jnp.int32, (CH, CH), 0)
    col = lax.broadcasted_iota(jnp.int32, (CH, CH), 1)
    lo = lane < 64
    r128 = lax.broadcasted_iota(jnp.int32, (LANES, LANES), 0)
    l128 = lax.broadcasted_iota(jnp.int32, (LANES, LANES), 1)
    bd = (r128 < 64) == (l128 < 64)

    lb = lb_ref[...]
    f_pre = x_ref[:, HG_QK:2 * HG_QK]
    log_f = jnp.log(lb + (1.0 - lb) * jax.nn.sigmoid(f_pre))
    k_in = (1.0 - lb) * jax.nn.sigmoid(-f_pre)
    cm = cm_ref[...]
    ex = None
    for part in _split3(log_f):
        t = _dot(cm, part)
        ex = t if ex is None else ex + t
    b = ex[0:CH]
    b_end = ex[CH:2 * CH]
    outs = []
    for p in range(2):
        ls = slice(128 * p, 128 * (p + 1))
        q_p = x_ref[:, ls]
        k_p = k_in[:, ls]
        v_p = x_ref[:, 2 * HG_QK + 128 * p:2 * HG_QK + 128 * (p + 1)]
        qk = q_p * k_p
        qe = (q_p * jnp.exp(b[:, ls])).astype(BF16)
        o = _seg_sum([_dot_nt(qe, s_scr[s, p].astype(BF16)) for s in range(n_seg)], rowi, t_len)
        for j in range(2):
            mj = lo if j == 0 else jnp.logical_not(lo)
            diag = jnp.sum(jnp.where(mj, qk, 0.0), axis=-1, keepdims=True)
            att = jnp.where(row == col, diag, 0.0)
            for li, n in enumerate(levels):
                base = 2 * CH + 2 * CH * li
                eq = ex[base:base + CH, ls]
                ek = ex[base + CH:base + 2 * CH, ls]
                tq = (rowi % n) >= (n // 2)
                qt = jnp.where(jnp.logical_and(mj, tq), q_p * jnp.exp(eq), 0.0)
                kt = jnp.where(tq, 0.0, k_p * jnp.exp(ek))
                att = att + jnp.where((row // n) == (col // n), _dot_nt(qt.astype(BF16), kt.astype(BF16)), 0.0)
            o = o + _dot(att.astype(BF16), jnp.where(mj, v_p, 0.0).astype(BF16))
        kw = (k_p * jnp.exp(b_end[:, ls] - b[:, ls])).astype(BF16)
        for s in range(n_seg):
            r0 = s * t_len
            upd = _dot_tn(_seg_rows(v_p, rowi, t_len, s, n_seg).astype(BF16), kw)
            s_scr[s, p] = jnp.exp(b_end[r0:r0 + 1, ls]) * s_scr[s, p] + jnp.where(bd, upd, 0.0)
        outs.append(_head_rmsnorm(o, lo))
    o_all = jnp.concatenate(outs, axis=-1) * ng_ref[...] * jax.nn.silu(x_ref[:, 3 * HG_QK:4 * HG_QK])
    o_ref[...] = o_all.astype(BF16)

    def finish():
        sn_ref[...] = s_scr[...]
    return finish


def hgrn_spec(st, w, lb, n_seg):
    t_len = CH // n_seg
    ng = jnp.tile(w['hgrn_norm'], HG_H).reshape(1, BRANCH_W)
    s0 = _to_bd(jnp.swapaxes(st['hgrn'], -1, -2))
    assert COLS['hg_f'][0] == COLS['hg_q'][0] + HG_QK and COLS['hg_g'][0] == COLS['hg_q'][0] + 3 * HG_QK
    spec = (_hgrn_kernel, [(COLS['hg_q'][0], 4 * HG_QK)], [s0],
            [_hgrn_cmat(t_len), lb.reshape(1, HG_QK), ng], [pltpu.VMEM((n_seg, 2, 128, 128), F32)])
    return spec, lambda sn: {'hgrn': jnp.swapaxes(_from_bd(sn), -1, -2)}


def _ssd_kernel(xin_ref, z_ref, sm_ref, conv0_ref, h0_ref, cw_ref, cb_ref, gp_ref, dvec_ref, ng_ref,
                o_ref, convn_ref, hn_ref, ext_scr, h_scr, *, n_seg, init):
    t_len = CH // n_seg
    if init:
        h_scr[...] = h0_ref[...].reshape(n_seg, 2, 128, SSD_N)
        ext_scr[:, 5:8, :] = conv0_ref[...]
        return None

    xc, tail = _conv_silu(xin_ref[...], ext_scr, cw_ref, cb_ref[...], t_len, n_seg)
    sx, bm, cm = xc[:, :256], xc[:, 256:512], xc[:, 512:768]
    same, tri, _ = _seg_masks(t_len)
    lane = lax.broadcasted_iota(jnp.int32, (CH, LANES), 1)
    rowi = lax.broadcasted_iota(jnp.int32, (CH, 1), 0)
    gl = jnp.logical_and(lane >= L_SSD_DT, lane < L_SSD_DT + SSD_H)
    dt = jnp.where(gl, _softplus(sm_ref[...] + gp_ref[1:2, :]), 0.0)
    da = -jnp.exp(gp_ref[0:1, :]) * dt
    cum = _dot(tri.astype(F32), da, precision=HI)
    cum_end = _dot(same.astype(F32), da, precision=HI)
    cum_r = _row_forms(cum, 16)
    dt_r = _row_forms(dt, 16)
    lo = lane < 64
    rsel = lax.broadcasted_iota(jnp.int32, (LANES, 1), 0) < 64
    ys = []
    for g in range(SSD_G):
        cg = cm[:, 128 * g:128 * (g + 1)].astype(BF16)
        bg = bm[:, 128 * g:128 * (g + 1)].astype(BF16)
        sxp = sx[:, 128 * g:128 * (g + 1)]
        cb = _dot_nt(cg, bg)
        yst = _seg_sum([_dot_nt(cg, h_scr[s, g].astype(BF16)) for s in range(n_seg)], rowi, t_len)
        yatt = jnp.zeros((CH, LANES), F32)
        cols = []
        for j in range(2):
            l = L_SSD_DT + 2 * g + j
            cc = cum[:, l:l + 1]
            dec = _masked_exp(cc - cum_r[l:l + 1, :], tri)
            att = cb * dec * dt_r[l:l + 1, :]
            xm = jnp.where(lo if j == 0 else jnp.logical_not(lo), sxp, 0.0)
            yatt = yatt + _dot(att.astype(BF16), xm.astype(BF16))
            cols.append((cc, dt[:, l:l + 1] * jnp.exp(cum_end[:, l:l + 1] - cc)))
        ys.append(jnp.where(lo, jnp.exp(cols[0][0]), jnp.exp(cols[1][0])) * yst + yatt)
        xw = sxp * jnp.where(lo, cols[0][1], cols[1][1])
        for s in range(n_seg):
            r0 = s * t_len
            l = L_SSD_DT + 2 * g
            e0 = jnp.exp(cum_end[r0:r0 + 1, l:l + 1])
            e1 = jnp.exp(cum_end[r0:r0 + 1, l + 1:l + 2])
            upd = _dot_tn(_seg_rows(xw, rowi, t_len, s, n_seg).astype(BF16), bg)
            h_scr[s, g] = jnp.where(rsel, e0, e1) * h_scr[s, g] + upd
    y_all = jnp.concatenate(ys, axis=-1) + dvec_ref[...] * sx
    o_ref[...] = _rms(y_all * jax.nn.silu(z_ref[...]), ng_ref[...]).astype(BF16)

    def finish():
        convn_ref[...] = tail
        hn_ref[...] = h_scr[...].reshape(n_seg, SSD_H, 64, SSD_N)
    return finish


def ssd_spec(st, w, lb, n_seg):
    t_len = CH // n_seg
    gp = _gate_rows([(L_SSD_DT, w['ssd_a_log']), (L_SSD_DT, w['ssd_dt_bias'])])
    dvec = jnp.repeat(w['ssd_d'], BRANCH_W // SSD_H).reshape(1, BRANCH_W)
    spec = (_ssd_kernel, [COLS['ssd_in'], COLS['ssd_z'], (GATE_COL0, LANES)],
            [st['ssd_conv'], st['ssd']],
            [w['ssd_conv_w'], w['ssd_conv_b'].reshape(1, SSD_CONV_CH), gp, dvec, w['ssd_norm'].reshape(1, BRANCH_W)],
            [pltpu.VMEM((n_seg, 8 + t_len, SSD_CONV_CH), F32), pltpu.VMEM((n_seg, 2, 128, SSD_N), F32)])
    return spec, lambda convn, hn: {'ssd_conv': convn, 'ssd': hn}


def _mlstm_kernel(x_ref, c0_ref, n0_ref, m0_ref, gp_ref, ng_ref, o_ref, cn_ref, nn_ref, mn_ref,
                  c_scr, n_scr, m_scr, *, n_seg, init):
    t_len = CH // n_seg
    if init:
        c_scr[...] = c0_ref[...]
        n_scr[...] = n0_ref[...]
        m_scr[...] = m0_ref[...]
        return None

    same, tri, _ = _seg_masks(t_len)
    lane = lax.broadcasted_iota(jnp.int32, (CH, LANES), 1)
    lane256 = lax.broadcasted_iota(jnp.int32, (CH, BRANCH_W), 1)
    rowi = lax.broadcasted_iota(jnp.int32, (CH, 1), 0)
    r128 = lax.broadcasted_iota(jnp.int32, (LANES, 1), 0)
    neg = jnp.float32(-jnp.inf)

    v_all = x_ref[:, 0:256]
    q_all = x_ref[:, 512:640]
    k_all = x_ref[:, 640:768] * (ML_DK ** -0.5)
    sm = x_ref[:, 768:896]
    ig = sm + gp_ref[0:1, :]
    fl = jnp.logical_and(lane >= L_ML_F, lane < L_ML_F + ML_H)
    lf = jnp.where(fl, -_softplus(-(sm + gp_ref[1:2, :])), 0.0)
    b = _dot(tri.astype(F32), lf, precision=HI)
    b_end = _dot(same.astype(F32), lf, precision=HI)
    b_r = _row_forms(b)
    ig_r = _row_forms(ig)
    mm = m_scr[...]
    qb = q_all.astype(BF16)
    kb = k_all.astype(BF16)
    qn = _seg_sum([_dot(qb, n_scr[s].astype(BF16)) for s in range(n_seg)], rowi, t_len)
    qc = _seg_sum([_dot(qb, c_scr[s].astype(BF16)) for s in range(n_seg)], rowi, t_len)
    num_att = jnp.zeros((CH, BRANCH_W), F32)
    w_ins, dens, w_ends, a_ends, m_ends = [], [], [], [], []
    for h in range(ML_H):
        li, lf_ = L_ML_I + h, L_ML_F + h
        bc = b[:, lf_:lf_ + 1]
        bec = b_end[:, lf_:lf_ + 1]
        igc = ig[:, li:li + 1]
        mmc = mm[:, li:li + 1]
        br = b_r[lf_:lf_ + 1, :]
        igr = ig_r[li:li + 1, :]
        diff = igr - br
        cmx = jnp.max(jnp.where(tri, diff, neg), axis=-1, keepdims=True)
        smx = jnp.max(jnp.where(same, diff, neg), axis=-1, keepdims=True)
        m_c = bc + jnp.maximum(mmc, cmx)
        m_end = bec + jnp.maximum(mmc, smx)
        w_in = jnp.exp(bc + mmc - m_c)
        logw = bc - br + igr - m_c
        mq = jnp.logical_and(lane >= ML_DK * h, lane < ML_DK * (h + 1))
        qk = _dot_nt(jnp.where(mq, q_all, 0.0).astype(BF16), kb)
        wts = _masked_exp(logw, tri) * qk
        mv = jnp.logical_and(lane256 >= 64 * h, lane256 < 64 * (h + 1))
        num_att = num_att + _dot(wts.astype(BF16), jnp.where(mv, v_all, 0.0).astype(BF16))
        nq = w_in * qn[:, li:li + 1] + jnp.sum(wts, axis=-1, keepdims=True)
        w_ins.append(w_in)
        dens.append(jnp.maximum(jnp.abs(nq), jnp.exp(-m_c)))
        w_ends.append(jnp.exp(bec - bc + igc - m_end))
        a_ends.append(jnp.exp(bec + mmc - m_end))
        m_ends.append(m_end)
    num = _quarter_sel(lane256, 64, w_ins) * qc + num_att
    hout = num / _quarter_sel(lane256, 64, dens)
    outs = [_head_rmsnorm(hout[:, 128 * p:128 * (p + 1)], lane < 64) for p in range(2)]
    o_all = jnp.concatenate(outs, axis=-1) * ng_ref[...] * jax.nn.sigmoid(x_ref[:, 256:512])
    o_ref[...] = o_all.astype(BF16)

    kw = k_all * _quarter_sel(lane, ML_DK, w_ends)
    wend_tile = jnp.zeros((CH, LANES), F32)
    m_tile = jnp.zeros((CH, LANES), F32)
    for h in range(ML_H):
        wend_tile = jnp.where(lane == L_ML_I + h, w_ends[h], wend_tile)
        m_tile = jnp.where(lane == L_ML_I + h, m_ends[h], m_tile)
    m_scr[...] = m_tile
    vb = v_all.astype(BF16)
    wb = wend_tile.astype(BF16)
    rc = lax.broadcasted_iota(jnp.int32, (LANES, BRANCH_W), 0)
    lc = lax.broadcasted_iota(jnp.int32, (LANES, BRANCH_W), 1)
    bd_c = (rc // ML_DK) == (lc // 64)
    rn = lax.broadcasted_iota(jnp.int32, (LANES, LANES), 0)
    ln = lax.broadcasted_iota(jnp.int32, (LANES, LANES), 1)
    bd_n = ln == (rn // ML_DK) + L_ML_I
    for s in range(n_seg):
        r0 = s * t_len
        a_sel = _quarter_sel(r128, ML_DK, [a[r0:r0 + 1, :] for a in a_ends])
        upd_c = _dot_tn(_seg_rows(kw, rowi, t_len, s, n_seg).astype(BF16), vb)
        upd_n = _dot_tn(_seg_rows(k_all, rowi, t_len, s, n_seg).astype(BF16), wb)
        c_scr[s] = a_sel * c_scr[s] + jnp.where(bd_c, upd_c, 0.0)
        n_scr[s] = a_sel * n_scr[s] + jnp.where(bd_n, upd_n, 0.0)

    def finish():
        cn_ref[...] = c_scr[...]
        nn_ref[...] = n_scr[...]
        mn_ref[...] = m_tile
    return finish


def mlstm_spec(st, w, lb, n_seg):
    t_len = CH // n_seg
    c0, n0, m0 = st['ml_c'], st['ml_n'], st['ml_m']
    bsz = c0.shape[0]
    gp = _gate_rows([(L_ML_I, w['ml_ig_b']), (L_ML_F, w['ml_fg_b'])])
    ng = jnp.tile(w['ml_norm'], ML_H).reshape(1, BRANCH_W)
    eye = jnp.eye(ML_H, dtype=F32)
    pad = ((0, 0), (L_ML_I, LANES - L_ML_I - ML_H))
    c_bd = jnp.einsum('bhkv,hg->bhkgv', c0, eye).reshape(bsz, ML_QK, BRANCH_W)
    n_bd = jnp.pad(jnp.einsum('bhk,hg->bhkg', n0, eye).reshape(bsz, ML_QK, ML_H), ((0, 0),) + pad)
    m_exp = jnp.pad(jnp.repeat(m0, t_len, axis=0), pad)
    assert (COLS['ml_o'][0], COLS['ml_q'][0], COLS['ml_k'][0], GATE_COL0) == tuple(
        COLS['ml_v'][0] + o for o in (256, 512, 640, 768))
    spec = (_mlstm_kernel, [(COLS['ml_v'][0], 896)], [c_bd, n_bd, m_exp], [gp, ng],
            [pltpu.VMEM((n_seg, ML_QK, BRANCH_W), F32), pltpu.VMEM((n_seg, ML_QK, LANES), F32),
             pltpu.VMEM((CH, LANES), F32)])

    def finish(cn, nn, mn):
        c_new = jnp.einsum('bhkgv,hg->bhkv', cn.reshape(bsz, ML_H, ML_DK, ML_H, 64), eye)
        n_new = jnp.einsum('bhkg,hg->bhk', nn[:, :, L_ML_I:L_ML_I + ML_H].reshape(bsz, ML_H, ML_DK, ML_H), eye)
        return {'ml_c': c_new, 'ml_n': n_new, 'ml_m': mn[::t_len, L_ML_I:L_ML_I + ML_H]}
    return spec, finish


def _merge_kernel(h_ref, gates_ref, br_ref, wb_ref, wo_ref, o_ref):
    merged = None
    for n in range(N_BRANCH):
        y = _dot(br_ref[:, n * BRANCH_W:(n + 1) * BRANCH_W], wb_ref[n])
        t = jax.nn.sigmoid(gates_ref[:, n * D_MODEL:(n + 1) * D_MODEL]) * y
        merged = t if merged is None else merged + t
    o_ref[...] = h_ref[...] + _dot(merged.astype(BF16), wo_ref[...])


def merge(h, proj, br, w_branch, w_out, tm):
    m, d = h.shape
    assert COLS['gates'] == (0, N_BRANCH * D_MODEL)
    return pl.pallas_call(
        _merge_kernel,
        grid=(m // tm,),
        in_specs=[pl.BlockSpec((tm, d), lambda i: (i, 0)),
                  pl.BlockSpec((tm, N_BRANCH * d), lambda i: (i, 0)),
                  pl.BlockSpec((tm, N_BRANCH * BRANCH_W), lambda i: (i, 0)),
                  pl.BlockSpec((N_BRANCH, BRANCH_W, d), lambda i: (0, 0, 0)),
                  pl.BlockSpec((d, d), lambda i: (0, 0))],
        out_specs=pl.BlockSpec((tm, d), lambda i: (i, 0)),
        out_shape=jax.ShapeDtypeStruct((m, d), F32),
        compiler_params=pltpu.CompilerParams(
            dimension_semantics=("parallel",), vmem_limit_bytes=VMEM_LIMIT),
        name="merge",
    )(h, proj, br, w_branch, w_out)


def _ffn_kernel(h_ref, g_ref, wg_ref, wu_ref, wd_ref, o_ref, u_ref):
    f = pl.program_id(1)

    @pl.when(f == 0)
    def _():
        u_ref[...] = _rms(h_ref[...], g_ref[...]).astype(BF16)

    u = u_ref[...]
    a = jax.nn.silu(_dot(u, wg_ref[...]))
    b = _dot(u, wu_ref[...])
    y = _dot((a * b).astype(BF16), wd_ref[...])

    @pl.when(f == 0)
    def _():
        o_ref[...] = h_ref[...] + y

    @pl.when(f != 0)
    def _():
        o_ref[...] += y


def ffn(h, g, wg, wu, wd, tm, tf):
    m, d = h.shape
    ff = wg.shape[1]
    return pl.pallas_call(
        _ffn_kernel,
        grid=(m // tm, ff // tf),
        in_specs=[pl.BlockSpec((tm, d), lambda i, f: (i, 0)),
                  pl.BlockSpec((1, d), lambda i, f: (0, 0)),
                  pl.BlockSpec((d, tf), lambda i, f: (0, f)),
                  pl.BlockSpec((d, tf), lambda i, f: (0, f)),
                  pl.BlockSpec((tf, d), lambda i, f: (f, 0))],
        out_specs=pl.BlockSpec((tm, d), lambda i, f: (i, 0)),
        out_shape=jax.ShapeDtypeStruct((m, d), F32),
        scratch_shapes=[pltpu.VMEM((tm, d), BF16)],
        compiler_params=pltpu.CompilerParams(
            dimension_semantics=("parallel", "arbitrary"), vmem_limit_bytes=VMEM_LIMIT),
        name="ffn",
    )(h, g.reshape(1, d), wg, wu, wd)


def _router_kernel(h_ref, g_ref, wr_ref, u_ref, w_ref, i_ref):
    u = _rms(h_ref[...], g_ref[...])
    u_ref[...] = u
    logits = _dot(u, wr_ref[...], precision=HI)
    lane = lax.broadcasted_iota(jnp.int32, logits.shape, 1)
    neg = jnp.float32(-jnp.inf)
    logits = jnp.where(lane < N_EXPERTS, logits, neg)
    m1 = jnp.max(logits, axis=-1, keepdims=True)
    i1 = jnp.min(jnp.where(logits == m1, lane, LANES), axis=-1, keepdims=True)
    rest = jnp.where(lane == i1, neg, logits)
    m2 = jnp.max(rest, axis=-1, keepdims=True)
    i2 = jnp.min(jnp.where(rest == m2, lane, LANES), axis=-1, keepdims=True)
    e = jnp.exp(m2 - m1)
    den = 1.0 + e
    w_ref[...] = jnp.where(lane == 0, 1.0 / den, jnp.where(lane == 1, e / den, 0.0))
    i_ref[...] = jnp.where(lane == 0, i1, jnp.where(lane == 1, i2, 0))


def router(h, g, w_router, tm):
    m, d = h.shape
    wr = jnp.pad(w_router, ((0, 0), (0, LANES - N_EXPERTS)))
    return pl.pallas_call(
        _router_kernel,
        grid=(m // tm,),
        in_specs=[pl.BlockSpec((tm, d), lambda i: (i, 0)),
                  pl.BlockSpec((1, d), lambda i: (0, 0)),
                  pl.BlockSpec((d, LANES), lambda i: (0, 0))],
        out_specs=[pl.BlockSpec((tm, d), lambda i: (i, 0)),
                   pl.BlockSpec((tm, LANES), lambda i: (i, 0)),
                   pl.BlockSpec((tm, LANES), lambda i: (i, 0))],
        out_shape=[jax.ShapeDtypeStruct((m, d), F32),
                   jax.ShapeDtypeStruct((m, LANES), F32),
                   jax.ShapeDtypeStruct((m, LANES), jnp.int32)],
        compiler_params=pltpu.CompilerParams(
            dimension_semantics=("parallel",), vmem_limit_bytes=VMEM_LIMIT),
        name="router",
    )(h, g.reshape(1, d), wr)


def _row_copies(src_hbm, dst_hbm, idx_ref, base, buf, sem, n_rows, gather, wait):
    def body(r, carry):
        row = idx_ref[base + r]
        if gather:
            cp = pltpu.make_async_copy(src_hbm.at[pl.ds(row, 1)], buf.at[pl.ds(r, 1)], sem)
        else:
            cp = pltpu.make_async_copy(buf.at[pl.ds(r, 1)], dst_hbm.at[pl.ds(row, 1)], sem)
        if wait:
            cp.wait()
        else:
            cp.start()
        return carry
    lax.fori_loop(0, n_rows, body, 0, unroll=8)


def _expert_kernel(te_ref, nt_ref, src_ref, dst_ref, u_hbm, wg_ref, wu_ref, wd_ref, y_hbm,
                   xbuf, xb_ref, acc_ref, obuf, gsem, ssem, *, tm):
    i = pl.program_id(0)
    f = pl.program_id(1)
    last_f = pl.num_programs(1) - 1
    nt = nt_ref[0]
    gather = functools.partial(_row_copies, u_hbm, None, src_ref, gather=True, n_rows=tm)
    scatter = functools.partial(_row_copies, None, y_hbm, dst_ref, buf=obuf, sem=ssem.at[0], gather=False, n_rows=tm)

    @pl.when(jnp.logical_and(i < nt, f == 0))
    def _():
        slot = i % 2

        @pl.when(i == 0)
        def _():
            gather(base=0, buf=xbuf.at[0], sem=gsem.at[0], wait=False)
            obuf[...] = jnp.zeros_like(obuf)
            fill = pltpu.make_async_copy(obuf, y_hbm.at[pl.ds(y_hbm.shape[0] - tm, tm)], ssem.at[0])
            fill.start()
            fill.wait()

        gather(base=i * tm, buf=xbuf.at[slot], sem=gsem.at[slot], wait=True)

        @pl.when(i + 1 < nt)
        def _():
            gather(base=(i + 1) * tm, buf=xbuf.at[1 - slot], sem=gsem.at[1 - slot], wait=False)

        xb_ref[...] = xbuf[slot].astype(BF16)

    @pl.when(i < nt)
    def _():
        x = xb_ref[...]
        a = jax.nn.silu(_dot(x, wg_ref[0]))
        b = _dot(x, wu_ref[0])
        y = _dot((a * b).astype(BF16), wd_ref[0])

        @pl.when(f == 0)
        def _():
            acc_ref[...] = y

        @pl.when(f != 0)
        def _():
            acc_ref[...] += y

        @pl.when(f == last_f)
        def _():
            @pl.when(i > 0)
            def _():
                scatter(base=(i - 1) * tm, wait=True)

            obuf[...] = acc_ref[...]
            scatter(base=i * tm, wait=False)

            @pl.when(i == nt - 1)
            def _():
                scatter(base=i * tm, wait=True)


def experts(u, tile_expert, n_tiles, src_tok, dst_row, n_out_rows, wg, wu, wd, tm, tf):
    d = u.shape[1]
    ff = wg.shape[2]
    n_row_tiles = tile_expert.shape[0]
    wmap = lambda i, f, te, nt, src, dst: (te[i], 0, f)
    grid_spec = pltpu.PrefetchScalarGridSpec(
        num_scalar_prefetch=4,
        grid=(n_row_tiles, ff // tf),
        in_specs=[pl.BlockSpec(memory_space=pl.ANY),
                  pl.BlockSpec((1, d, tf), wmap),
                  pl.BlockSpec((1, d, tf), wmap),
                  pl.BlockSpec((1, tf, d), lambda i, f, te, nt, src, dst: (te[i], f, 0))],
        out_specs=pl.BlockSpec(memory_space=pl.ANY),
        scratch_shapes=[pltpu.VMEM((2, tm, d), F32), pltpu.VMEM((tm, d), BF16), pltpu.VMEM((tm, d), F32),
                        pltpu.VMEM((tm, d), F32), pltpu.SemaphoreType.DMA((2,)), pltpu.SemaphoreType.DMA((1,))],
    )
    return pl.pallas_call(
        functools.partial(_expert_kernel, tm=tm),
        grid_spec=grid_spec,
        out_shape=jax.ShapeDtypeStruct((n_out_rows, d), F32),
        compiler_params=pltpu.CompilerParams(
            dimension_semantics=("arbitrary", "arbitrary"), vmem_limit_bytes=VMEM_LIMIT,
            disable_bounds_checks=True),
        name="experts",
    )(tile_expert, n_tiles, src_tok, dst_row, u, wg, wu, wd)


def moe(h, g, w_router, wg, wu, wd, tm_route, tm_e, tf):
    m, d = h.shape
    u, top_w, top_i = router(h, g, w_router, tm_route)
    n_pairs = TOP_K * m
    flat_e = top_i[:, :TOP_K].reshape(-1)
    onehot = (flat_e[:, None] == jnp.arange(N_EXPERTS, dtype=jnp.int32)[None, :]).astype(jnp.int32)
    rank = jnp.sum((jnp.cumsum(onehot, axis=0) - 1) * onehot, axis=1)
    counts = jnp.sum(onehot, axis=0)
    tiles_per = (counts + tm_e - 1) // tm_e
    tile_end = jnp.cumsum(tiles_per)
    tile_start = tile_end - tiles_per
    grouped_row = tile_start[flat_e] * tm_e + rank
    n_rows = n_pairs + N_EXPERTS * tm_e
    n_row_tiles = n_rows // tm_e
    pair_ids = jnp.arange(n_pairs, dtype=jnp.int32)
    src_tok = jnp.zeros((n_rows,), jnp.int32).at[grouped_row].set(pair_ids // TOP_K)
    spare = n_pairs + jnp.arange(n_rows, dtype=jnp.int32) % tm_e
    dst_row = spare.at[grouped_row].set(pair_ids)
    tile_ids = jnp.arange(n_row_tiles, dtype=jnp.int32)
    tile_expert = jnp.minimum(jnp.sum((tile_ids[:, None] >= tile_end[None, :]).astype(jnp.int32), axis=1),
                              N_EXPERTS - 1).astype(jnp.int32)
    n_tiles = tile_end[-1:].astype(jnp.int32)
    last_e = tile_expert[jnp.maximum(n_tiles[0] - 1, 0)]
    tile_expert = jnp.where(tile_ids < n_tiles[0], tile_expert, last_e)
    y = experts(u, tile_expert, n_tiles, src_tok, dst_row, n_pairs + tm_e, wg, wu, wd, tm_e, tf)
    return y.reshape((n_pairs + tm_e) // TOP_K, TOP_K * d), top_w


def _ple_kernel(*refs, final, combine):
    if combine:
        h_ref, y_ref, tw_ref, p_ref, g_ref, wg_ref, wp_ref, gf_ref, o_ref = refs
        d = h_ref.shape[1]
        h = h_ref[...] + (tw_ref[:, 0:1] * y_ref[:, :d] + tw_ref[:, 1:2] * y_ref[:, d:])
    else:
        h_ref, p_ref, g_ref, wg_ref, wp_ref, gf_ref, o_ref = refs
        h = h_ref[...]
    v = _rms(h, g_ref[...]).astype(BF16)
    pg = jax.nn.sigmoid(_dot(v, wg_ref[...]))
    e = _dot(p_ref[...].astype(BF16), wp_ref[...])
    out = h + pg * e
    if final:
        out = _rms(out, gf_ref[...])
    o_ref[...] = out


def ple(h, p, g, w_gate, w_p, g_final, tm, final, expert_out=None):
    m, d = h.shape
    dp = p.shape[1]
    rows = lambda wd: pl.BlockSpec((tm, wd), lambda i: (i, 0))
    whole = lambda a, b: pl.BlockSpec((a, b), lambda i: (0, 0))
    extra, extra_specs = (), []
    if expert_out is not None:
        extra = expert_out
        extra_specs = [rows(TOP_K * d), rows(LANES)]
    return pl.pallas_call(
        functools.partial(_ple_kernel, final=final, combine=expert_out is not None),
        grid=(m // tm,),
        in_specs=[rows(d)] + extra_specs + [rows(dp), whole(1, d), whole(d, d), whole(dp, d), whole(1, d)],
        out_specs=rows(d),
        out_shape=jax.ShapeDtypeStruct((m, d), F32),
        compiler_params=pltpu.CompilerParams(
            dimension_semantics=("parallel",), vmem_limit_bytes=VMEM_LIMIT),
        name="ple",
    )(h, *extra, p, g.reshape(1, d), w_gate, w_p, g_final.reshape(1, d))


STATE_KEYS = ('gdn_conv', 'gdn', 'hgrn', 'ssd_conv', 'ssd', 'ml_c', 'ml_n', 'ml_m')
MIXER_KEYS = ('gdn_conv_w', 'gdn_a_log', 'gdn_dt_bias', 'gdn_norm', 'hgrn_norm',
              'ssd_conv_w', 'ssd_conv_b', 'ssd_a_log', 'ssd_dt_bias', 'ssd_d', 'ssd_norm',
              'ml_ig_b', 'ml_fg_b', 'ml_norm')


MIXER_SPECS = (gdn_spec, hgrn_spec, ssd_spec, mlstm_spec)


def _mixers(proj, row_blk0, st, w, lb, **grid):
    built = [f(st, w, lb, grid['n_seg']) for f in MIXER_SPECS]
    o, *new_states = _mixer_call([b[0] for b in built], proj, row_blk0, **grid)
    new, at = {}, 0
    for spec, finish in built:
        n = len(spec[2])
        new.update(finish(*new_states[at:at + n]))
        at += n
    return o, new


def kernel(x_prompt, x_sample, state_gdn_conv, state_gdn, state_hgrn, state_ssd_conv, state_ssd, state_mlstm_c, state_mlstm_n, state_mlstm_m, p_prompt, p_sample, g_mix, w_in, gdn_conv_w, gdn_a_log, gdn_dt_bias, gdn_norm, hgrn_lb, hgrn_norm, ssd_conv_w, ssd_conv_b, ssd_a_log, ssd_dt_bias, ssd_d, ssd_norm, ml_ig_b, ml_fg_b, ml_norm, w_branch, w_out, g_ffn, w_ff_gate, w_ff_up, w_ff_down, w_router, w_ex_gate, w_ex_up, w_ex_down, w_ple, w_ple_gate, g_ple, g_final):
    prm = {'gdn_conv_w': gdn_conv_w, 'gdn_a_log': gdn_a_log, 'gdn_dt_bias': gdn_dt_bias,
           'gdn_norm': gdn_norm, 'hgrn_norm': hgrn_norm, 'ssd_conv_w': ssd_conv_w,
           'ssd_conv_b': ssd_conv_b, 'ssd_a_log': ssd_a_log, 'ssd_dt_bias': ssd_dt_bias,
           'ssd_d': ssd_d, 'ssd_norm': ssd_norm, 'ml_ig_b': ml_ig_b, 'ml_fg_b': ml_fg_b,
           'ml_norm': ml_norm}
    bp, lp, d = x_prompt.shape
    bs, ls, _ = x_sample.shape
    mp, ms = bp * lp, bs * ls
    assert lp % CH == 0 and CH % ls == 0 and ms % CH == 0 and mp % CH == 0
    st_s = {'gdn_conv': state_gdn_conv, 'gdn': state_gdn, 'hgrn': state_hgrn, 'ssd_conv': state_ssd_conv,
            'ssd': state_ssd, 'ml_c': state_mlstm_c, 'ml_n': state_mlstm_n, 'ml_m': state_mlstm_m}
    st_p = {k: jnp.zeros((bp,) + v.shape[2:], F32) for k, v in st_s.items()}
    grid_p = dict(n_outer=bp, n_chunks=lp // CH, n_seg=1)
    grid_s = dict(n_outer=ms // CH, n_chunks=1, n_seg=CH // ls)

    sm = jax.nn.softmax(hgrn_lb, axis=0)
    lb_all = jnp.cumsum(sm, axis=0) - sm[0]

    h = jnp.concatenate([x_prompt.reshape(mp, d), x_sample.reshape(ms, d)], axis=0)
    new_p = {k: [] for k in STATE_KEYS}
    new_s = {k: [] for k in STATE_KEYS}
    for l in range(DEPTH):
        wl = {k: prm[k][l] for k in MIXER_KEYS}
        proj = norm_matmul(h, g_mix[l], _permute_w_in(w_in[l]), tm=1024, tn=1152)
        br_p, np_ = _mixers(proj, 0, st_p, wl, lb_all[l], **grid_p)
        br_s, ns_ = _mixers(proj, mp // CH, {k: v[l] for k, v in st_s.items()}, wl, lb_all[l], **grid_s)
        br = jnp.concatenate([br_p, br_s], axis=0)
        h = merge(h, proj, br, w_branch[l].astype(BF16), w_out[l].astype(BF16), tm=512)
        j = l // 2
        expert_out = None
        if l % 2 == 0:
            h = ffn(h, g_ffn[l], w_ff_gate[j].astype(BF16), w_ff_up[j].astype(BF16),
                    w_ff_down[j].astype(BF16), tm=512, tf=D_FF // 2)
        else:
            expert_out = moe(h, g_ffn[l], w_router[j], w_ex_gate[j].astype(BF16), w_ex_up[j].astype(BF16),
                             w_ex_down[j].astype(BF16), tm_route=512, tm_e=512, tf=D_FF // 2)
        p = jnp.concatenate([p_prompt[l].reshape(mp, D_PLE), p_sample[l].reshape(ms, D_PLE)], axis=0)
        h = ple(h, p, g_ple[l], w_ple_gate[l].astype(BF16), w_ple[l].astype(BF16), g_final,
                tm=512, final=(l == DEPTH - 1), expert_out=expert_out)
        for k in STATE_KEYS:
            new_p[k].append(np_[k])
            new_s[k].append(ns_[k])
    y_prompt = h[:mp].reshape(bp, lp, d)
    y_sample = h[mp:].reshape(bs, ls, d)
    sp = {k: jnp.stack(v) for k, v in new_p.items()}
    ss = {k: jnp.stack(v) for k, v in new_s.items()}
    return (y_prompt, y_sample,
            sp['gdn_conv'], sp['gdn'], sp['hgrn'], sp['ssd_conv'], sp['ssd'], sp['ml_c'], sp['ml_n'], sp['ml_m'],
            ss['gdn_conv'], ss['gdn'], ss['hgrn'], ss['ssd_conv'], ss['ssd'], ss['ml_c'], ss['ml_n'], ss['ml_m'])
```

```python
import functools

import numpy as np
import jax
import jax.numpy as jnp
from jax import lax
from jax.experimental import pallas as pl
from jax.experimental.pallas import tpu as pltpu

F32 = jnp.float32
BF16 = jnp.bfloat16
HI = lax.Precision.HIGHEST

D_MODEL = 1024
DEPTH = 2
D_PLE = 256
N_BRANCH = 4
BRANCH_W = D_MODEL // N_BRANCH
CONV_W = 4
EPS = 1e-6

GDN_H = 4
GDN_DK = 64
GDN_QK = GDN_H * GDN_DK
GDN_CONV_CH = 2 * GDN_QK + BRANCH_W
HG_H = 4
HG_QK = 256
SSD_H = 4
SSD_G = 2
SSD_N = 128
SSD_CONV_CH = BRANCH_W + 2 * SSD_G * SSD_N
ML_H = 4
ML_DK = 32
ML_QK = ML_H * ML_DK
D_FF = ((8 * D_MODEL // 3 + 255) // 256) * 256
N_EXPERTS = 8
TOP_K = 2

_REF_SPLITS = (('gdn_in', GDN_CONV_CH), ('gdn_b', GDN_H), ('gdn_a', GDN_H), ('gdn_z', BRANCH_W),
               ('hg_q', HG_QK), ('hg_f', HG_QK), ('hg_v', BRANCH_W), ('hg_g', BRANCH_W),
               ('ssd_z', BRANCH_W), ('ssd_in', SSD_CONV_CH), ('ssd_dt', SSD_H),
               ('ml_q', ML_QK), ('ml_k', ML_QK), ('ml_v', BRANCH_W), ('ml_i', ML_H), ('ml_f', ML_H),
               ('ml_o', BRANCH_W), ('gates', N_BRANCH * D_MODEL))
_MY_ORDER = ('gates', 'gdn_z', 'ssd_z', 'gdn_in', 'ssd_in', 'hg_q', 'hg_f', 'hg_v', 'hg_g',
             'ml_v', 'ml_o', 'ml_q', 'ml_k', 'gdn_b', 'gdn_a', 'ssd_dt', 'ml_i', 'ml_f')
LANES = 128
CH = 64


def _layout():
    widths = dict(_REF_SPLITS)
    off, out = 0, {}
    for name in _MY_ORDER:
        out[name] = (off, widths[name])
        off += widths[name]
    return out, -(-off // LANES) * LANES


COLS, N_PROJ = _layout()
GATE_COL0 = COLS['gdn_b'][0]
L_GDN_B, L_GDN_A, L_SSD_DT, L_ML_I, L_ML_F = (COLS[n][0] - GATE_COL0 for n in ('gdn_b', 'gdn_a', 'ssd_dt', 'ml_i', 'ml_f'))
VMEM_LIMIT = 56 * 1024 * 1024


def _ref_offsets():
    off, acc = {}, 0
    for name, wd in _REF_SPLITS:
        off[name] = acc
        acc += wd
    return off, acc


def _permute_kernel(w_ref, ws_ref, o_ref):
    ref_off, _ = _ref_offsets()
    for name in _MY_ORDER:
        dst, wd = COLS[name]
        if wd >= LANES:
            o_ref[:, dst:dst + wd] = w_ref[:, ref_off[name]:ref_off[name] + wd].astype(BF16)
    o_ref[:, GATE_COL0:GATE_COL0 + LANES] = ws_ref[...].astype(BF16)


def _permute_w_in(w, tk=256):
    ref_off, n_in = _ref_offsets()
    small = [n for n in _MY_ORDER if COLS[n][1] < LANES]
    ws = jnp.concatenate([w[:, ref_off[n]:ref_off[n] + COLS[n][1]] for n in small], axis=1)
    ws = jnp.pad(ws, ((0, 0), (0, LANES - ws.shape[1])))
    d = w.shape[0]
    return pl.pallas_call(
        _permute_kernel, grid=(d // tk,),
        in_specs=[pl.BlockSpec((tk, n_in), lambda i: (i, 0)), pl.BlockSpec((tk, LANES), lambda i: (i, 0))],
        out_specs=pl.BlockSpec((tk, N_PROJ), lambda i: (i, 0)),
        out_shape=jax.ShapeDtypeStruct((d, N_PROJ), BF16),
        compiler_params=pltpu.CompilerParams(dimension_semantics=("parallel",), vmem_limit_bytes=VMEM_LIMIT),
        name="permute_w_in",
    )(w, ws)


def _rms(x, g):
    return x * lax.rsqrt(jnp.mean(x * x, axis=-1, keepdims=True) + EPS) * g


def _dot(a, b, **kw):
    return jnp.dot(a, b, preferred_element_type=F32, **kw)


def _dot_nt(a, b, **kw):
    return lax.dot_general(a, b, (((1,), (1,)), ((), ())), preferred_element_type=F32, **kw)


def _dot_tn(a, b, **kw):
    return lax.dot_general(a, b, (((0,), (0,)), ((), ())), preferred_element_type=F32, **kw)


def _norm_matmul_kernel(x_ref, g_ref, w_ref, o_ref, xn_ref):
    @pl.when(pl.program_id(1) == 0)
    def _():
        xn_ref[...] = _rms(x_ref[...], g_ref[...]).astype(BF16)

    o_ref[...] = _dot(xn_ref[...], w_ref[...])


def norm_matmul(x, g, w, tm, tn):
    m, d = x.shape
    n = w.shape[1]
    return pl.pallas_call(
        _norm_matmul_kernel,
        grid=(m // tm, n // tn),
        in_specs=[pl.BlockSpec((tm, d), lambda i, j: (i, 0)),
                  pl.BlockSpec((1, d), lambda i, j: (0, 0)),
                  pl.BlockSpec((d, tn), lambda i, j: (0, j))],
        out_specs=pl.BlockSpec((tm, tn), lambda i, j: (i, j)),
        out_shape=jax.ShapeDtypeStruct((m, n), F32),
        scratch_shapes=[pltpu.VMEM((tm, d), BF16)],
        compiler_params=pltpu.CompilerParams(
            dimension_semantics=("parallel", "arbitrary"), vmem_limit_bytes=VMEM_LIMIT),
        name="norm_matmul",
    )(x, g.reshape(1, d), w)


def _seg_masks(t_len):
    row = lax.broadcasted_iota(jnp.int32, (CH, CH), 0)
    col = lax.broadcasted_iota(jnp.int32, (CH, CH), 1)
    same = (row // t_len) == (col // t_len)
    tri = jnp.logical_and(same, col <= row)
    strict = jnp.logical_and(same, col < row)
    return same, tri, strict


def _row_forms(x, n_rows=24):
    r = lax.broadcasted_iota(jnp.int32, (n_rows, LANES), 0)
    l = lax.broadcasted_iota(jnp.int32, (n_rows, LANES), 1)
    return _dot_nt((r == l).astype(F32), x, precision=HI)


def _softplus(x):
    return jnp.maximum(x, 0.0) + jnp.log1p(jnp.exp(-jnp.abs(x)))


def _split2(x):
    hi = x.astype(BF16)
    return hi, (x - hi.astype(F32)).astype(BF16)


def _dot3(a, b):
    return _dot(a[0], b[0]) + (_dot(a[0], b[1]) + _dot(a[1], b[0]))


def _masked_exp(d, mask):
    return jnp.where(mask, jnp.exp(jnp.where(mask, d, 0.0)), 0.0)


def _conv_silu(x, ext_scr, cw_ref, bias, t_len, n_seg):
    w = x.shape[-1]
    ext_scr[:, 8:8 + t_len, :] = x.reshape(n_seg, t_len, w)
    y = cw_ref[3:4, :] * x
    for j in range(1, CONV_W):
        y = y + cw_ref[3 - j:4 - j, :] * ext_scr[:, 8 - j:8 - j + t_len, :].reshape(CH, w)
    if bias is not None:
        y = y + bias
    tail = ext_scr[:, 5 + t_len:8 + t_len, :]
    ext_scr[:, 5:8, :] = tail
    return jax.nn.silu(y), tail


def _halves(xp, lo):
    s_lo = jnp.sum(jnp.where(lo, xp, 0.0), axis=-1, keepdims=True)
    s_hi = jnp.sum(jnp.where(lo, 0.0, xp), axis=-1, keepdims=True)
    return jnp.where(lo, s_lo, s_hi)


def _head_rmsnorm(xp, lo):
    return xp * lax.rsqrt(_halves(xp * xp, lo) * (1.0 / 64) + EPS)


def _head_l2norm(xp, lo):
    return xp * lax.rsqrt(_halves(xp * xp, lo) + EPS)


def _seg_sum(parts, rowi, t_len):
    if len(parts) == 1:
        return parts[0]
    acc = jnp.where(rowi // t_len == 0, parts[0], 0.0)
    for s in range(1, len(parts)):
        acc = acc + jnp.where(rowi // t_len == s, parts[s], 0.0)
    return acc


def _seg_rows(x, rowi, t_len, s, n_seg):
    return x if n_seg == 1 else jnp.where(rowi // t_len == s, x, 0.0)


def _quarter_sel(idx, width, vals):
    out = vals[3]
    for h in (2, 1, 0):
        out = jnp.where(idx < (h + 1) * width, vals[h], out)
    return out


def _gate_rows(pairs):
    t = jnp.zeros((8, LANES), F32)
    for r, (off, v) in enumerate(pairs):
        t = t.at[r, off:off + v.shape[0]].set(v.astype(F32))
    return t


def _to_bd(s):
    b, _, a, c = s.shape
    s = s.reshape(b, 2, 2, a, c)
    z = jnp.zeros_like(s[:, :, 0])
    top = jnp.concatenate([s[:, :, 0], z], axis=-1)
    bot = jnp.concatenate([z, s[:, :, 1]], axis=-1)
    return jnp.concatenate([top, bot], axis=-2)


def _from_bd(s):
    b, _, a2, c2 = s.shape
    a, c = a2 // 2, c2 // 2
    return jnp.stack([s[:, :, :a, :c], s[:, :, a:, c:]], axis=2).reshape(b, 4, a, c)


def _interleave(gens, n_stages):
    results = [None] * len(gens)
    pos = [0] * len(gens)
    live = set(range(len(gens)))
    while live:
        k = min(live, key=lambda i: ((pos[i] + 1) / n_stages[i], i))
        try:
            next(gens[k])
            pos[k] += 1
        except StopIteration as stop:
            results[k] = stop.value
            live.remove(k)
    return results


def _fused_mixer_kernel(*refs, n_seg, parts):
    tot = [sum(p[j] for p in parts) for j in range(1, 5)]
    ins, rest = refs[:tot[0]], refs[tot[0]:]
    sts, rest = rest[:tot[1]], rest[tot[1]:]
    prs, rest = rest[:tot[2]], rest[tot[2]:]
    o_ref, rest = rest[0], rest[1:]
    outs, scr = rest[:tot[1]], rest[tot[1]:]
    at = [0, 0, 0, 0]
    calls = []
    for k, (body, n_in, n_st, n_pr, n_scr) in enumerate(parts):
        take = lambda seq, j, n: seq[at[j]:at[j] + n]
        calls.append(functools.partial(
            body, *take(ins, 0, n_in), *take(sts, 1, n_st), *take(prs, 2, n_pr),
            o_ref.at[:, k * BRANCH_W:(k + 1) * BRANCH_W],
            *take(outs, 1, n_st), *take(scr, 3, n_scr), n_seg=n_seg))
        for j, n in enumerate((n_in, n_st, n_pr, n_scr)):
            at[j] += n

    n_stages = [p[0].n_stages(CH // n_seg) for p in parts]

    @pl.when(pl.program_id(1) == 0)
    def _():
        _interleave([call(init=True) for call in calls], n_stages)

    finishers = _interleave([call(init=False) for call in calls], n_stages)

    @pl.when(pl.program_id(1) == pl.num_programs(1) - 1)
    def _():
        for fin in finishers:
            fin()


def _mixer_call(specs, proj, row_blk0, *, n_outer, n_chunks, n_seg):
    rows = n_outer * n_chunks * CH
    rmap = lambda blk: (lambda i, c: (row_blk0 + i * n_chunks + c, blk))
    full = lambda a: pl.BlockSpec(a.shape, lambda i, c: (0,) * a.ndim)
    sblk = lambda a: pl.BlockSpec((a.shape[0] // n_outer,) + a.shape[1:], lambda i, c: (i,) + (0,) * (a.ndim - 1))
    in_blocks = [b for s in specs for b in s[1]]
    state_ins = [a for s in specs for a in s[2]]
    params = [a for s in specs for a in s[3]]
    scratch = [a for s in specs for a in s[4]]
    for off, wd in in_blocks:
        assert off % wd == 0
    parts = tuple((s[0], len(s[1]), len(s[2]), len(s[3]), len(s[4])) for s in specs)
    width = len(specs) * BRANCH_W
    return pl.pallas_call(
        functools.partial(_fused_mixer_kernel, n_seg=n_seg, parts=parts),
        grid=(n_outer, n_chunks),
        in_specs=([pl.BlockSpec((CH, wd), rmap(off // wd)) for off, wd in in_blocks]
                  + [sblk(a) for a in state_ins] + [full(a) for a in params]),
        out_specs=[pl.BlockSpec((CH, width), lambda i, c: (i * n_chunks + c, 0))] + [sblk(a) for a in state_ins],
        out_shape=([jax.ShapeDtypeStruct((rows, width), BF16)]
                   + [jax.ShapeDtypeStruct(a.shape, F32) for a in state_ins]),
        scratch_shapes=scratch,
        compiler_params=pltpu.CompilerParams(
            dimension_semantics=("parallel", "arbitrary"), vmem_limit_bytes=VMEM_LIMIT),
        name="token_mixers",
    )(*([proj] * len(in_blocks)), *state_ins, *params)


def _gdn_kernel(xin_ref, z_ref, sm_ref, conv0_ref, s0_ref, cw_ref, gp_ref, ng_ref,
                o_ref, convn_ref, sn_ref, ext_scr, s_scr, *, n_seg, init):
    t_len = CH // n_seg
    if init:
        s_scr[...] = s0_ref[...]
        ext_scr[:, 5:8, :] = conv0_ref[...]
        return None

    xc, tail = _conv_silu(xin_ref[...], ext_scr, cw_ref, None, t_len, n_seg)
    yield
    same, tri, strict = _seg_masks(t_len)
    lane = lax.broadcasted_iota(jnp.int32, (CH, LANES), 1)
    rowi = lax.broadcasted_iota(jnp.int32, (CH, 1), 0)
    lo = lane < 64
    sm = sm_ref[...]
    beta = jax.nn.sigmoid(sm)
    gl = jnp.logical_and(lane >= L_GDN_A, lane < L_GDN_A + GDN_H)
    g = jnp.where(gl, -jnp.exp(gp_ref[0:1, :]) * _softplus(sm + gp_ref[1:2, :]), 0.0)
    gam = _dot(tri.astype(F32), g, precision=HI)
    gam_end = _dot(same.astype(F32), g, precision=HI)
    gam_r = _row_forms(gam, 8)
    yield
    r128 = lax.broadcasted_iota(jnp.int32, (LANES, LANES), 0)
    l128 = lax.broadcasted_iota(jnp.int32, (LANES, LANES), 1)
    bd = (r128 < 64) == (l128 < 64)
    rsel = lax.broadcasted_iota(jnp.int32, (LANES, 1), 0) < 64
    qs_, ks_, atts_, gcs_, a_, x_ = [], [], [], [], [], []
    for p in range(2):
        q_p = _head_l2norm(xc[:, 128 * p:128 * (p + 1)], lo) * (GDN_DK ** -0.5)
        k_p = _head_l2norm(xc[:, GDN_QK + 128 * p:GDN_QK + 128 * (p + 1)], lo)
        v_p = xc[:, 2 * GDN_QK + 128 * p:2 * GDN_QK + 128 * (p + 1)]
        kb = k_p.astype(BF16)
        qs_.append(q_p)
        ks_.append(k_p)
        for j in range(2):
            h = 2 * p + j
            mj = lo if j == 0 else jnp.logical_not(lo)
            kk = _dot_nt(jnp.where(mj, k_p, 0.0).astype(BF16), kb)
            qk = _dot_nt(jnp.where(mj, q_p, 0.0).astype(BF16), kb)
            gc = gam[:, L_GDN_A + h:L_GDN_A + h + 1]
            bc = beta[:, L_GDN_B + h:L_GDN_B + h + 1]
            dec = _masked_exp(gc - gam_r[L_GDN_A + h:L_GDN_A + h + 1, :], tri)
            a_.append(jnp.where(strict, bc * kk * dec, 0.0))
            x_.append(jnp.concatenate([jnp.where(mj, bc * v_p, 0.0),
                                       jnp.where(mj, (bc * jnp.exp(gc)) * k_p, 0.0)], axis=-1))
            atts_.append(qk * dec)
            gcs_.append(gc)
            yield
    sa = [_split2(a) for a in a_]
    x_ = [x - _dot3(s, _split2(x)) for s, x in zip(sa, x_)]
    yield
    n = 2
    while n < t_len:
        sa = [_split2(_dot3(s, s)) for s in sa]
        yield
        x_ = [x + _dot3(s, _split2(x)) for s, x in zip(sa, x_)]
        yield
        n *= 2
    outs = []
    for p in range(2):
        q_p, k_p = qs_[p], ks_[p]
        solv = x_[2 * p][:, :LANES] + x_[2 * p + 1][:, :LANES]
        solk = x_[2 * p][:, LANES:] + x_[2 * p + 1][:, LANES:]
        atts, gcs = atts_[2 * p:2 * p + 2], gcs_[2 * p:2 * p + 2]
        solk_b = solk.astype(BF16)
        qb = q_p.astype(BF16)
        u = solv - _seg_sum([_dot(solk_b, s_scr[s, p].astype(BF16)) for s in range(n_seg)], rowi, t_len)
        qs = _seg_sum([_dot(qb, s_scr[s, p].astype(BF16)) for s in range(n_seg)], rowi, t_len)
        o = jnp.where(lo, jnp.exp(gcs[0]), jnp.exp(gcs[1])) * qs
        for j in range(2):
            mj = lo if j == 0 else jnp.logical_not(lo)
            o = o + _dot(atts[j].astype(BF16), jnp.where(mj, u, 0.0).astype(BF16))
        yield
        ge0 = gam_end[:, L_GDN_A + 2 * p:L_GDN_A + 2 * p + 1]
        ge1 = gam_end[:, L_GDN_A + 2 * p + 1:L_GDN_A + 2 * p + 2]
        kw = k_p * jnp.where(lo, jnp.exp(ge0 - gcs[0]), jnp.exp(ge1 - gcs[1]))
        ub = u.astype(BF16)
        for s in range(n_seg):
            r0 = s * t_len
            dec_s = jnp.where(rsel, jnp.exp(ge0[r0:r0 + 1, :]), jnp.exp(ge1[r0:r0 + 1, :]))
            upd = _dot_tn(_seg_rows(kw, rowi, t_len, s, n_seg).astype(BF16), ub)
            s_scr[s, p] = dec_s * s_scr[s, p] + jnp.where(bd, upd, 0.0)
        outs.append(_head_rmsnorm(o, lo))
        yield
    o_all = jnp.concatenate(outs, axis=-1) * ng_ref[...] * jax.nn.silu(z_ref[...])
    o_ref[...] = o_all.astype(BF16)

    def finish():
        convn_ref[...] = tail
        sn_ref[...] = s_scr[...]
    return finish


def gdn_spec(st, w, lb, n_seg):
    t_len = CH // n_seg
    gp = _gate_rows([(L_GDN_A, w['gdn_a_log']), (L_GDN_A, w['gdn_dt_bias'])])
    ng = jnp.tile(w['gdn_norm'], GDN_H).reshape(1, BRANCH_W)
    spec = (_gdn_kernel, [COLS['gdn_in'], COLS['gdn_z'], (GATE_COL0, LANES)],
            [st['gdn_conv'], _to_bd(st['gdn'])], [w['gdn_conv_w'], gp, ng],
            [pltpu.VMEM((n_seg, 8 + t_len, GDN_CONV_CH), F32), pltpu.VMEM((n_seg, 2, 128, 128), F32)])
    return spec, lambda convn, sn: {'gdn_conv': convn, 'gdn': _from_bd(sn)}


def _hgrn_levels(t_len):
    lv, n = [], t_len
    while n >= 2:
        lv.append(n)
        n //= 2
    return lv


def _hgrn_cmat(t_len):
    t = np.arange(CH)[:, None]
    j = np.arange(CH)[None, :]
    same = (t // t_len) == (j // t_len)
    mats = [same & (j <= t), same]
    for n in _hgrn_levels(t_len):
        mid = (t // n) * n + n // 2
        second = t % n >= n // 2
        mats.append((second & (j >= mid) & (j <= t)) | (~second & (j > t) & (j <= mid - 1)))
    return jnp.asarray(np.concatenate(mats, axis=0).astype(np.float32), dtype=BF16)


def _split3(x):
    hi = x.astype(BF16)
    r = x - hi.astype(F32)
    mid = r.astype(BF16)
    return hi, mid, (r - mid.astype(F32)).astype(BF16)


def _hgrn_kernel(x_ref, s0_ref, cm_ref, lb_ref, ng_ref, o_ref, sn_ref, s_scr, *, n_seg, init):
    t_len = CH // n_seg
    levels = _hgrn_levels(t_len)
    if init:
        s_scr[...] = s0_ref[...]
        return None

    lane = lax.broadcasted_iota(jnp.int32, (CH, LANES), 1)
    rowi = lax.broadcasted_iota(jnp.int32, (CH, 1), 0)
    row = lax.broadcasted_iota(jnp.int32, (CH, CH), 0)
    col = lax.broadcasted_iota(jnp.int32, (CH, CH), 1)
    lo = lane < 64
    r128 = lax.broadcasted_iota(jnp.int32, (LANES, LANES), 0)
    l128 = lax.broadcasted_iota(jnp.int32, (LANES, LANES), 1)
    bd = (r128 < 64) == (l128 < 64)

    lb = lb_ref[...]
    f_pre = x_ref[:, HG_QK:2 * HG_QK]
    log_f = jnp.log(lb + (1.0 - lb) * jax.nn.sigmoid(f_pre))
    k_in = (1.0 - lb) * jax.nn.sigmoid(-f_pre)
    cm = cm_ref[...]
    ex = None
    for part in _split3(log_f):
        t = _dot(cm, part)
        ex = t if ex is None else ex + t
    b = ex[0:CH]
    b_end = ex[CH:2 * CH]
    yield
    outs = []
    for p in range(2):
        ls = slice(128 * p, 128 * (p + 1))
        q_p = x_ref[:, ls]
        k_p = k_in[:, ls]
        v_p = x_ref[:, 2 * HG_QK + 128 * p:2 * HG_QK + 128 * (p + 1)]
        qk = q_p * k_p
        qe = (q_p * jnp.exp(b[:, ls])).astype(BF16)
        o = _seg_sum([_dot_nt(qe, s_scr[s, p].astype(BF16)) for s in range(n_seg)], rowi, t_len)
        scales = [jnp.exp(ex[(2 + li) * CH:(3 + li) * CH, ls]) for li in range(len(levels))]
        for j in range(2):
            mj = lo if j == 0 else jnp.logical_not(lo)
            diag = jnp.sum(jnp.where(mj, qk, 0.0), axis=-1, keepdims=True)
            att = jnp.where(row == col, diag, 0.0)
            for li, n in enumerate(levels):
                tq = (rowi % n) >= (n // 2)
                qt = jnp.where(jnp.logical_and(mj, tq), q_p * scales[li], 0.0)
                kt = jnp.where(tq, 0.0, k_p * scales[li])
                att = att + jnp.where((row // n) == (col // n), _dot_nt(qt.astype(BF16), kt.astype(BF16)), 0.0)
                if li % 2 == 1:
                    yield
            o = o + _dot(att.astype(BF16), jnp.where(mj, v_p, 0.0).astype(BF16))
            yield
        kw = (k_p * jnp.exp(b_end[:, ls] - b[:, ls])).astype(BF16)
        for s in range(n_seg):
            r0 = s * t_len
            upd = _dot_tn(_seg_rows(v_p, rowi, t_len, s, n_seg).astype(BF16), kw)
            s_scr[s, p] = jnp.exp(b_end[r0:r0 + 1, ls]) * s_scr[s, p] + jnp.where(bd, upd, 0.0)
        outs.append(_head_rmsnorm(o, lo))
    o_all = jnp.concatenate(outs, axis=-1) * ng_ref[...] * jax.nn.silu(x_ref[:, 3 * HG_QK:4 * HG_QK])
    o_ref[...] = o_all.astype(BF16)

    def finish():
        sn_ref[...] = s_scr[...]
    return finish


def hgrn_spec(st, w, lb, n_seg):
    t_len = CH // n_seg
    ng = jnp.tile(w['hgrn_norm'], HG_H).reshape(1, BRANCH_W)
    s0 = _to_bd(jnp.swapaxes(st['hgrn'], -1, -2))
    assert COLS['hg_f'][0] == COLS['hg_q'][0] + HG_QK and COLS['hg_g'][0] == COLS['hg_q'][0] + 3 * HG_QK
    spec = (_hgrn_kernel, [(COLS['hg_q'][0], 4 * HG_QK)], [s0],
            [_hgrn_cmat(t_len), lb.reshape(1, HG_QK), ng], [pltpu.VMEM((n_seg, 2, 128, 128), F32)])
    return spec, lambda sn: {'hgrn': jnp.swapaxes(_from_bd(sn), -1, -2)}


def _ssd_kernel(xin_ref, z_ref, sm_ref, conv0_ref, h0_ref, cw_ref, cb_ref, gp_ref, dvec_ref, ng_ref,
                o_ref, convn_ref, hn_ref, ext_scr, h_scr, *, n_seg, init):
    t_len = CH // n_seg
    if init:
        h_scr[...] = h0_ref[...].reshape(n_seg, 2, 128, SSD_N)
        ext_scr[:, 5:8, :] = conv0_ref[...]
        return None

    xc, tail = _conv_silu(xin_ref[...], ext_scr, cw_ref, cb_ref[...], t_len, n_seg)
    yield
    sx, bm, cm = xc[:, :256], xc[:, 256:512], xc[:, 512:768]
    same, tri, _ = _seg_masks(t_len)
    lane = lax.broadcasted_iota(jnp.int32, (CH, LANES), 1)
    rowi = lax.broadcasted_iota(jnp.int32, (CH, 1), 0)
    gl = jnp.logical_and(lane >= L_SSD_DT, lane < L_SSD_DT + SSD_H)
    dt = jnp.where(gl, _softplus(sm_ref[...] + gp_ref[1:2, :]), 0.0)
    da = -jnp.exp(gp_ref[0:1, :]) * dt
    cum = _dot(tri.astype(F32), da, precision=HI)
    cum_end = _dot(same.astype(F32), da, precision=HI)
    cum_r = _row_forms(cum, 16)
    dt_r = _row_forms(dt, 16)
    yield
    lo = lane < 64
    rsel = lax.broadcasted_iota(jnp.int32, (LANES, 1), 0) < 64
    ys = []
    for g in range(SSD_G):
        cg = cm[:, 128 * g:128 * (g + 1)].astype(BF16)
        bg = bm[:, 128 * g:128 * (g + 1)].astype(BF16)
        sxp = sx[:, 128 * g:128 * (g + 1)]
        cb = _dot_nt(cg, bg)
        yst = _seg_sum([_dot_nt(cg, h_scr[s, g].astype(BF16)) for s in range(n_seg)], rowi, t_len)
        yatt = jnp.zeros((CH, LANES), F32)
        cols = []
        for j in range(2):
            l = L_SSD_DT + 2 * g + j
            cc = cum[:, l:l + 1]
            dec = _masked_exp(cc - cum_r[l:l + 1, :], tri)
            att = cb * dec * dt_r[l:l + 1, :]
            xm = jnp.where(lo if j == 0 else jnp.logical_not(lo), sxp, 0.0)
            yatt = yatt + _dot(att.astype(BF16), xm.astype(BF16))
            cols.append((cc, dt[:, l:l + 1] * jnp.exp(cum_end[:, l:l + 1] - cc)))
            yield
        ys.append(jnp.where(lo, jnp.exp(cols[0][0]), jnp.exp(cols[1][0])) * yst + yatt)
        xw = sxp * jnp.where(lo, cols[0][1], cols[1][1])
        for s in range(n_seg):
            r0 = s * t_len
            l = L_SSD_DT + 2 * g
            e0 = jnp.exp(cum_end[r0:r0 + 1, l:l + 1])
            e1 = jnp.exp(cum_end[r0:r0 + 1, l + 1:l + 2])
            upd = _dot_tn(_seg_rows(xw, rowi, t_len, s, n_seg).astype(BF16), bg)
            h_scr[s, g] = jnp.where(rsel, e0, e1) * h_scr[s, g] + upd
        yield
    y_all = jnp.concatenate(ys, axis=-1) + dvec_ref[...] * sx
    o_ref[...] = _rms(y_all * jax.nn.silu(z_ref[...]), ng_ref[...]).astype(BF16)

    def finish():
        convn_ref[...] = tail
        hn_ref[...] = h_scr[...].reshape(n_seg, SSD_H, 64, SSD_N)
    return finish


def ssd_spec(st, w, lb, n_seg):
    t_len = CH // n_seg
    gp = _gate_rows([(L_SSD_DT, w['ssd_a_log']), (L_SSD_DT, w['ssd_dt_bias'])])
    dvec = jnp.repeat(w['ssd_d'], BRANCH_W // SSD_H).reshape(1, BRANCH_W)
    spec = (_ssd_kernel, [COLS['ssd_in'], COLS['ssd_z'], (GATE_COL0, LANES)],
            [st['ssd_conv'], st['ssd']],
            [w['ssd_conv_w'], w['ssd_conv_b'].reshape(1, SSD_CONV_CH), gp, dvec, w['ssd_norm'].reshape(1, BRANCH_W)],
            [pltpu.VMEM((n_seg, 8 + t_len, SSD_CONV_CH), F32), pltpu.VMEM((n_seg, 2, 128, SSD_N), F32)])
    return spec, lambda convn, hn: {'ssd_conv': convn, 'ssd': hn}


def _mlstm_kernel(x_ref, c0_ref, n0_ref, m0_ref, gp_ref, ng_ref, o_ref, cn_ref, nn_ref, mn_ref,
                  c_scr, n_scr, m_scr, *, n_seg, init):
    t_len = CH // n_seg
    if init:
        c_scr[...] = c0_ref[...]
        n_scr[...] = n0_ref[...]
        m_scr[...] = m0_ref[...]
        return None

    same, tri, _ = _seg_masks(t_len)
    lane = lax.broadcasted_iota(jnp.int32, (CH, LANES), 1)
    lane256 = lax.broadcasted_iota(jnp.int32, (CH, BRANCH_W), 1)
    rowi = lax.broadcasted_iota(jnp.int32, (CH, 1), 0)
    r128 = lax.broadcasted_iota(jnp.int32, (LANES, 1), 0)
    neg = jnp.float32(-jnp.inf)

    v_all = x_ref[:, 0:256]
    q_all = x_ref[:, 512:640]
    k_all = x_ref[:, 640:768] * (ML_DK ** -0.5)
    sm = x_ref[:, 768:896]
    ig = sm + gp_ref[0:1, :]
    fl = jnp.logical_and(lane >= L_ML_F, lane < L_ML_F + ML_H)
    lf = jnp.where(fl, -_softplus(-(sm + gp_ref[1:2, :])), 0.0)
    b = _dot(tri.astype(F32), lf, precision=HI)
    b_end = _dot(same.astype(F32), lf, precision=HI)
    b_r = _row_forms(b)
    ig_r = _row_forms(ig)
    yield
    mm = m_scr[...]
    qb = q_all.astype(BF16)
    kb = k_all.astype(BF16)
    qn = _seg_sum([_dot(qb, n_scr[s].astype(BF16)) for s in range(n_seg)], rowi, t_len)
    qc = _seg_sum([_dot(qb, c_scr[s].astype(BF16)) for s in range(n_seg)], rowi, t_len)
    yield
    num_att = jnp.zeros((CH, BRANCH_W), F32)
    w_ins, dens, w_ends, a_ends, m_ends = [], [], [], [], []
    for h in range(ML_H):
        li, lf_ = L_ML_I + h, L_ML_F + h
        bc = b[:, lf_:lf_ + 1]
        bec = b_end[:, lf_:lf_ + 1]
        igc = ig[:, li:li + 1]
        mmc = mm[:, li:li + 1]
        br = b_r[lf_:lf_ + 1, :]
        igr = ig_r[li:li + 1, :]
        diff = igr - br
        cmx = jnp.max(jnp.where(tri, diff, neg), axis=-1, keepdims=True)
        smx = jnp.max(jnp.where(same, diff, neg), axis=-1, keepdims=True)
        m_c = bc + jnp.maximum(mmc, cmx)
        m_end = bec + jnp.maximum(mmc, smx)
        w_in = jnp.exp(bc + mmc - m_c)
        logw = bc - br + igr - m_c
        mq = jnp.logical_and(lane >= ML_DK * h, lane < ML_DK * (h + 1))
        qk = _dot_nt(jnp.where(mq, q_all, 0.0).astype(BF16), kb)
        wts = _masked_exp(logw, tri) * qk
        yield
        mv = jnp.logical_and(lane256 >= 64 * h, lane256 < 64 * (h + 1))
        num_att = num_att + _dot(wts.astype(BF16), jnp.where(mv, v_all, 0.0).astype(BF16))
        nq = w_in * qn[:, li:li + 1] + jnp.sum(wts, axis=-1, keepdims=True)
        w_ins.append(w_in)
        dens.append(jnp.maximum(jnp.abs(nq), jnp.exp(-m_c)))
        w_ends.append(jnp.exp(bec - bc + igc - m_end))
        a_ends.append(jnp.exp(bec + mmc - m_end))
        m_ends.append(m_end)
        yield
    num = _quarter_sel(lane256, 64, w_ins) * qc + num_att
    hout = num / _quarter_sel(lane256, 64, dens)
    outs = [_head_rmsnorm(hout[:, 128 * p:128 * (p + 1)], lane < 64) for p in range(2)]
    o_all = jnp.concatenate(outs, axis=-1) * ng_ref[...] * jax.nn.sigmoid(x_ref[:, 256:512])
    o_ref[...] = o_all.astype(BF16)
    yield

    kw = k_all * _quarter_sel(lane, ML_DK, w_ends)
    wend_tile = jnp.zeros((CH, LANES), F32)
    m_tile = jnp.zeros((CH, LANES), F32)
    for h in range(ML_H):
        wend_tile = jnp.where(lane == L_ML_I + h, w_ends[h], wend_tile)
        m_tile = jnp.where(lane == L_ML_I + h, m_ends[h], m_tile)
    m_scr[...] = m_tile
    vb = v_all.astype(BF16)
    wb = wend_tile.astype(BF16)
    rc = lax.broadcasted_iota(jnp.int32, (LANES, BRANCH_W), 0)
    lc = lax.broadcasted_iota(jnp.int32, (LANES, BRANCH_W), 1)
    bd_c = (rc // ML_DK) == (lc // 64)
    rn = lax.broadcasted_iota(jnp.int32, (LANES, LANES), 0)
    ln = lax.broadcasted_iota(jnp.int32, (LANES, LANES), 1)
    bd_n = ln == (rn // ML_DK) + L_ML_I
    for s in range(n_seg):
        r0 = s * t_len
        a_sel = _quarter_sel(r128, ML_DK, [a[r0:r0 + 1, :] for a in a_ends])
        upd_c = _dot_tn(_seg_rows(kw, rowi, t_len, s, n_seg).astype(BF16), vb)
        upd_n = _dot_tn(_seg_rows(k_all, rowi, t_len, s, n_seg).astype(BF16), wb)
        c_scr[s] = a_sel * c_scr[s] + jnp.where(bd_c, upd_c, 0.0)
        n_scr[s] = a_sel * n_scr[s] + jnp.where(bd_n, upd_n, 0.0)

    def finish():
        cn_ref[...] = c_scr[...]
        nn_ref[...] = n_scr[...]
        mn_ref[...] = m_tile
    return finish


def mlstm_spec(st, w, lb, n_seg):
    t_len = CH // n_seg
    c0, n0, m0 = st['ml_c'], st['ml_n'], st['ml_m']
    bsz = c0.shape[0]
    gp = _gate_rows([(L_ML_I, w['ml_ig_b']), (L_ML_F, w['ml_fg_b'])])
    ng = jnp.tile(w['ml_norm'], ML_H).reshape(1, BRANCH_W)
    eye = jnp.eye(ML_H, dtype=F32)
    pad = ((0, 0), (L_ML_I, LANES - L_ML_I - ML_H))
    c_bd = jnp.einsum('bhkv,hg->bhkgv', c0, eye).reshape(bsz, ML_QK, BRANCH_W)
    n_bd = jnp.pad(jnp.einsum('bhk,hg->bhkg', n0, eye).reshape(bsz, ML_QK, ML_H), ((0, 0),) + pad)
    m_exp = jnp.pad(jnp.repeat(m0, t_len, axis=0), pad)
    assert (COLS['ml_o'][0], COLS['ml_q'][0], COLS['ml_k'][0], GATE_COL0) == tuple(
        COLS['ml_v'][0] + o for o in (256, 512, 640, 768))
    spec = (_mlstm_kernel, [(COLS['ml_v'][0], 896)], [c_bd, n_bd, m_exp], [gp, ng],
            [pltpu.VMEM((n_seg, ML_QK, BRANCH_W), F32), pltpu.VMEM((n_seg, ML_QK, LANES), F32),
             pltpu.VMEM((CH, LANES), F32)])

    def finish(cn, nn, mn):
        c_new = jnp.einsum('bhkgv,hg->bhkv', cn.reshape(bsz, ML_H, ML_DK, ML_H, 64), eye)
        n_new = jnp.einsum('bhkg,hg->bhk', nn[:, :, L_ML_I:L_ML_I + ML_H].reshape(bsz, ML_H, ML_DK, ML_H), eye)
        return {'ml_c': c_new, 'ml_n': n_new, 'ml_m': mn[::t_len, L_ML_I:L_ML_I + ML_H]}
    return spec, finish


def _merge_kernel(h_ref, gates_ref, br_ref, wb_ref, wo_ref, o_ref):
    merged = None
    for n in range(N_BRANCH):
        y = _dot(br_ref[:, n * BRANCH_W:(n + 1) * BRANCH_W], wb_ref[n])
        t = jax.nn.sigmoid(gates_ref[:, n * D_MODEL:(n + 1) * D_MODEL]) * y
        merged = t if merged is None else merged + t
    o_ref[...] = h_ref[...] + _dot(merged.astype(BF16), wo_ref[...])


def merge(h, proj, br, w_branch, w_out, tm):
    m, d = h.shape
    assert COLS['gates'] == (0, N_BRANCH * D_MODEL)
    return pl.pallas_call(
        _merge_kernel,
        grid=(m // tm,),
        in_specs=[pl.BlockSpec((tm, d), lambda i: (i, 0)),
                  pl.BlockSpec((tm, N_BRANCH * d), lambda i: (i, 0)),
                  pl.BlockSpec((tm, N_BRANCH * BRANCH_W), lambda i: (i, 0)),
                  pl.BlockSpec((N_BRANCH, BRANCH_W, d), lambda i: (0, 0, 0)),
                  pl.BlockSpec((d, d), lambda i: (0, 0))],
        out_specs=pl.BlockSpec((tm, d), lambda i: (i, 0)),
        out_shape=jax.ShapeDtypeStruct((m, d), F32),
        compiler_params=pltpu.CompilerParams(
            dimension_semantics=("parallel",), vmem_limit_bytes=VMEM_LIMIT),
        name="merge",
    )(h, proj, br, w_branch, w_out)


def _ffn_kernel(h_ref, g_ref, wg_ref, wu_ref, wd_ref, o_ref, u_ref):
    f = pl.program_id(1)

    @pl.when(f == 0)
    def _():
        u_ref[...] = _rms(h_ref[...], g_ref[...]).astype(BF16)

    u = u_ref[...]
    a = jax.nn.silu(_dot(u, wg_ref[...]))
    b = _dot(u, wu_ref[...])
    y = _dot((a * b).astype(BF16), wd_ref[...])

    @pl.when(f == 0)
    def _():
        o_ref[...] = h_ref[...] + y

    @pl.when(f != 0)
    def _():
        o_ref[...] += y


def ffn(h, g, wg, wu, wd, tm, tf):
    m, d = h.shape
    ff = wg.shape[1]
    return pl.pallas_call(
        _ffn_kernel,
        grid=(m // tm, ff // tf),
        in_specs=[pl.BlockSpec((tm, d), lambda i, f: (i, 0)),
                  pl.BlockSpec((1, d), lambda i, f: (0, 0)),
                  pl.BlockSpec((d, tf), lambda i, f: (0, f)),
                  pl.BlockSpec((d, tf), lambda i, f: (0, f)),
                  pl.BlockSpec((tf, d), lambda i, f: (f, 0))],
        out_specs=pl.BlockSpec((tm, d), lambda i, f: (i, 0)),
        out_shape=jax.ShapeDtypeStruct((m, d), F32),
        scratch_shapes=[pltpu.VMEM((tm, d), BF16)],
        compiler_params=pltpu.CompilerParams(
            dimension_semantics=("parallel", "arbitrary"), vmem_limit_bytes=VMEM_LIMIT),
        name="ffn",
    )(h, g.reshape(1, d), wg, wu, wd)


def _router_kernel(h_ref, g_ref, wr_ref, u_ref, w_ref, i_ref):
    u = _rms(h_ref[...], g_ref[...])
    u_ref[...] = u
    logits = _dot(u, wr_ref[...], precision=HI)
    lane = lax.broadcasted_iota(jnp.int32, logits.shape, 1)
    neg = jnp.float32(-jnp.inf)
    logits = jnp.where(lane < N_EXPERTS, logits, neg)
    m1 = jnp.max(logits, axis=-1, keepdims=True)
    i1 = jnp.min(jnp.where(logits == m1, lane, LANES), axis=-1, keepdims=True)
    rest = jnp.where(lane == i1, neg, logits)
    m2 = jnp.max(rest, axis=-1, keepdims=True)
    i2 = jnp.min(jnp.where(rest == m2, lane, LANES), axis=-1, keepdims=True)
    e = jnp.exp(m2 - m1)
    den = 1.0 + e
    w_ref[...] = jnp.where(lane == 0, 1.0 / den, jnp.where(lane == 1, e / den, 0.0))
    i_ref[...] = jnp.where(lane == 0, i1, jnp.where(lane == 1, i2, 0))


def router(h, g, w_router, tm):
    m, d = h.shape
    wr = jnp.pad(w_router, ((0, 0), (0, LANES - N_EXPERTS)))
    return pl.pallas_call(
        _router_kernel,
        grid=(m // tm,),
        in_specs=[pl.BlockSpec((tm, d), lambda i: (i, 0)),
                  pl.BlockSpec((1, d), lambda i: (0, 0)),
                  pl.BlockSpec((d, LANES), lambda i: (0, 0))],
        out_specs=[pl.BlockSpec((tm, d), lambda i: (i, 0)),
                   pl.BlockSpec((tm, LANES), lambda i: (i, 0)),
                   pl.BlockSpec((tm, LANES), lambda i: (i, 0))],
        out_shape=[jax.ShapeDtypeStruct((m, d), F32),
                   jax.ShapeDtypeStruct((m, LANES), F32),
                   jax.ShapeDtypeStruct((m, LANES), jnp.int32)],
        compiler_params=pltpu.CompilerParams(
            dimension_semantics=("parallel",), vmem_limit_bytes=VMEM_LIMIT),
        name="router",
    )(h, g.reshape(1, d), wr)


def _row_copies(src_hbm, dst_hbm, idx_ref, base, buf, sem, n_rows, gather, wait):
    def body(r, carry):
        row = idx_ref[base + r]
        if gather:
            cp = pltpu.make_async_copy(src_hbm.at[pl.ds(row, 1)], buf.at[pl.ds(r, 1)], sem)
        else:
            cp = pltpu.make_async_copy(buf.at[pl.ds(r, 1)], dst_hbm.at[pl.ds(row, 1)], sem)
        if wait:
            cp.wait()
        else:
            cp.start()
        return carry
    lax.fori_loop(0, n_rows, body, 0, unroll=8)


def _expert_kernel(te_ref, nt_ref, src_ref, dst_ref, u_hbm, wg_ref, wu_ref, wd_ref, y_hbm,
                   xbuf, xb_ref, acc_ref, obuf, gsem, ssem, *, tm):
    i = pl.program_id(0)
    f = pl.program_id(1)
    last_f = pl.num_programs(1) - 1
    nt = nt_ref[0]
    gather = functools.partial(_row_copies, u_hbm, None, src_ref, gather=True, n_rows=tm)
    scatter = functools.partial(_row_copies, None, y_hbm, dst_ref, buf=obuf, sem=ssem.at[0], gather=False, n_rows=tm)

    @pl.when(jnp.logical_and(i < nt, f == 0))
    def _():
        slot = i % 2

        @pl.when(i == 0)
        def _():
            gather(base=0, buf=xbuf.at[0], sem=gsem.at[0], wait=False)
            obuf[...] = jnp.zeros_like(obuf)
            fill = pltpu.make_async_copy(obuf, y_hbm.at[pl.ds(y_hbm.shape[0] - tm, tm)], ssem.at[0])
            fill.start()
            fill.wait()

        gather(base=i * tm, buf=xbuf.at[slot], sem=gsem.at[slot], wait=True)

        @pl.when(i + 1 < nt)
        def _():
            gather(base=(i + 1) * tm, buf=xbuf.at[1 - slot], sem=gsem.at[1 - slot], wait=False)

        xb_ref[...] = xbuf[slot].astype(BF16)

    @pl.when(i < nt)
    def _():
        x = xb_ref[...]
        a = jax.nn.silu(_dot(x, wg_ref[0]))
        b = _dot(x, wu_ref[0])
        y = _dot((a * b).astype(BF16), wd_ref[0])

        @pl.when(f == 0)
        def _():
            acc_ref[...] = y

        @pl.when(f != 0)
        def _():
            acc_ref[...] += y

        @pl.when(f == last_f)
        def _():
            @pl.when(i > 0)
            def _():
                scatter(base=(i - 1) * tm, wait=True)

            obuf[...] = acc_ref[...]
            scatter(base=i * tm, wait=False)

            @pl.when(i == nt - 1)
            def _():
                scatter(base=i * tm, wait=True)


def experts(u, tile_expert, n_tiles, src_tok, dst_row, n_out_rows, wg, wu, wd, tm, tf):
    d = u.shape[1]
    ff = wg.shape[2]
    n_row_tiles = tile_expert.shape[0]
    wmap = lambda i, f, te, nt, src, dst: (te[i], 0, f)
    grid_spec = pltpu.PrefetchScalarGridSpec(
        num_scalar_prefetch=4,
        grid=(n_row_tiles, ff // tf),
        in_specs=[pl.BlockSpec(memory_space=pl.ANY),
                  pl.BlockSpec((1, d, tf), wmap),
                  pl.BlockSpec((1, d, tf), wmap),
                  pl.BlockSpec((1, tf, d), lambda i, f, te, nt, src, dst: (te[i], f, 0))],
        out_specs=pl.BlockSpec(memory_space=pl.ANY),
        scratch_shapes=[pltpu.VMEM((2, tm, d), F32), pltpu.VMEM((tm, d), BF16), pltpu.VMEM((tm, d), F32),
                        pltpu.VMEM((tm, d), F32), pltpu.SemaphoreType.DMA((2,)), pltpu.SemaphoreType.DMA((1,))],
    )
    return pl.pallas_call(
        functools.partial(_expert_kernel, tm=tm),
        grid_spec=grid_spec,
        out_shape=jax.ShapeDtypeStruct((n_out_rows, d), F32),
        compiler_params=pltpu.CompilerParams(
            dimension_semantics=("arbitrary", "arbitrary"), vmem_limit_bytes=VMEM_LIMIT,
            disable_bounds_checks=True),
        name="experts",
    )(tile_expert, n_tiles, src_tok, dst_row, u, wg, wu, wd)


def moe(h, g, w_router, wg, wu, wd, tm_route, tm_e, tf):
    m, d = h.shape
    u, top_w, top_i = router(h, g, w_router, tm_route)
    n_pairs = TOP_K * m
    flat_e = top_i[:, :TOP_K].reshape(-1)
    onehot = (flat_e[:, None] == jnp.arange(N_EXPERTS, dtype=jnp.int32)[None, :]).astype(jnp.int32)
    rank = jnp.sum((jnp.cumsum(onehot, axis=0) - 1) * onehot, axis=1)
    counts = jnp.sum(onehot, axis=0)
    tiles_per = (counts + tm_e - 1) // tm_e
    tile_end = jnp.cumsum(tiles_per)
    tile_start = tile_end - tiles_per
    grouped_row = tile_start[flat_e] * tm_e + rank
    n_rows = n_pairs + N_EXPERTS * tm_e
    n_row_tiles = n_rows // tm_e
    pair_ids = jnp.arange(n_pairs, dtype=jnp.int32)
    src_tok = jnp.zeros((n_rows,), jnp.int32).at[grouped_row].set(pair_ids // TOP_K)
    spare = n_pairs + jnp.arange(n_rows, dtype=jnp.int32) % tm_e
    dst_row = spare.at[grouped_row].set((pair_ids % TOP_K) * m + pair_ids // TOP_K)
    tile_ids = jnp.arange(n_row_tiles, dtype=jnp.int32)
    tile_expert = jnp.minimum(jnp.sum((tile_ids[:, None] >= tile_end[None, :]).astype(jnp.int32), axis=1),
                              N_EXPERTS - 1).astype(jnp.int32)
    n_tiles = tile_end[-1:].astype(jnp.int32)
    last_e = tile_expert[jnp.maximum(n_tiles[0] - 1, 0)]
    tile_expert = jnp.where(tile_ids < n_tiles[0], tile_expert, last_e)
    y = experts(u, tile_expert, n_tiles, src_tok, dst_row, n_pairs + tm_e, wg, wu, wd, tm_e, tf)
    return y, top_w


def _ple_kernel(*refs, final, combine):
    if combine:
        h_ref, y0_ref, y1_ref, tw_ref, p_ref, g_ref, wg_ref, wp_ref, gf_ref, o_ref = refs
        h = h_ref[...] + (tw_ref[:, 0:1] * y0_ref[...] + tw_ref[:, 1:2] * y1_ref[...])
    else:
        h_ref, p_ref, g_ref, wg_ref, wp_ref, gf_ref, o_ref = refs
        h = h_ref[...]
    v = _rms(h, g_ref[...]).astype(BF16)
    pg = jax.nn.sigmoid(_dot(v, wg_ref[...]))
    e = _dot(p_ref[...].astype(BF16), wp_ref[...])
    out = h + pg * e
    if final:
        out = _rms(out, gf_ref[...])
    o_ref[...] = out


def ple(h, p, g, w_gate, w_p, g_final, tm, final, expert_out=None):
    m, d = h.shape
    dp = p.shape[1]
    rows = lambda wd: pl.BlockSpec((tm, wd), lambda i: (i, 0))
    whole = lambda a, b: pl.BlockSpec((a, b), lambda i: (0, 0))
    extra, extra_specs = (), []
    if expert_out is not None:
        y, top_w = expert_out
        extra = (y, y, top_w)
        extra_specs = [rows(d), pl.BlockSpec((tm, d), lambda i: (i + m // tm, 0)), rows(LANES)]
    return pl.pallas_call(
        functools.partial(_ple_kernel, final=final, combine=expert_out is not None),
        grid=(m // tm,),
        in_specs=[rows(d)] + extra_specs + [rows(dp), whole(1, d), whole(d, d), whole(dp, d), whole(1, d)],
        out_specs=rows(d),
        out_shape=jax.ShapeDtypeStruct((m, d), F32),
        compiler_params=pltpu.CompilerParams(
            dimension_semantics=("parallel",), vmem_limit_bytes=VMEM_LIMIT),
        name="ple",
    )(h, *extra, p, g.reshape(1, d), w_gate, w_p, g_final.reshape(1, d))


STATE_KEYS = ('gdn_conv', 'gdn', 'hgrn', 'ssd_conv', 'ssd', 'ml_c', 'ml_n', 'ml_m')
MIXER_KEYS = ('gdn_conv_w', 'gdn_a_log', 'gdn_dt_bias', 'gdn_norm', 'hgrn_norm',
              'ssd_conv_w', 'ssd_conv_b', 'ssd_a_log', 'ssd_dt_bias', 'ssd_d', 'ssd_norm',
              'ml_ig_b', 'ml_fg_b', 'ml_norm')


MIXER_SPECS = (gdn_spec, hgrn_spec, ssd_spec, mlstm_spec)
_gdn_kernel.n_stages = lambda t_len: 11 + 2 * (t_len.bit_length() - 2)
_hgrn_kernel.n_stages = lambda t_len: 1 + HG_H * (len(_hgrn_levels(t_len)) // 2 + 1)
_ssd_kernel.n_stages = lambda t_len: 8
_mlstm_kernel.n_stages = lambda t_len: 11


def _mixers(proj, row_blk0, st, w, lb, **grid):
    built = [f(st, w, lb, grid['n_seg']) for f in MIXER_SPECS]
    o, *new_states = _mixer_call([b[0] for b in built], proj, row_blk0, **grid)
    new, at = {}, 0
    for spec, finish in built:
        n = len(spec[2])
        new.update(finish(*new_states[at:at + n]))
        at += n
    return o, new


def kernel(x_prompt, x_sample, state_gdn_conv, state_gdn, state_hgrn, state_ssd_conv, state_ssd, state_mlstm_c, state_mlstm_n, state_mlstm_m, p_prompt, p_sample, g_mix, w_in, gdn_conv_w, gdn_a_log, gdn_dt_bias, gdn_norm, hgrn_lb, hgrn_norm, ssd_conv_w, ssd_conv_b, ssd_a_log, ssd_dt_bias, ssd_d, ssd_norm, ml_ig_b, ml_fg_b, ml_norm, w_branch, w_out, g_ffn, w_ff_gate, w_ff_up, w_ff_down, w_router, w_ex_gate, w_ex_up, w_ex_down, w_ple, w_ple_gate, g_ple, g_final):
    prm = {'gdn_conv_w': gdn_conv_w, 'gdn_a_log': gdn_a_log, 'gdn_dt_bias': gdn_dt_bias,
           'gdn_norm': gdn_norm, 'hgrn_norm': hgrn_norm, 'ssd_conv_w': ssd_conv_w,
           'ssd_conv_b': ssd_conv_b, 'ssd_a_log': ssd_a_log, 'ssd_dt_bias': ssd_dt_bias,
           'ssd_d': ssd_d, 'ssd_norm': ssd_norm, 'ml_ig_b': ml_ig_b, 'ml_fg_b': ml_fg_b,
           'ml_norm': ml_norm}
    bp, lp, d = x_prompt.shape
    bs, ls, _ = x_sample.shape
    mp, ms = bp * lp, bs * ls
    assert lp % CH == 0 and CH % ls == 0 and ms % CH == 0 and mp % CH == 0
    st_s = {'gdn_conv': state_gdn_conv, 'gdn': state_gdn, 'hgrn': state_hgrn, 'ssd_conv': state_ssd_conv,
            'ssd': state_ssd, 'ml_c': state_mlstm_c, 'ml_n': state_mlstm_n, 'ml_m': state_mlstm_m}
    st_p = {k: jnp.zeros((bp,) + v.shape[2:], F32) for k, v in st_s.items()}
    grid_p = dict(n_outer=bp, n_chunks=lp // CH, n_seg=1)
    grid_s = dict(n_outer=ms // CH, n_chunks=1, n_seg=CH // ls)

    sm = jax.nn.softmax(hgrn_lb, axis=0)
    lb_all = jnp.cumsum(sm, axis=0) - sm[0]

    h = jnp.concatenate([x_prompt.reshape(mp, d), x_sample.reshape(ms, d)], axis=0)
    new_p = {k: [] for k in STATE_KEYS}
    new_s = {k: [] for k in STATE_KEYS}
    for l in range(DEPTH):
        wl = {k: prm[k][l] for k in MIXER_KEYS}
        proj = norm_matmul(h, g_mix[l], _permute_w_in(w_in[l]), tm=1024, tn=1152)
        br_p, np_ = _mixers(proj, 0, st_p, wl, lb_all[l], **grid_p)
        br_s, ns_ = _mixers(proj, mp // CH, {k: v[l] for k, v in st_s.items()}, wl, lb_all[l], **grid_s)
        br = jnp.concatenate([br_p, br_s], axis=0)
        h = merge(h, proj, br, w_branch[l].astype(BF16), w_out[l].astype(BF16), tm=512)
        j = l // 2
        expert_out = None
        if l % 2 == 0:
            h = ffn(h, g_ffn[l], w_ff_gate[j].astype(BF16), w_ff_up[j].astype(BF16),
                    w_ff_down[j].astype(BF16), tm=512, tf=D_FF // 2)
        else:
            expert_out = moe(h, g_ffn[l], w_router[j], w_ex_gate[j].astype(BF16), w_ex_up[j].astype(BF16),
                             w_ex_down[j].astype(BF16), tm_route=512, tm_e=512, tf=D_FF // 2)
        p = jnp.concatenate([p_prompt[l].reshape(mp, D_PLE), p_sample[l].reshape(ms, D_PLE)], axis=0)
        h = ple(h, p, g_ple[l], w_ple_gate[l].astype(BF16), w_ple[l].astype(BF16), g_final,
                tm=512, final=(l == DEPTH - 1), expert_out=expert_out)
        for k in STATE_KEYS:
            new_p[k].append(np_[k])
            new_s[k].append(ns_[k])
    y_prompt = h[:mp].reshape(bp, lp, d)
    y_sample = h[mp:].reshape(bs, ls, d)
    sp = {k: jnp.stack(v) for k, v in new_p.items()}
    ss = {k: jnp.stack(v) for k, v in new_s.items()}
    return (y_prompt, y_sample,
            sp['gdn_conv'], sp['gdn'], sp['hgrn'], sp['ssd_conv'], sp['ssd'], sp['ml_c'], sp['ml_n'], sp['ml_m'],
            ss['gdn_conv'], ss['gdn'], ss['hgrn'], ss['ssd_conv'], ss['ssd'], ss['ml_c'], ss['ml_n'], ss['ml_m'])
```

```python
import functools

import numpy as np
import jax
import jax.numpy as jnp
from jax import lax
from jax.experimental import pallas as pl
from jax.experimental.pallas import tpu as pltpu

F32 = jnp.float32
BF16 = jnp.bfloat16
HI = lax.Precision.HIGHEST

D_MODEL = 1024
DEPTH = 2
D_PLE = 256
N_BRANCH = 4
BRANCH_W = D_MODEL // N_BRANCH
CONV_W = 4
EPS = 1e-6

GDN_H = 4
GDN_DK = 64
GDN_QK = GDN_H * GDN_DK
GDN_CONV_CH = 2 * GDN_QK + BRANCH_W
HG_H = 4
HG_QK = 256
SSD_H = 4
SSD_G = 2
SSD_N = 128
SSD_CONV_CH = BRANCH_W + 2 * SSD_G * SSD_N
ML_H = 4
ML_DK = 32
ML_QK = ML_H * ML_DK
D_FF = ((8 * D_MODEL // 3 + 255) // 256) * 256
N_EXPERTS = 8
TOP_K = 2

_REF_SPLITS = (('gdn_in', GDN_CONV_CH), ('gdn_b', GDN_H), ('gdn_a', GDN_H), ('gdn_z', BRANCH_W),
               ('hg_q', HG_QK), ('hg_f', HG_QK), ('hg_v', BRANCH_W), ('hg_g', BRANCH_W),
               ('ssd_z', BRANCH_W), ('ssd_in', SSD_CONV_CH), ('ssd_dt', SSD_H),
               ('ml_q', ML_QK), ('ml_k', ML_QK), ('ml_v', BRANCH_W), ('ml_i', ML_H), ('ml_f', ML_H),
               ('ml_o', BRANCH_W), ('gates', N_BRANCH * D_MODEL))
_MY_ORDER = ('gdn_in', 'ssd_in', 'gdn_z', 'ssd_z', 'hg_q', 'hg_f', 'hg_v', 'hg_g',
             'ml_v', 'ml_o', 'ml_q', 'ml_k', 'gdn_b', 'gdn_a', 'ssd_dt', 'ml_i', 'ml_f')
LANES = 128
MXU_TILE = 256
CH = 64
N_GATES = N_BRANCH * D_MODEL


def _layout():
    widths = dict(_REF_SPLITS)
    off, out = 0, {}
    for name in _MY_ORDER:
        out[name] = (off, widths[name])
        off += widths[name]
    return out, -(-off // MXU_TILE) * MXU_TILE


COLS, N_MIX = _layout()
GATE_COL0 = COLS['gdn_b'][0]
L_GDN_B, L_GDN_A, L_SSD_DT, L_ML_I, L_ML_F = (COLS[n][0] - GATE_COL0 for n in ('gdn_b', 'gdn_a', 'ssd_dt', 'ml_i', 'ml_f'))
VMEM_LIMIT = 56 * 1024 * 1024


def _ref_offsets():
    off, acc = {}, 0
    for name, wd in _REF_SPLITS:
        off[name] = acc
        acc += wd
    return off, acc


def _permute_kernel(w_ref, ws_ref, o_ref):
    ref_off, _ = _ref_offsets()
    o_ref[:, :N_GATES] = w_ref[:, ref_off['gates']:ref_off['gates'] + N_GATES].astype(BF16)
    for name in _MY_ORDER:
        dst, wd = COLS[name]
        if wd >= LANES:
            o_ref[:, N_GATES + dst:N_GATES + dst + wd] = w_ref[:, ref_off[name]:ref_off[name] + wd].astype(BF16)
    o_ref[:, N_GATES + GATE_COL0:N_GATES + GATE_COL0 + LANES] = ws_ref[...].astype(BF16)
    pad0 = N_GATES + GATE_COL0 + LANES
    o_ref[:, pad0:] = jnp.zeros((o_ref.shape[0], N_GATES + N_MIX - pad0), BF16)


def _permute_w_in(w, tk=256):
    ref_off, n_in = _ref_offsets()
    small = [n for n in _MY_ORDER if COLS[n][1] < LANES]
    ws = jnp.concatenate([w[:, ref_off[n]:ref_off[n] + COLS[n][1]] for n in small], axis=1)
    ws = jnp.pad(ws, ((0, 0), (0, LANES - ws.shape[1])))
    d = w.shape[0]
    return pl.pallas_call(
        _permute_kernel, grid=(d // tk,),
        in_specs=[pl.BlockSpec((tk, n_in), lambda i: (i, 0)), pl.BlockSpec((tk, LANES), lambda i: (i, 0))],
        out_specs=pl.BlockSpec((tk, N_GATES + N_MIX), lambda i: (i, 0)),
        out_shape=jax.ShapeDtypeStruct((d, N_GATES + N_MIX), BF16),
        compiler_params=pltpu.CompilerParams(dimension_semantics=("parallel",), vmem_limit_bytes=VMEM_LIMIT),
        name="permute_w_in",
    )(w, ws)


def _rms(x, g):
    return x * lax.rsqrt(jnp.mean(x * x, axis=-1, keepdims=True) + EPS) * g


def _dot(a, b, **kw):
    return jnp.dot(a, b, preferred_element_type=F32, **kw)


def _dot_nt(a, b, **kw):
    return lax.dot_general(a, b, (((1,), (1,)), ((), ())), preferred_element_type=F32, **kw)


def _dot_tn(a, b, **kw):
    return lax.dot_general(a, b, (((0,), (0,)), ((), ())), preferred_element_type=F32, **kw)


def _norm_matmul_kernel(x_ref, g_ref, w_ref, gates_ref, mix_ref, xn_ref, *, n_gate_tiles):
    j = pl.program_id(1)

    @pl.when(j == 0)
    def _():
        xn_ref[...] = _rms(x_ref[...], g_ref[...]).astype(BF16)

    y = _dot(xn_ref[...], w_ref[...])

    @pl.when(j < n_gate_tiles)
    def _():
        gates_ref[...] = y.astype(BF16)

    @pl.when(j >= n_gate_tiles)
    def _():
        mix_ref[...] = y


def norm_matmul(x, g, w, tm, tn):
    m, d = x.shape
    assert w.shape[1] == N_GATES + N_MIX and N_GATES % tn == 0 and N_MIX % tn == 0
    ng = N_GATES // tn
    return pl.pallas_call(
        functools.partial(_norm_matmul_kernel, n_gate_tiles=ng),
        grid=(m // tm, (N_GATES + N_MIX) // tn),
        in_specs=[pl.BlockSpec((tm, d), lambda i, j: (i, 0)),
                  pl.BlockSpec((1, d), lambda i, j: (0, 0)),
                  pl.BlockSpec((d, tn), lambda i, j: (0, j))],
        out_specs=[pl.BlockSpec((tm, tn), lambda i, j: (i, jnp.minimum(j, ng - 1))),
                   pl.BlockSpec((tm, tn), lambda i, j: (i, jnp.maximum(j - ng, 0)))],
        out_shape=[jax.ShapeDtypeStruct((m, N_GATES), BF16), jax.ShapeDtypeStruct((m, N_MIX), F32)],
        scratch_shapes=[pltpu.VMEM((tm, d), BF16)],
        compiler_params=pltpu.CompilerParams(
            dimension_semantics=("parallel", "arbitrary"), vmem_limit_bytes=VMEM_LIMIT),
        name="norm_matmul",
    )(x, g.reshape(1, d), w)


def _seg_masks(t_len):
    row = lax.broadcasted_iota(jnp.int32, (CH, CH), 0)
    col = lax.broadcasted_iota(jnp.int32, (CH, CH), 1)
    same = (row // t_len) == (col // t_len)
    tri = jnp.logical_and(same, col <= row)
    strict = jnp.logical_and(same, col < row)
    return same, tri, strict


def _row_forms(x, n_rows=24):
    r = lax.broadcasted_iota(jnp.int32, (n_rows, LANES), 0)
    l = lax.broadcasted_iota(jnp.int32, (n_rows, LANES), 1)
    return _dot_nt((r == l).astype(F32), x, precision=HI)


def _softplus(x):
    return jnp.maximum(x, 0.0) + jnp.log1p(jnp.exp(-jnp.abs(x)))


def _split2(x):
    hi = x.astype(BF16)
    return hi, (x - hi.astype(F32)).astype(BF16)


def _dot3(a, b):
    return _dot(a[0], b[0]) + (_dot(a[0], b[1]) + _dot(a[1], b[0]))


def _masked_exp(d, mask):
    return jnp.where(mask, jnp.exp(jnp.where(mask, d, 0.0)), 0.0)


def _conv_silu(x, ext_scr, cw_ref, bias, t_len, n_seg):
    w = x.shape[-1]
    ext_scr[:, 8:8 + t_len, :] = x.reshape(n_seg, t_len, w)
    y = cw_ref[3:4, :] * x
    for j in range(1, CONV_W):
        y = y + cw_ref[3 - j:4 - j, :] * ext_scr[:, 8 - j:8 - j + t_len, :].reshape(CH, w)
    if bias is not None:
        y = y + bias
    tail = ext_scr[:, 5 + t_len:8 + t_len, :]
    ext_scr[:, 5:8, :] = tail
    return jax.nn.silu(y), tail


def _halves(xp, lo):
    s_lo = jnp.sum(jnp.where(lo, xp, 0.0), axis=-1, keepdims=True)
    s_hi = jnp.sum(jnp.where(lo, 0.0, xp), axis=-1, keepdims=True)
    return jnp.where(lo, s_lo, s_hi)


def _head_rmsnorm(xp, lo):
    return xp * lax.rsqrt(_halves(xp * xp, lo) * (1.0 / 64) + EPS)


def _head_l2norm(xp, lo):
    return xp * lax.rsqrt(_halves(xp * xp, lo) + EPS)


def _seg_sum(parts, rowi, t_len):
    if len(parts) == 1:
        return parts[0]
    acc = jnp.where(rowi // t_len == 0, parts[0], 0.0)
    for s in range(1, len(parts)):
        acc = acc + jnp.where(rowi // t_len == s, parts[s], 0.0)
    return acc


def _seg_rows(x, rowi, t_len, s, n_seg):
    return x if n_seg == 1 else jnp.where(rowi // t_len == s, x, 0.0)


def _quarter_sel(idx, width, vals):
    out = vals[3]
    for h in (2, 1, 0):
        out = jnp.where(idx < (h + 1) * width, vals[h], out)
    return out


def _gate_rows(pairs):
    t = jnp.zeros((8, LANES), F32)
    for r, (off, v) in enumerate(pairs):
        t = t.at[r, off:off + v.shape[0]].set(v.astype(F32))
    return t


def _to_bd(s):
    b, _, a, c = s.shape
    s = s.reshape(b, 2, 2, a, c)
    z = jnp.zeros_like(s[:, :, 0])
    top = jnp.concatenate([s[:, :, 0], z], axis=-1)
    bot = jnp.concatenate([z, s[:, :, 1]], axis=-1)
    return jnp.concatenate([top, bot], axis=-2)


def _from_bd(s):
    b, _, a2, c2 = s.shape
    a, c = a2 // 2, c2 // 2
    return jnp.stack([s[:, :, :a, :c], s[:, :, a:, c:]], axis=2).reshape(b, 4, a, c)


def _interleave(gens, n_stages):
    results = [None] * len(gens)
    pos = [0] * len(gens)
    live = set(range(len(gens)))
    while live:
        k = min(live, key=lambda i: ((pos[i] + 1) / n_stages[i], i))
        try:
            next(gens[k])
            pos[k] += 1
        except StopIteration as stop:
            results[k] = stop.value
            live.remove(k)
    return results


def _fused_mixer_kernel(*refs, n_seg, parts):
    tot = [sum(p[j] for p in parts) for j in range(1, 5)]
    ins, rest = refs[:tot[0]], refs[tot[0]:]
    sts, rest = rest[:tot[1]], rest[tot[1]:]
    prs, rest = rest[:tot[2]], rest[tot[2]:]
    o_ref, rest = rest[0], rest[1:]
    outs, scr = rest[:tot[1]], rest[tot[1]:]
    at = [0, 0, 0, 0]
    calls = []
    for k, (body, n_in, n_st, n_pr, n_scr) in enumerate(parts):
        take = lambda seq, j, n: seq[at[j]:at[j] + n]
        calls.append(functools.partial(
            body, *take(ins, 0, n_in), *take(sts, 1, n_st), *take(prs, 2, n_pr),
            o_ref.at[:, k * BRANCH_W:(k + 1) * BRANCH_W],
            *take(outs, 1, n_st), *take(scr, 3, n_scr), n_seg=n_seg))
        for j, n in enumerate((n_in, n_st, n_pr, n_scr)):
            at[j] += n

    n_stages = [p[0].n_stages(CH // n_seg) for p in parts]

    @pl.when(pl.program_id(1) == 0)
    def _():
        _interleave([call(init=True) for call in calls], n_stages)

    finishers = _interleave([call(init=False) for call in calls], n_stages)

    @pl.when(pl.program_id(1) == pl.num_programs(1) - 1)
    def _():
        for fin in finishers:
            fin()


def _mixer_call(specs, proj, row_blk0, *, n_outer, n_chunks, n_seg):
    rows = n_outer * n_chunks * CH
    rmap = lambda blk: (lambda i, c: (row_blk0 + i * n_chunks + c, blk))
    full = lambda a: pl.BlockSpec(a.shape, lambda i, c: (0,) * a.ndim)
    sblk = lambda a: pl.BlockSpec((a.shape[0] // n_outer,) + a.shape[1:], lambda i, c: (i,) + (0,) * (a.ndim - 1))
    in_blocks = [b for s in specs for b in s[1]]
    state_ins = [a for s in specs for a in s[2]]
    params = [a for s in specs for a in s[3]]
    scratch = [a for s in specs for a in s[4]]
    for off, wd in in_blocks:
        assert off % wd == 0
    parts = tuple((s[0], len(s[1]), len(s[2]), len(s[3]), len(s[4])) for s in specs)
    width = len(specs) * BRANCH_W
    return pl.pallas_call(
        functools.partial(_fused_mixer_kernel, n_seg=n_seg, parts=parts),
        grid=(n_outer, n_chunks),
        in_specs=([pl.BlockSpec((CH, wd), rmap(off // wd)) for off, wd in in_blocks]
                  + [sblk(a) for a in state_ins] + [full(a) for a in params]),
        out_specs=[pl.BlockSpec((CH, width), lambda i, c: (i * n_chunks + c, 0))] + [sblk(a) for a in state_ins],
        out_shape=([jax.ShapeDtypeStruct((rows, width), BF16)]
                   + [jax.ShapeDtypeStruct(a.shape, F32) for a in state_ins]),
        scratch_shapes=scratch,
        compiler_params=pltpu.CompilerParams(
            dimension_semantics=("parallel", "arbitrary"), vmem_limit_bytes=VMEM_LIMIT),
        name="token_mixers",
    )(*([proj] * len(in_blocks)), *state_ins, *params)


def _gdn_kernel(xin_ref, z_ref, sm_ref, conv0_ref, s0_ref, cw_ref, gp_ref, ng_ref,
                o_ref, convn_ref, sn_ref, ext_scr, s_scr, *, n_seg, init):
    t_len = CH // n_seg
    if init:
        s_scr[...] = s0_ref[...]
        ext_scr[:, 5:8, :] = conv0_ref[...]
        return None

    xc, tail = _conv_silu(xin_ref[...], ext_scr, cw_ref, None, t_len, n_seg)
    yield
    same, tri, strict = _seg_masks(t_len)
    lane = lax.broadcasted_iota(jnp.int32, (CH, LANES), 1)
    rowi = lax.broadcasted_iota(jnp.int32, (CH, 1), 0)
    lo = lane < 64
    sm = sm_ref[...]
    beta = jax.nn.sigmoid(sm)
    gl = jnp.logical_and(lane >= L_GDN_A, lane < L_GDN_A + GDN_H)
    g = jnp.where(gl, -jnp.exp(gp_ref[0:1, :]) * _softplus(sm + gp_ref[1:2, :]), 0.0)
    gam = _dot(tri.astype(F32), g, precision=HI)
    gam_end = _dot(same.astype(F32), g, precision=HI)
    gam_r = _row_forms(gam, 8)
    yield
    r128 = lax.broadcasted_iota(jnp.int32, (LANES, LANES), 0)
    l128 = lax.broadcasted_iota(jnp.int32, (LANES, LANES), 1)
    bd = (r128 < 64) == (l128 < 64)
    rsel = lax.broadcasted_iota(jnp.int32, (LANES, 1), 0) < 64
    qs_, ks_, atts_, gcs_, a_, x_ = [], [], [], [], [], []
    for p in range(2):
        q_p = _head_l2norm(xc[:, 128 * p:128 * (p + 1)], lo) * (GDN_DK ** -0.5)
        k_p = _head_l2norm(xc[:, GDN_QK + 128 * p:GDN_QK + 128 * (p + 1)], lo)
        v_p = xc[:, 2 * GDN_QK + 128 * p:2 * GDN_QK + 128 * (p + 1)]
        kb = k_p.astype(BF16)
        qs_.append(q_p)
        ks_.append(k_p)
        for j in range(2):
            h = 2 * p + j
            mj = lo if j == 0 else jnp.logical_not(lo)
            kk = _dot_nt(jnp.where(mj, k_p, 0.0).astype(BF16), kb)
            qk = _dot_nt(jnp.where(mj, q_p, 0.0).astype(BF16), kb)
            gc = gam[:, L_GDN_A + h:L_GDN_A + h + 1]
            bc = beta[:, L_GDN_B + h:L_GDN_B + h + 1]
            dec = _masked_exp(gc - gam_r[L_GDN_A + h:L_GDN_A + h + 1, :], tri)
            a_.append(jnp.where(strict, bc * kk * dec, 0.0))
            x_.append(jnp.concatenate([jnp.where(mj, bc * v_p, 0.0),
                                       jnp.where(mj, (bc * jnp.exp(gc)) * k_p, 0.0)], axis=-1))
            atts_.append(qk * dec)
            gcs_.append(gc)
            yield
    sa = [_split2(a) for a in a_]
    x_ = [x - _dot3(s, _split2(x)) for s, x in zip(sa, x_)]
    yield
    n = 2
    while n < t_len:
        sa = [_split2(_dot3(s, s)) for s in sa]
        yield
        x_ = [x + _dot3(s, _split2(x)) for s, x in zip(sa, x_)]
        yield
        n *= 2
    outs = []
    for p in range(2):
        q_p, k_p = qs_[p], ks_[p]
        solv = x_[2 * p][:, :LANES] + x_[2 * p + 1][:, :LANES]
        solk = x_[2 * p][:, LANES:] + x_[2 * p + 1][:, LANES:]
        atts, gcs = atts_[2 * p:2 * p + 2], gcs_[2 * p:2 * p + 2]
        solk_b = solk.astype(BF16)
        qb = q_p.astype(BF16)
        u = solv - _seg_sum([_dot(solk_b, s_scr[s, p].astype(BF16)) for s in range(n_seg)], rowi, t_len)
        qs = _seg_sum([_dot(qb, s_scr[s, p].astype(BF16)) for s in range(n_seg)], rowi, t_len)
        o = jnp.where(lo, jnp.exp(gcs[0]), jnp.exp(gcs[1])) * qs
        for j in range(2):
            mj = lo if j == 0 else jnp.logical_not(lo)
            o = o + _dot(atts[j].astype(BF16), jnp.where(mj, u, 0.0).astype(BF16))
        yield
        ge0 = gam_end[:, L_GDN_A + 2 * p:L_GDN_A + 2 * p + 1]
        ge1 = gam_end[:, L_GDN_A + 2 * p + 1:L_GDN_A + 2 * p + 2]
        kw = k_p * jnp.where(lo, jnp.exp(ge0 - gcs[0]), jnp.exp(ge1 - gcs[1]))
        ub = u.astype(BF16)
        for s in range(n_seg):
            r0 = s * t_len
            dec_s = jnp.where(rsel, jnp.exp(ge0[r0:r0 + 1, :]), jnp.exp(ge1[r0:r0 + 1, :]))
            upd = _dot_tn(_seg_rows(kw, rowi, t_len, s, n_seg).astype(BF16), ub)
            s_scr[s, p] = dec_s * s_scr[s, p] + jnp.where(bd, upd, 0.0)
        outs.append(_head_rmsnorm(o, lo))
        yield
    o_all = jnp.concatenate(outs, axis=-1) * ng_ref[...] * jax.nn.silu(z_ref[...])
    o_ref[...] = o_all.astype(BF16)

    def finish():
        convn_ref[...] = tail
        sn_ref[...] = s_scr[...]
    return finish


def gdn_spec(st, w, lb, n_seg):
    t_len = CH // n_seg
    gp = _gate_rows([(L_GDN_A, w['gdn_a_log']), (L_GDN_A, w['gdn_dt_bias'])])
    ng = jnp.tile(w['gdn_norm'], GDN_H).reshape(1, BRANCH_W)
    spec = (_gdn_kernel, [COLS['gdn_in'], COLS['gdn_z'], (GATE_COL0, LANES)],
            [st['gdn_conv'], _to_bd(st['gdn'])], [w['gdn_conv_w'], gp, ng],
            [pltpu.VMEM((n_seg, 8 + t_len, GDN_CONV_CH), F32), pltpu.VMEM((n_seg, 2, 128, 128), F32)])
    return spec, lambda convn, sn: {'gdn_conv': convn, 'gdn': _from_bd(sn)}


def _hgrn_levels(t_len):
    lv, n = [], t_len
    while n >= 2:
        lv.append(n)
        n //= 2
    return lv


def _hgrn_cmat(t_len):
    t = np.arange(CH)[:, None]
    j = np.arange(CH)[None, :]
    same = (t // t_len) == (j // t_len)
    mats = [same & (j <= t), same]
    for n in _hgrn_levels(t_len):
        mid = (t // n) * n + n // 2
        second = t % n >= n // 2
        mats.append((second & (j >= mid) & (j <= t)) | (~second & (j > t) & (j <= mid - 1)))
    return jnp.asarray(np.concatenate(mats, axis=0).astype(np.float32), dtype=BF16)


def _split3(x):
    hi = x.astype(BF16)
    r = x - hi.astype(F32)
    mid = r.astype(BF16)
    return hi, mid, (r - mid.astype(F32)).astype(BF16)


def _hgrn_kernel(x_ref, s0_ref, cm_ref, lb_ref, ng_ref, o_ref, sn_ref, s_scr, *, n_seg, init):
    t_len = CH // n_seg
    levels = _hgrn_levels(t_len)
    if init:
        s_scr[...] = s0_ref[...]
        return None

    lane = lax.broadcasted_iota(jnp.int32, (CH, LANES), 1)
    rowi = lax.broadcasted_iota(jnp.int32, (CH, 1), 0)
    row = lax.broadcasted_iota(jnp.int32, (CH, CH), 0)
    col = lax.broadcasted_iota(jnp.int32, (CH, CH), 1)
    lo = lane < 64
    r128 = lax.broadcasted_iota(jnp.int32, (LANES, LANES), 0)
    l128 = lax.broadcasted_iota(jnp.int32, (LANES, LANES), 1)
    bd = (r128 < 64) == (l128 < 64)

    lb = lb_ref[...]
    f_pre = x_ref[:, HG_QK:2 * HG_QK]
    log_f = jnp.log(lb + (1.0 - lb) * jax.nn.sigmoid(f_pre))
    k_in = (1.0 - lb) * jax.nn.sigmoid(-f_pre)
    cm = cm_ref[...]
    ex = None
    for part in _split3(log_f):
        t = _dot(cm, part)
        ex = t if ex is None else ex + t
    b = ex[0:CH]
    b_end = ex[CH:2 * CH]
    yield
    outs = []
    for p in range(2):
        ls = slice(128 * p, 128 * (p + 1))
        q_p = x_ref[:, ls]
        k_p = k_in[:, ls]
        v_p = x_ref[:, 2 * HG_QK + 128 * p:2 * HG_QK + 128 * (p + 1)]
        qk = q_p * k_p
        qe = (q_p * jnp.exp(b[:, ls])).astype(BF16)
        o = _seg_sum([_dot_nt(qe, s_scr[s, p].astype(BF16)) for s in range(n_seg)], rowi, t_len)
        scales = [jnp.exp(ex[(2 + li) * CH:(3 + li) * CH, ls]) for li in range(len(levels))]
        for j in range(2):
            mj = lo if j == 0 else jnp.logical_not(lo)
            diag = jnp.sum(jnp.where(mj, qk, 0.0), axis=-1, keepdims=True)
            att = jnp.where(row == col, diag, 0.0)
            for li, n in enumerate(levels):
                tq = (rowi % n) >= (n // 2)
                qt = jnp.where(jnp.logical_and(mj, tq), q_p * scales[li], 0.0)
                kt = jnp.where(tq, 0.0, k_p * scales[li])
                att = att + jnp.where((row // n) == (col // n), _dot_nt(qt.astype(BF16), kt.astype(BF16)), 0.0)
                if li % 2 == 1:
                    yield
            o = o + _dot(att.astype(BF16), jnp.where(mj, v_p, 0.0).astype(BF16))
            yield
        kw = (k_p * jnp.exp(b_end[:, ls] - b[:, ls])).astype(BF16)
        for s in range(n_seg):
            r0 = s * t_len
            upd = _dot_tn(_seg_rows(v_p, rowi, t_len, s, n_seg).astype(BF16), kw)
            s_scr[s, p] = jnp.exp(b_end[r0:r0 + 1, ls]) * s_scr[s, p] + jnp.where(bd, upd, 0.0)
        outs.append(_head_rmsnorm(o, lo))
    o_all = jnp.concatenate(outs, axis=-1) * ng_ref[...] * jax.nn.silu(x_ref[:, 3 * HG_QK:4 * HG_QK])
    o_ref[...] = o_all.astype(BF16)

    def finish():
        sn_ref[...] = s_scr[...]
    return finish


def hgrn_spec(st, w, lb, n_seg):
    t_len = CH // n_seg
    ng = jnp.tile(w['hgrn_norm'], HG_H).reshape(1, BRANCH_W)
    s0 = _to_bd(jnp.swapaxes(st['hgrn'], -1, -2))
    assert COLS['hg_f'][0] == COLS['hg_q'][0] + HG_QK and COLS['hg_g'][0] == COLS['hg_q'][0] + 3 * HG_QK
    spec = (_hgrn_kernel, [(COLS['hg_q'][0], 4 * HG_QK)], [s0],
            [_hgrn_cmat(t_len), lb.reshape(1, HG_QK), ng], [pltpu.VMEM((n_seg, 2, 128, 128), F32)])
    return spec, lambda sn: {'hgrn': jnp.swapaxes(_from_bd(sn), -1, -2)}


def _ssd_kernel(xin_ref, z_ref, sm_ref, conv0_ref, h0_ref, cw_ref, cb_ref, gp_ref, dvec_ref, ng_ref,
                o_ref, convn_ref, hn_ref, ext_scr, h_scr, *, n_seg, init):
    t_len = CH // n_seg
    if init:
        h_scr[...] = h0_ref[...].reshape(n_seg, 2, 128, SSD_N)
        ext_scr[:, 5:8, :] = conv0_ref[...]
        return None

    xc, tail = _conv_silu(xin_ref[...], ext_scr, cw_ref, cb_ref[...], t_len, n_seg)
    yield
    sx, bm, cm = xc[:, :256], xc[:, 256:512], xc[:, 512:768]
    same, tri, _ = _seg_masks(t_len)
    lane = lax.broadcasted_iota(jnp.int32, (CH, LANES), 1)
    rowi = lax.broadcasted_iota(jnp.int32, (CH, 1), 0)
    gl = jnp.logical_and(lane >= L_SSD_DT, lane < L_SSD_DT + SSD_H)
    dt = jnp.where(gl, _softplus(sm_ref[...] + gp_ref[1:2, :]), 0.0)
    da = -jnp.exp(gp_ref[0:1, :]) * dt
    cum = _dot(tri.astype(F32), da, precision=HI)
    cum_end = _dot(same.astype(F32), da, precision=HI)
    cum_r = _row_forms(cum, 16)
    dt_r = _row_forms(dt, 16)
    yield
    lo = lane < 64
    rsel = lax.broadcasted_iota(jnp.int32, (LANES, 1), 0) < 64
    ys = []
    for g in range(SSD_G):
        cg = cm[:, 128 * g:128 * (g + 1)].astype(BF16)
        bg = bm[:, 128 * g:128 * (g + 1)].astype(BF16)
        sxp = sx[:, 128 * g:128 * (g + 1)]
        cb = _dot_nt(cg, bg)
        yst = _seg_sum([_dot_nt(cg, h_scr[s, g].astype(BF16)) for s in range(n_seg)], rowi, t_len)
        yatt = jnp.zeros((CH, LANES), F32)
        cols = []
        for j in range(2):
            l = L_SSD_DT + 2 * g + j
            cc = cum[:, l:l + 1]
            dec = _masked_exp(cc - cum_r[l:l + 1, :], tri)
            att = cb * dec * dt_r[l:l + 1, :]
            xm = jnp.where(lo if j == 0 else jnp.logical_not(lo), sxp, 0.0)
            yatt = yatt + _dot(att.astype(BF16), xm.astype(BF16))
            cols.append((cc, dt[:, l:l + 1] * jnp.exp(cum_end[:, l:l + 1] - cc)))
            yield
        ys.append(jnp.where(lo, jnp.exp(cols[0][0]), jnp.exp(cols[1][0])) * yst + yatt)
        xw = sxp * jnp.where(lo, cols[0][1], cols[1][1])
        for s in range(n_seg):
            r0 = s * t_len
            l = L_SSD_DT + 2 * g
            e0 = jnp.exp(cum_end[r0:r0 + 1, l:l + 1])
            e1 = jnp.exp(cum_end[r0:r0 + 1, l + 1:l + 2])
            upd = _dot_tn(_seg_rows(xw, rowi, t_len, s, n_seg).astype(BF16), bg)
            h_scr[s, g] = jnp.where(rsel, e0, e1) * h_scr[s, g] + upd
        yield
    y_all = jnp.concatenate(ys, axis=-1) + dvec_ref[...] * sx
    o_ref[...] = _rms(y_all * jax.nn.silu(z_ref[...]), ng_ref[...]).astype(BF16)

    def finish():
        convn_ref[...] = tail
        hn_ref[...] = h_scr[...].reshape(n_seg, SSD_H, 64, SSD_N)
    return finish


def ssd_spec(st, w, lb, n_seg):
    t_len = CH // n_seg
    gp = _gate_rows([(L_SSD_DT, w['ssd_a_log']), (L_SSD_DT, w['ssd_dt_bias'])])
    dvec = jnp.repeat(w['ssd_d'], BRANCH_W // SSD_H).reshape(1, BRANCH_W)
    spec = (_ssd_kernel, [COLS['ssd_in'], COLS['ssd_z'], (GATE_COL0, LANES)],
            [st['ssd_conv'], st['ssd']],
            [w['ssd_conv_w'], w['ssd_conv_b'].reshape(1, SSD_CONV_CH), gp, dvec, w['ssd_norm'].reshape(1, BRANCH_W)],
            [pltpu.VMEM((n_seg, 8 + t_len, SSD_CONV_CH), F32), pltpu.VMEM((n_seg, 2, 128, SSD_N), F32)])
    return spec, lambda convn, hn: {'ssd_conv': convn, 'ssd': hn}


def _mlstm_kernel(vo_ref, qk_ref, sm_ref, c0_ref, n0_ref, m0_ref, gp_ref, ng_ref, o_ref, cn_ref, nn_ref, mn_ref,
                  c_scr, n_scr, m_scr, *, n_seg, init):
    t_len = CH // n_seg
    if init:
        c_scr[...] = c0_ref[...]
        n_scr[...] = n0_ref[...]
        m_scr[...] = m0_ref[...]
        return None

    same, tri, _ = _seg_masks(t_len)
    lane = lax.broadcasted_iota(jnp.int32, (CH, LANES), 1)
    lane256 = lax.broadcasted_iota(jnp.int32, (CH, BRANCH_W), 1)
    rowi = lax.broadcasted_iota(jnp.int32, (CH, 1), 0)
    r128 = lax.broadcasted_iota(jnp.int32, (LANES, 1), 0)
    neg = jnp.float32(-jnp.inf)

    v_all = vo_ref[:, 0:BRANCH_W]
    q_all = qk_ref[:, 0:ML_QK]
    k_all = qk_ref[:, ML_QK:2 * ML_QK] * (ML_DK ** -0.5)
    sm = sm_ref[...]
    ig = sm + gp_ref[0:1, :]
    fl = jnp.logical_and(lane >= L_ML_F, lane < L_ML_F + ML_H)
    lf = jnp.where(fl, -_softplus(-(sm + gp_ref[1:2, :])), 0.0)
    b = _dot(tri.astype(F32), lf, precision=HI)
    b_end = _dot(same.astype(F32), lf, precision=HI)
    b_r = _row_forms(b)
    ig_r = _row_forms(ig)
    yield
    mm = m_scr[...]
    qb = q_all.astype(BF16)
    kb = k_all.astype(BF16)
    qn = _seg_sum([_dot(qb, n_scr[s].astype(BF16)) for s in range(n_seg)], rowi, t_len)
    qc = _seg_sum([_dot(qb, c_scr[s].astype(BF16)) for s in range(n_seg)], rowi, t_len)
    yield
    num_att = jnp.zeros((CH, BRANCH_W), F32)
    w_ins, dens, w_ends, a_ends, m_ends = [], [], [], [], []
    for h in range(ML_H):
        li, lf_ = L_ML_I + h, L_ML_F + h
        bc = b[:, lf_:lf_ + 1]
        bec = b_end[:, lf_:lf_ + 1]
        igc = ig[:, li:li + 1]
        mmc = mm[:, li:li + 1]
        br = b_r[lf_:lf_ + 1, :]
        igr = ig_r[li:li + 1, :]
        diff = igr - br
        cmx = jnp.max(jnp.where(tri, diff, neg), axis=-1, keepdims=True)
        smx = jnp.max(jnp.where(same, diff, neg), axis=-1, keepdims=True)
        m_c = bc + jnp.maximum(mmc, cmx)
        m_end = bec + jnp.maximum(mmc, smx)
        w_in = jnp.exp(bc + mmc - m_c)
        logw = bc - br + igr - m_c
        mq = jnp.logical_and(lane >= ML_DK * h, lane < ML_DK * (h + 1))
        qk = _dot_nt(jnp.where(mq, q_all, 0.0).astype(BF16), kb)
        wts = _masked_exp(logw, tri) * qk
        yield
        mv = jnp.logical_and(lane256 >= 64 * h, lane256 < 64 * (h + 1))
        num_att = num_att + _dot(wts.astype(BF16), jnp.where(mv, v_all, 0.0).astype(BF16))
        nq = w_in * qn[:, li:li + 1] + jnp.sum(wts, axis=-1, keepdims=True)
        w_ins.append(w_in)
        dens.append(jnp.maximum(jnp.abs(nq), jnp.exp(-m_c)))
        w_ends.append(jnp.exp(bec - bc + igc - m_end))
        a_ends.append(jnp.exp(bec + mmc - m_end))
        m_ends.append(m_end)
        yield
    num = _quarter_sel(lane256, 64, w_ins) * qc + num_att
    hout = num / _quarter_sel(lane256, 64, dens)
    outs = [_head_rmsnorm(hout[:, 128 * p:128 * (p + 1)], lane < 64) for p in range(2)]
    o_all = jnp.concatenate(outs, axis=-1) * ng_ref[...] * jax.nn.sigmoid(vo_ref[:, BRANCH_W:2 * BRANCH_W])
    o_ref[...] = o_all.astype(BF16)
    yield

    kw = k_all * _quarter_sel(lane, ML_DK, w_ends)
    wend_tile = jnp.zeros((CH, LANES), F32)
    m_tile = jnp.zeros((CH, LANES), F32)
    for h in range(ML_H):
        wend_tile = jnp.where(lane == L_ML_I + h, w_ends[h], wend_tile)
        m_tile = jnp.where(lane == L_ML_I + h, m_ends[h], m_tile)
    m_scr[...] = m_tile
    vb = v_all.astype(BF16)
    wb = wend_tile.astype(BF16)
    rc = lax.broadcasted_iota(jnp.int32, (LANES, BRANCH_W), 0)
    lc = lax.broadcasted_iota(jnp.int32, (LANES, BRANCH_W), 1)
    bd_c = (rc // ML_DK) == (lc // 64)
    rn = lax.broadcasted_iota(jnp.int32, (LANES, LANES), 0)
    ln = lax.broadcasted_iota(jnp.int32, (LANES, LANES), 1)
    bd_n = ln == (rn // ML_DK) + L_ML_I
    for s in range(n_seg):
        r0 = s * t_len
        a_sel = _quarter_sel(r128, ML_DK, [a[r0:r0 + 1, :] for a in a_ends])
        upd_c = _dot_tn(_seg_rows(kw, rowi, t_len, s, n_seg).astype(BF16), vb)
        upd_n = _dot_tn(_seg_rows(k_all, rowi, t_len, s, n_seg).astype(BF16), wb)
        c_scr[s] = a_sel * c_scr[s] + jnp.where(bd_c, upd_c, 0.0)
        n_scr[s] = a_sel * n_scr[s] + jnp.where(bd_n, upd_n, 0.0)

    def finish():
        cn_ref[...] = c_scr[...]
        nn_ref[...] = n_scr[...]
        mn_ref[...] = m_tile
    return finish


def mlstm_spec(st, w, lb, n_seg):
    t_len = CH // n_seg
    c0, n0, m0 = st['ml_c'], st['ml_n'], st['ml_m']
    bsz = c0.shape[0]
    gp = _gate_rows([(L_ML_I, w['ml_ig_b']), (L_ML_F, w['ml_fg_b'])])
    ng = jnp.tile(w['ml_norm'], ML_H).reshape(1, BRANCH_W)
    eye = jnp.eye(ML_H, dtype=F32)
    pad = ((0, 0), (L_ML_I, LANES - L_ML_I - ML_H))
    c_bd = jnp.einsum('bhkv,hg->bhkgv', c0, eye).reshape(bsz, ML_QK, BRANCH_W)
    n_bd = jnp.pad(jnp.einsum('bhk,hg->bhkg', n0, eye).reshape(bsz, ML_QK, ML_H), ((0, 0),) + pad)
    m_exp = jnp.pad(jnp.repeat(m0, t_len, axis=0), pad)
    assert COLS['ml_o'][0] == COLS['ml_v'][0] + BRANCH_W and COLS['ml_k'][0] == COLS['ml_q'][0] + ML_QK
    spec = (_mlstm_kernel, [(COLS['ml_v'][0], 2 * BRANCH_W), (COLS['ml_q'][0], 2 * ML_QK), (GATE_COL0, LANES)],
            [c_bd, n_bd, m_exp], [gp, ng],
            [pltpu.VMEM((n_seg, ML_QK, BRANCH_W), F32), pltpu.VMEM((n_seg, ML_QK, LANES), F32),
             pltpu.VMEM((CH, LANES), F32)])

    def finish(cn, nn, mn):
        c_new = jnp.einsum('bhkgv,hg->bhkv', cn.reshape(bsz, ML_H, ML_DK, ML_H, 64), eye)
        n_new = jnp.einsum('bhkg,hg->bhk', nn[:, :, L_ML_I:L_ML_I + ML_H].reshape(bsz, ML_H, ML_DK, ML_H), eye)
        return {'ml_c': c_new, 'ml_n': n_new, 'ml_m': mn[::t_len, L_ML_I:L_ML_I + ML_H]}
    return spec, finish


def _merge_kernel(h_ref, gates_ref, br_ref, wb_ref, wo_ref, o_ref):
    merged = None
    for n in range(N_BRANCH):
        y = _dot(br_ref[:, n * BRANCH_W:(n + 1) * BRANCH_W], wb_ref[n])
        z = gates_ref[:, n * D_MODEL:(n + 1) * D_MODEL].astype(F32)
        t = (0.5 * jnp.tanh(0.5 * z) + 0.5) * y
        merged = t if merged is None else merged + t
    o_ref[...] = h_ref[...] + _dot(merged.astype(BF16), wo_ref[...])


def merge(h, gates, br, w_branch, w_out, tm):
    m, d = h.shape
    return pl.pallas_call(
        _merge_kernel,
        grid=(m // tm,),
        in_specs=[pl.BlockSpec((tm, d), lambda i: (i, 0)),
                  pl.BlockSpec((tm, N_BRANCH * d), lambda i: (i, 0)),
                  pl.BlockSpec((tm, N_BRANCH * BRANCH_W), lambda i: (i, 0)),
                  pl.BlockSpec((N_BRANCH, BRANCH_W, d), lambda i: (0, 0, 0)),
                  pl.BlockSpec((d, d), lambda i: (0, 0))],
        out_specs=pl.BlockSpec((tm, d), lambda i: (i, 0)),
        out_shape=jax.ShapeDtypeStruct((m, d), F32),
        compiler_params=pltpu.CompilerParams(
            dimension_semantics=("parallel",), vmem_limit_bytes=VMEM_LIMIT),
        name="merge",
    )(h, gates, br, w_branch, w_out)


def _ffn_kernel(h_ref, g_ref, wg_ref, wu_ref, wd_ref, o_ref, u_ref):
    f = pl.program_id(1)

    @pl.when(f == 0)
    def _():
        u_ref[...] = _rms(h_ref[...], g_ref[...]).astype(BF16)

    u = u_ref[...]
    a = jax.nn.silu(_dot(u, wg_ref[...]))
    b = _dot(u, wu_ref[...])
    y = _dot((a * b).astype(BF16), wd_ref[...])

    @pl.when(f == 0)
    def _():
        o_ref[...] = h_ref[...] + y

    @pl.when(f != 0)
    def _():
        o_ref[...] += y


def ffn(h, g, wg, wu, wd, tm, tf):
    m, d = h.shape
    ff = wg.shape[1]
    return pl.pallas_call(
        _ffn_kernel,
        grid=(m // tm, ff // tf),
        in_specs=[pl.BlockSpec((tm, d), lambda i, f: (i, 0)),
                  pl.BlockSpec((1, d), lambda i, f: (0, 0)),
                  pl.BlockSpec((d, tf), lambda i, f: (0, f)),
                  pl.BlockSpec((d, tf), lambda i, f: (0, f)),
                  pl.BlockSpec((tf, d), lambda i, f: (f, 0))],
        out_specs=pl.BlockSpec((tm, d), lambda i, f: (i, 0)),
        out_shape=jax.ShapeDtypeStruct((m, d), F32),
        scratch_shapes=[pltpu.VMEM((tm, d), BF16)],
        compiler_params=pltpu.CompilerParams(
            dimension_semantics=("parallel", "arbitrary"), vmem_limit_bytes=VMEM_LIMIT),
        name="ffn",
    )(h, g.reshape(1, d), wg, wu, wd)


def _router_kernel(h_ref, g_ref, wr_ref, u_ref, w_ref, i_ref):
    u = _rms(h_ref[...], g_ref[...])
    u_ref[...] = u
    logits = _dot(u, wr_ref[...], precision=HI)
    lane = lax.broadcasted_iota(jnp.int32, logits.shape, 1)
    neg = jnp.float32(-jnp.inf)
    logits = jnp.where(lane < N_EXPERTS, logits, neg)
    m1 = jnp.max(logits, axis=-1, keepdims=True)
    i1 = jnp.min(jnp.where(logits == m1, lane, LANES), axis=-1, keepdims=True)
    rest = jnp.where(lane == i1, neg, logits)
    m2 = jnp.max(rest, axis=-1, keepdims=True)
    i2 = jnp.min(jnp.where(rest == m2, lane, LANES), axis=-1, keepdims=True)
    e = jnp.exp(m2 - m1)
    den = 1.0 + e
    w_ref[...] = jnp.where(lane == 0, 1.0 / den, jnp.where(lane == 1, e / den, 0.0))
    i_ref[...] = jnp.where(lane == 0, i1, jnp.where(lane == 1, i2, 0))


def router(h, g, w_router, tm):
    m, d = h.shape
    wr = jnp.pad(w_router, ((0, 0), (0, LANES - N_EXPERTS)))
    return pl.pallas_call(
        _router_kernel,
        grid=(m // tm,),
        in_specs=[pl.BlockSpec((tm, d), lambda i: (i, 0)),
                  pl.BlockSpec((1, d), lambda i: (0, 0)),
                  pl.BlockSpec((d, LANES), lambda i: (0, 0))],
        out_specs=[pl.BlockSpec((tm, d), lambda i: (i, 0)),
                   pl.BlockSpec((tm, LANES), lambda i: (i, 0)),
                   pl.BlockSpec((tm, LANES), lambda i: (i, 0))],
        out_shape=[jax.ShapeDtypeStruct((m, d), F32),
                   jax.ShapeDtypeStruct((m, LANES), F32),
                   jax.ShapeDtypeStruct((m, LANES), jnp.int32)],
        compiler_params=pltpu.CompilerParams(
            dimension_semantics=("parallel",), vmem_limit_bytes=VMEM_LIMIT),
        name="router",
    )(h, g.reshape(1, d), wr)


def _row_copies(src_hbm, dst_hbm, idx_ref, base, buf, sem, n_rows, gather, wait):
    def body(r, carry):
        row = idx_ref[base + r]
        if gather:
            cp = pltpu.make_async_copy(src_hbm.at[pl.ds(row, 1)], buf.at[pl.ds(r, 1)], sem)
        else:
            cp = pltpu.make_async_copy(buf.at[pl.ds(r, 1)], dst_hbm.at[pl.ds(row, 1)], sem)
        if wait:
            cp.wait()
        else:
            cp.start()
        return carry
    lax.fori_loop(0, n_rows, body, 0, unroll=8)


def _expert_kernel(te_ref, nt_ref, src_ref, dst_ref, u_hbm, wg_ref, wu_ref, wd_ref, y_hbm,
                   xbuf, xb_ref, acc_ref, obuf, gsem, ssem, *, tm):
    i = pl.program_id(0)
    f = pl.program_id(1)
    last_f = pl.num_programs(1) - 1
    nt = nt_ref[0]
    gather = functools.partial(_row_copies, u_hbm, None, src_ref, gather=True, n_rows=tm)
    scatter = functools.partial(_row_copies, None, y_hbm, dst_ref, buf=obuf, sem=ssem.at[0], gather=False, n_rows=tm)

    @pl.when(jnp.logical_and(i < nt, f == 0))
    def _():
        slot = i % 2

        @pl.when(i == 0)
        def _():
            gather(base=0, buf=xbuf.at[0], sem=gsem.at[0], wait=False)
            obuf[...] = jnp.zeros_like(obuf)
            fill = pltpu.make_async_copy(obuf, y_hbm.at[pl.ds(y_hbm.shape[0] - tm, tm)], ssem.at[0])
            fill.start()
            fill.wait()

        gather(base=i * tm, buf=xbuf.at[slot], sem=gsem.at[slot], wait=True)

        @pl.when(i + 1 < nt)
        def _():
            gather(base=(i + 1) * tm, buf=xbuf.at[1 - slot], sem=gsem.at[1 - slot], wait=False)

        xb_ref[...] = xbuf[slot].astype(BF16)

    @pl.when(i < nt)
    def _():
        x = xb_ref[...]
        a = jax.nn.silu(_dot(x, wg_ref[0]))
        b = _dot(x, wu_ref[0])
        y = _dot((a * b).astype(BF16), wd_ref[0])

        @pl.when(f == 0)
        def _():
            acc_ref[...] = y

        @pl.when(f != 0)
        def _():
            acc_ref[...] += y

        @pl.when(f == last_f)
        def _():
            @pl.when(i > 0)
            def _():
                scatter(base=(i - 1) * tm, wait=True)

            obuf[...] = acc_ref[...]
            scatter(base=i * tm, wait=False)

            @pl.when(i == nt - 1)
            def _():
                scatter(base=i * tm, wait=True)


def experts(u, tile_expert, n_tiles, src_tok, dst_row, n_out_rows, wg, wu, wd, tm, tf):
    d = u.shape[1]
    ff = wg.shape[2]
    n_row_tiles = tile_expert.shape[0]
    wmap = lambda i, f, te, nt, src, dst: (te[i], 0, f)
    grid_spec = pltpu.PrefetchScalarGridSpec(
        num_scalar_prefetch=4,
        grid=(n_row_tiles, ff // tf),
        in_specs=[pl.BlockSpec(memory_space=pl.ANY),
                  pl.BlockSpec((1, d, tf), wmap),
                  pl.BlockSpec((1, d, tf), wmap),
                  pl.BlockSpec((1, tf, d), lambda i, f, te, nt, src, dst: (te[i], f, 0))],
        out_specs=pl.BlockSpec(memory_space=pl.ANY),
        scratch_shapes=[pltpu.VMEM((2, tm, d), F32), pltpu.VMEM((tm, d), BF16), pltpu.VMEM((tm, d), F32),
                        pltpu.VMEM((tm, d), F32), pltpu.SemaphoreType.DMA((2,)), pltpu.SemaphoreType.DMA((1,))],
    )
    return pl.pallas_call(
        functools.partial(_expert_kernel, tm=tm),
        grid_spec=grid_spec,
        out_shape=jax.ShapeDtypeStruct((n_out_rows, d), F32),
        compiler_params=pltpu.CompilerParams(
            dimension_semantics=("arbitrary", "arbitrary"), vmem_limit_bytes=VMEM_LIMIT,
            disable_bounds_checks=True),
        name="experts",
    )(tile_expert, n_tiles, src_tok, dst_row, u, wg, wu, wd)


def moe(h, g, w_router, wg, wu, wd, tm_route, tm_e, tf):
    m, d = h.shape
    u, top_w, top_i = router(h, g, w_router, tm_route)
    n_pairs = TOP_K * m
    flat_e = top_i[:, :TOP_K].reshape(-1)
    onehot = (flat_e[:, None] == jnp.arange(N_EXPERTS, dtype=jnp.int32)[None, :]).astype(jnp.int32)
    rank = jnp.sum((jnp.cumsum(onehot, axis=0) - 1) * onehot, axis=1)
    counts = jnp.sum(onehot, axis=0)
    tiles_per = (counts + tm_e - 1) // tm_e
    tile_end = jnp.cumsum(tiles_per)
    tile_start = tile_end - tiles_per
    grouped_row = tile_start[flat_e] * tm_e + rank
    n_rows = n_pairs + N_EXPERTS * tm_e
    n_row_tiles = n_rows // tm_e
    pair_ids = jnp.arange(n_pairs, dtype=jnp.int32)
    pair_at = jnp.full((n_rows,), -1, jnp.int32).at[grouped_row].set(pair_ids)
    spare = n_pairs + jnp.arange(n_rows, dtype=jnp.int32) % tm_e
    src_tok = jnp.where(pair_at >= 0, pair_at // TOP_K, 0)
    dst_row = jnp.where(pair_at >= 0, (pair_at % TOP_K) * m + pair_at // TOP_K, spare)
    tile_ids = jnp.arange(n_row_tiles, dtype=jnp.int32)
    tile_expert = jnp.minimum(jnp.sum((tile_ids[:, None] >= tile_end[None, :]).astype(jnp.int32), axis=1),
                              N_EXPERTS - 1).astype(jnp.int32)
    n_tiles = tile_end[-1:].astype(jnp.int32)
    last_e = tile_expert[jnp.maximum(n_tiles[0] - 1, 0)]
    tile_expert = jnp.where(tile_ids < n_tiles[0], tile_expert, last_e)
    y = experts(u, tile_expert, n_tiles, src_tok, dst_row, n_pairs + tm_e, wg, wu, wd, tm_e, tf)
    return y, top_w


def _ple_kernel(*refs, final, combine):
    if combine:
        h_ref, y0_ref, y1_ref, tw_ref, p_ref, g_ref, wg_ref, wp_ref, gf_ref, o_ref = refs
        h = h_ref[...] + (tw_ref[:, 0:1] * y0_ref[...] + tw_ref[:, 1:2] * y1_ref[...])
    else:
        h_ref, p_ref, g_ref, wg_ref, wp_ref, gf_ref, o_ref = refs
        h = h_ref[...]
    v = _rms(h, g_ref[...]).astype(BF16)
    pg = jax.nn.sigmoid(_dot(v, wg_ref[...]))
    e = _dot(p_ref[...].astype(BF16), wp_ref[...])
    out = h + pg * e
    if final:
        out = _rms(out, gf_ref[...])
    o_ref[...] = out


def ple(h, p, g, w_gate, w_p, g_final, tm, final, expert_out=None):
    m, d = h.shape
    dp = p.shape[1]
    rows = lambda wd: pl.BlockSpec((tm, wd), lambda i: (i, 0))
    whole = lambda a, b: pl.BlockSpec((a, b), lambda i: (0, 0))
    extra, extra_specs = (), []
    if expert_out is not None:
        y, top_w = expert_out
        extra = (y, y, top_w)
        extra_specs = [rows(d), pl.BlockSpec((tm, d), lambda i: (i + m // tm, 0)), rows(LANES)]
    return pl.pallas_call(
        functools.partial(_ple_kernel, final=final, combine=expert_out is not None),
        grid=(m // tm,),
        in_specs=[rows(d)] + extra_specs + [rows(dp), whole(1, d), whole(d, d), whole(dp, d), whole(1, d)],
        out_specs=rows(d),
        out_shape=jax.ShapeDtypeStruct((m, d), F32),
        compiler_params=pltpu.CompilerParams(
            dimension_semantics=("parallel",), vmem_limit_bytes=VMEM_LIMIT),
        name="ple",
    )(h, *extra, p, g.reshape(1, d), w_gate, w_p, g_final.reshape(1, d))


STATE_KEYS = ('gdn_conv', 'gdn', 'hgrn', 'ssd_conv', 'ssd', 'ml_c', 'ml_n', 'ml_m')
MIXER_KEYS = ('gdn_conv_w', 'gdn_a_log', 'gdn_dt_bias', 'gdn_norm', 'hgrn_norm',
              'ssd_conv_w', 'ssd_conv_b', 'ssd_a_log', 'ssd_dt_bias', 'ssd_d', 'ssd_norm',
              'ml_ig_b', 'ml_fg_b', 'ml_norm')


MIXER_SPECS = (gdn_spec, hgrn_spec, ssd_spec, mlstm_spec)
_gdn_kernel.n_stages = lambda t_len: 11 + 2 * (t_len.bit_length() - 2)
_hgrn_kernel.n_stages = lambda t_len: 1 + HG_H * (len(_hgrn_levels(t_len)) // 2 + 1)
_ssd_kernel.n_stages = lambda t_len: 8
_mlstm_kernel.n_stages = lambda t_len: 11


def _mixers(proj, row_blk0, st, w, lb, **grid):
    built = [f(st, w, lb, grid['n_seg']) for f in MIXER_SPECS]
    o, *new_states = _mixer_call([b[0] for b in built], proj, row_blk0, **grid)
    new, at = {}, 0
    for spec, finish in built:
        n = len(spec[2])
        new.update(finish(*new_states[at:at + n]))
        at += n
    return o, new


def kernel(x_prompt, x_sample, state_gdn_conv, state_gdn, state_hgrn, state_ssd_conv, state_ssd, state_mlstm_c, state_mlstm_n, state_mlstm_m, p_prompt, p_sample, g_mix, w_in, gdn_conv_w, gdn_a_log, gdn_dt_bias, gdn_norm, hgrn_lb, hgrn_norm, ssd_conv_w, ssd_conv_b, ssd_a_log, ssd_dt_bias, ssd_d, ssd_norm, ml_ig_b, ml_fg_b, ml_norm, w_branch, w_out, g_ffn, w_ff_gate, w_ff_up, w_ff_down, w_router, w_ex_gate, w_ex_up, w_ex_down, w_ple, w_ple_gate, g_ple, g_final):
    prm = {'gdn_conv_w': gdn_conv_w, 'gdn_a_log': gdn_a_log, 'gdn_dt_bias': gdn_dt_bias,
           'gdn_norm': gdn_norm, 'hgrn_norm': hgrn_norm, 'ssd_conv_w': ssd_conv_w,
           'ssd_conv_b': ssd_conv_b, 'ssd_a_log': ssd_a_log, 'ssd_dt_bias': ssd_dt_bias,
           'ssd_d': ssd_d, 'ssd_norm': ssd_norm, 'ml_ig_b': ml_ig_b, 'ml_fg_b': ml_fg_b,
           'ml_norm': ml_norm}
    bp, lp, d = x_prompt.shape
    bs, ls, _ = x_sample.shape
    mp, ms = bp * lp, bs * ls
    assert lp % CH == 0 and CH % ls == 0 and ms % CH == 0 and mp % CH == 0
    st_s = {'gdn_conv': state_gdn_conv, 'gdn': state_gdn, 'hgrn': state_hgrn, 'ssd_conv': state_ssd_conv,
            'ssd': state_ssd, 'ml_c': state_mlstm_c, 'ml_n': state_mlstm_n, 'ml_m': state_mlstm_m}
    st_p = {k: jnp.zeros((bp,) + v.shape[2:], F32) for k, v in st_s.items()}
    grid_p = dict(n_outer=bp, n_chunks=lp // CH, n_seg=1)
    grid_s = dict(n_outer=ms // CH, n_chunks=1, n_seg=CH // ls)

    sm = jax.nn.softmax(hgrn_lb, axis=0)
    lb_all = jnp.cumsum(sm, axis=0) - sm[0]

    h = jnp.concatenate([x_prompt.reshape(mp, d), x_sample.reshape(ms, d)], axis=0)
    new_p = {k: [] for k in STATE_KEYS}
    new_s = {k: [] for k in STATE_KEYS}
    for l in range(DEPTH):
        wl = {k: prm[k][l] for k in MIXER_KEYS}
        gates, mix = norm_matmul(h, g_mix[l], _permute_w_in(w_in[l]), tm=1024, tn=1024)
        br_p, np_ = _mixers(mix, 0, st_p, wl, lb_all[l], **grid_p)
        br_s, ns_ = _mixers(mix, mp // CH, {k: v[l] for k, v in st_s.items()}, wl, lb_all[l], **grid_s)
        br = jnp.concatenate([br_p, br_s], axis=0)
        h = merge(h, gates, br, w_branch[l].astype(BF16), w_out[l].astype(BF16), tm=512)
        j = l // 2
        expert_out = None
        if l % 2 == 0:
            h = ffn(h, g_ffn[l], w_ff_gate[j].astype(BF16), w_ff_up[j].astype(BF16),
                    w_ff_down[j].astype(BF16), tm=512, tf=D_FF // 2)
        else:
            expert_out = moe(h, g_ffn[l], w_router[j], w_ex_gate[j].astype(BF16), w_ex_up[j].astype(BF16),
                             w_ex_down[j].astype(BF16), tm_route=512, tm_e=512, tf=D_FF // 2)
        p = jnp.concatenate([p_prompt[l].reshape(mp, D_PLE), p_sample[l].reshape(ms, D_PLE)], axis=0)
        h = ple(h, p, g_ple[l], w_ple_gate[l].astype(BF16), w_ple[l].astype(BF16), g_final,
                tm=512, final=(l == DEPTH - 1), expert_out=expert_out)
        for k in STATE_KEYS:
            new_p[k].append(np_[k])
            new_s[k].append(ns_[k])
    y_prompt = h[:mp].reshape(bp, lp, d)
    y_sample = h[mp:].reshape(bs, ls, d)
    sp = {k: jnp.stack(v) for k, v in new_p.items()}
    ss = {k: jnp.stack(v) for k, v in new_s.items()}
    return (y_prompt, y_sample,
            sp['gdn_conv'], sp['gdn'], sp['hgrn'], sp['ssd_conv'], sp['ssd'], sp['ml_c'], sp['ml_n'], sp['ml_m'],
            ss['gdn_conv'], ss['gdn'], ss['hgrn'], ss['ssd_conv'], ss['ssd'], ss['ml_c'], ss['ml_n'], ss['ml_m'])
```

```python
import functools

import numpy as np
import jax
import jax.numpy as jnp
from jax import lax
from jax.experimental import pallas as pl
from jax.experimental.pallas import tpu as pltpu

F32 = jnp.float32
BF16 = jnp.bfloat16
HI = lax.Precision.HIGHEST

D_MODEL = 1024
DEPTH = 2
D_PLE = 256
N_BRANCH = 4
BRANCH_W = D_MODEL // N_BRANCH
CONV_W = 4
EPS = 1e-6

GDN_H = 4
GDN_DK = 64
GDN_QK = GDN_H * GDN_DK
GDN_CONV_CH = 2 * GDN_QK + BRANCH_W
HG_H = 4
HG_QK = 256
SSD_H = 4
SSD_G = 2
SSD_N = 128
SSD_CONV_CH = BRANCH_W + 2 * SSD_G * SSD_N
ML_H = 4
ML_DK = 32
ML_QK = ML_H * ML_DK
D_FF = ((8 * D_MODEL // 3 + 255) // 256) * 256
N_EXPERTS = 8
TOP_K = 2

_REF_SPLITS = (('gdn_in', GDN_CONV_CH), ('gdn_b', GDN_H), ('gdn_a', GDN_H), ('gdn_z', BRANCH_W),
               ('hg_q', HG_QK), ('hg_f', HG_QK), ('hg_v', BRANCH_W), ('hg_g', BRANCH_W),
               ('ssd_z', BRANCH_W), ('ssd_in', SSD_CONV_CH), ('ssd_dt', SSD_H),
               ('ml_q', ML_QK), ('ml_k', ML_QK), ('ml_v', BRANCH_W), ('ml_i', ML_H), ('ml_f', ML_H),
               ('ml_o', BRANCH_W), ('gates', N_BRANCH * D_MODEL))
_MY_ORDER = ('gdn_in', 'ssd_in', 'gdn_z', 'ssd_z', 'hg_q', 'hg_f', 'hg_v', 'hg_g',
             'ml_v', 'ml_o', 'ml_q', 'ml_k', 'gdn_b', 'gdn_a', 'ssd_dt', 'ml_i', 'ml_f')
LANES = 128
MXU_TILE = 256
CH = 64
N_GATES = N_BRANCH * D_MODEL


def _layout():
    widths = dict(_REF_SPLITS)
    off, out = 0, {}
    for name in _MY_ORDER:
        out[name] = (off, widths[name])
        off += widths[name]
    return out, -(-off // MXU_TILE) * MXU_TILE


COLS, N_MIX = _layout()
GATE_COL0 = COLS['gdn_b'][0]
L_GDN_B, L_GDN_A, L_SSD_DT, L_ML_I, L_ML_F = (COLS[n][0] - GATE_COL0 for n in ('gdn_b', 'gdn_a', 'ssd_dt', 'ml_i', 'ml_f'))
VMEM_LIMIT = 56 * 1024 * 1024


def _ref_offsets():
    off, acc = {}, 0
    for name, wd in _REF_SPLITS:
        off[name] = acc
        acc += wd
    return off, acc


def _permute_kernel(w_ref, ws_ref, o_ref):
    ref_off, _ = _ref_offsets()
    o_ref[:, :N_GATES] = w_ref[:, ref_off['gates']:ref_off['gates'] + N_GATES].astype(BF16)
    for name in _MY_ORDER:
        dst, wd = COLS[name]
        if wd >= LANES:
            o_ref[:, N_GATES + dst:N_GATES + dst + wd] = w_ref[:, ref_off[name]:ref_off[name] + wd].astype(BF16)
    o_ref[:, N_GATES + GATE_COL0:N_GATES + GATE_COL0 + LANES] = ws_ref[...].astype(BF16)
    pad0 = N_GATES + GATE_COL0 + LANES
    o_ref[:, pad0:] = jnp.zeros((o_ref.shape[0], N_GATES + N_MIX - pad0), BF16)


def _permute_w_in(w, tk=256):
    ref_off, n_in = _ref_offsets()
    small = [n for n in _MY_ORDER if COLS[n][1] < LANES]
    ws = jnp.concatenate([w[:, ref_off[n]:ref_off[n] + COLS[n][1]] for n in small], axis=1)
    ws = jnp.pad(ws, ((0, 0), (0, LANES - ws.shape[1])))
    d = w.shape[0]
    return pl.pallas_call(
        _permute_kernel, grid=(d // tk,),
        in_specs=[pl.BlockSpec((tk, n_in), lambda i: (i, 0)), pl.BlockSpec((tk, LANES), lambda i: (i, 0))],
        out_specs=pl.BlockSpec((tk, N_GATES + N_MIX), lambda i: (i, 0)),
        out_shape=jax.ShapeDtypeStruct((d, N_GATES + N_MIX), BF16),
        compiler_params=pltpu.CompilerParams(dimension_semantics=("parallel",), vmem_limit_bytes=VMEM_LIMIT),
        name="permute_w_in",
    )(w, ws)


def _rms(x, g):
    return x * lax.rsqrt(jnp.mean(x * x, axis=-1, keepdims=True) + EPS) * g


def _dot(a, b, **kw):
    return jnp.dot(a, b, preferred_element_type=F32, **kw)


def _dot_nt(a, b, **kw):
    return lax.dot_general(a, b, (((1,), (1,)), ((), ())), preferred_element_type=F32, **kw)


def _dot_tn(a, b, **kw):
    return lax.dot_general(a, b, (((0,), (0,)), ((), ())), preferred_element_type=F32, **kw)


def _norm_matmul_kernel(x_ref, g_ref, w_ref, gates_ref, mix_ref, xn_ref, *, n_gate_tiles):
    j = pl.program_id(1)

    @pl.when(j == 0)
    def _():
        xn_ref[...] = _rms(x_ref[...], g_ref[...]).astype(BF16)

    y = _dot(xn_ref[...], w_ref[...])

    @pl.when(j < n_gate_tiles)
    def _():
        gates_ref[...] = y.astype(BF16)

    @pl.when(j >= n_gate_tiles)
    def _():
        mix_ref[...] = y


def norm_matmul(x, g, w, tm, tn):
    m, d = x.shape
    assert w.shape[1] == N_GATES + N_MIX and N_GATES % tn == 0 and N_MIX % tn == 0
    ng = N_GATES // tn
    return pl.pallas_call(
        functools.partial(_norm_matmul_kernel, n_gate_tiles=ng),
        grid=(m // tm, (N_GATES + N_MIX) // tn),
        in_specs=[pl.BlockSpec((tm, d), lambda i, j: (i, 0)),
                  pl.BlockSpec((1, d), lambda i, j: (0, 0)),
                  pl.BlockSpec((d, tn), lambda i, j: (0, j))],
        out_specs=[pl.BlockSpec((tm, tn), lambda i, j: (i, jnp.minimum(j, ng - 1))),
                   pl.BlockSpec((tm, tn), lambda i, j: (i, jnp.maximum(j - ng, 0)))],
        out_shape=[jax.ShapeDtypeStruct((m, N_GATES), BF16), jax.ShapeDtypeStruct((m, N_MIX), F32)],
        scratch_shapes=[pltpu.VMEM((tm, d), BF16)],
        compiler_params=pltpu.CompilerParams(
            dimension_semantics=("parallel", "arbitrary"), vmem_limit_bytes=VMEM_LIMIT),
        name="norm_matmul",
    )(x, g.reshape(1, d), w)


def _seg_masks(t_len):
    row = lax.broadcasted_iota(jnp.int32, (CH, CH), 0)
    col = lax.broadcasted_iota(jnp.int32, (CH, CH), 1)
    same = (row // t_len) == (col // t_len)
    tri = jnp.logical_and(same, col <= row)
    strict = jnp.logical_and(same, col < row)
    return same, tri, strict


def _row_forms(x, n_rows=24):
    r = lax.broadcasted_iota(jnp.int32, (n_rows, LANES), 0)
    l = lax.broadcasted_iota(jnp.int32, (n_rows, LANES), 1)
    return _dot_nt((r == l).astype(F32), x, precision=HI)


def _softplus(x):
    return jnp.maximum(x, 0.0) + jnp.log1p(jnp.exp(-jnp.abs(x)))


def _split2(x):
    hi = x.astype(BF16)
    return hi, (x - hi.astype(F32)).astype(BF16)


def _dot3(a, b):
    return _dot(a[0], b[0]) + (_dot(a[0], b[1]) + _dot(a[1], b[0]))


def _masked_exp(d, mask):
    return jnp.where(mask, jnp.exp(jnp.where(mask, d, 0.0)), 0.0)


def _conv_silu(x, ext_scr, cw_ref, bias, t_len, n_seg):
    w = x.shape[-1]
    ext_scr[:, 8:8 + t_len, :] = x.reshape(n_seg, t_len, w)
    y = cw_ref[3:4, :] * x
    for j in range(1, CONV_W):
        y = y + cw_ref[3 - j:4 - j, :] * ext_scr[:, 8 - j:8 - j + t_len, :].reshape(CH, w)
    if bias is not None:
        y = y + bias
    tail = ext_scr[:, 5 + t_len:8 + t_len, :]
    ext_scr[:, 5:8, :] = tail
    return jax.nn.silu(y), tail


def _halves(xp, lo):
    s_lo = jnp.sum(jnp.where(lo, xp, 0.0), axis=-1, keepdims=True)
    s_hi = jnp.sum(jnp.where(lo, 0.0, xp), axis=-1, keepdims=True)
    return jnp.where(lo, s_lo, s_hi)


def _head_rmsnorm(xp, lo):
    return xp * lax.rsqrt(_halves(xp * xp, lo) * (1.0 / 64) + EPS)


def _head_l2norm(xp, lo):
    return xp * lax.rsqrt(_halves(xp * xp, lo) + EPS)


def _seg_sum(parts, rowi, t_len):
    if len(parts) == 1:
        return parts[0]
    acc = jnp.where(rowi // t_len == 0, parts[0], 0.0)
    for s in range(1, len(parts)):
        acc = acc + jnp.where(rowi // t_len == s, parts[s], 0.0)
    return acc


def _seg_rows(x, rowi, t_len, s, n_seg):
    return x if n_seg == 1 else jnp.where(rowi // t_len == s, x, 0.0)


def _quarter_sel(idx, width, vals):
    out = vals[3]
    for h in (2, 1, 0):
        out = jnp.where(idx < (h + 1) * width, vals[h], out)
    return out


def _gate_rows(pairs):
    t = jnp.zeros((8, LANES), F32)
    for r, (off, v) in enumerate(pairs):
        t = t.at[r, off:off + v.shape[0]].set(v.astype(F32))
    return t


def _block_diag(blocks):
    n = len(blocks)
    z = jnp.zeros_like(blocks[0])
    return jnp.concatenate(
        [jnp.concatenate([blocks[i] if i == j else z for j in range(n)], axis=1) for i in range(n)], axis=0)


def _transpose64(x):
    r = lax.broadcasted_iota(jnp.int32, x.shape, 0)
    c = lax.broadcasted_iota(jnp.int32, x.shape, 1)
    return _dot_nt((r == c).astype(F32), x, precision=HI)


def _load_pair_states(s0_ref, s_scr, n_seg, transpose):
    prep = _transpose64 if transpose else (lambda t: t)
    for s in range(n_seg):
        for p in range(2):
            s_scr[s, p] = _block_diag([prep(s0_ref[s, 2 * p]), prep(s0_ref[s, 2 * p + 1])])


def _store_pair_states(s_scr, sn_ref, n_seg, transpose):
    prep = _transpose64 if transpose else (lambda t: t)
    for s in range(n_seg):
        for p in range(2):
            sn_ref[s, 2 * p] = prep(s_scr[s, p, 0:64, 0:64])
            sn_ref[s, 2 * p + 1] = prep(s_scr[s, p, 64:128, 64:128])


def _interleave(gens, n_stages):
    results = [None] * len(gens)
    pos = [0] * len(gens)
    live = set(range(len(gens)))
    while live:
        k = min(live, key=lambda i: ((pos[i] + 1) / n_stages[i], i))
        try:
            next(gens[k])
            pos[k] += 1
        except StopIteration as stop:
            results[k] = stop.value
            live.remove(k)
    return results


def _fused_mixer_kernel(*refs, n_seg, parts):
    tot = [sum(p[j] for p in parts) for j in range(1, 5)]
    ins, rest = refs[:tot[0]], refs[tot[0]:]
    sts, rest = rest[:tot[1]], rest[tot[1]:]
    prs, rest = rest[:tot[2]], rest[tot[2]:]
    o_ref, rest = rest[0], rest[1:]
    outs, scr = rest[:tot[1]], rest[tot[1]:]
    at = [0, 0, 0, 0]
    calls = []
    for k, (body, n_in, n_st, n_pr, n_scr) in enumerate(parts):
        take = lambda seq, j, n: seq[at[j]:at[j] + n]
        calls.append(functools.partial(
            body, *take(ins, 0, n_in), *take(sts, 1, n_st), *take(prs, 2, n_pr),
            o_ref.at[:, k * BRANCH_W:(k + 1) * BRANCH_W],
            *take(outs, 1, n_st), *take(scr, 3, n_scr), n_seg=n_seg))
        for j, n in enumerate((n_in, n_st, n_pr, n_scr)):
            at[j] += n

    n_stages = [p[0].n_stages(CH // n_seg) for p in parts]

    @pl.when(pl.program_id(1) == 0)
    def _():
        _interleave([call(init=True) for call in calls], n_stages)

    finishers = _interleave([call(init=False) for call in calls], n_stages)

    @pl.when(pl.program_id(1) == pl.num_programs(1) - 1)
    def _():
        for fin in finishers:
            fin()


def _mixer_call(specs, proj, row_blk0, *, n_outer, n_chunks, n_seg):
    rows = n_outer * n_chunks * CH
    rmap = lambda blk: (lambda i, c: (row_blk0 + i * n_chunks + c, blk))
    full = lambda a: pl.BlockSpec(a.shape, lambda i, c: (0,) * a.ndim)
    sblk = lambda a: pl.BlockSpec((a.shape[0] // n_outer,) + a.shape[1:], lambda i, c: (i,) + (0,) * (a.ndim - 1))
    in_blocks = [b for s in specs for b in s[1]]
    state_ins = [a for s in specs for a in s[2]]
    params = [a for s in specs for a in s[3]]
    scratch = [a for s in specs for a in s[4]]
    for off, wd in in_blocks:
        assert off % wd == 0
    parts = tuple((s[0], len(s[1]), len(s[2]), len(s[3]), len(s[4])) for s in specs)
    width = len(specs) * BRANCH_W
    return pl.pallas_call(
        functools.partial(_fused_mixer_kernel, n_seg=n_seg, parts=parts),
        grid=(n_outer, n_chunks),
        in_specs=([pl.BlockSpec((CH, wd), rmap(off // wd)) for off, wd in in_blocks]
                  + [sblk(a) for a in state_ins] + [full(a) for a in params]),
        out_specs=[pl.BlockSpec((CH, width), lambda i, c: (i * n_chunks + c, 0))] + [sblk(a) for a in state_ins],
        out_shape=([jax.ShapeDtypeStruct((rows, width), BF16)]
                   + [jax.ShapeDtypeStruct(a.shape, F32) for a in state_ins]),
        scratch_shapes=scratch,
        compiler_params=pltpu.CompilerParams(
            dimension_semantics=("parallel", "arbitrary"), vmem_limit_bytes=VMEM_LIMIT),
        name="token_mixers",
    )(*([proj] * len(in_blocks)), *state_ins, *params)


def _gdn_kernel(xin_ref, z_ref, sm_ref, conv0_ref, s0_ref, cw_ref, gp_ref, ng_ref,
                o_ref, convn_ref, sn_ref, ext_scr, s_scr, *, n_seg, init):
    t_len = CH // n_seg
    if init:
        _load_pair_states(s0_ref, s_scr, n_seg, transpose=False)
        ext_scr[:, 5:8, :] = conv0_ref[...]
        return None

    xc, tail = _conv_silu(xin_ref[...], ext_scr, cw_ref, None, t_len, n_seg)
    yield
    same, tri, strict = _seg_masks(t_len)
    lane = lax.broadcasted_iota(jnp.int32, (CH, LANES), 1)
    rowi = lax.broadcasted_iota(jnp.int32, (CH, 1), 0)
    lo = lane < 64
    sm = sm_ref[...]
    beta = jax.nn.sigmoid(sm)
    gl = jnp.logical_and(lane >= L_GDN_A, lane < L_GDN_A + GDN_H)
    g = jnp.where(gl, -jnp.exp(gp_ref[0:1, :]) * _softplus(sm + gp_ref[1:2, :]), 0.0)
    gam = _dot(tri.astype(F32), g, precision=HI)
    gam_end = _dot(same.astype(F32), g, precision=HI)
    gam_r = _row_forms(gam, 8)
    yield
    r128 = lax.broadcasted_iota(jnp.int32, (LANES, LANES), 0)
    l128 = lax.broadcasted_iota(jnp.int32, (LANES, LANES), 1)
    bd = (r128 < 64) == (l128 < 64)
    rsel = lax.broadcasted_iota(jnp.int32, (LANES, 1), 0) < 64
    qs_, ks_, atts_, gcs_, a_, x_ = [], [], [], [], [], []
    for p in range(2):
        q_p = _head_l2norm(xc[:, 128 * p:128 * (p + 1)], lo) * (GDN_DK ** -0.5)
        k_p = _head_l2norm(xc[:, GDN_QK + 128 * p:GDN_QK + 128 * (p + 1)], lo)
        v_p = xc[:, 2 * GDN_QK + 128 * p:2 * GDN_QK + 128 * (p + 1)]
        kb = k_p.astype(BF16)
        qs_.append(q_p)
        ks_.append(k_p)
        for j in range(2):
            h = 2 * p + j
            mj = lo if j == 0 else jnp.logical_not(lo)
            kk = _dot_nt(jnp.where(mj, k_p, 0.0).astype(BF16), kb)
            qk = _dot_nt(jnp.where(mj, q_p, 0.0).astype(BF16), kb)
            gc = gam[:, L_GDN_A + h:L_GDN_A + h + 1]
            bc = beta[:, L_GDN_B + h:L_GDN_B + h + 1]
            dec = _masked_exp(gc - gam_r[L_GDN_A + h:L_GDN_A + h + 1, :], tri)
            a_.append(jnp.where(strict, bc * kk * dec, 0.0))
            x_.append(jnp.concatenate([jnp.where(mj, bc * v_p, 0.0),
                                       jnp.where(mj, (bc * jnp.exp(gc)) * k_p, 0.0)], axis=-1))
            atts_.append(qk * dec)
            gcs_.append(gc)
            yield
    sa = [_split2(a) for a in a_]
    x_ = [x - _dot3(s, _split2(x)) for s, x in zip(sa, x_)]
    yield
    n = 2
    while n < t_len:
        sa = [_split2(_dot3(s, s)) for s in sa]
        yield
        x_ = [x + _dot3(s, _split2(x)) for s, x in zip(sa, x_)]
        yield
        n *= 2
    outs = []
    for p in range(2):
        q_p, k_p = qs_[p], ks_[p]
        solv = x_[2 * p][:, :LANES] + x_[2 * p + 1][:, :LANES]
        solk = x_[2 * p][:, LANES:] + x_[2 * p + 1][:, LANES:]
        atts, gcs = atts_[2 * p:2 * p + 2], gcs_[2 * p:2 * p + 2]
        solk_b = solk.astype(BF16)
        qb = q_p.astype(BF16)
        u = solv - _seg_sum([_dot(solk_b, s_scr[s, p].astype(BF16)) for s in range(n_seg)], rowi, t_len)
        qs = _seg_sum([_dot(qb, s_scr[s, p].astype(BF16)) for s in range(n_seg)], rowi, t_len)
        o = jnp.where(lo, jnp.exp(gcs[0]), jnp.exp(gcs[1])) * qs
        for j in range(2):
            mj = lo if j == 0 else jnp.logical_not(lo)
            o = o + _dot(atts[j].astype(BF16), jnp.where(mj, u, 0.0).astype(BF16))
        yield
        ge0 = gam_end[:, L_GDN_A + 2 * p:L_GDN_A + 2 * p + 1]
        ge1 = gam_end[:, L_GDN_A + 2 * p + 1:L_GDN_A + 2 * p + 2]
        kw = k_p * jnp.where(lo, jnp.exp(ge0 - gcs[0]), jnp.exp(ge1 - gcs[1]))
        ub = u.astype(BF16)
        for s in range(n_seg):
            r0 = s * t_len
            dec_s = jnp.where(rsel, jnp.exp(ge0[r0:r0 + 1, :]), jnp.exp(ge1[r0:r0 + 1, :]))
            upd = _dot_tn(_seg_rows(kw, rowi, t_len, s, n_seg).astype(BF16), ub)
            s_scr[s, p] = dec_s * s_scr[s, p] + jnp.where(bd, upd, 0.0)
        outs.append(_head_rmsnorm(o, lo))
        yield
    o_all = jnp.concatenate(outs, axis=-1) * ng_ref[...] * jax.nn.silu(z_ref[...])
    o_ref[...] = o_all.astype(BF16)

    def finish():
        convn_ref[...] = tail
        _store_pair_states(s_scr, sn_ref, n_seg, transpose=False)
    return finish


def gdn_spec(st, w, lb, n_seg):
    t_len = CH // n_seg
    gp = _gate_rows([(L_GDN_A, w['gdn_a_log']), (L_GDN_A, w['gdn_dt_bias'])])
    ng = jnp.tile(w['gdn_norm'], GDN_H).reshape(1, BRANCH_W)
    spec = (_gdn_kernel, [COLS['gdn_in'], COLS['gdn_z'], (GATE_COL0, LANES)],
            [st['gdn_conv'], st['gdn']], [w['gdn_conv_w'], gp, ng],
            [pltpu.VMEM((n_seg, 8 + t_len, GDN_CONV_CH), F32), pltpu.VMEM((n_seg, 2, 128, 128), F32)])
    return spec, lambda convn, sn: {'gdn_conv': convn, 'gdn': sn}


def _hgrn_levels(t_len):
    lv, n = [], t_len
    while n >= 2:
        lv.append(n)
        n //= 2
    return lv


def _hgrn_cmat(t_len):
    t = np.arange(CH)[:, None]
    j = np.arange(CH)[None, :]
    same = (t // t_len) == (j // t_len)
    mats = [same & (j <= t), same]
    for n in _hgrn_levels(t_len):
        mid = (t // n) * n + n // 2
        second = t % n >= n // 2
        mats.append((second & (j >= mid) & (j <= t)) | (~second & (j > t) & (j <= mid - 1)))
    return jnp.asarray(np.concatenate(mats, axis=0).astype(np.float32), dtype=BF16)


def _split3(x):
    hi = x.astype(BF16)
    r = x - hi.astype(F32)
    mid = r.astype(BF16)
    return hi, mid, (r - mid.astype(F32)).astype(BF16)


def _hgrn_kernel(x_ref, s0_ref, cm_ref, lb_ref, ng_ref, o_ref, sn_ref, s_scr, *, n_seg, init):
    t_len = CH // n_seg
    levels = _hgrn_levels(t_len)
    if init:
        _load_pair_states(s0_ref, s_scr, n_seg, transpose=True)
        return None

    lane = lax.broadcasted_iota(jnp.int32, (CH, LANES), 1)
    rowi = lax.broadcasted_iota(jnp.int32, (CH, 1), 0)
    row = lax.broadcasted_iota(jnp.int32, (CH, CH), 0)
    col = lax.broadcasted_iota(jnp.int32, (CH, CH), 1)
    lo = lane < 64
    r128 = lax.broadcasted_iota(jnp.int32, (LANES, LANES), 0)
    l128 = lax.broadcasted_iota(jnp.int32, (LANES, LANES), 1)
    bd = (r128 < 64) == (l128 < 64)

    lb = lb_ref[...]
    f_pre = x_ref[:, HG_QK:2 * HG_QK]
    log_f = jnp.log(lb + (1.0 - lb) * jax.nn.sigmoid(f_pre))
    k_in = (1.0 - lb) * jax.nn.sigmoid(-f_pre)
    cm = cm_ref[...]
    ex = None
    for part in _split3(log_f):
        t = _dot(cm, part)
        ex = t if ex is None else ex + t
    b = ex[0:CH]
    b_end = ex[CH:2 * CH]
    yield
    outs = []
    for p in range(2):
        ls = slice(128 * p, 128 * (p + 1))
        q_p = x_ref[:, ls]
        k_p = k_in[:, ls]
        v_p = x_ref[:, 2 * HG_QK + 128 * p:2 * HG_QK + 128 * (p + 1)]
        qk = q_p * k_p
        qe = (q_p * jnp.exp(b[:, ls])).astype(BF16)
        o = _seg_sum([_dot_nt(qe, s_scr[s, p].astype(BF16)) for s in range(n_seg)], rowi, t_len)
        scales = [jnp.exp(ex[(2 + li) * CH:(3 + li) * CH, ls]) for li in range(len(levels))]
        for j in range(2):
            mj = lo if j == 0 else jnp.logical_not(lo)
            diag = jnp.sum(jnp.where(mj, qk, 0.0), axis=-1, keepdims=True)
            att = jnp.where(row == col, diag, 0.0)
            for li, n in enumerate(levels):
                tq = (rowi % n) >= (n // 2)
                qt = jnp.where(jnp.logical_and(mj, tq), q_p * scales[li], 0.0)
                kt = jnp.where(tq, 0.0, k_p * scales[li])
                att = att + jnp.where((row // n) == (col // n), _dot_nt(qt.astype(BF16), kt.astype(BF16)), 0.0)
                if li % 2 == 1:
                    yield
            o = o + _dot(att.astype(BF16), jnp.where(mj, v_p, 0.0).astype(BF16))
            yield
        kw = (k_p * jnp.exp(b_end[:, ls] - b[:, ls])).astype(BF16)
        for s in range(n_seg):
            r0 = s * t_len
            upd = _dot_tn(_seg_rows(v_p, rowi, t_len, s, n_seg).astype(BF16), kw)
            s_scr[s, p] = jnp.exp(b_end[r0:r0 + 1, ls]) * s_scr[s, p] + jnp.where(bd, upd, 0.0)
        outs.append(_head_rmsnorm(o, lo))
    o_all = jnp.concatenate(outs, axis=-1) * ng_ref[...] * jax.nn.silu(x_ref[:, 3 * HG_QK:4 * HG_QK])
    o_ref[...] = o_all.astype(BF16)

    def finish():
        _store_pair_states(s_scr, sn_ref, n_seg, transpose=True)
    return finish


def hgrn_spec(st, w, lb, n_seg):
    t_len = CH // n_seg
    ng = jnp.tile(w['hgrn_norm'], HG_H).reshape(1, BRANCH_W)
    assert COLS['hg_f'][0] == COLS['hg_q'][0] + HG_QK and COLS['hg_g'][0] == COLS['hg_q'][0] + 3 * HG_QK
    spec = (_hgrn_kernel, [(COLS['hg_q'][0], 4 * HG_QK)], [st['hgrn']],
            [_hgrn_cmat(t_len), lb.reshape(1, HG_QK), ng], [pltpu.VMEM((n_seg, 2, 128, 128), F32)])
    return spec, lambda sn: {'hgrn': sn}


def _ssd_kernel(xin_ref, z_ref, sm_ref, conv0_ref, h0_ref, cw_ref, cb_ref, gp_ref, dvec_ref, ng_ref,
                o_ref, convn_ref, hn_ref, ext_scr, h_scr, *, n_seg, init):
    t_len = CH // n_seg
    if init:
        h_scr[...] = h0_ref[...].reshape(n_seg, 2, 128, SSD_N)
        ext_scr[:, 5:8, :] = conv0_ref[...]
        return None

    xc, tail = _conv_silu(xin_ref[...], ext_scr, cw_ref, cb_ref[...], t_len, n_seg)
    yield
    sx, bm, cm = xc[:, :256], xc[:, 256:512], xc[:, 512:768]
    same, tri, _ = _seg_masks(t_len)
    lane = lax.broadcasted_iota(jnp.int32, (CH, LANES), 1)
    rowi = lax.broadcasted_iota(jnp.int32, (CH, 1), 0)
    gl = jnp.logical_and(lane >= L_SSD_DT, lane < L_SSD_DT + SSD_H)
    dt = jnp.where(gl, _softplus(sm_ref[...] + gp_ref[1:2, :]), 0.0)
    da = -jnp.exp(gp_ref[0:1, :]) * dt
    cum = _dot(tri.astype(F32), da, precision=HI)
    cum_end = _dot(same.astype(F32), da, precision=HI)
    cum_r = _row_forms(cum, 16)
    dt_r = _row_forms(dt, 16)
    yield
    lo = lane < 64
    rsel = lax.broadcasted_iota(jnp.int32, (LANES, 1), 0) < 64
    ys = []
    for g in range(SSD_G):
        cg = cm[:, 128 * g:128 * (g + 1)].astype(BF16)
        bg = bm[:, 128 * g:128 * (g + 1)].astype(BF16)
        sxp = sx[:, 128 * g:128 * (g + 1)]
        cb = _dot_nt(cg, bg)
        yst = _seg_sum([_dot_nt(cg, h_scr[s, g].astype(BF16)) for s in range(n_seg)], rowi, t_len)
        yatt = jnp.zeros((CH, LANES), F32)
        cols = []
        for j in range(2):
            l = L_SSD_DT + 2 * g + j
            cc = cum[:, l:l + 1]
            dec = _masked_exp(cc - cum_r[l:l + 1, :], tri)
            att = cb * dec * dt_r[l:l + 1, :]
            xm = jnp.where(lo if j == 0 else jnp.logical_not(lo), sxp, 0.0)
            yatt = yatt + _dot(att.astype(BF16), xm.astype(BF16))
            cols.append((cc, dt[:, l:l + 1] * jnp.exp(cum_end[:, l:l + 1] - cc)))
            yield
        ys.append(jnp.where(lo, jnp.exp(cols[0][0]), jnp.exp(cols[1][0])) * yst + yatt)
        xw = sxp * jnp.where(lo, cols[0][1], cols[1][1])
        for s in range(n_seg):
            r0 = s * t_len
            l = L_SSD_DT + 2 * g
            e0 = jnp.exp(cum_end[r0:r0 + 1, l:l + 1])
            e1 = jnp.exp(cum_end[r0:r0 + 1, l + 1:l + 2])
            upd = _dot_tn(_seg_rows(xw, rowi, t_len, s, n_seg).astype(BF16), bg)
            h_scr[s, g] = jnp.where(rsel, e0, e1) * h_scr[s, g] + upd
        yield
    y_all = jnp.concatenate(ys, axis=-1) + dvec_ref[...] * sx
    o_ref[...] = _rms(y_all * jax.nn.silu(z_ref[...]), ng_ref[...]).astype(BF16)

    def finish():
        convn_ref[...] = tail
        hn_ref[...] = h_scr[...].reshape(n_seg, SSD_H, 64, SSD_N)
    return finish


def ssd_spec(st, w, lb, n_seg):
    t_len = CH // n_seg
    gp = _gate_rows([(L_SSD_DT, w['ssd_a_log']), (L_SSD_DT, w['ssd_dt_bias'])])
    dvec = jnp.repeat(w['ssd_d'], BRANCH_W // SSD_H).reshape(1, BRANCH_W)
    spec = (_ssd_kernel, [COLS['ssd_in'], COLS['ssd_z'], (GATE_COL0, LANES)],
            [st['ssd_conv'], st['ssd']],
            [w['ssd_conv_w'], w['ssd_conv_b'].reshape(1, SSD_CONV_CH), gp, dvec, w['ssd_norm'].reshape(1, BRANCH_W)],
            [pltpu.VMEM((n_seg, 8 + t_len, SSD_CONV_CH), F32), pltpu.VMEM((n_seg, 2, 128, SSD_N), F32)])
    return spec, lambda convn, hn: {'ssd_conv': convn, 'ssd': hn}


def _mlstm_kernel(vo_ref, qk_ref, sm_ref, c0_ref, n0_ref, m0_ref, gp_ref, ng_ref, o_ref, cn_ref, nn_ref, mn_ref,
                  c_scr, n_scr, m_scr, *, n_seg, init):
    t_len = CH // n_seg
    if init:
        for s in range(n_seg):
            c_scr[s] = jnp.concatenate(
                [jnp.concatenate([c0_ref[s, h] if g == h else jnp.zeros((ML_DK, 64), F32) for g in range(ML_H)], axis=1)
                 for h in range(ML_H)], axis=0)
        n_scr[...] = n0_ref[...]
        m_scr[...] = m0_ref[...]
        return None

    same, tri, _ = _seg_masks(t_len)
    lane = lax.broadcasted_iota(jnp.int32, (CH, LANES), 1)
    lane256 = lax.broadcasted_iota(jnp.int32, (CH, BRANCH_W), 1)
    rowi = lax.broadcasted_iota(jnp.int32, (CH, 1), 0)
    r128 = lax.broadcasted_iota(jnp.int32, (LANES, 1), 0)
    neg = jnp.float32(-jnp.inf)

    v_all = vo_ref[:, 0:BRANCH_W]
    q_all = qk_ref[:, 0:ML_QK]
    k_all = qk_ref[:, ML_QK:2 * ML_QK] * (ML_DK ** -0.5)
    sm = sm_ref[...]
    ig = sm + gp_ref[0:1, :]
    fl = jnp.logical_and(lane >= L_ML_F, lane < L_ML_F + ML_H)
    lf = jnp.where(fl, -_softplus(-(sm + gp_ref[1:2, :])), 0.0)
    b = _dot(tri.astype(F32), lf, precision=HI)
    b_end = _dot(same.astype(F32), lf, precision=HI)
    b_r = _row_forms(b)
    ig_r = _row_forms(ig)
    yield
    mm = m_scr[...]
    qb = q_all.astype(BF16)
    kb = k_all.astype(BF16)
    qn = _seg_sum([_dot(qb, n_scr[s].astype(BF16)) for s in range(n_seg)], rowi, t_len)
    qc = _seg_sum([_dot(qb, c_scr[s].astype(BF16)) for s in range(n_seg)], rowi, t_len)
    yield
    num_att = jnp.zeros((CH, BRANCH_W), F32)
    w_ins, dens, w_ends, a_ends, m_ends = [], [], [], [], []
    for h in range(ML_H):
        li, lf_ = L_ML_I + h, L_ML_F + h
        bc = b[:, lf_:lf_ + 1]
        bec = b_end[:, lf_:lf_ + 1]
        igc = ig[:, li:li + 1]
        mmc = mm[:, li:li + 1]
        br = b_r[lf_:lf_ + 1, :]
        igr = ig_r[li:li + 1, :]
        diff = igr - br
        cmx = jnp.max(jnp.where(tri, diff, neg), axis=-1, keepdims=True)
        smx = jnp.max(jnp.where(same, diff, neg), axis=-1, keepdims=True)
        m_c = bc + jnp.maximum(mmc, cmx)
        m_end = bec + jnp.maximum(mmc, smx)
        w_in = jnp.exp(bc + mmc - m_c)
        logw = bc - br + igr - m_c
        mq = jnp.logical_and(lane >= ML_DK * h, lane < ML_DK * (h + 1))
        qk = _dot_nt(jnp.where(mq, q_all, 0.0).astype(BF16), kb)
        wts = _masked_exp(logw, tri) * qk
        yield
        mv = jnp.logical_and(lane256 >= 64 * h, lane256 < 64 * (h + 1))
        num_att = num_att + _dot(wts.astype(BF16), jnp.where(mv, v_all, 0.0).astype(BF16))
        nq = w_in * qn[:, li:li + 1] + jnp.sum(wts, axis=-1, keepdims=True)
        w_ins.append(w_in)
        dens.append(jnp.maximum(jnp.abs(nq), jnp.exp(-m_c)))
        w_ends.append(jnp.exp(bec - bc + igc - m_end))
        a_ends.append(jnp.exp(bec + mmc - m_end))
        m_ends.append(m_end)
        yield
    num = _quarter_sel(lane256, 64, w_ins) * qc + num_att
    hout = num / _quarter_sel(lane256, 64, dens)
    outs = [_head_rmsnorm(hout[:, 128 * p:128 * (p + 1)], lane < 64) for p in range(2)]
    o_all = jnp.concatenate(outs, axis=-1) * ng_ref[...] * jax.nn.sigmoid(vo_ref[:, BRANCH_W:2 * BRANCH_W])
    o_ref[...] = o_all.astype(BF16)
    yield

    kw = k_all * _quarter_sel(lane, ML_DK, w_ends)
    wend_tile = jnp.zeros((CH, LANES), F32)
    m_tile = jnp.zeros((CH, LANES), F32)
    for h in range(ML_H):
        wend_tile = jnp.where(lane == L_ML_I + h, w_ends[h], wend_tile)
        m_tile = jnp.where(lane == L_ML_I + h, m_ends[h], m_tile)
    m_scr[...] = m_tile
    vb = v_all.astype(BF16)
    wb = wend_tile.astype(BF16)
    rc = lax.broadcasted_iota(jnp.int32, (LANES, BRANCH_W), 0)
    lc = lax.broadcasted_iota(jnp.int32, (LANES, BRANCH_W), 1)
    bd_c = (rc // ML_DK) == (lc // 64)
    rn = lax.broadcasted_iota(jnp.int32, (LANES, LANES), 0)
    ln = lax.broadcasted_iota(jnp.int32, (LANES, LANES), 1)
    bd_n = ln == (rn // ML_DK) + L_ML_I
    for s in range(n_seg):
        r0 = s * t_len
        a_sel = _quarter_sel(r128, ML_DK, [a[r0:r0 + 1, :] for a in a_ends])
        upd_c = _dot_tn(_seg_rows(kw, rowi, t_len, s, n_seg).astype(BF16), vb)
        upd_n = _dot_tn(_seg_rows(k_all, rowi, t_len, s, n_seg).astype(BF16), wb)
        c_scr[s] = a_sel * c_scr[s] + jnp.where(bd_c, upd_c, 0.0)
        n_scr[s] = a_sel * n_scr[s] + jnp.where(bd_n, upd_n, 0.0)

    def finish():
        for s in range(n_seg):
            for h in range(ML_H):
                cn_ref[s, h] = c_scr[s, ML_DK * h:ML_DK * (h + 1), 64 * h:64 * (h + 1)]
        nn_ref[...] = n_scr[...]
        mn_ref[...] = m_tile
    return finish


def mlstm_spec(st, w, lb, n_seg):
    t_len = CH // n_seg
    c0, n0, m0 = st['ml_c'], st['ml_n'], st['ml_m']
    bsz = c0.shape[0]
    gp = _gate_rows([(L_ML_I, w['ml_ig_b']), (L_ML_F, w['ml_fg_b'])])
    ng = jnp.tile(w['ml_norm'], ML_H).reshape(1, BRANCH_W)
    eye = jnp.eye(ML_H, dtype=F32)
    pad = ((0, 0), (L_ML_I, LANES - L_ML_I - ML_H))
    n_bd = jnp.pad(jnp.einsum('bhk,hg->bhkg', n0, eye).reshape(bsz, ML_QK, ML_H), ((0, 0),) + pad)
    m_exp = jnp.pad(jnp.repeat(m0, t_len, axis=0), pad)
    assert COLS['ml_o'][0] == COLS['ml_v'][0] + BRANCH_W and COLS['ml_k'][0] == COLS['ml_q'][0] + ML_QK
    spec = (_mlstm_kernel, [(COLS['ml_v'][0], 2 * BRANCH_W), (COLS['ml_q'][0], 2 * ML_QK), (GATE_COL0, LANES)],
            [c0, n_bd, m_exp], [gp, ng],
            [pltpu.VMEM((n_seg, ML_QK, BRANCH_W), F32), pltpu.VMEM((n_seg, ML_QK, LANES), F32),
             pltpu.VMEM((CH, LANES), F32)])

    def finish(c_new, nn, mn):
        n_new = jnp.einsum('bhkg,hg->bhk', nn[:, :, L_ML_I:L_ML_I + ML_H].reshape(bsz, ML_H, ML_DK, ML_H), eye)
        return {'ml_c': c_new, 'ml_n': n_new, 'ml_m': mn[::t_len, L_ML_I:L_ML_I + ML_H]}
    return spec, finish


def _group_specs(tm, width, n_first):
    return [pl.BlockSpec((tm, width), lambda i: (jnp.minimum(i, n_first - 1), 0)),
            pl.BlockSpec((tm, width), lambda i: (jnp.maximum(i - n_first, 0), 0))]


def _merge_kernel(h_ref, gates_ref, brp_ref, brs_ref, wb_ref, wo_ref, o_ref, *, n_first):
    br = jnp.where(pl.program_id(0) < n_first, brp_ref[...], brs_ref[...])
    merged = None
    for n in range(N_BRANCH):
        y = _dot(br[:, n * BRANCH_W:(n + 1) * BRANCH_W], wb_ref[n])
        z = gates_ref[:, n * D_MODEL:(n + 1) * D_MODEL].astype(F32)
        t = (0.5 * jnp.tanh(0.5 * z) + 0.5) * y
        merged = t if merged is None else merged + t
    o_ref[...] = h_ref[...] + _dot(merged.astype(BF16), wo_ref[...])


def merge(h, gates, br_groups, w_branch, w_out, tm):
    m, d = h.shape
    n_first = br_groups[0].shape[0] // tm
    assert br_groups[0].shape[0] % tm == 0 and br_groups[1].shape[0] % tm == 0
    return pl.pallas_call(
        functools.partial(_merge_kernel, n_first=n_first),
        grid=(m // tm,),
        in_specs=[pl.BlockSpec((tm, d), lambda i: (i, 0)),
                  pl.BlockSpec((tm, N_BRANCH * d), lambda i: (i, 0)),
                  *_group_specs(tm, N_BRANCH * BRANCH_W, n_first),
                  pl.BlockSpec((N_BRANCH, BRANCH_W, d), lambda i: (0, 0, 0)),
                  pl.BlockSpec((d, d), lambda i: (0, 0))],
        out_specs=pl.BlockSpec((tm, d), lambda i: (i, 0)),
        out_shape=jax.ShapeDtypeStruct((m, d), F32),
        compiler_params=pltpu.CompilerParams(
            dimension_semantics=("parallel",), vmem_limit_bytes=VMEM_LIMIT),
        name="merge",
    )(h, gates, *br_groups, w_branch, w_out)


def _ffn_kernel(h_ref, g_ref, wg_ref, wu_ref, wd_ref, o_ref, u_ref):
    f = pl.program_id(1)

    @pl.when(f == 0)
    def _():
        u_ref[...] = _rms(h_ref[...], g_ref[...]).astype(BF16)

    u = u_ref[...]
    a = jax.nn.silu(_dot(u, wg_ref[...]))
    b = _dot(u, wu_ref[...])
    y = _dot((a * b).astype(BF16), wd_ref[...])

    @pl.when(f == 0)
    def _():
        o_ref[...] = h_ref[...] + y

    @pl.when(f != 0)
    def _():
        o_ref[...] += y


def ffn(h, g, wg, wu, wd, tm, tf):
    m, d = h.shape
    ff = wg.shape[1]
    return pl.pallas_call(
        _ffn_kernel,
        grid=(m // tm, ff // tf),
        in_specs=[pl.BlockSpec((tm, d), lambda i, f: (i, 0)),
                  pl.BlockSpec((1, d), lambda i, f: (0, 0)),
                  pl.BlockSpec((d, tf), lambda i, f: (0, f)),
                  pl.BlockSpec((d, tf), lambda i, f: (0, f)),
                  pl.BlockSpec((tf, d), lambda i, f: (f, 0))],
        out_specs=pl.BlockSpec((tm, d), lambda i, f: (i, 0)),
        out_shape=jax.ShapeDtypeStruct((m, d), F32),
        scratch_shapes=[pltpu.VMEM((tm, d), BF16)],
        compiler_params=pltpu.CompilerParams(
            dimension_semantics=("parallel", "arbitrary"), vmem_limit_bytes=VMEM_LIMIT),
        name="ffn",
    )(h, g.reshape(1, d), wg, wu, wd)


def _router_kernel(h_ref, g_ref, wr_ref, u_ref, w_ref, i_ref):
    u = _rms(h_ref[...], g_ref[...])
    u_ref[...] = u
    logits = _dot(u, wr_ref[...], precision=HI)
    lane = lax.broadcasted_iota(jnp.int32, logits.shape, 1)
    neg = jnp.float32(-jnp.inf)
    logits = jnp.where(lane < N_EXPERTS, logits, neg)
    m1 = jnp.max(logits, axis=-1, keepdims=True)
    i1 = jnp.min(jnp.where(logits == m1, lane, LANES), axis=-1, keepdims=True)
    rest = jnp.where(lane == i1, neg, logits)
    m2 = jnp.max(rest, axis=-1, keepdims=True)
    i2 = jnp.min(jnp.where(rest == m2, lane, LANES), axis=-1, keepdims=True)
    e = jnp.exp(m2 - m1)
    den = 1.0 + e
    w_ref[...] = jnp.where(lane == 0, 1.0 / den, jnp.where(lane == 1, e / den, 0.0))
    i_ref[...] = jnp.where(lane == 0, i1, jnp.where(lane == 1, i2, 0))


def router(h, g, w_router, tm):
    m, d = h.shape
    wr = jnp.pad(w_router, ((0, 0), (0, LANES - N_EXPERTS)))
    return pl.pallas_call(
        _router_kernel,
        grid=(m // tm,),
        in_specs=[pl.BlockSpec((tm, d), lambda i: (i, 0)),
                  pl.BlockSpec((1, d), lambda i: (0, 0)),
                  pl.BlockSpec((d, LANES), lambda i: (0, 0))],
        out_specs=[pl.BlockSpec((tm, d), lambda i: (i, 0)),
                   pl.BlockSpec((tm, LANES), lambda i: (i, 0)),
                   pl.BlockSpec((tm, LANES), lambda i: (i, 0))],
        out_shape=[jax.ShapeDtypeStruct((m, d), F32),
                   jax.ShapeDtypeStruct((m, LANES), F32),
                   jax.ShapeDtypeStruct((m, LANES), jnp.int32)],
        compiler_params=pltpu.CompilerParams(
            dimension_semantics=("parallel",), vmem_limit_bytes=VMEM_LIMIT),
        name="router",
    )(h, g.reshape(1, d), wr)


def _row_copies(src_hbm, dst_hbm, idx_ref, base, buf, sem, n_rows, gather, wait):
    def body(r, carry):
        row = idx_ref[base + r]
        if gather:
            cp = pltpu.make_async_copy(src_hbm.at[pl.ds(row, 1)], buf.at[pl.ds(r, 1)], sem)
        else:
            cp = pltpu.make_async_copy(buf.at[pl.ds(r, 1)], dst_hbm.at[pl.ds(row, 1)], sem)
        if wait:
            cp.wait()
        else:
            cp.start()
        return carry
    lax.fori_loop(0, n_rows, body, 0, unroll=8)


def _expert_kernel(te_ref, nt_ref, src_ref, dst_ref, u_hbm, wg_ref, wu_ref, wd_ref, y_hbm,
                   xbuf, xb_ref, acc_ref, obuf, gsem, ssem, *, tm):
    i = pl.program_id(0)
    f = pl.program_id(1)
    last_f = pl.num_programs(1) - 1
    nt = nt_ref[0]
    gather = functools.partial(_row_copies, u_hbm, None, src_ref, gather=True, n_rows=tm)
    scatter = functools.partial(_row_copies, None, y_hbm, dst_ref, buf=obuf, sem=ssem.at[0], gather=False, n_rows=tm)

    @pl.when(jnp.logical_and(i < nt, f == 0))
    def _():
        slot = i % 2

        @pl.when(i == 0)
        def _():
            gather(base=0, buf=xbuf.at[0], sem=gsem.at[0], wait=False)
            obuf[...] = jnp.zeros_like(obuf)
            fill = pltpu.make_async_copy(obuf, y_hbm.at[pl.ds(y_hbm.shape[0] - tm, tm)], ssem.at[0])
            fill.start()
            fill.wait()

        gather(base=i * tm, buf=xbuf.at[slot], sem=gsem.at[slot], wait=True)

        @pl.when(i + 1 < nt)
        def _():
            gather(base=(i + 1) * tm, buf=xbuf.at[1 - slot], sem=gsem.at[1 - slot], wait=False)

        xb_ref[...] = xbuf[slot].astype(BF16)

    @pl.when(i < nt)
    def _():
        x = xb_ref[...]
        a = jax.nn.silu(_dot(x, wg_ref[0]))
        b = _dot(x, wu_ref[0])
        y = _dot((a * b).astype(BF16), wd_ref[0])

        @pl.when(f == 0)
        def _():
            acc_ref[...] = y

        @pl.when(f != 0)
        def _():
            acc_ref[...] += y

        @pl.when(f == last_f)
        def _():
            @pl.when(i > 0)
            def _():
                scatter(base=(i - 1) * tm, wait=True)

            obuf[...] = acc_ref[...]
            scatter(base=i * tm, wait=False)

            @pl.when(i == nt - 1)
            def _():
                scatter(base=i * tm, wait=True)


def experts(u, tile_expert, n_tiles, src_tok, dst_row, n_out_rows, wg, wu, wd, tm, tf):
    d = u.shape[1]
    ff = wg.shape[2]
    n_row_tiles = tile_expert.shape[0]
    wmap = lambda i, f, te, nt, src, dst: (te[i], 0, f)
    grid_spec = pltpu.PrefetchScalarGridSpec(
        num_scalar_prefetch=4,
        grid=(n_row_tiles, ff // tf),
        in_specs=[pl.BlockSpec(memory_space=pl.ANY),
                  pl.BlockSpec((1, d, tf), wmap),
                  pl.BlockSpec((1, d, tf), wmap),
                  pl.BlockSpec((1, tf, d), lambda i, f, te, nt, src, dst: (te[i], f, 0))],
        out_specs=pl.BlockSpec(memory_space=pl.ANY),
        scratch_shapes=[pltpu.VMEM((2, tm, d), F32), pltpu.VMEM((tm, d), BF16), pltpu.VMEM((tm, d), F32),
                        pltpu.VMEM((tm, d), F32), pltpu.SemaphoreType.DMA((2,)), pltpu.SemaphoreType.DMA((1,))],
    )
    return pl.pallas_call(
        functools.partial(_expert_kernel, tm=tm),
        grid_spec=grid_spec,
        out_shape=jax.ShapeDtypeStruct((n_out_rows, d), F32),
        compiler_params=pltpu.CompilerParams(
            dimension_semantics=("arbitrary", "arbitrary"), vmem_limit_bytes=VMEM_LIMIT,
            disable_bounds_checks=True),
        name="experts",
    )(tile_expert, n_tiles, src_tok, dst_row, u, wg, wu, wd)


def moe(h, g, w_router, wg, wu, wd, tm_route, tm_e, tf):
    m, d = h.shape
    u, top_w, top_i = router(h, g, w_router, tm_route)
    n_pairs = TOP_K * m
    flat_e = top_i[:, :TOP_K].reshape(-1)
    onehot = (flat_e[:, None] == jnp.arange(N_EXPERTS, dtype=jnp.int32)[None, :]).astype(jnp.int32)
    rank = jnp.sum((jnp.cumsum(onehot, axis=0) - 1) * onehot, axis=1)
    counts = jnp.sum(onehot, axis=0)
    tiles_per = (counts + tm_e - 1) // tm_e
    tile_end = jnp.cumsum(tiles_per)
    tile_start = tile_end - tiles_per
    grouped_row = tile_start[flat_e] * tm_e + rank
    n_rows = n_pairs + N_EXPERTS * tm_e
    n_row_tiles = n_rows // tm_e
    pair_ids = jnp.arange(n_pairs, dtype=jnp.int32)
    pair_at = jnp.full((n_rows,), -1, jnp.int32).at[grouped_row].set(pair_ids)
    spare = n_pairs + jnp.arange(n_rows, dtype=jnp.int32) % tm_e
    src_tok = jnp.where(pair_at >= 0, pair_at // TOP_K, 0)
    dst_row = jnp.where(pair_at >= 0, (pair_at % TOP_K) * m + pair_at // TOP_K, spare)
    tile_ids = jnp.arange(n_row_tiles, dtype=jnp.int32)
    tile_expert = jnp.minimum(jnp.sum((tile_ids[:, None] >= tile_end[None, :]).astype(jnp.int32), axis=1),
                              N_EXPERTS - 1).astype(jnp.int32)
    n_tiles = tile_end[-1:].astype(jnp.int32)
    last_e = tile_expert[jnp.maximum(n_tiles[0] - 1, 0)]
    tile_expert = jnp.where(tile_ids < n_tiles[0], tile_expert, last_e)
    y = experts(u, tile_expert, n_tiles, src_tok, dst_row, n_pairs + tm_e, wg, wu, wd, tm_e, tf)
    return y, top_w


def _ple_kernel(*refs, final, combine, n_first):
    if combine:
        h_ref, y0_ref, y1_ref, tw_ref, *refs = refs
        h = h_ref[...] + (tw_ref[:, 0:1] * y0_ref[...] + tw_ref[:, 1:2] * y1_ref[...])
    else:
        h_ref, *refs = refs
        h = h_ref[...]
    pp_ref, ps_ref, g_ref, wg_ref, wp_ref, gf_ref, *o_refs = refs
    first = pl.program_id(0) < n_first
    p = jnp.where(first, pp_ref[...], ps_ref[...])
    v = _rms(h, g_ref[...]).astype(BF16)
    pg = jax.nn.sigmoid(_dot(v, wg_ref[...]))
    e = _dot(p.astype(BF16), wp_ref[...])
    out = h + pg * e
    if not final:
        o_refs[0][...] = out
        return
    out = _rms(out, gf_ref[...])

    @pl.when(first)
    def _():
        o_refs[0][...] = out

    @pl.when(jnp.logical_not(first))
    def _():
        o_refs[1][...] = out


def ple(h, p_groups, g, w_gate, w_p, g_final, tm, final, expert_out=None):
    m, d = h.shape
    dp = p_groups[0].shape[1]
    n_first = p_groups[0].shape[0] // tm
    assert p_groups[0].shape[0] % tm == 0 and p_groups[1].shape[0] % tm == 0
    rows = lambda wd: pl.BlockSpec((tm, wd), lambda i: (i, 0))
    whole = lambda a, b: pl.BlockSpec((a, b), lambda i: (0, 0))
    extra, extra_specs = (), []
    if expert_out is not None:
        y, top_w = expert_out
        extra = (y, y, top_w)
        extra_specs = [rows(d), pl.BlockSpec((tm, d), lambda i: (i + m // tm, 0)), rows(LANES)]
    if final:
        out_specs = _group_specs(tm, d, n_first)
        out_shape = [jax.ShapeDtypeStruct((a.shape[0], d), F32) for a in p_groups]
    else:
        out_specs, out_shape = rows(d), jax.ShapeDtypeStruct((m, d), F32)
    return pl.pallas_call(
        functools.partial(_ple_kernel, final=final, combine=expert_out is not None, n_first=n_first),
        grid=(m // tm,),
        in_specs=([rows(d)] + extra_specs + _group_specs(tm, dp, n_first)
                  + [whole(1, d), whole(d, d), whole(dp, d), whole(1, d)]),
        out_specs=out_specs,
        out_shape=out_shape,
        compiler_params=pltpu.CompilerParams(
            dimension_semantics=("parallel",), vmem_limit_bytes=VMEM_LIMIT),
        name="ple",
    )(h, *extra, *p_groups, g.reshape(1, d), w_gate, w_p, g_final.reshape(1, d))


STATE_KEYS = ('gdn_conv', 'gdn', 'hgrn', 'ssd_conv', 'ssd', 'ml_c', 'ml_n', 'ml_m')
MIXER_KEYS = ('gdn_conv_w', 'gdn_a_log', 'gdn_dt_bias', 'gdn_norm', 'hgrn_norm',
              'ssd_conv_w', 'ssd_conv_b', 'ssd_a_log', 'ssd_dt_bias', 'ssd_d', 'ssd_norm',
              'ml_ig_b', 'ml_fg_b', 'ml_norm')


MIXER_SPECS = (gdn_spec, hgrn_spec, ssd_spec, mlstm_spec)
_gdn_kernel.n_stages = lambda t_len: 11 + 2 * (t_len.bit_length() - 2)
_hgrn_kernel.n_stages = lambda t_len: 1 + HG_H * (len(_hgrn_levels(t_len)) // 2 + 1)
_ssd_kernel.n_stages = lambda t_len: 8
_mlstm_kernel.n_stages = lambda t_len: 11


def _mixers(proj, row_blk0, st, w, lb, **grid):
    built = [f(st, w, lb, grid['n_seg']) for f in MIXER_SPECS]
    o, *new_states = _mixer_call([b[0] for b in built], proj, row_blk0, **grid)
    new, at = {}, 0
    for spec, finish in built:
        n = len(spec[2])
        new.update(finish(*new_states[at:at + n]))
        at += n
    return o, new


def kernel(x_prompt, x_sample, state_gdn_conv, state_gdn, state_hgrn, state_ssd_conv, state_ssd, state_mlstm_c, state_mlstm_n, state_mlstm_m, p_prompt, p_sample, g_mix, w_in, gdn_conv_w, gdn_a_log, gdn_dt_bias, gdn_norm, hgrn_lb, hgrn_norm, ssd_conv_w, ssd_conv_b, ssd_a_log, ssd_dt_bias, ssd_d, ssd_norm, ml_ig_b, ml_fg_b, ml_norm, w_branch, w_out, g_ffn, w_ff_gate, w_ff_up, w_ff_down, w_router, w_ex_gate, w_ex_up, w_ex_down, w_ple, w_ple_gate, g_ple, g_final):
    prm = {'gdn_conv_w': gdn_conv_w, 'gdn_a_log': gdn_a_log, 'gdn_dt_bias': gdn_dt_bias,
           'gdn_norm': gdn_norm, 'hgrn_norm': hgrn_norm, 'ssd_conv_w': ssd_conv_w,
           'ssd_conv_b': ssd_conv_b, 'ssd_a_log': ssd_a_log, 'ssd_dt_bias': ssd_dt_bias,
           'ssd_d': ssd_d, 'ssd_norm': ssd_norm, 'ml_ig_b': ml_ig_b, 'ml_fg_b': ml_fg_b,
           'ml_norm': ml_norm}
    bp, lp, d = x_prompt.shape
    bs, ls, _ = x_sample.shape
    mp, ms = bp * lp, bs * ls
    assert lp % CH == 0 and CH % ls == 0 and ms % CH == 0 and mp % CH == 0
    st_s = {'gdn_conv': state_gdn_conv, 'gdn': state_gdn, 'hgrn': state_hgrn, 'ssd_conv': state_ssd_conv,
            'ssd': state_ssd, 'ml_c': state_mlstm_c, 'ml_n': state_mlstm_n, 'ml_m': state_mlstm_m}
    st_p = {k: jnp.zeros((bp,) + v.shape[2:], F32) for k, v in st_s.items()}
    grid_p = dict(n_outer=bp, n_chunks=lp // CH, n_seg=1)
    grid_s = dict(n_outer=ms // CH, n_chunks=1, n_seg=CH // ls)

    sm = jax.nn.softmax(hgrn_lb, axis=0)
    lb_all = jnp.cumsum(sm, axis=0) - sm[0]

    h = jnp.concatenate([x_prompt.reshape(mp, d), x_sample.reshape(ms, d)], axis=0)
    new_p = {k: [] for k in STATE_KEYS}
    new_s = {k: [] for k in STATE_KEYS}
    for l in range(DEPTH):
        wl = {k: prm[k][l] for k in MIXER_KEYS}
        gates, mix = norm_matmul(h, g_mix[l], _permute_w_in(w_in[l]), tm=1024, tn=1024)
        br_p, np_ = _mixers(mix, 0, st_p, wl, lb_all[l], **grid_p)
        br_s, ns_ = _mixers(mix, mp // CH, {k: v[l] for k, v in st_s.items()}, wl, lb_all[l], **grid_s)
        h = merge(h, gates, (br_p, br_s), w_branch[l].astype(BF16), w_out[l].astype(BF16), tm=512)
        j = l // 2
        expert_out = None
        if l % 2 == 0:
            h = ffn(h, g_ffn[l], w_ff_gate[j].astype(BF16), w_ff_up[j].astype(BF16),
                    w_ff_down[j].astype(BF16), tm=512, tf=D_FF // 2)
        else:
            expert_out = moe(h, g_ffn[l], w_router[j], w_ex_gate[j].astype(BF16), w_ex_up[j].astype(BF16),
                             w_ex_down[j].astype(BF16), tm_route=512, tm_e=512, tf=D_FF // 2)
        p_groups = (p_prompt[l].reshape(mp, D_PLE), p_sample[l].reshape(ms, D_PLE))
        h = ple(h, p_groups, g_ple[l], w_ple_gate[l].astype(BF16), w_ple[l].astype(BF16), g_final,
                tm=512, final=(l == DEPTH - 1), expert_out=expert_out)
        for k in STATE_KEYS:
            new_p[k].append(np_[k])
            new_s[k].append(ns_[k])
    y_prompt = h[0].reshape(bp, lp, d)
    y_sample = h[1].reshape(bs, ls, d)
    sp = {k: jnp.stack(v) for k, v in new_p.items()}
    ss = {k: jnp.stack(v) for k, v in new_s.items()}
    return (y_prompt, y_sample,
            sp['gdn_conv'], sp['gdn'], sp['hgrn'], sp['ssd_conv'], sp['ssd'], sp['ml_c'], sp['ml_n'], sp['ml_m'],
            ss['gdn_conv'], ss['gdn'], ss['hgrn'], ss['ssd_conv'], ss['ssd'], ss['ml_c'], ss['ml_n'], ss['ml_m'])
```

```python
import functools

import numpy as np
import jax
import jax.numpy as jnp
from jax import lax
from jax.experimental import pallas as pl
from jax.experimental.pallas import tpu as pltpu

F32 = jnp.float32
BF16 = jnp.bfloat16
HI = lax.Precision.HIGHEST

D_MODEL = 1024
DEPTH = 2
D_PLE = 256
N_BRANCH = 4
BRANCH_W = D_MODEL // N_BRANCH
CONV_W = 4
EPS = 1e-6

GDN_H = 4
GDN_DK = 64
GDN_QK = GDN_H * GDN_DK
GDN_CONV_CH = 2 * GDN_QK + BRANCH_W
HG_H = 4
HG_QK = 256
SSD_H = 4
SSD_G = 2
SSD_N = 128
SSD_CONV_CH = BRANCH_W + 2 * SSD_G * SSD_N
ML_H = 4
ML_DK = 32
ML_QK = ML_H * ML_DK
D_FF = ((8 * D_MODEL // 3 + 255) // 256) * 256
N_EXPERTS = 8
TOP_K = 2

_REF_SPLITS = (('gdn_in', GDN_CONV_CH), ('gdn_b', GDN_H), ('gdn_a', GDN_H), ('gdn_z', BRANCH_W),
               ('hg_q', HG_QK), ('hg_f', HG_QK), ('hg_v', BRANCH_W), ('hg_g', BRANCH_W),
               ('ssd_z', BRANCH_W), ('ssd_in', SSD_CONV_CH), ('ssd_dt', SSD_H),
               ('ml_q', ML_QK), ('ml_k', ML_QK), ('ml_v', BRANCH_W), ('ml_i', ML_H), ('ml_f', ML_H),
               ('ml_o', BRANCH_W), ('gates', N_BRANCH * D_MODEL))
_MY_ORDER = ('gdn_in', 'ssd_in', 'gdn_z', 'ssd_z', 'hg_q', 'hg_f', 'hg_v', 'hg_g',
             'ml_v', 'ml_o', 'ml_q', 'ml_k', 'gdn_b', 'gdn_a', 'ssd_dt', 'ml_i', 'ml_f')
LANES = 128
MXU_TILE = 256
CH = 64
N_GATES = N_BRANCH * D_MODEL


def _layout():
    widths = dict(_REF_SPLITS)
    off, out = 0, {}
    for name in _MY_ORDER:
        out[name] = (off, widths[name])
        off += widths[name]
    return out, -(-off // MXU_TILE) * MXU_TILE


COLS, N_MIX = _layout()
GATE_COL0 = COLS['gdn_b'][0]
L_GDN_B, L_GDN_A, L_SSD_DT, L_ML_I, L_ML_F = (COLS[n][0] - GATE_COL0 for n in ('gdn_b', 'gdn_a', 'ssd_dt', 'ml_i', 'ml_f'))
VMEM_LIMIT = 56 * 1024 * 1024


def _ref_offsets():
    off, acc = {}, 0
    for name, wd in _REF_SPLITS:
        off[name] = acc
        acc += wd
    return off, acc


def _permute_kernel(w_ref, ws_ref, o_ref):
    ref_off, _ = _ref_offsets()
    o_ref[:, :N_GATES] = w_ref[:, ref_off['gates']:ref_off['gates'] + N_GATES].astype(BF16)
    for name in _MY_ORDER:
        dst, wd = COLS[name]
        if wd >= LANES:
            o_ref[:, N_GATES + dst:N_GATES + dst + wd] = w_ref[:, ref_off[name]:ref_off[name] + wd].astype(BF16)
    o_ref[:, N_GATES + GATE_COL0:N_GATES + GATE_COL0 + LANES] = ws_ref[...].astype(BF16)
    pad0 = N_GATES + GATE_COL0 + LANES
    o_ref[:, pad0:] = jnp.zeros((o_ref.shape[0], N_GATES + N_MIX - pad0), BF16)


def _permute_w_in(w, tk=256):
    ref_off, n_in = _ref_offsets()
    small = [n for n in _MY_ORDER if COLS[n][1] < LANES]
    ws = jnp.concatenate([w[:, ref_off[n]:ref_off[n] + COLS[n][1]] for n in small], axis=1)
    ws = jnp.pad(ws, ((0, 0), (0, LANES - ws.shape[1])))
    d = w.shape[0]
    return pl.pallas_call(
        _permute_kernel, grid=(d // tk,),
        in_specs=[pl.BlockSpec((tk, n_in), lambda i: (i, 0)), pl.BlockSpec((tk, LANES), lambda i: (i, 0))],
        out_specs=pl.BlockSpec((tk, N_GATES + N_MIX), lambda i: (i, 0)),
        out_shape=jax.ShapeDtypeStruct((d, N_GATES + N_MIX), BF16),
        compiler_params=pltpu.CompilerParams(dimension_semantics=("parallel",), vmem_limit_bytes=VMEM_LIMIT),
        name="permute_w_in",
    )(w, ws)


def _rms(x, g):
    return x * lax.rsqrt(jnp.mean(x * x, axis=-1, keepdims=True) + EPS) * g


def _dot(a, b, **kw):
    return jnp.dot(a, b, preferred_element_type=F32, **kw)


def _dot_nt(a, b, **kw):
    return lax.dot_general(a, b, (((1,), (1,)), ((), ())), preferred_element_type=F32, **kw)


def _dot_tn(a, b, **kw):
    return lax.dot_general(a, b, (((0,), (0,)), ((), ())), preferred_element_type=F32, **kw)


def _norm_matmul_kernel(x_ref, g_ref, w_ref, gates_ref, mix_ref, *, tn):
    xn = _rms(x_ref[...], g_ref[...]).astype(BF16)
    for j in range(N_GATES // tn):
        gates_ref[:, j * tn:(j + 1) * tn] = _dot(xn, w_ref[:, j * tn:(j + 1) * tn]).astype(BF16)
    for j in range(N_MIX // tn):
        mix_ref[:, j * tn:(j + 1) * tn] = _dot(xn, w_ref[:, N_GATES + j * tn:N_GATES + (j + 1) * tn])


def norm_matmul(x, g, w, tm, tn):
    m, d = x.shape
    assert w.shape[1] == N_GATES + N_MIX and N_GATES % tn == 0 and N_MIX % tn == 0
    return pl.pallas_call(
        functools.partial(_norm_matmul_kernel, tn=tn),
        grid=(m // tm,),
        in_specs=[pl.BlockSpec((tm, d), lambda i: (i, 0)),
                  pl.BlockSpec((1, d), lambda i: (0, 0)),
                  pl.BlockSpec((d, N_GATES + N_MIX), lambda i: (0, 0), pipeline_mode=pl.Buffered(1))],
        out_specs=[pl.BlockSpec((tm, N_GATES), lambda i: (i, 0)),
                   pl.BlockSpec((tm, N_MIX), lambda i: (i, 0))],
        out_shape=[jax.ShapeDtypeStruct((m, N_GATES), BF16), jax.ShapeDtypeStruct((m, N_MIX), F32)],
        compiler_params=pltpu.CompilerParams(
            dimension_semantics=("parallel",), vmem_limit_bytes=VMEM_LIMIT),
        name="norm_matmul",
    )(x, g.reshape(1, d), w)


def _seg_masks(t_len):
    row = lax.broadcasted_iota(jnp.int32, (CH, CH), 0)
    col = lax.broadcasted_iota(jnp.int32, (CH, CH), 1)
    same = (row // t_len) == (col // t_len)
    tri = jnp.logical_and(same, col <= row)
    strict = jnp.logical_and(same, col < row)
    return same, tri, strict


def _row_forms(x, n_rows=24):
    r = lax.broadcasted_iota(jnp.int32, (n_rows, LANES), 0)
    l = lax.broadcasted_iota(jnp.int32, (n_rows, LANES), 1)
    return _dot_nt((r == l).astype(F32), x, precision=HI)


def _softplus(x):
    return jnp.maximum(x, 0.0) + jnp.log1p(jnp.exp(-jnp.abs(x)))


def _split2(x):
    hi = x.astype(BF16)
    return hi, (x - hi.astype(F32)).astype(BF16)


def _dot3(a, b):
    return _dot(a[0], b[0]) + (_dot(a[0], b[1]) + _dot(a[1], b[0]))


def _masked_exp(d, mask):
    return jnp.where(mask, jnp.exp(jnp.where(mask, d, 0.0)), 0.0)


def _conv_silu(x, ext_scr, cw_ref, bias, t_len, n_seg):
    w = x.shape[-1]
    ext_scr[:, 8:8 + t_len, :] = x.reshape(n_seg, t_len, w)
    y = cw_ref[3:4, :] * x
    for j in range(1, CONV_W):
        y = y + cw_ref[3 - j:4 - j, :] * ext_scr[:, 8 - j:8 - j + t_len, :].reshape(CH, w)
    if bias is not None:
        y = y + bias
    tail = ext_scr[:, 5 + t_len:8 + t_len, :]
    ext_scr[:, 5:8, :] = tail
    return jax.nn.silu(y), tail


def _halves(xp, lo):
    s_lo = jnp.sum(jnp.where(lo, xp, 0.0), axis=-1, keepdims=True)
    s_hi = jnp.sum(jnp.where(lo, 0.0, xp), axis=-1, keepdims=True)
    return jnp.where(lo, s_lo, s_hi)


def _head_rmsnorm(xp, lo):
    return xp * lax.rsqrt(_halves(xp * xp, lo) * (1.0 / 64) + EPS)


def _head_l2norm(xp, lo):
    return xp * lax.rsqrt(_halves(xp * xp, lo) + EPS)


def _seg_sum(parts, rowi, t_len):
    if len(parts) == 1:
        return parts[0]
    acc = jnp.where(rowi // t_len == 0, parts[0], 0.0)
    for s in range(1, len(parts)):
        acc = acc + jnp.where(rowi // t_len == s, parts[s], 0.0)
    return acc


def _seg_rows(x, rowi, t_len, s, n_seg):
    return x if n_seg == 1 else jnp.where(rowi // t_len == s, x, 0.0)


def _quarter_sel(idx, width, vals):
    out = vals[3]
    for h in (2, 1, 0):
        out = jnp.where(idx < (h + 1) * width, vals[h], out)
    return out


def _gate_rows(pairs):
    t = jnp.zeros((8, LANES), F32)
    for r, (off, v) in enumerate(pairs):
        t = t.at[r, off:off + v.shape[0]].set(v.astype(F32))
    return t


def _block_diag(blocks):
    n = len(blocks)
    z = jnp.zeros_like(blocks[0])
    return jnp.concatenate(
        [jnp.concatenate([blocks[i] if i == j else z for j in range(n)], axis=1) for i in range(n)], axis=0)


def _transpose64(x):
    r = lax.broadcasted_iota(jnp.int32, x.shape, 0)
    c = lax.broadcasted_iota(jnp.int32, x.shape, 1)
    return _dot_nt((r == c).astype(F32), x, precision=HI)


def _load_pair_states(s0_ref, s_scr, n_seg, transpose):
    prep = _transpose64 if transpose else (lambda t: t)
    for s in range(n_seg):
        for p in range(2):
            s_scr[s, p] = _block_diag([prep(s0_ref[s, 2 * p]), prep(s0_ref[s, 2 * p + 1])])


def _store_pair_states(s_scr, sn_ref, n_seg, transpose):
    prep = _transpose64 if transpose else (lambda t: t)
    for s in range(n_seg):
        for p in range(2):
            sn_ref[s, 2 * p] = prep(s_scr[s, p, 0:64, 0:64])
            sn_ref[s, 2 * p + 1] = prep(s_scr[s, p, 64:128, 64:128])


def _interleave(gens, n_stages):
    results = [None] * len(gens)
    pos = [0] * len(gens)
    live = set(range(len(gens)))
    while live:
        k = min(live, key=lambda i: ((pos[i] + 1) / n_stages[i], i))
        try:
            next(gens[k])
            pos[k] += 1
        except StopIteration as stop:
            results[k] = stop.value
            live.remove(k)
    return results


def _fused_mixer_kernel(*refs, n_seg, parts):
    tot = [sum(p[j] for p in parts) for j in range(1, 5)]
    ins, rest = refs[:tot[0]], refs[tot[0]:]
    sts, rest = rest[:tot[1]], rest[tot[1]:]
    prs, rest = rest[:tot[2]], rest[tot[2]:]
    o_ref, rest = rest[0], rest[1:]
    outs, scr = rest[:tot[1]], rest[tot[1]:]
    at = [0, 0, 0, 0]
    calls = []
    for k, (body, n_in, n_st, n_pr, n_scr) in enumerate(parts):
        take = lambda seq, j, n: seq[at[j]:at[j] + n]
        calls.append(functools.partial(
            body, *take(ins, 0, n_in), *take(sts, 1, n_st), *take(prs, 2, n_pr),
            o_ref.at[:, k * BRANCH_W:(k + 1) * BRANCH_W],
            *take(outs, 1, n_st), *take(scr, 3, n_scr), n_seg=n_seg))
        for j, n in enumerate((n_in, n_st, n_pr, n_scr)):
            at[j] += n

    n_stages = [p[0].n_stages(CH // n_seg) for p in parts]

    @pl.when(pl.program_id(1) == 0)
    def _():
        _interleave([call(init=True) for call in calls], n_stages)

    finishers = _interleave([call(init=False) for call in calls], n_stages)

    @pl.when(pl.program_id(1) == pl.num_programs(1) - 1)
    def _():
        for fin in finishers:
            fin()


def _mixer_call(specs, proj, row_blk0, *, n_outer, n_chunks, n_seg):
    rows = n_outer * n_chunks * CH
    rmap = lambda blk: (lambda i, c: (row_blk0 + i * n_chunks + c, blk))
    full = lambda a: pl.BlockSpec(a.shape, lambda i, c: (0,) * a.ndim)
    sblk = lambda a: pl.BlockSpec((a.shape[0] // n_outer,) + a.shape[1:], lambda i, c: (i,) + (0,) * (a.ndim - 1))
    in_blocks = [b for s in specs for b in s[1]]
    state_ins = [a for s in specs for a in s[2]]
    params = [a for s in specs for a in s[3]]
    scratch = [a for s in specs for a in s[4]]
    for off, wd in in_blocks:
        assert off % wd == 0
    parts = tuple((s[0], len(s[1]), len(s[2]), len(s[3]), len(s[4])) for s in specs)
    width = len(specs) * BRANCH_W
    return pl.pallas_call(
        functools.partial(_fused_mixer_kernel, n_seg=n_seg, parts=parts),
        grid=(n_outer, n_chunks),
        in_specs=([pl.BlockSpec((CH, wd), rmap(off // wd)) for off, wd in in_blocks]
                  + [sblk(a) for a in state_ins] + [full(a) for a in params]),
        out_specs=[pl.BlockSpec((CH, width), lambda i, c: (i * n_chunks + c, 0))] + [sblk(a) for a in state_ins],
        out_shape=([jax.ShapeDtypeStruct((rows, width), BF16)]
                   + [jax.ShapeDtypeStruct(a.shape, F32) for a in state_ins]),
        scratch_shapes=scratch,
        compiler_params=pltpu.CompilerParams(
            dimension_semantics=("parallel", "arbitrary"), vmem_limit_bytes=VMEM_LIMIT),
        name="token_mixers",
    )(*([proj] * len(in_blocks)), *state_ins, *params)


def _gdn_kernel(xin_ref, z_ref, sm_ref, conv0_ref, s0_ref, cw_ref, gp_ref, ng_ref,
                o_ref, convn_ref, sn_ref, ext_scr, s_scr, *, n_seg, init):
    t_len = CH // n_seg
    if init:
        _load_pair_states(s0_ref, s_scr, n_seg, transpose=False)
        ext_scr[:, 5:8, :] = conv0_ref[...]
        return None

    xc, tail = _conv_silu(xin_ref[...], ext_scr, cw_ref, None, t_len, n_seg)
    yield
    same, tri, strict = _seg_masks(t_len)
    lane = lax.broadcasted_iota(jnp.int32, (CH, LANES), 1)
    rowi = lax.broadcasted_iota(jnp.int32, (CH, 1), 0)
    lo = lane < 64
    sm = sm_ref[...]
    beta = jax.nn.sigmoid(sm)
    gl = jnp.logical_and(lane >= L_GDN_A, lane < L_GDN_A + GDN_H)
    g = jnp.where(gl, -jnp.exp(gp_ref[0:1, :]) * _softplus(sm + gp_ref[1:2, :]), 0.0)
    gam = _dot(tri.astype(F32), g, precision=HI)
    gam_end = _dot(same.astype(F32), g, precision=HI)
    gam_r = _row_forms(gam, 8)
    yield
    r128 = lax.broadcasted_iota(jnp.int32, (LANES, LANES), 0)
    l128 = lax.broadcasted_iota(jnp.int32, (LANES, LANES), 1)
    bd = (r128 < 64) == (l128 < 64)
    rsel = lax.broadcasted_iota(jnp.int32, (LANES, 1), 0) < 64
    qs_, ks_, atts_, gcs_, a_, x_ = [], [], [], [], [], []
    for p in range(2):
        q_p = _head_l2norm(xc[:, 128 * p:128 * (p + 1)], lo) * (GDN_DK ** -0.5)
        k_p = _head_l2norm(xc[:, GDN_QK + 128 * p:GDN_QK + 128 * (p + 1)], lo)
        v_p = xc[:, 2 * GDN_QK + 128 * p:2 * GDN_QK + 128 * (p + 1)]
        kb = k_p.astype(BF16)
        qs_.append(q_p)
        ks_.append(k_p)
        for j in range(2):
            h = 2 * p + j
            mj = lo if j == 0 else jnp.logical_not(lo)
            kk = _dot_nt(jnp.where(mj, k_p, 0.0).astype(BF16), kb)
            qk = _dot_nt(jnp.where(mj, q_p, 0.0).astype(BF16), kb)
            gc = gam[:, L_GDN_A + h:L_GDN_A + h + 1]
            bc = beta[:, L_GDN_B + h:L_GDN_B + h + 1]
            dec = _masked_exp(gc - gam_r[L_GDN_A + h:L_GDN_A + h + 1, :], tri)
            a_.append(jnp.where(strict, bc * kk * dec, 0.0))
            x_.append(jnp.concatenate([jnp.where(mj, bc * v_p, 0.0),
                                       jnp.where(mj, (bc * jnp.exp(gc)) * k_p, 0.0)], axis=-1))
            atts_.append(qk * dec)
            gcs_.append(gc)
            yield
    eye = jnp.logical_and(tri, jnp.logical_not(strict)).astype(F32)
    sa = [_split2(a) for a in a_]
    t_ = [eye - a for a in a_]
    n = 2
    while n < t_len:
        sa = [_split2(_dot3(s, s)) for s in sa]
        yield
        t_ = [t + _dot3(_split2(t), s) for t, s in zip(t_, sa)]
        yield
        n *= 2
    x_ = [_dot3(_split2(t), _split2(x)) for t, x in zip(t_, x_)]
    yield
    outs = []
    for p in range(2):
        q_p, k_p = qs_[p], ks_[p]
        solv = x_[2 * p][:, :LANES] + x_[2 * p + 1][:, :LANES]
        solk = x_[2 * p][:, LANES:] + x_[2 * p + 1][:, LANES:]
        atts, gcs = atts_[2 * p:2 * p + 2], gcs_[2 * p:2 * p + 2]
        solk_b = solk.astype(BF16)
        qb = q_p.astype(BF16)
        u = solv - _seg_sum([_dot(solk_b, s_scr[s, p].astype(BF16)) for s in range(n_seg)], rowi, t_len)
        qs = _seg_sum([_dot(qb, s_scr[s, p].astype(BF16)) for s in range(n_seg)], rowi, t_len)
        o = jnp.where(lo, jnp.exp(gcs[0]), jnp.exp(gcs[1])) * qs
        for j in range(2):
            mj = lo if j == 0 else jnp.logical_not(lo)
            o = o + _dot(atts[j].astype(BF16), jnp.where(mj, u, 0.0).astype(BF16))
        yield
        ge0 = gam_end[:, L_GDN_A + 2 * p:L_GDN_A + 2 * p + 1]
        ge1 = gam_end[:, L_GDN_A + 2 * p + 1:L_GDN_A + 2 * p + 2]
        kw = k_p * jnp.where(lo, jnp.exp(ge0 - gcs[0]), jnp.exp(ge1 - gcs[1]))
        ub = u.astype(BF16)
        for s in range(n_seg):
            r0 = s * t_len
            dec_s = jnp.where(rsel, jnp.exp(ge0[r0:r0 + 1, :]), jnp.exp(ge1[r0:r0 + 1, :]))
            upd = _dot_tn(_seg_rows(kw, rowi, t_len, s, n_seg).astype(BF16), ub)
            s_scr[s, p] = dec_s * s_scr[s, p] + jnp.where(bd, upd, 0.0)
        outs.append(_head_rmsnorm(o, lo))
        yield
    o_all = jnp.concatenate(outs, axis=-1) * ng_ref[...] * jax.nn.silu(z_ref[...])
    o_ref[...] = o_all.astype(BF16)

    def finish():
        convn_ref[...] = tail
        _store_pair_states(s_scr, sn_ref, n_seg, transpose=False)
    return finish


def gdn_spec(st, w, lb, n_seg):
    t_len = CH // n_seg
    gp = _gate_rows([(L_GDN_A, w['gdn_a_log']), (L_GDN_A, w['gdn_dt_bias'])])
    ng = jnp.tile(w['gdn_norm'], GDN_H).reshape(1, BRANCH_W)
    spec = (_gdn_kernel, [COLS['gdn_in'], COLS['gdn_z'], (GATE_COL0, LANES)],
            [st['gdn_conv'], st['gdn']], [w['gdn_conv_w'], gp, ng],
            [pltpu.VMEM((n_seg, 8 + t_len, GDN_CONV_CH), F32), pltpu.VMEM((n_seg, 2, 128, 128), F32)])
    return spec, lambda convn, sn: {'gdn_conv': convn, 'gdn': sn}


def _hgrn_levels(t_len):
    lv, n = [], t_len
    while n >= 2:
        lv.append(n)
        n //= 2
    return lv


def _hgrn_cmat(t_len):
    t = np.arange(CH)[:, None]
    j = np.arange(CH)[None, :]
    same = (t // t_len) == (j // t_len)
    mats = [same & (j <= t), same]
    for n in _hgrn_levels(t_len):
        mid = (t // n) * n + n // 2
        second = t % n >= n // 2
        mats.append((second & (j >= mid) & (j <= t)) | (~second & (j > t) & (j <= mid - 1)))
    return jnp.asarray(np.concatenate(mats, axis=0).astype(np.float32), dtype=BF16)


def _split3(x):
    hi = x.astype(BF16)
    r = x - hi.astype(F32)
    mid = r.astype(BF16)
    return hi, mid, (r - mid.astype(F32)).astype(BF16)


def _hgrn_kernel(x_ref, s0_ref, cm_ref, lb_ref, ng_ref, o_ref, sn_ref, s_scr, *, n_seg, init):
    t_len = CH // n_seg
    levels = _hgrn_levels(t_len)
    if init:
        _load_pair_states(s0_ref, s_scr, n_seg, transpose=True)
        return None

    lane = lax.broadcasted_iota(jnp.int32, (CH, LANES), 1)
    rowi = lax.broadcasted_iota(jnp.int32, (CH, 1), 0)
    row = lax.broadcasted_iota(jnp.int32, (CH, CH), 0)
    col = lax.broadcasted_iota(jnp.int32, (CH, CH), 1)
    lo = lane < 64
    r128 = lax.broadcasted_iota(jnp.int32, (LANES, LANES), 0)
    l128 = lax.broadcasted_iota(jnp.int32, (LANES, LANES), 1)
    bd = (r128 < 64) == (l128 < 64)

    lb = lb_ref[...]
    f_pre = x_ref[:, HG_QK:2 * HG_QK]
    log_f = jnp.log(lb + (1.0 - lb) * jax.nn.sigmoid(f_pre))
    k_in = (1.0 - lb) * jax.nn.sigmoid(-f_pre)
    cm = cm_ref[...]
    ex = None
    for part in _split3(log_f):
        t = _dot(cm, part)
        ex = t if ex is None else ex + t
    b = ex[0:CH]
    b_end = ex[CH:2 * CH]
    yield
    outs = []
    for p in range(2):
        ls = slice(128 * p, 128 * (p + 1))
        q_p = x_ref[:, ls]
        k_p = k_in[:, ls]
        v_p = x_ref[:, 2 * HG_QK + 128 * p:2 * HG_QK + 128 * (p + 1)]
        qk = q_p * k_p
        qe = (q_p * jnp.exp(b[:, ls])).astype(BF16)
        o = _seg_sum([_dot_nt(qe, s_scr[s, p].astype(BF16)) for s in range(n_seg)], rowi, t_len)
        scales = [jnp.exp(ex[(2 + li) * CH:(3 + li) * CH, ls]) for li in range(len(levels))]
        for j in range(2):
            mj = lo if j == 0 else jnp.logical_not(lo)
            diag = jnp.sum(jnp.where(mj, qk, 0.0), axis=-1, keepdims=True)
            att = jnp.where(row == col, diag, 0.0)
            for li, n in enumerate(levels):
                tq = (rowi % n) >= (n // 2)
                qt = jnp.where(jnp.logical_and(mj, tq), q_p * scales[li], 0.0)
                kt = jnp.where(tq, 0.0, k_p * scales[li])
                att = att + jnp.where((row // n) == (col // n), _dot_nt(qt.astype(BF16), kt.astype(BF16)), 0.0)
                if li % 2 == 1:
                    yield
            o = o + _dot(att.astype(BF16), jnp.where(mj, v_p, 0.0).astype(BF16))
            yield
        kw = (k_p * jnp.exp(b_end[:, ls] - b[:, ls])).astype(BF16)
        for s in range(n_seg):
            r0 = s * t_len
            upd = _dot_tn(_seg_rows(v_p, rowi, t_len, s, n_seg).astype(BF16), kw)
            s_scr[s, p] = jnp.exp(b_end[r0:r0 + 1, ls]) * s_scr[s, p] + jnp.where(bd, upd, 0.0)
        outs.append(_head_rmsnorm(o, lo))
    o_all = jnp.concatenate(outs, axis=-1) * ng_ref[...] * jax.nn.silu(x_ref[:, 3 * HG_QK:4 * HG_QK])
    o_ref[...] = o_all.astype(BF16)

    def finish():
        _store_pair_states(s_scr, sn_ref, n_seg, transpose=True)
    return finish


def hgrn_spec(st, w, lb, n_seg):
    t_len = CH // n_seg
    ng = jnp.tile(w['hgrn_norm'], HG_H).reshape(1, BRANCH_W)
    assert COLS['hg_f'][0] == COLS['hg_q'][0] + HG_QK and COLS['hg_g'][0] == COLS['hg_q'][0] + 3 * HG_QK
    spec = (_hgrn_kernel, [(COLS['hg_q'][0], 4 * HG_QK)], [st['hgrn']],
            [_hgrn_cmat(t_len), lb.reshape(1, HG_QK), ng], [pltpu.VMEM((n_seg, 2, 128, 128), F32)])
    return spec, lambda sn: {'hgrn': sn}


def _ssd_kernel(xin_ref, z_ref, sm_ref, conv0_ref, h0_ref, cw_ref, cb_ref, gp_ref, dvec_ref, ng_ref,
                o_ref, convn_ref, hn_ref, ext_scr, h_scr, *, n_seg, init):
    t_len = CH // n_seg
    if init:
        h_scr[...] = h0_ref[...].reshape(n_seg, 2, 128, SSD_N)
        ext_scr[:, 5:8, :] = conv0_ref[...]
        return None

    xc, tail = _conv_silu(xin_ref[...], ext_scr, cw_ref, cb_ref[...], t_len, n_seg)
    yield
    sx, bm, cm = xc[:, :256], xc[:, 256:512], xc[:, 512:768]
    same, tri, _ = _seg_masks(t_len)
    lane = lax.broadcasted_iota(jnp.int32, (CH, LANES), 1)
    rowi = lax.broadcasted_iota(jnp.int32, (CH, 1), 0)
    gl = jnp.logical_and(lane >= L_SSD_DT, lane < L_SSD_DT + SSD_H)
    dt = jnp.where(gl, _softplus(sm_ref[...] + gp_ref[1:2, :]), 0.0)
    da = -jnp.exp(gp_ref[0:1, :]) * dt
    cum = _dot(tri.astype(F32), da, precision=HI)
    cum_end = _dot(same.astype(F32), da, precision=HI)
    cum_r = _row_forms(cum, 16)
    dt_r = _row_forms(dt, 16)
    yield
    lo = lane < 64
    rsel = lax.broadcasted_iota(jnp.int32, (LANES, 1), 0) < 64
    ys = []
    for g in range(SSD_G):
        cg = cm[:, 128 * g:128 * (g + 1)].astype(BF16)
        bg = bm[:, 128 * g:128 * (g + 1)].astype(BF16)
        sxp = sx[:, 128 * g:128 * (g + 1)]
        cb = _dot_nt(cg, bg)
        yst = _seg_sum([_dot_nt(cg, h_scr[s, g].astype(BF16)) for s in range(n_seg)], rowi, t_len)
        yatt = jnp.zeros((CH, LANES), F32)
        cols = []
        for j in range(2):
            l = L_SSD_DT + 2 * g + j
            cc = cum[:, l:l + 1]
            dec = _masked_exp(cc - cum_r[l:l + 1, :], tri)
            att = cb * dec * dt_r[l:l + 1, :]
            xm = jnp.where(lo if j == 0 else jnp.logical_not(lo), sxp, 0.0)
            yatt = yatt + _dot(att.astype(BF16), xm.astype(BF16))
            cols.append((cc, dt[:, l:l + 1] * jnp.exp(cum_end[:, l:l + 1] - cc)))
            yield
        ys.append(jnp.where(lo, jnp.exp(cols[0][0]), jnp.exp(cols[1][0])) * yst + yatt)
        xw = sxp * jnp.where(lo, cols[0][1], cols[1][1])
        for s in range(n_seg):
            r0 = s * t_len
            l = L_SSD_DT + 2 * g
            e0 = jnp.exp(cum_end[r0:r0 + 1, l:l + 1])
            e1 = jnp.exp(cum_end[r0:r0 + 1, l + 1:l + 2])
            upd = _dot_tn(_seg_rows(xw, rowi, t_len, s, n_seg).astype(BF16), bg)
            h_scr[s, g] = jnp.where(rsel, e0, e1) * h_scr[s, g] + upd
        yield
    y_all = jnp.concatenate(ys, axis=-1) + dvec_ref[...] * sx
    o_ref[...] = _rms(y_all * jax.nn.silu(z_ref[...]), ng_ref[...]).astype(BF16)

    def finish():
        convn_ref[...] = tail
        hn_ref[...] = h_scr[...].reshape(n_seg, SSD_H, 64, SSD_N)
    return finish


def ssd_spec(st, w, lb, n_seg):
    t_len = CH // n_seg
    gp = _gate_rows([(L_SSD_DT, w['ssd_a_log']), (L_SSD_DT, w['ssd_dt_bias'])])
    dvec = jnp.repeat(w['ssd_d'], BRANCH_W // SSD_H).reshape(1, BRANCH_W)
    spec = (_ssd_kernel, [COLS['ssd_in'], COLS['ssd_z'], (GATE_COL0, LANES)],
            [st['ssd_conv'], st['ssd']],
            [w['ssd_conv_w'], w['ssd_conv_b'].reshape(1, SSD_CONV_CH), gp, dvec, w['ssd_norm'].reshape(1, BRANCH_W)],
            [pltpu.VMEM((n_seg, 8 + t_len, SSD_CONV_CH), F32), pltpu.VMEM((n_seg, 2, 128, SSD_N), F32)])
    return spec, lambda convn, hn: {'ssd_conv': convn, 'ssd': hn}


def _mlstm_kernel(vo_ref, qk_ref, sm_ref, c0_ref, n0_ref, m0_ref, gp_ref, ng_ref, o_ref, cn_ref, nn_ref, mn_ref,
                  c_scr, n_scr, m_scr, *, n_seg, init):
    t_len = CH // n_seg
    if init:
        for s in range(n_seg):
            c_scr[s] = jnp.concatenate(
                [jnp.concatenate([c0_ref[s, h] if g == h else jnp.zeros((ML_DK, 64), F32) for g in range(ML_H)], axis=1)
                 for h in range(ML_H)], axis=0)
        n_scr[...] = n0_ref[...]
        m_scr[...] = m0_ref[...]
        return None

    same, tri, _ = _seg_masks(t_len)
    lane = lax.broadcasted_iota(jnp.int32, (CH, LANES), 1)
    lane256 = lax.broadcasted_iota(jnp.int32, (CH, BRANCH_W), 1)
    rowi = lax.broadcasted_iota(jnp.int32, (CH, 1), 0)
    r128 = lax.broadcasted_iota(jnp.int32, (LANES, 1), 0)
    neg = jnp.float32(-jnp.inf)

    v_all = vo_ref[:, 0:BRANCH_W]
    q_all = qk_ref[:, 0:ML_QK]
    k_all = qk_ref[:, ML_QK:2 * ML_QK] * (ML_DK ** -0.5)
    sm = sm_ref[...]
    ig = sm + gp_ref[0:1, :]
    fl = jnp.logical_and(lane >= L_ML_F, lane < L_ML_F + ML_H)
    lf = jnp.where(fl, -_softplus(-(sm + gp_ref[1:2, :])), 0.0)
    b = _dot(tri.astype(F32), lf, precision=HI)
    b_end = _dot(same.astype(F32), lf, precision=HI)
    b_r = _row_forms(b)
    ig_r = _row_forms(ig)
    yield
    mm = m_scr[...]
    qb = q_all.astype(BF16)
    kb = k_all.astype(BF16)
    qn = _seg_sum([_dot(qb, n_scr[s].astype(BF16)) for s in range(n_seg)], rowi, t_len)
    qc = _seg_sum([_dot(qb, c_scr[s].astype(BF16)) for s in range(n_seg)], rowi, t_len)
    yield
    num_att = jnp.zeros((CH, BRANCH_W), F32)
    w_ins, dens, w_ends, a_ends, m_ends = [], [], [], [], []
    for h in range(ML_H):
        li, lf_ = L_ML_I + h, L_ML_F + h
        bc = b[:, lf_:lf_ + 1]
        bec = b_end[:, lf_:lf_ + 1]
        igc = ig[:, li:li + 1]
        mmc = mm[:, li:li + 1]
        br = b_r[lf_:lf_ + 1, :]
        igr = ig_r[li:li + 1, :]
        diff = igr - br
        cmx = jnp.max(jnp.where(tri, diff, neg), axis=-1, keepdims=True)
        smx = jnp.max(jnp.where(same, diff, neg), axis=-1, keepdims=True)
        m_c = bc + jnp.maximum(mmc, cmx)
        m_end = bec + jnp.maximum(mmc, smx)
        w_in = jnp.exp(bc + mmc - m_c)
        logw = bc - br + igr - m_c
        mq = jnp.logical_and(lane >= ML_DK * h, lane < ML_DK * (h + 1))
        qk = _dot_nt(jnp.where(mq, q_all, 0.0).astype(BF16), kb)
        wts = _masked_exp(logw, tri) * qk
        yield
        mv = jnp.logical_and(lane256 >= 64 * h, lane256 < 64 * (h + 1))
        num_att = num_att + _dot(wts.astype(BF16), jnp.where(mv, v_all, 0.0).astype(BF16))
        nq = w_in * qn[:, li:li + 1] + jnp.sum(wts, axis=-1, keepdims=True)
        w_ins.append(w_in)
        dens.append(jnp.maximum(jnp.abs(nq), jnp.exp(-m_c)))
        w_ends.append(jnp.exp(bec - bc + igc - m_end))
        a_ends.append(jnp.exp(bec + mmc - m_end))
        m_ends.append(m_end)
        yield
    num = _quarter_sel(lane256, 64, w_ins) * qc + num_att
    hout = num / _quarter_sel(lane256, 64, dens)
    outs = [_head_rmsnorm(hout[:, 128 * p:128 * (p + 1)], lane < 64) for p in range(2)]
    o_all = jnp.concatenate(outs, axis=-1) * ng_ref[...] * jax.nn.sigmoid(vo_ref[:, BRANCH_W:2 * BRANCH_W])
    o_ref[...] = o_all.astype(BF16)
    yield

    kw = k_all * _quarter_sel(lane, ML_DK, w_ends)
    wend_tile = jnp.zeros((CH, LANES), F32)
    m_tile = jnp.zeros((CH, LANES), F32)
    for h in range(ML_H):
        wend_tile = jnp.where(lane == L_ML_I + h, w_ends[h], wend_tile)
        m_tile = jnp.where(lane == L_ML_I + h, m_ends[h], m_tile)
    m_scr[...] = m_tile
    vb = v_all.astype(BF16)
    wb = wend_tile.astype(BF16)
    rc = lax.broadcasted_iota(jnp.int32, (LANES, BRANCH_W), 0)
    lc = lax.broadcasted_iota(jnp.int32, (LANES, BRANCH_W), 1)
    bd_c = (rc // ML_DK) == (lc // 64)
    rn = lax.broadcasted_iota(jnp.int32, (LANES, LANES), 0)
    ln = lax.broadcasted_iota(jnp.int32, (LANES, LANES), 1)
    bd_n = ln == (rn // ML_DK) + L_ML_I
    for s in range(n_seg):
        r0 = s * t_len
        a_sel = _quarter_sel(r128, ML_DK, [a[r0:r0 + 1, :] for a in a_ends])
        upd_c = _dot_tn(_seg_rows(kw, rowi, t_len, s, n_seg).astype(BF16), vb)
        upd_n = _dot_tn(_seg_rows(k_all, rowi, t_len, s, n_seg).astype(BF16), wb)
        c_scr[s] = a_sel * c_scr[s] + jnp.where(bd_c, upd_c, 0.0)
        n_scr[s] = a_sel * n_scr[s] + jnp.where(bd_n, upd_n, 0.0)

    def finish():
        for s in range(n_seg):
            for h in range(ML_H):
                cn_ref[s, h] = c_scr[s, ML_DK * h:ML_DK * (h + 1), 64 * h:64 * (h + 1)]
        nn_ref[...] = n_scr[...]
        mn_ref[...] = m_tile
    return finish


def mlstm_spec(st, w, lb, n_seg):
    t_len = CH // n_seg
    c0, n0, m0 = st['ml_c'], st['ml_n'], st['ml_m']
    bsz = c0.shape[0]
    gp = _gate_rows([(L_ML_I, w['ml_ig_b']), (L_ML_F, w['ml_fg_b'])])
    ng = jnp.tile(w['ml_norm'], ML_H).reshape(1, BRANCH_W)
    eye = jnp.eye(ML_H, dtype=F32)
    pad = ((0, 0), (L_ML_I, LANES - L_ML_I - ML_H))
    n_bd = jnp.pad(jnp.einsum('bhk,hg->bhkg', n0, eye).reshape(bsz, ML_QK, ML_H), ((0, 0),) + pad)
    m_exp = jnp.pad(jnp.repeat(m0, t_len, axis=0), pad)
    assert COLS['ml_o'][0] == COLS['ml_v'][0] + BRANCH_W and COLS['ml_k'][0] == COLS['ml_q'][0] + ML_QK
    spec = (_mlstm_kernel, [(COLS['ml_v'][0], 2 * BRANCH_W), (COLS['ml_q'][0], 2 * ML_QK), (GATE_COL0, LANES)],
            [c0, n_bd, m_exp], [gp, ng],
            [pltpu.VMEM((n_seg, ML_QK, BRANCH_W), F32), pltpu.VMEM((n_seg, ML_QK, LANES), F32),
             pltpu.VMEM((CH, LANES), F32)])

    def finish(c_new, nn, mn):
        n_new = jnp.einsum('bhkg,hg->bhk', nn[:, :, L_ML_I:L_ML_I + ML_H].reshape(bsz, ML_H, ML_DK, ML_H), eye)
        return {'ml_c': c_new, 'ml_n': n_new, 'ml_m': mn[::t_len, L_ML_I:L_ML_I + ML_H]}
    return spec, finish


def _group_specs(tm, width, n_first):
    return [pl.BlockSpec((tm, width), lambda i: (jnp.minimum(i, n_first - 1), 0)),
            pl.BlockSpec((tm, width), lambda i: (jnp.maximum(i - n_first, 0), 0))]


def _merge_kernel(h_ref, gates_ref, brp_ref, brs_ref, wb_ref, wo_ref, o_ref, *, n_first):
    br = jnp.where(pl.program_id(0) < n_first, brp_ref[...], brs_ref[...])
    merged = None
    for n in range(N_BRANCH):
        y = _dot(br[:, n * BRANCH_W:(n + 1) * BRANCH_W], wb_ref[n])
        z = gates_ref[:, n * D_MODEL:(n + 1) * D_MODEL].astype(F32)
        t = (0.5 * jnp.tanh(0.5 * z) + 0.5) * y
        merged = t if merged is None else merged + t
    o_ref[...] = h_ref[...] + _dot(merged.astype(BF16), wo_ref[...])


def merge(h, gates, br_groups, w_branch, w_out, tm):
    m, d = h.shape
    n_first = br_groups[0].shape[0] // tm
    assert br_groups[0].shape[0] % tm == 0 and br_groups[1].shape[0] % tm == 0
    return pl.pallas_call(
        functools.partial(_merge_kernel, n_first=n_first),
        grid=(m // tm,),
        in_specs=[pl.BlockSpec((tm, d), lambda i: (i, 0)),
                  pl.BlockSpec((tm, N_BRANCH * d), lambda i: (i, 0)),
                  *_group_specs(tm, N_BRANCH * BRANCH_W, n_first),
                  pl.BlockSpec((N_BRANCH, BRANCH_W, d), lambda i: (0, 0, 0)),
                  pl.BlockSpec((d, d), lambda i: (0, 0))],
        out_specs=pl.BlockSpec((tm, d), lambda i: (i, 0)),
        out_shape=jax.ShapeDtypeStruct((m, d), F32),
        compiler_params=pltpu.CompilerParams(
            dimension_semantics=("parallel",), vmem_limit_bytes=VMEM_LIMIT),
        name="merge",
    )(h, gates, *br_groups, w_branch, w_out)


def _ffn_kernel(h_ref, g_ref, wg_ref, wu_ref, wd_ref, o_ref, *, tf):
    h = h_ref[...]
    u = _rms(h, g_ref[...]).astype(BF16)
    y = h
    for c in range(wg_ref.shape[1] // tf):
        cols = slice(c * tf, (c + 1) * tf)
        a = jax.nn.silu(_dot(u, wg_ref[:, cols]))
        b = _dot(u, wu_ref[:, cols])
        y = y + _dot((a * b).astype(BF16), wd_ref[cols, :])
    o_ref[...] = y


def ffn(h, g, wg, wu, wd, tm, tf):
    m, d = h.shape
    ff = wg.shape[1]
    assert ff % tf == 0
    resident = lambda a: pl.BlockSpec(a.shape, lambda i: (0, 0), pipeline_mode=pl.Buffered(1))
    return pl.pallas_call(
        functools.partial(_ffn_kernel, tf=tf),
        grid=(m // tm,),
        in_specs=[pl.BlockSpec((tm, d), lambda i: (i, 0)),
                  pl.BlockSpec((1, d), lambda i: (0, 0)),
                  resident(wg), resident(wu), resident(wd)],
        out_specs=pl.BlockSpec((tm, d), lambda i: (i, 0)),
        out_shape=jax.ShapeDtypeStruct((m, d), F32),
        compiler_params=pltpu.CompilerParams(
            dimension_semantics=("parallel",), vmem_limit_bytes=VMEM_LIMIT),
        name="ffn",
    )(h, g.reshape(1, d), wg, wu, wd)


def _router_kernel(h_ref, g_ref, wr_ref, u_ref, w_ref, i_ref):
    u = _rms(h_ref[...], g_ref[...])
    u_ref[...] = u
    logits = _dot(u, wr_ref[...], precision=HI)
    lane = lax.broadcasted_iota(jnp.int32, logits.shape, 1)
    neg = jnp.float32(-jnp.inf)
    logits = jnp.where(lane < N_EXPERTS, logits, neg)
    m1 = jnp.max(logits, axis=-1, keepdims=True)
    i1 = jnp.min(jnp.where(logits == m1, lane, LANES), axis=-1, keepdims=True)
    rest = jnp.where(lane == i1, neg, logits)
    m2 = jnp.max(rest, axis=-1, keepdims=True)
    i2 = jnp.min(jnp.where(rest == m2, lane, LANES), axis=-1, keepdims=True)
    e = jnp.exp(m2 - m1)
    den = 1.0 + e
    w_ref[...] = jnp.where(lane == 0, 1.0 / den, jnp.where(lane == 1, e / den, 0.0))
    i_ref[...] = jnp.where(lane == 0, i1, jnp.where(lane == 1, i2, 0))


def router(h, g, w_router, tm):
    m, d = h.shape
    wr = jnp.pad(w_router, ((0, 0), (0, LANES - N_EXPERTS)))
    return pl.pallas_call(
        _router_kernel,
        grid=(m // tm,),
        in_specs=[pl.BlockSpec((tm, d), lambda i: (i, 0)),
                  pl.BlockSpec((1, d), lambda i: (0, 0)),
                  pl.BlockSpec((d, LANES), lambda i: (0, 0))],
        out_specs=[pl.BlockSpec((tm, d), lambda i: (i, 0)),
                   pl.BlockSpec((tm, LANES), lambda i: (i, 0)),
                   pl.BlockSpec((tm, LANES), lambda i: (i, 0))],
        out_shape=[jax.ShapeDtypeStruct((m, d), F32),
                   jax.ShapeDtypeStruct((m, LANES), F32),
                   jax.ShapeDtypeStruct((m, LANES), jnp.int32)],
        compiler_params=pltpu.CompilerParams(
            dimension_semantics=("parallel",), vmem_limit_bytes=VMEM_LIMIT),
        name="router",
    )(h, g.reshape(1, d), wr)


def _row_copies(src_hbm, dst_hbm, idx_ref, base, buf, sem, n_rows, gather, wait):
    def body(r, carry):
        row = idx_ref[base + r]
        if gather:
            cp = pltpu.make_async_copy(src_hbm.at[pl.ds(row, 1)], buf.at[pl.ds(r, 1)], sem)
        else:
            cp = pltpu.make_async_copy(buf.at[pl.ds(r, 1)], dst_hbm.at[pl.ds(row, 1)], sem)
        if wait:
            cp.wait()
        else:
            cp.start()
        return carry
    lax.fori_loop(0, n_rows, body, 0, unroll=8)


def _expert_kernel(te_ref, nt_ref, src_ref, dst_ref, u_hbm, wg_ref, wu_ref, wd_ref, y_hbm,
                   xbuf, xb_ref, acc_ref, obuf, gsem, ssem, *, tm):
    i = pl.program_id(0)
    f = pl.program_id(1)
    last_f = pl.num_programs(1) - 1
    nt = nt_ref[0]
    gather = functools.partial(_row_copies, u_hbm, None, src_ref, gather=True, n_rows=tm)
    scatter = functools.partial(_row_copies, None, y_hbm, dst_ref, buf=obuf, sem=ssem.at[0], gather=False, n_rows=tm)

    @pl.when(jnp.logical_and(i < nt, f == 0))
    def _():
        slot = i % 2

        @pl.when(i == 0)
        def _():
            gather(base=0, buf=xbuf.at[0], sem=gsem.at[0], wait=False)
            obuf[...] = jnp.zeros_like(obuf)
            fill = pltpu.make_async_copy(obuf, y_hbm.at[pl.ds(y_hbm.shape[0] - tm, tm)], ssem.at[0])
            fill.start()
            fill.wait()

        gather(base=i * tm, buf=xbuf.at[slot], sem=gsem.at[slot], wait=True)

        @pl.when(i + 1 < nt)
        def _():
            gather(base=(i + 1) * tm, buf=xbuf.at[1 - slot], sem=gsem.at[1 - slot], wait=False)

        xb_ref[...] = xbuf[slot].astype(BF16)

    @pl.when(i < nt)
    def _():
        x = xb_ref[...]
        a = jax.nn.silu(_dot(x, wg_ref[0]))
        b = _dot(x, wu_ref[0])
        y = _dot((a * b).astype(BF16), wd_ref[0])

        @pl.when(f == 0)
        def _():
            acc_ref[...] = y

        @pl.when(f != 0)
        def _():
            acc_ref[...] += y

        @pl.when(f == last_f)
        def _():
            @pl.when(i > 0)
            def _():
                scatter(base=(i - 1) * tm, wait=True)

            obuf[...] = acc_ref[...]
            scatter(base=i * tm, wait=False)

            @pl.when(i == nt - 1)
            def _():
                scatter(base=i * tm, wait=True)


def experts(u, tile_expert, n_tiles, src_tok, dst_row, n_out_rows, wg, wu, wd, tm, tf):
    d = u.shape[1]
    ff = wg.shape[2]
    n_row_tiles = tile_expert.shape[0]
    wmap = lambda i, f, te, nt, src, dst: (te[i], 0, f)
    grid_spec = pltpu.PrefetchScalarGridSpec(
        num_scalar_prefetch=4,
        grid=(n_row_tiles, ff // tf),
        in_specs=[pl.BlockSpec(memory_space=pl.ANY),
                  pl.BlockSpec((1, d, tf), wmap),
                  pl.BlockSpec((1, d, tf), wmap),
                  pl.BlockSpec((1, tf, d), lambda i, f, te, nt, src, dst: (te[i], f, 0))],
        out_specs=pl.BlockSpec(memory_space=pl.ANY),
        scratch_shapes=[pltpu.VMEM((2, tm, d), F32), pltpu.VMEM((tm, d), BF16), pltpu.VMEM((tm, d), F32),
                        pltpu.VMEM((tm, d), F32), pltpu.SemaphoreType.DMA((2,)), pltpu.SemaphoreType.DMA((1,))],
    )
    return pl.pallas_call(
        functools.partial(_expert_kernel, tm=tm),
        grid_spec=grid_spec,
        out_shape=jax.ShapeDtypeStruct((n_out_rows, d), F32),
        compiler_params=pltpu.CompilerParams(
            dimension_semantics=("arbitrary", "arbitrary"), vmem_limit_bytes=VMEM_LIMIT,
            disable_bounds_checks=True),
        name="experts",
    )(tile_expert, n_tiles, src_tok, dst_row, u, wg, wu, wd)


def moe(h, g, w_router, wg, wu, wd, tm_route, tm_e, tf):
    m, d = h.shape
    u, top_w, top_i = router(h, g, w_router, tm_route)
    n_pairs = TOP_K * m
    flat_e = top_i[:, :TOP_K].reshape(-1)
    onehot = (flat_e[:, None] == jnp.arange(N_EXPERTS, dtype=jnp.int32)[None, :]).astype(jnp.int32)
    rank = jnp.sum((jnp.cumsum(onehot, axis=0) - 1) * onehot, axis=1)
    counts = jnp.sum(onehot, axis=0)
    tiles_per = (counts + tm_e - 1) // tm_e
    tile_end = jnp.cumsum(tiles_per)
    tile_start = tile_end - tiles_per
    grouped_row = tile_start[flat_e] * tm_e + rank
    n_rows = n_pairs + N_EXPERTS * tm_e
    n_row_tiles = n_rows // tm_e
    pair_ids = jnp.arange(n_pairs, dtype=jnp.int32)
    pair_at = jnp.full((n_rows,), -1, jnp.int32).at[grouped_row].set(pair_ids)
    spare = n_pairs + jnp.arange(n_rows, dtype=jnp.int32) % tm_e
    src_tok = jnp.where(pair_at >= 0, pair_at // TOP_K, 0)
    dst_row = jnp.where(pair_at >= 0, (pair_at % TOP_K) * m + pair_at // TOP_K, spare)
    tile_ids = jnp.arange(n_row_tiles, dtype=jnp.int32)
    tile_expert = jnp.minimum(jnp.sum((tile_ids[:, None] >= tile_end[None, :]).astype(jnp.int32), axis=1),
                              N_EXPERTS - 1).astype(jnp.int32)
    n_tiles = tile_end[-1:].astype(jnp.int32)
    last_e = tile_expert[jnp.maximum(n_tiles[0] - 1, 0)]
    tile_expert = jnp.where(tile_ids < n_tiles[0], tile_expert, last_e)
    y = experts(u, tile_expert, n_tiles, src_tok, dst_row, n_pairs + tm_e, wg, wu, wd, tm_e, tf)
    return y, top_w


def _ple_kernel(*refs, final, combine, n_first):
    if combine:
        h_ref, y0_ref, y1_ref, tw_ref, *refs = refs
        h = h_ref[...] + (tw_ref[:, 0:1] * y0_ref[...] + tw_ref[:, 1:2] * y1_ref[...])
    else:
        h_ref, *refs = refs
        h = h_ref[...]
    pp_ref, ps_ref, g_ref, wg_ref, wp_ref, gf_ref, *o_refs = refs
    first = pl.program_id(0) < n_first
    p = jnp.where(first, pp_ref[...], ps_ref[...])
    v = _rms(h, g_ref[...]).astype(BF16)
    pg = jax.nn.sigmoid(_dot(v, wg_ref[...]))
    e = _dot(p.astype(BF16), wp_ref[...])
    out = h + pg * e
    if not final:
        o_refs[0][...] = out
        return
    out = _rms(out, gf_ref[...])

    @pl.when(first)
    def _():
        o_refs[0][...] = out

    @pl.when(jnp.logical_not(first))
    def _():
        o_refs[1][...] = out


def ple(h, p_groups, g, w_gate, w_p, g_final, tm, final, expert_out=None):
    m, d = h.shape
    dp = p_groups[0].shape[1]
    n_first = p_groups[0].shape[0] // tm
    assert p_groups[0].shape[0] % tm == 0 and p_groups[1].shape[0] % tm == 0
    rows = lambda wd: pl.BlockSpec((tm, wd), lambda i: (i, 0))
    whole = lambda a, b: pl.BlockSpec((a, b), lambda i: (0, 0))
    extra, extra_specs = (), []
    if expert_out is not None:
        y, top_w = expert_out
        extra = (y, y, top_w)
        extra_specs = [rows(d), pl.BlockSpec((tm, d), lambda i: (i + m // tm, 0)), rows(LANES)]
    if final:
        out_specs = _group_specs(tm, d, n_first)
        out_shape = [jax.ShapeDtypeStruct((a.shape[0], d), F32) for a in p_groups]
    else:
        out_specs, out_shape = rows(d), jax.ShapeDtypeStruct((m, d), F32)
    return pl.pallas_call(
        functools.partial(_ple_kernel, final=final, combine=expert_out is not None, n_first=n_first),
        grid=(m // tm,),
        in_specs=([rows(d)] + extra_specs + _group_specs(tm, dp, n_first)
                  + [whole(1, d), whole(d, d), whole(dp, d), whole(1, d)]),
        out_specs=out_specs,
        out_shape=out_shape,
        compiler_params=pltpu.CompilerParams(
            dimension_semantics=("parallel",), vmem_limit_bytes=VMEM_LIMIT),
        name="ple",
    )(h, *extra, *p_groups, g.reshape(1, d), w_gate, w_p, g_final.reshape(1, d))


STATE_KEYS = ('gdn_conv', 'gdn', 'hgrn', 'ssd_conv', 'ssd', 'ml_c', 'ml_n', 'ml_m')
MIXER_KEYS = ('gdn_conv_w', 'gdn_a_log', 'gdn_dt_bias', 'gdn_norm', 'hgrn_norm',
              'ssd_conv_w', 'ssd_conv_b', 'ssd_a_log', 'ssd_dt_bias', 'ssd_d', 'ssd_norm',
              'ml_ig_b', 'ml_fg_b', 'ml_norm')


MIXER_SPECS = (gdn_spec, hgrn_spec, ssd_spec, mlstm_spec)
_gdn_kernel.n_stages = lambda t_len: 11 + 2 * (t_len.bit_length() - 2)
_hgrn_kernel.n_stages = lambda t_len: 1 + HG_H * (len(_hgrn_levels(t_len)) // 2 + 1)
_ssd_kernel.n_stages = lambda t_len: 8
_mlstm_kernel.n_stages = lambda t_len: 11


def _mixers(proj, row_blk0, st, w, lb, **grid):
    built = [f(st, w, lb, grid['n_seg']) for f in MIXER_SPECS]
    o, *new_states = _mixer_call([b[0] for b in built], proj, row_blk0, **grid)
    new, at = {}, 0
    for spec, finish in built:
        n = len(spec[2])
        new.update(finish(*new_states[at:at + n]))
        at += n
    return o, new


def kernel(x_prompt, x_sample, state_gdn_conv, state_gdn, state_hgrn, state_ssd_conv, state_ssd, state_mlstm_c, state_mlstm_n, state_mlstm_m, p_prompt, p_sample, g_mix, w_in, gdn_conv_w, gdn_a_log, gdn_dt_bias, gdn_norm, hgrn_lb, hgrn_norm, ssd_conv_w, ssd_conv_b, ssd_a_log, ssd_dt_bias, ssd_d, ssd_norm, ml_ig_b, ml_fg_b, ml_norm, w_branch, w_out, g_ffn, w_ff_gate, w_ff_up, w_ff_down, w_router, w_ex_gate, w_ex_up, w_ex_down, w_ple, w_ple_gate, g_ple, g_final):
    prm = {'gdn_conv_w': gdn_conv_w, 'gdn_a_log': gdn_a_log, 'gdn_dt_bias': gdn_dt_bias,
           'gdn_norm': gdn_norm, 'hgrn_norm': hgrn_norm, 'ssd_conv_w': ssd_conv_w,
           'ssd_conv_b': ssd_conv_b, 'ssd_a_log': ssd_a_log, 'ssd_dt_bias': ssd_dt_bias,
           'ssd_d': ssd_d, 'ssd_norm': ssd_norm, 'ml_ig_b': ml_ig_b, 'ml_fg_b': ml_fg_b,
           'ml_norm': ml_norm}
    bp, lp, d = x_prompt.shape
    bs, ls, _ = x_sample.shape
    mp, ms = bp * lp, bs * ls
    assert lp % CH == 0 and CH % ls == 0 and ms % CH == 0 and mp % CH == 0
    st_s = {'gdn_conv': state_gdn_conv, 'gdn': state_gdn, 'hgrn': state_hgrn, 'ssd_conv': state_ssd_conv,
            'ssd': state_ssd, 'ml_c': state_mlstm_c, 'ml_n': state_mlstm_n, 'ml_m': state_mlstm_m}
    st_p = {k: jnp.zeros((bp,) + v.shape[2:], F32) for k, v in st_s.items()}
    grid_p = dict(n_outer=bp, n_chunks=lp // CH, n_seg=1)
    grid_s = dict(n_outer=ms // CH, n_chunks=1, n_seg=CH // ls)

    sm = jax.nn.softmax(hgrn_lb, axis=0)
    lb_all = jnp.cumsum(sm, axis=0) - sm[0]

    h = jnp.concatenate([x_prompt.reshape(mp, d), x_sample.reshape(ms, d)], axis=0)
    new_p = {k: [] for k in STATE_KEYS}
    new_s = {k: [] for k in STATE_KEYS}
    for l in range(DEPTH):
        wl = {k: prm[k][l] for k in MIXER_KEYS}
        gates, mix = norm_matmul(h, g_mix[l], _permute_w_in(w_in[l]), tm=512, tn=1024)
        br_p, np_ = _mixers(mix, 0, st_p, wl, lb_all[l], **grid_p)
        br_s, ns_ = _mixers(mix, mp // CH, {k: v[l] for k, v in st_s.items()}, wl, lb_all[l], **grid_s)
        h = merge(h, gates, (br_p, br_s), w_branch[l].astype(BF16), w_out[l].astype(BF16), tm=512)
        j = l // 2
        expert_out = None
        if l % 2 == 0:
            h = ffn(h, g_ffn[l], w_ff_gate[j].astype(BF16), w_ff_up[j].astype(BF16),
                    w_ff_down[j].astype(BF16), tm=512, tf=D_FF)
        else:
            expert_out = moe(h, g_ffn[l], w_router[j], w_ex_gate[j].astype(BF16), w_ex_up[j].astype(BF16),
                             w_ex_down[j].astype(BF16), tm_route=512, tm_e=512, tf=D_FF // 2)
        p_groups = (p_prompt[l].reshape(mp, D_PLE), p_sample[l].reshape(ms, D_PLE))
        h = ple(h, p_groups, g_ple[l], w_ple_gate[l].astype(BF16), w_ple[l].astype(BF16), g_final,
                tm=512, final=(l == DEPTH - 1), expert_out=expert_out)
        for k in STATE_KEYS:
            new_p[k].append(np_[k])
            new_s[k].append(ns_[k])
    y_prompt = h[0].reshape(bp, lp, d)
    y_sample = h[1].reshape(bs, ls, d)
    sp = {k: jnp.stack(v) for k, v in new_p.items()}
    ss = {k: jnp.stack(v) for k, v in new_s.items()}
    return (y_prompt, y_sample,
            sp['gdn_conv'], sp['gdn'], sp['hgrn'], sp['ssd_conv'], sp['ssd'], sp['ml_c'], sp['ml_n'], sp['ml_m'],
            ss['gdn_conv'], ss['gdn'], ss['hgrn'], ss['ssd_conv'], ss['ssd'], ss['ml_c'], ss['ml_n'], ss['ml_m'])
```

```python
import functools

import numpy as np
import jax
import jax.numpy as jnp
from jax import lax
from jax.experimental import pallas as pl
from jax.experimental.pallas import tpu as pltpu

F32 = jnp.float32
BF16 = jnp.bfloat16
HI = lax.Precision.HIGHEST

D_MODEL = 1024
DEPTH = 2
D_PLE = 256
N_BRANCH = 4
BRANCH_W = D_MODEL // N_BRANCH
CONV_W = 4
EPS = 1e-6

GDN_H = 4
GDN_DK = 64
GDN_QK = GDN_H * GDN_DK
GDN_CONV_CH = 2 * GDN_QK + BRANCH_W
HG_H = 4
HG_QK = 256
SSD_H = 4
SSD_G = 2
SSD_N = 128
SSD_CONV_CH = BRANCH_W + 2 * SSD_G * SSD_N
ML_H = 4
ML_DK = 32
ML_QK = ML_H * ML_DK
D_FF = ((8 * D_MODEL // 3 + 255) // 256) * 256
N_EXPERTS = 8
TOP_K = 2

_REF_SPLITS = (('gdn_in', GDN_CONV_CH), ('gdn_b', GDN_H), ('gdn_a', GDN_H), ('gdn_z', BRANCH_W),
               ('hg_q', HG_QK), ('hg_f', HG_QK), ('hg_v', BRANCH_W), ('hg_g', BRANCH_W),
               ('ssd_z', BRANCH_W), ('ssd_in', SSD_CONV_CH), ('ssd_dt', SSD_H),
               ('ml_q', ML_QK), ('ml_k', ML_QK), ('ml_v', BRANCH_W), ('ml_i', ML_H), ('ml_f', ML_H),
               ('ml_o', BRANCH_W), ('gates', N_BRANCH * D_MODEL))
_MY_ORDER = ('gdn_in', 'ssd_in', 'gdn_z', 'ssd_z', 'hg_q', 'hg_f', 'hg_v', 'hg_g',
             'ml_v', 'ml_o', 'ml_q', 'ml_k', 'gdn_b', 'gdn_a', 'ssd_dt', 'ml_i', 'ml_f')
LANES = 128
MXU_TILE = 256
CH = 64
N_GATES = N_BRANCH * D_MODEL


def _layout():
    widths = dict(_REF_SPLITS)
    off, out = 0, {}
    for name in _MY_ORDER:
        out[name] = (off, widths[name])
        off += widths[name]
    return out, -(-off // MXU_TILE) * MXU_TILE


COLS, N_MIX = _layout()
GATE_COL0 = COLS['gdn_b'][0]
L_GDN_B, L_GDN_A, L_SSD_DT, L_ML_I, L_ML_F = (COLS[n][0] - GATE_COL0 for n in ('gdn_b', 'gdn_a', 'ssd_dt', 'ml_i', 'ml_f'))
VMEM_LIMIT = 56 * 1024 * 1024


def _ref_offsets():
    off, acc = {}, 0
    for name, wd in _REF_SPLITS:
        off[name] = acc
        acc += wd
    return off, acc


def _permute_kernel(w_ref, o_ref):
    ref_off, _ = _ref_offsets()
    tk = o_ref.shape[0]
    o_ref[:, :N_GATES] = w_ref[:, ref_off['gates']:ref_off['gates'] + N_GATES].astype(BF16)
    lane = lax.broadcasted_iota(jnp.int32, (tk, LANES), 1)
    gate_group = jnp.zeros((tk, LANES), F32)
    for name in _MY_ORDER:
        dst, wd = COLS[name]
        src = ref_off[name]
        if wd >= LANES:
            o_ref[:, N_GATES + dst:N_GATES + dst + wd] = w_ref[:, src:src + wd].astype(BF16)
        else:
            win0, at = src // LANES * LANES, src % LANES
            to = dst - GATE_COL0
            assert at + wd <= LANES and to + wd <= LANES
            window = pltpu.roll(w_ref[:, win0:win0 + LANES], (to - at) % LANES, axis=1)
            gate_group = jnp.where(jnp.logical_and(lane >= to, lane < to + wd), window, gate_group)
    o_ref[:, N_GATES + GATE_COL0:N_GATES + GATE_COL0 + LANES] = gate_group.astype(BF16)
    pad0 = N_GATES + GATE_COL0 + LANES
    o_ref[:, pad0:] = jnp.zeros((tk, N_GATES + N_MIX - pad0), BF16)


def _permute_w_in(w_in, layer, tk=256):
    _, d, n_in = w_in.shape
    return pl.pallas_call(
        _permute_kernel, grid=(d // tk,),
        in_specs=[pl.BlockSpec((None, tk, n_in), lambda i: (layer, i, 0))],
        out_specs=pl.BlockSpec((tk, N_GATES + N_MIX), lambda i: (i, 0)),
        out_shape=jax.ShapeDtypeStruct((d, N_GATES + N_MIX), BF16),
        compiler_params=pltpu.CompilerParams(dimension_semantics=("parallel",), vmem_limit_bytes=VMEM_LIMIT),
        name="permute_w_in",
    )(w_in)


def _rms(x, g):
    return x * lax.rsqrt(jnp.mean(x * x, axis=-1, keepdims=True) + EPS) * g


def _dot(a, b, **kw):
    return jnp.dot(a, b, preferred_element_type=F32, **kw)


def _dot_nt(a, b, **kw):
    return lax.dot_general(a, b, (((1,), (1,)), ((), ())), preferred_element_type=F32, **kw)


def _dot_tn(a, b, **kw):
    return lax.dot_general(a, b, (((0,), (0,)), ((), ())), preferred_element_type=F32, **kw)


def _norm_matmul_kernel(x_ref, g_ref, w_ref, gates_ref, mix_ref, *, tn):
    xn = _rms(x_ref[...], g_ref[...]).astype(BF16)
    for j in range(N_GATES // tn):
        gates_ref[:, j * tn:(j + 1) * tn] = _dot(xn, w_ref[:, j * tn:(j + 1) * tn]).astype(BF16)
    for j in range(N_MIX // tn):
        mix_ref[:, j * tn:(j + 1) * tn] = _dot(xn, w_ref[:, N_GATES + j * tn:N_GATES + (j + 1) * tn])


def norm_matmul(x, g, w, tm, tn):
    m, d = x.shape
    assert w.shape[1] == N_GATES + N_MIX and N_GATES % tn == 0 and N_MIX % tn == 0
    return pl.pallas_call(
        functools.partial(_norm_matmul_kernel, tn=tn),
        grid=(m // tm,),
        in_specs=[pl.BlockSpec((tm, d), lambda i: (i, 0)),
                  pl.BlockSpec((1, d), lambda i: (0, 0)),
                  pl.BlockSpec((d, N_GATES + N_MIX), lambda i: (0, 0), pipeline_mode=pl.Buffered(1))],
        out_specs=[pl.BlockSpec((tm, N_GATES), lambda i: (i, 0)),
                   pl.BlockSpec((tm, N_MIX), lambda i: (i, 0))],
        out_shape=[jax.ShapeDtypeStruct((m, N_GATES), BF16), jax.ShapeDtypeStruct((m, N_MIX), F32)],
        compiler_params=pltpu.CompilerParams(
            dimension_semantics=("parallel",), vmem_limit_bytes=VMEM_LIMIT),
        name="norm_matmul",
    )(x, g.reshape(1, d), w)


def _seg_masks(t_len):
    row = lax.broadcasted_iota(jnp.int32, (CH, CH), 0)
    col = lax.broadcasted_iota(jnp.int32, (CH, CH), 1)
    same = (row // t_len) == (col // t_len)
    tri = jnp.logical_and(same, col <= row)
    strict = jnp.logical_and(same, col < row)
    return same, tri, strict


def _row_forms(x, n_rows=24):
    r = lax.broadcasted_iota(jnp.int32, (n_rows, LANES), 0)
    l = lax.broadcasted_iota(jnp.int32, (n_rows, LANES), 1)
    return _dot_nt((r == l).astype(F32), x, precision=HI)


def _softplus(x):
    return jnp.maximum(x, 0.0) + jnp.log1p(jnp.exp(-jnp.abs(x)))


def _split2(x):
    hi = x.astype(BF16)
    return hi, (x - hi.astype(F32)).astype(BF16)


def _dot3(a, b):
    return _dot(a[0], b[0]) + (_dot(a[0], b[1]) + _dot(a[1], b[0]))


def _masked_exp(d, mask):
    return jnp.where(mask, jnp.exp(jnp.where(mask, d, 0.0)), 0.0)


def _conv_silu(x, ext_scr, cw_ref, bias, t_len, n_seg):
    w = x.shape[-1]
    ext_scr[:, 8:8 + t_len, :] = x.reshape(n_seg, t_len, w)
    y = cw_ref[3:4, :] * x
    for j in range(1, CONV_W):
        y = y + cw_ref[3 - j:4 - j, :] * ext_scr[:, 8 - j:8 - j + t_len, :].reshape(CH, w)
    if bias is not None:
        y = y + bias
    tail = ext_scr[:, 5 + t_len:8 + t_len, :]
    ext_scr[:, 5:8, :] = tail
    return jax.nn.silu(y), tail


def _halves(xp, lo):
    s_lo = jnp.sum(jnp.where(lo, xp, 0.0), axis=-1, keepdims=True)
    s_hi = jnp.sum(jnp.where(lo, 0.0, xp), axis=-1, keepdims=True)
    return jnp.where(lo, s_lo, s_hi)


def _head_rmsnorm(xp, lo):
    return xp * lax.rsqrt(_halves(xp * xp, lo) * (1.0 / 64) + EPS)


def _head_l2norm(xp, lo):
    return xp * lax.rsqrt(_halves(xp * xp, lo) + EPS)


def _seg_sum(parts, rowi, t_len):
    if len(parts) == 1:
        return parts[0]
    acc = jnp.where(rowi // t_len == 0, parts[0], 0.0)
    for s in range(1, len(parts)):
        acc = acc + jnp.where(rowi // t_len == s, parts[s], 0.0)
    return acc


def _seg_rows(x, rowi, t_len, s, n_seg):
    return x if n_seg == 1 else jnp.where(rowi // t_len == s, x, 0.0)


def _quarter_sel(idx, width, vals):
    out = vals[3]
    for h in (2, 1, 0):
        out = jnp.where(idx < (h + 1) * width, vals[h], out)
    return out


def _gate_rows(pairs):
    t = jnp.zeros((8, LANES), F32)
    for r, (off, v) in enumerate(pairs):
        t = t.at[r, off:off + v.shape[0]].set(v.astype(F32))
    return t


def _block_diag(blocks):
    n = len(blocks)
    z = jnp.zeros_like(blocks[0])
    return jnp.concatenate(
        [jnp.concatenate([blocks[i] if i == j else z for j in range(n)], axis=1) for i in range(n)], axis=0)


def _transpose64(x):
    r = lax.broadcasted_iota(jnp.int32, x.shape, 0)
    c = lax.broadcasted_iota(jnp.int32, x.shape, 1)
    eye = (r == c).astype(BF16)
    hi, mid, lo = _split3(x)
    return (_dot_nt(eye, hi) + _dot_nt(eye, mid)) + _dot_nt(eye, lo)


def _load_pair_states(s0_ref, s_scr, n_seg, transpose):
    prep = _transpose64 if transpose else (lambda t: t)
    for s in range(n_seg):
        for p in range(2):
            s_scr[s, p] = _block_diag([prep(s0_ref[s, 2 * p]), prep(s0_ref[s, 2 * p + 1])])


def _store_pair_states(s_scr, sn_ref, n_seg, transpose):
    prep = _transpose64 if transpose else (lambda t: t)
    for s in range(n_seg):
        for p in range(2):
            sn_ref[s, 2 * p] = prep(s_scr[s, p, 0:64, 0:64])
            sn_ref[s, 2 * p + 1] = prep(s_scr[s, p, 64:128, 64:128])


class LayerOf:
    def __init__(self, stack, layer):
        self.stack, self.layer = stack, layer
        self.shape, self.ndim = stack.shape[1:], stack.ndim - 1


def _operand(a):
    return a.stack if isinstance(a, LayerOf) else a


def _row_block_spec(a, block_rows, index_of_step):
    rest = (0,) * (a.ndim - 1)
    if isinstance(a, LayerOf):
        return pl.BlockSpec((None, block_rows) + a.shape[1:], lambda *ids: (a.layer, index_of_step(*ids)) + rest)
    return pl.BlockSpec((block_rows,) + a.shape[1:], lambda *ids: (index_of_step(*ids),) + rest)


def _interleave(gens, n_stages):
    results = [None] * len(gens)
    pos = [0] * len(gens)
    live = set(range(len(gens)))
    while live:
        k = min(live, key=lambda i: ((pos[i] + 1) / n_stages[i], i))
        try:
            next(gens[k])
            pos[k] += 1
        except StopIteration as stop:
            results[k] = stop.value
            live.remove(k)
    return results


def _fused_mixer_kernel(*refs, n_seg, parts):
    tot = [sum(p[j] for p in parts) for j in range(1, 5)]
    ins, rest = refs[:tot[0]], refs[tot[0]:]
    sts, rest = rest[:tot[1]], rest[tot[1]:]
    prs, rest = rest[:tot[2]], rest[tot[2]:]
    o_ref, rest = rest[0], rest[1:]
    outs, scr = rest[:tot[1]], rest[tot[1]:]
    at = [0, 0, 0, 0]
    calls = []
    for k, (body, n_in, n_st, n_pr, n_scr) in enumerate(parts):
        take = lambda seq, j, n: seq[at[j]:at[j] + n]
        calls.append(functools.partial(
            body, *take(ins, 0, n_in), *take(sts, 1, n_st), *take(prs, 2, n_pr),
            o_ref.at[:, k * BRANCH_W:(k + 1) * BRANCH_W],
            *take(outs, 1, n_st), *take(scr, 3, n_scr), n_seg=n_seg))
        for j, n in enumerate((n_in, n_st, n_pr, n_scr)):
            at[j] += n

    n_stages = [p[0].n_stages(CH // n_seg) for p in parts]

    @pl.when(pl.program_id(1) == 0)
    def _():
        _interleave([call(init=True) for call in calls], n_stages)

    finishers = _interleave([call(init=False) for call in calls], n_stages)

    @pl.when(pl.program_id(1) == pl.num_programs(1) - 1)
    def _():
        for fin in finishers:
            fin()


def _mixer_call(specs, proj, row_blk0, *, n_outer, n_chunks, n_seg):
    rows = n_outer * n_chunks * CH
    rmap = lambda blk: (lambda i, c: (row_blk0 + i * n_chunks + c, blk))
    full = lambda a: pl.BlockSpec(a.shape, lambda i, c: (0,) * a.ndim)
    sblk = lambda a: _row_block_spec(a, a.shape[0] // n_outer, lambda i, c: i)
    oblk = lambda a: pl.BlockSpec((a.shape[0] // n_outer,) + a.shape[1:], lambda i, c: (i,) + (0,) * (a.ndim - 1))
    in_blocks = [b for s in specs for b in s[1]]
    state_ins = [a for s in specs for a in s[2]]
    params = [a for s in specs for a in s[3]]
    scratch = [a for s in specs for a in s[4]]
    for off, wd in in_blocks:
        assert off % wd == 0
    parts = tuple((s[0], len(s[1]), len(s[2]), len(s[3]), len(s[4])) for s in specs)
    width = len(specs) * BRANCH_W
    return pl.pallas_call(
        functools.partial(_fused_mixer_kernel, n_seg=n_seg, parts=parts),
        grid=(n_outer, n_chunks),
        in_specs=([pl.BlockSpec((CH, wd), rmap(off // wd)) for off, wd in in_blocks]
                  + [sblk(a) for a in state_ins] + [full(a) for a in params]),
        out_specs=[pl.BlockSpec((CH, width), lambda i, c: (i * n_chunks + c, 0))] + [oblk(a) for a in state_ins],
        out_shape=([jax.ShapeDtypeStruct((rows, width), BF16)]
                   + [jax.ShapeDtypeStruct(a.shape, F32) for a in state_ins]),
        scratch_shapes=scratch,
        compiler_params=pltpu.CompilerParams(
            dimension_semantics=("parallel", "arbitrary"), vmem_limit_bytes=VMEM_LIMIT),
        name="token_mixers",
    )(*([proj] * len(in_blocks)), *[_operand(a) for a in state_ins], *params)


def _gdn_kernel(xin_ref, z_ref, sm_ref, conv0_ref, s0_ref, cw_ref, gp_ref, ng_ref,
                o_ref, convn_ref, sn_ref, ext_scr, s_scr, *, n_seg, init):
    t_len = CH // n_seg
    if init:
        _load_pair_states(s0_ref, s_scr, n_seg, transpose=False)
        ext_scr[:, 5:8, :] = conv0_ref[...]
        return None

    xc, tail = _conv_silu(xin_ref[...], ext_scr, cw_ref, None, t_len, n_seg)
    yield
    same, tri, strict = _seg_masks(t_len)
    lane = lax.broadcasted_iota(jnp.int32, (CH, LANES), 1)
    rowi = lax.broadcasted_iota(jnp.int32, (CH, 1), 0)
    lo = lane < 64
    sm = sm_ref[...]
    beta = jax.nn.sigmoid(sm)
    gl = jnp.logical_and(lane >= L_GDN_A, lane < L_GDN_A + GDN_H)
    g = jnp.where(gl, -jnp.exp(gp_ref[0:1, :]) * _softplus(sm + gp_ref[1:2, :]), 0.0)
    gam = _dot(tri.astype(F32), g, precision=HI)
    gam_end = _dot(same.astype(F32), g, precision=HI)
    gam_r = _row_forms(gam, 8)
    yield
    r128 = lax.broadcasted_iota(jnp.int32, (LANES, LANES), 0)
    l128 = lax.broadcasted_iota(jnp.int32, (LANES, LANES), 1)
    bd = (r128 < 64) == (l128 < 64)
    rsel = lax.broadcasted_iota(jnp.int32, (LANES, 1), 0) < 64
    qs_, ks_, atts_, gcs_, a_, x_ = [], [], [], [], [], []
    for p in range(2):
        q_p = _head_l2norm(xc[:, 128 * p:128 * (p + 1)], lo) * (GDN_DK ** -0.5)
        k_p = _head_l2norm(xc[:, GDN_QK + 128 * p:GDN_QK + 128 * (p + 1)], lo)
        v_p = xc[:, 2 * GDN_QK + 128 * p:2 * GDN_QK + 128 * (p + 1)]
        kb = k_p.astype(BF16)
        qs_.append(q_p)
        ks_.append(k_p)
        for j in range(2):
            h = 2 * p + j
            mj = lo if j == 0 else jnp.logical_not(lo)
            kk = _dot_nt(jnp.where(mj, k_p, 0.0).astype(BF16), kb)
            qk = _dot_nt(jnp.where(mj, q_p, 0.0).astype(BF16), kb)
            gc = gam[:, L_GDN_A + h:L_GDN_A + h + 1]
            bc = beta[:, L_GDN_B + h:L_GDN_B + h + 1]
            dec = _masked_exp(gc - gam_r[L_GDN_A + h:L_GDN_A + h + 1, :], tri)
            a_.append(jnp.where(strict, bc * kk * dec, 0.0))
            x_.append(jnp.concatenate([jnp.where(mj, bc * v_p, 0.0),
                                       jnp.where(mj, (bc * jnp.exp(gc)) * k_p, 0.0)], axis=-1))
            atts_.append(qk * dec)
            gcs_.append(gc)
            yield
    eye = jnp.logical_and(tri, jnp.logical_not(strict)).astype(F32)
    sa = [_split2(a) for a in a_]
    t_ = [eye - a for a in a_]
    n = 2
    while n < t_len:
        sa = [_split2(_dot3(s, s)) for s in sa]
        yield
        t_ = [t + _dot3(_split2(t), s) for t, s in zip(t_, sa)]
        yield
        n *= 2
    x_ = [_dot3(_split2(t), _split2(x)) for t, x in zip(t_, x_)]
    yield
    outs = []
    for p in range(2):
        q_p, k_p = qs_[p], ks_[p]
        solv = x_[2 * p][:, :LANES] + x_[2 * p + 1][:, :LANES]
        solk = x_[2 * p][:, LANES:] + x_[2 * p + 1][:, LANES:]
        atts, gcs = atts_[2 * p:2 * p + 2], gcs_[2 * p:2 * p + 2]
        solk_b = solk.astype(BF16)
        qb = q_p.astype(BF16)
        u = solv - _seg_sum([_dot(solk_b, s_scr[s, p].astype(BF16)) for s in range(n_seg)], rowi, t_len)
        qs = _seg_sum([_dot(qb, s_scr[s, p].astype(BF16)) for s in range(n_seg)], rowi, t_len)
        o = jnp.where(lo, jnp.exp(gcs[0]), jnp.exp(gcs[1])) * qs
        for j in range(2):
            mj = lo if j == 0 else jnp.logical_not(lo)
            o = o + _dot(atts[j].astype(BF16), jnp.where(mj, u, 0.0).astype(BF16))
        yield
        ge0 = gam_end[:, L_GDN_A + 2 * p:L_GDN_A + 2 * p + 1]
        ge1 = gam_end[:, L_GDN_A + 2 * p + 1:L_GDN_A + 2 * p + 2]
        kw = k_p * jnp.where(lo, jnp.exp(ge0 - gcs[0]), jnp.exp(ge1 - gcs[1]))
        ub = u.astype(BF16)
        for s in range(n_seg):
            r0 = s * t_len
            dec_s = jnp.where(rsel, jnp.exp(ge0[r0:r0 + 1, :]), jnp.exp(ge1[r0:r0 + 1, :]))
            upd = _dot_tn(_seg_rows(kw, rowi, t_len, s, n_seg).astype(BF16), ub)
            s_scr[s, p] = dec_s * s_scr[s, p] + jnp.where(bd, upd, 0.0)
        outs.append(_head_rmsnorm(o, lo))
        yield
    o_all = jnp.concatenate(outs, axis=-1) * ng_ref[...] * jax.nn.silu(z_ref[...])
    o_ref[...] = o_all.astype(BF16)

    def finish():
        convn_ref[...] = tail
        _store_pair_states(s_scr, sn_ref, n_seg, transpose=False)
    return finish


def gdn_spec(st, w, lb, n_seg):
    t_len = CH // n_seg
    gp = _gate_rows([(L_GDN_A, w['gdn_a_log']), (L_GDN_A, w['gdn_dt_bias'])])
    ng = jnp.tile(w['gdn_norm'], GDN_H).reshape(1, BRANCH_W)
    spec = (_gdn_kernel, [COLS['gdn_in'], COLS['gdn_z'], (GATE_COL0, LANES)],
            [st['gdn_conv'], st['gdn']], [w['gdn_conv_w'], gp, ng],
            [pltpu.VMEM((n_seg, 8 + t_len, GDN_CONV_CH), F32), pltpu.VMEM((n_seg, 2, 128, 128), F32)])
    return spec, lambda convn, sn: {'gdn_conv': convn, 'gdn': sn}


def _hgrn_levels(t_len):
    lv, n = [], t_len
    while n >= 2:
        lv.append(n)
        n //= 2
    return lv


def _hgrn_cmat(t_len):
    t = np.arange(CH)[:, None]
    j = np.arange(CH)[None, :]
    same = (t // t_len) == (j // t_len)
    mats = [same & (j <= t), same]
    for n in _hgrn_levels(t_len):
        mid = (t // n) * n + n // 2
        second = t % n >= n // 2
        mats.append((second & (j >= mid) & (j <= t)) | (~second & (j > t) & (j <= mid - 1)))
    return jnp.asarray(np.concatenate(mats, axis=0).astype(np.float32), dtype=BF16)


def _split3(x):
    hi = x.astype(BF16)
    r = x - hi.astype(F32)
    mid = r.astype(BF16)
    return hi, mid, (r - mid.astype(F32)).astype(BF16)


def _hgrn_kernel(x_ref, s0_ref, cm_ref, lb_ref, ng_ref, o_ref, sn_ref, s_scr, *, n_seg, init):
    t_len = CH // n_seg
    levels = _hgrn_levels(t_len)
    if init:
        _load_pair_states(s0_ref, s_scr, n_seg, transpose=True)
        return None

    lane = lax.broadcasted_iota(jnp.int32, (CH, LANES), 1)
    rowi = lax.broadcasted_iota(jnp.int32, (CH, 1), 0)
    row = lax.broadcasted_iota(jnp.int32, (CH, CH), 0)
    col = lax.broadcasted_iota(jnp.int32, (CH, CH), 1)
    lo = lane < 64
    r128 = lax.broadcasted_iota(jnp.int32, (LANES, LANES), 0)
    l128 = lax.broadcasted_iota(jnp.int32, (LANES, LANES), 1)
    bd = (r128 < 64) == (l128 < 64)

    lb = lb_ref[...]
    f_pre = x_ref[:, HG_QK:2 * HG_QK]
    log_f = jnp.log(lb + (1.0 - lb) * jax.nn.sigmoid(f_pre))
    k_in = (1.0 - lb) * jax.nn.sigmoid(-f_pre)
    cm = cm_ref[...]
    ex = None
    for part in _split3(log_f):
        t = _dot(cm, part)
        ex = t if ex is None else ex + t
    b = ex[0:CH]
    b_end = ex[CH:2 * CH]
    yield
    outs = []
    for p in range(2):
        ls = slice(128 * p, 128 * (p + 1))
        q_p = x_ref[:, ls]
        k_p = k_in[:, ls]
        v_p = x_ref[:, 2 * HG_QK + 128 * p:2 * HG_QK + 128 * (p + 1)]
        qk = q_p * k_p
        qe = (q_p * jnp.exp(b[:, ls])).astype(BF16)
        o = _seg_sum([_dot_nt(qe, s_scr[s, p].astype(BF16)) for s in range(n_seg)], rowi, t_len)
        scales = [jnp.exp(ex[(2 + li) * CH:(3 + li) * CH, ls]) for li in range(len(levels))]
        for j in range(2):
            mj = lo if j == 0 else jnp.logical_not(lo)
            diag = jnp.sum(jnp.where(mj, qk, 0.0), axis=-1, keepdims=True)
            att = jnp.where(row == col, diag, 0.0)
            for li, n in enumerate(levels):
                tq = (rowi % n) >= (n // 2)
                qt = jnp.where(jnp.logical_and(mj, tq), q_p * scales[li], 0.0)
                kt = jnp.where(tq, 0.0, k_p * scales[li])
                att = att + jnp.where((row // n) == (col // n), _dot_nt(qt.astype(BF16), kt.astype(BF16)), 0.0)
                if li % 2 == 1:
                    yield
            o = o + _dot(att.astype(BF16), jnp.where(mj, v_p, 0.0).astype(BF16))
            yield
        kw = (k_p * jnp.exp(b_end[:, ls] - b[:, ls])).astype(BF16)
        for s in range(n_seg):
            r0 = s * t_len
            upd = _dot_tn(_seg_rows(v_p, rowi, t_len, s, n_seg).astype(BF16), kw)
            s_scr[s, p] = jnp.exp(b_end[r0:r0 + 1, ls]) * s_scr[s, p] + jnp.where(bd, upd, 0.0)
        outs.append(_head_rmsnorm(o, lo))
    o_all = jnp.concatenate(outs, axis=-1) * ng_ref[...] * jax.nn.silu(x_ref[:, 3 * HG_QK:4 * HG_QK])
    o_ref[...] = o_all.astype(BF16)

    def finish():
        _store_pair_states(s_scr, sn_ref, n_seg, transpose=True)
    return finish


def hgrn_spec(st, w, lb, n_seg):
    t_len = CH // n_seg
    ng = jnp.tile(w['hgrn_norm'], HG_H).reshape(1, BRANCH_W)
    assert COLS['hg_f'][0] == COLS['hg_q'][0] + HG_QK and COLS['hg_g'][0] == COLS['hg_q'][0] + 3 * HG_QK
    spec = (_hgrn_kernel, [(COLS['hg_q'][0], 4 * HG_QK)], [st['hgrn']],
            [_hgrn_cmat(t_len), lb.reshape(1, HG_QK), ng], [pltpu.VMEM((n_seg, 2, 128, 128), F32)])
    return spec, lambda sn: {'hgrn': sn}


def _ssd_kernel(xin_ref, z_ref, sm_ref, conv0_ref, h0_ref, cw_ref, cb_ref, gp_ref, dvec_ref, ng_ref,
                o_ref, convn_ref, hn_ref, ext_scr, h_scr, *, n_seg, init):
    t_len = CH // n_seg
    if init:
        h_scr[...] = h0_ref[...].reshape(n_seg, 2, 128, SSD_N)
        ext_scr[:, 5:8, :] = conv0_ref[...]
        return None

    xc, tail = _conv_silu(xin_ref[...], ext_scr, cw_ref, cb_ref[...], t_len, n_seg)
    yield
    sx, bm, cm = xc[:, :256], xc[:, 256:512], xc[:, 512:768]
    same, tri, _ = _seg_masks(t_len)
    lane = lax.broadcasted_iota(jnp.int32, (CH, LANES), 1)
    rowi = lax.broadcasted_iota(jnp.int32, (CH, 1), 0)
    gl = jnp.logical_and(lane >= L_SSD_DT, lane < L_SSD_DT + SSD_H)
    dt = jnp.where(gl, _softplus(sm_ref[...] + gp_ref[1:2, :]), 0.0)
    da = -jnp.exp(gp_ref[0:1, :]) * dt
    cum = _dot(tri.astype(F32), da, precision=HI)
    cum_end = _dot(same.astype(F32), da, precision=HI)
    cum_r = _row_forms(cum, 16)
    dt_r = _row_forms(dt, 16)
    yield
    lo = lane < 64
    rsel = lax.broadcasted_iota(jnp.int32, (LANES, 1), 0) < 64
    ys = []
    for g in range(SSD_G):
        cg = cm[:, 128 * g:128 * (g + 1)].astype(BF16)
        bg = bm[:, 128 * g:128 * (g + 1)].astype(BF16)
        sxp = sx[:, 128 * g:128 * (g + 1)]
        cb = _dot_nt(cg, bg)
        yst = _seg_sum([_dot_nt(cg, h_scr[s, g].astype(BF16)) for s in range(n_seg)], rowi, t_len)
        yatt = jnp.zeros((CH, LANES), F32)
        cols = []
        for j in range(2):
            l = L_SSD_DT + 2 * g + j
            cc = cum[:, l:l + 1]
            dec = _masked_exp(cc - cum_r[l:l + 1, :], tri)
            att = cb * dec * dt_r[l:l + 1, :]
            xm = jnp.where(lo if j == 0 else jnp.logical_not(lo), sxp, 0.0)
            yatt = yatt + _dot(att.astype(BF16), xm.astype(BF16))
            cols.append((cc, dt[:, l:l + 1] * jnp.exp(cum_end[:, l:l + 1] - cc)))
            yield
        ys.append(jnp.where(lo, jnp.exp(cols[0][0]), jnp.exp(cols[1][0])) * yst + yatt)
        xw = sxp * jnp.where(lo, cols[0][1], cols[1][1])
        for s in range(n_seg):
            r0 = s * t_len
            l = L_SSD_DT + 2 * g
            e0 = jnp.exp(cum_end[r0:r0 + 1, l:l + 1])
            e1 = jnp.exp(cum_end[r0:r0 + 1, l + 1:l + 2])
            upd = _dot_tn(_seg_rows(xw, rowi, t_len, s, n_seg).astype(BF16), bg)
            h_scr[s, g] = jnp.where(rsel, e0, e1) * h_scr[s, g] + upd
        yield
    y_all = jnp.concatenate(ys, axis=-1) + dvec_ref[...] * sx
    o_ref[...] = _rms(y_all * jax.nn.silu(z_ref[...]), ng_ref[...]).astype(BF16)

    def finish():
        convn_ref[...] = tail
        hn_ref[...] = h_scr[...].reshape(n_seg, SSD_H, 64, SSD_N)
    return finish


def ssd_spec(st, w, lb, n_seg):
    t_len = CH // n_seg
    gp = _gate_rows([(L_SSD_DT, w['ssd_a_log']), (L_SSD_DT, w['ssd_dt_bias'])])
    dvec = jnp.repeat(w['ssd_d'], BRANCH_W // SSD_H).reshape(1, BRANCH_W)
    spec = (_ssd_kernel, [COLS['ssd_in'], COLS['ssd_z'], (GATE_COL0, LANES)],
            [st['ssd_conv'], st['ssd']],
            [w['ssd_conv_w'], w['ssd_conv_b'].reshape(1, SSD_CONV_CH), gp, dvec, w['ssd_norm'].reshape(1, BRANCH_W)],
            [pltpu.VMEM((n_seg, 8 + t_len, SSD_CONV_CH), F32), pltpu.VMEM((n_seg, 2, 128, SSD_N), F32)])
    return spec, lambda convn, hn: {'ssd_conv': convn, 'ssd': hn}


def _mlstm_kernel(vo_ref, qk_ref, sm_ref, c0_ref, n0_ref, m0_ref, gp_ref, ng_ref, o_ref, cn_ref, nn_ref, mn_ref,
                  c_scr, n_scr, m_scr, *, n_seg, init):
    t_len = CH // n_seg
    if init:
        for s in range(n_seg):
            c_scr[s] = jnp.concatenate(
                [jnp.concatenate([c0_ref[s, h] if g == h else jnp.zeros((ML_DK, 64), F32) for g in range(ML_H)], axis=1)
                 for h in range(ML_H)], axis=0)
        n_scr[...] = n0_ref[...]
        m_scr[...] = m0_ref[...]
        return None

    same, tri, _ = _seg_masks(t_len)
    lane = lax.broadcasted_iota(jnp.int32, (CH, LANES), 1)
    lane256 = lax.broadcasted_iota(jnp.int32, (CH, BRANCH_W), 1)
    rowi = lax.broadcasted_iota(jnp.int32, (CH, 1), 0)
    r128 = lax.broadcasted_iota(jnp.int32, (LANES, 1), 0)
    neg = jnp.float32(-jnp.inf)

    v_all = vo_ref[:, 0:BRANCH_W]
    q_all = qk_ref[:, 0:ML_QK]
    k_all = qk_ref[:, ML_QK:2 * ML_QK] * (ML_DK ** -0.5)
    sm = sm_ref[...]
    ig = sm + gp_ref[0:1, :]
    fl = jnp.logical_and(lane >= L_ML_F, lane < L_ML_F + ML_H)
    lf = jnp.where(fl, -_softplus(-(sm + gp_ref[1:2, :])), 0.0)
    b = _dot(tri.astype(F32), lf, precision=HI)
    b_end = _dot(same.astype(F32), lf, precision=HI)
    b_r = _row_forms(b)
    ig_r = _row_forms(ig)
    yield
    mm = m_scr[...]
    qb = q_all.astype(BF16)
    kb = k_all.astype(BF16)
    qn = _seg_sum([_dot(qb, n_scr[s].astype(BF16)) for s in range(n_seg)], rowi, t_len)
    qc = _seg_sum([_dot(qb, c_scr[s].astype(BF16)) for s in range(n_seg)], rowi, t_len)
    yield
    num_att = jnp.zeros((CH, BRANCH_W), F32)
    w_ins, dens, w_ends, a_ends, m_ends = [], [], [], [], []
    for h in range(ML_H):
        li, lf_ = L_ML_I + h, L_ML_F + h
        bc = b[:, lf_:lf_ + 1]
        bec = b_end[:, lf_:lf_ + 1]
        igc = ig[:, li:li + 1]
        mmc = mm[:, li:li + 1]
        br = b_r[lf_:lf_ + 1, :]
        igr = ig_r[li:li + 1, :]
        diff = igr - br
        cmx = jnp.max(jnp.where(tri, diff, neg), axis=-1, keepdims=True)
        smx = jnp.max(jnp.where(same, diff, neg), axis=-1, keepdims=True)
        m_c = bc + jnp.maximum(mmc, cmx)
        m_end = bec + jnp.maximum(mmc, smx)
        w_in = jnp.exp(bc + mmc - m_c)
        logw = bc - br + igr - m_c
        mq = jnp.logical_and(lane >= ML_DK * h, lane < ML_DK * (h + 1))
        qk = _dot_nt(jnp.where(mq, q_all, 0.0).astype(BF16), kb)
        wts = _masked_exp(logw, tri) * qk
        yield
        mv = jnp.logical_and(lane256 >= 64 * h, lane256 < 64 * (h + 1))
        num_att = num_att + _dot(wts.astype(BF16), jnp.where(mv, v_all, 0.0).astype(BF16))
        nq = w_in * qn[:, li:li + 1] + jnp.sum(wts, axis=-1, keepdims=True)
        w_ins.append(w_in)
        dens.append(jnp.maximum(jnp.abs(nq), jnp.exp(-m_c)))
        w_ends.append(jnp.exp(bec - bc + igc - m_end))
        a_ends.append(jnp.exp(bec + mmc - m_end))
        m_ends.append(m_end)
        yield
    num = _quarter_sel(lane256, 64, w_ins) * qc + num_att
    hout = num / _quarter_sel(lane256, 64, dens)
    outs = [_head_rmsnorm(hout[:, 128 * p:128 * (p + 1)], lane < 64) for p in range(2)]
    o_all = jnp.concatenate(outs, axis=-1) * ng_ref[...] * jax.nn.sigmoid(vo_ref[:, BRANCH_W:2 * BRANCH_W])
    o_ref[...] = o_all.astype(BF16)
    yield

    kw = k_all * _quarter_sel(lane, ML_DK, w_ends)
    wend_tile = jnp.zeros((CH, LANES), F32)
    m_tile = jnp.zeros((CH, LANES), F32)
    for h in range(ML_H):
        wend_tile = jnp.where(lane == L_ML_I + h, w_ends[h], wend_tile)
        m_tile = jnp.where(lane == L_ML_I + h, m_ends[h], m_tile)
    m_scr[...] = m_tile
    vb = v_all.astype(BF16)
    wb = wend_tile.astype(BF16)
    rc = lax.broadcasted_iota(jnp.int32, (LANES, BRANCH_W), 0)
    lc = lax.broadcasted_iota(jnp.int32, (LANES, BRANCH_W), 1)
    bd_c = (rc // ML_DK) == (lc // 64)
    rn = lax.broadcasted_iota(jnp.int32, (LANES, LANES), 0)
    ln = lax.broadcasted_iota(jnp.int32, (LANES, LANES), 1)
    bd_n = ln == (rn // ML_DK) + L_ML_I
    for s in range(n_seg):
        r0 = s * t_len
        a_sel = _quarter_sel(r128, ML_DK, [a[r0:r0 + 1, :] for a in a_ends])
        upd_c = _dot_tn(_seg_rows(kw, rowi, t_len, s, n_seg).astype(BF16), vb)
        upd_n = _dot_tn(_seg_rows(k_all, rowi, t_len, s, n_seg).astype(BF16), wb)
        c_scr[s] = a_sel * c_scr[s] + jnp.where(bd_c, upd_c, 0.0)
        n_scr[s] = a_sel * n_scr[s] + jnp.where(bd_n, upd_n, 0.0)

    def finish():
        for s in range(n_seg):
            for h in range(ML_H):
                cn_ref[s, h] = c_scr[s, ML_DK * h:ML_DK * (h + 1), 64 * h:64 * (h + 1)]
        nn_ref[...] = n_scr[...]
        mn_ref[...] = m_tile
    return finish


def mlstm_spec(st, w, lb, n_seg):
    t_len = CH // n_seg
    c0, n0, m0 = st['ml_c'], st['ml_n'], st['ml_m']
    bsz = c0.shape[0]
    gp = _gate_rows([(L_ML_I, w['ml_ig_b']), (L_ML_F, w['ml_fg_b'])])
    ng = jnp.tile(w['ml_norm'], ML_H).reshape(1, BRANCH_W)
    eye = jnp.eye(ML_H, dtype=F32)
    pad = ((0, 0), (L_ML_I, LANES - L_ML_I - ML_H))
    n_bd = jnp.pad(jnp.einsum('bhk,hg->bhkg', n0, eye).reshape(bsz, ML_QK, ML_H), ((0, 0),) + pad)
    m_exp = jnp.pad(jnp.repeat(m0, t_len, axis=0), pad)
    assert COLS['ml_o'][0] == COLS['ml_v'][0] + BRANCH_W and COLS['ml_k'][0] == COLS['ml_q'][0] + ML_QK
    spec = (_mlstm_kernel, [(COLS['ml_v'][0], 2 * BRANCH_W), (COLS['ml_q'][0], 2 * ML_QK), (GATE_COL0, LANES)],
            [c0, n_bd, m_exp], [gp, ng],
            [pltpu.VMEM((n_seg, ML_QK, BRANCH_W), F32), pltpu.VMEM((n_seg, ML_QK, LANES), F32),
             pltpu.VMEM((CH, LANES), F32)])

    def finish(c_new, nn, mn):
        n_new = jnp.einsum('bhkg,hg->bhk', nn[:, :, L_ML_I:L_ML_I + ML_H].reshape(bsz, ML_H, ML_DK, ML_H), eye)
        return {'ml_c': c_new, 'ml_n': n_new, 'ml_m': mn[::t_len, L_ML_I:L_ML_I + ML_H]}
    return spec, finish


def _group_specs(tm, width, n_first):
    return [pl.BlockSpec((tm, width), lambda i: (jnp.minimum(i, n_first - 1), 0)),
            pl.BlockSpec((tm, width), lambda i: (jnp.maximum(i - n_first, 0), 0))]


def _group_in_specs(groups, tm, n_first):
    return [_row_block_spec(groups[0], tm, lambda i: jnp.minimum(i, n_first - 1)),
            _row_block_spec(groups[1], tm, lambda i: jnp.maximum(i - n_first, 0))]


def _merge_kernel(h_ref, gates_ref, brp_ref, brs_ref, wb_ref, wo_ref, o_ref, *, n_first):
    br = jnp.where(pl.program_id(0) < n_first, brp_ref[...], brs_ref[...])
    merged = None
    for n in range(N_BRANCH):
        y = _dot(br[:, n * BRANCH_W:(n + 1) * BRANCH_W], wb_ref[n])
        z = gates_ref[:, n * D_MODEL:(n + 1) * D_MODEL].astype(F32)
        t = (0.5 * jnp.tanh(0.5 * z) + 0.5) * y
        merged = t if merged is None else merged + t
    o_ref[...] = h_ref[...] + _dot(merged.astype(BF16), wo_ref[...])


def merge(h, gates, br_groups, w_branch, w_out, tm):
    m, d = h.shape
    n_first = br_groups[0].shape[0] // tm
    assert br_groups[0].shape[0] % tm == 0 and br_groups[1].shape[0] % tm == 0
    return pl.pallas_call(
        functools.partial(_merge_kernel, n_first=n_first),
        grid=(m // tm,),
        in_specs=[pl.BlockSpec((tm, d), lambda i: (i, 0)),
                  pl.BlockSpec((tm, N_BRANCH * d), lambda i: (i, 0)),
                  *_group_specs(tm, N_BRANCH * BRANCH_W, n_first),
                  pl.BlockSpec((N_BRANCH, BRANCH_W, d), lambda i: (0, 0, 0)),
                  pl.BlockSpec((d, d), lambda i: (0, 0))],
        out_specs=pl.BlockSpec((tm, d), lambda i: (i, 0)),
        out_shape=jax.ShapeDtypeStruct((m, d), F32),
        compiler_params=pltpu.CompilerParams(
            dimension_semantics=("parallel",), vmem_limit_bytes=VMEM_LIMIT),
        name="merge",
    )(h, gates, *br_groups, w_branch, w_out)


def _ffn_kernel(h_ref, g_ref, wg_ref, wu_ref, wd_ref, o_ref, *, tf):
    h = h_ref[...]
    u = _rms(h, g_ref[...]).astype(BF16)
    y = h
    for c in range(wg_ref.shape[1] // tf):
        cols = slice(c * tf, (c + 1) * tf)
        a = jax.nn.silu(_dot(u, wg_ref[:, cols]))
        b = _dot(u, wu_ref[:, cols])
        y = y + _dot((a * b).astype(BF16), wd_ref[cols, :])
    o_ref[...] = y


def ffn(h, g, wg, wu, wd, tm, tf):
    m, d = h.shape
    ff = wg.shape[1]
    assert ff % tf == 0
    resident = lambda a: pl.BlockSpec(a.shape, lambda i: (0, 0), pipeline_mode=pl.Buffered(1))
    return pl.pallas_call(
        functools.partial(_ffn_kernel, tf=tf),
        grid=(m // tm,),
        in_specs=[pl.BlockSpec((tm, d), lambda i: (i, 0)),
                  pl.BlockSpec((1, d), lambda i: (0, 0)),
                  resident(wg), resident(wu), resident(wd)],
        out_specs=pl.BlockSpec((tm, d), lambda i: (i, 0)),
        out_shape=jax.ShapeDtypeStruct((m, d), F32),
        compiler_params=pltpu.CompilerParams(
            dimension_semantics=("parallel",), vmem_limit_bytes=VMEM_LIMIT),
        name="ffn",
    )(h, g.reshape(1, d), wg, wu, wd)


def _router_kernel(h_ref, g_ref, wr_ref, u_ref, w_ref, i_ref):
    u = _rms(h_ref[...], g_ref[...])
    u_ref[...] = u
    logits = _dot(u, wr_ref[...], precision=HI)
    lane = lax.broadcasted_iota(jnp.int32, logits.shape, 1)
    neg = jnp.float32(-jnp.inf)
    logits = jnp.where(lane < N_EXPERTS, logits, neg)
    m1 = jnp.max(logits, axis=-1, keepdims=True)
    i1 = jnp.min(jnp.where(logits == m1, lane, LANES), axis=-1, keepdims=True)
    rest = jnp.where(lane == i1, neg, logits)
    m2 = jnp.max(rest, axis=-1, keepdims=True)
    i2 = jnp.min(jnp.where(rest == m2, lane, LANES), axis=-1, keepdims=True)
    e = jnp.exp(m2 - m1)
    den = 1.0 + e
    w_ref[...] = jnp.where(lane == 0, 1.0 / den, jnp.where(lane == 1, e / den, 0.0))
    i_ref[...] = jnp.where(lane == 0, i1, jnp.where(lane == 1, i2, 0))


def router(h, g, w_router, tm):
    m, d = h.shape
    wr = jnp.pad(w_router, ((0, 0), (0, LANES - N_EXPERTS)))
    return pl.pallas_call(
        _router_kernel,
        grid=(m // tm,),
        in_specs=[pl.BlockSpec((tm, d), lambda i: (i, 0)),
                  pl.BlockSpec((1, d), lambda i: (0, 0)),
                  pl.BlockSpec((d, LANES), lambda i: (0, 0))],
        out_specs=[pl.BlockSpec((tm, d), lambda i: (i, 0)),
                   pl.BlockSpec((tm, LANES), lambda i: (i, 0)),
                   pl.BlockSpec((tm, LANES), lambda i: (i, 0))],
        out_shape=[jax.ShapeDtypeStruct((m, d), F32),
                   jax.ShapeDtypeStruct((m, LANES), F32),
                   jax.ShapeDtypeStruct((m, LANES), jnp.int32)],
        compiler_params=pltpu.CompilerParams(
            dimension_semantics=("parallel",), vmem_limit_bytes=VMEM_LIMIT),
        name="router",
    )(h, g.reshape(1, d), wr)


def _row_copies(src_hbm, dst_hbm, idx_ref, base, buf, sem, n_rows, gather, wait):
    def body(r, carry):
        row = idx_ref[base + r]
        if gather:
            cp = pltpu.make_async_copy(src_hbm.at[pl.ds(row, 1)], buf.at[pl.ds(r, 1)], sem)
        else:
            cp = pltpu.make_async_copy(buf.at[pl.ds(r, 1)], dst_hbm.at[pl.ds(row, 1)], sem)
        if wait:
            cp.wait()
        else:
            cp.start()
        return carry
    lax.fori_loop(0, n_rows, body, 0, unroll=8)


def _expert_kernel(te_ref, nt_ref, src_ref, dst_ref, u_hbm, wg_ref, wu_ref, wd_ref, y_hbm,
                   xbuf, xb_ref, acc_ref, obuf, gsem, ssem, *, tm):
    i = pl.program_id(0)
    f = pl.program_id(1)
    last_f = pl.num_programs(1) - 1
    nt = nt_ref[0]
    gather = functools.partial(_row_copies, u_hbm, None, src_ref, gather=True, n_rows=tm)
    scatter = functools.partial(_row_copies, None, y_hbm, dst_ref, buf=obuf, sem=ssem.at[0], gather=False, n_rows=tm)

    @pl.when(jnp.logical_and(i < nt, f == 0))
    def _():
        slot = i % 2

        @pl.when(i == 0)
        def _():
            gather(base=0, buf=xbuf.at[0], sem=gsem.at[0], wait=False)
            obuf[...] = jnp.zeros_like(obuf)
            fill = pltpu.make_async_copy(obuf, y_hbm.at[pl.ds(y_hbm.shape[0] - tm, tm)], ssem.at[0])
            fill.start()
            fill.wait()

        gather(base=i * tm, buf=xbuf.at[slot], sem=gsem.at[slot], wait=True)

        @pl.when(i + 1 < nt)
        def _():
            gather(base=(i + 1) * tm, buf=xbuf.at[1 - slot], sem=gsem.at[1 - slot], wait=False)

        xb_ref[...] = xbuf[slot].astype(BF16)

    @pl.when(i < nt)
    def _():
        x = xb_ref[...]
        a = jax.nn.silu(_dot(x, wg_ref[0]))
        b = _dot(x, wu_ref[0])
        y = _dot((a * b).astype(BF16), wd_ref[0])

        @pl.when(f == 0)
        def _():
            acc_ref[...] = y

        @pl.when(f != 0)
        def _():
            acc_ref[...] += y

        @pl.when(f == last_f)
        def _():
            @pl.when(i > 0)
            def _():
                scatter(base=(i - 1) * tm, wait=True)

            obuf[...] = acc_ref[...]
            scatter(base=i * tm, wait=False)

            @pl.when(i == nt - 1)
            def _():
                scatter(base=i * tm, wait=True)


def experts(u, tile_expert, n_tiles, src_tok, dst_row, n_out_rows, wg, wu, wd, tm, tf):
    d = u.shape[1]
    ff = wg.shape[2]
    n_row_tiles = tile_expert.shape[0]
    wmap = lambda i, f, te, nt, src, dst: (te[i], 0, f)
    grid_spec = pltpu.PrefetchScalarGridSpec(
        num_scalar_prefetch=4,
        grid=(n_row_tiles, ff // tf),
        in_specs=[pl.BlockSpec(memory_space=pl.ANY),
                  pl.BlockSpec((1, d, tf), wmap),
                  pl.BlockSpec((1, d, tf), wmap),
                  pl.BlockSpec((1, tf, d), lambda i, f, te, nt, src, dst: (te[i], f, 0))],
        out_specs=pl.BlockSpec(memory_space=pl.ANY),
        scratch_shapes=[pltpu.VMEM((2, tm, d), F32), pltpu.VMEM((tm, d), BF16), pltpu.VMEM((tm, d), F32),
                        pltpu.VMEM((tm, d), F32), pltpu.SemaphoreType.DMA((2,)), pltpu.SemaphoreType.DMA((1,))],
    )
    return pl.pallas_call(
        functools.partial(_expert_kernel, tm=tm),
        grid_spec=grid_spec,
        out_shape=jax.ShapeDtypeStruct((n_out_rows, d), F32),
        compiler_params=pltpu.CompilerParams(
            dimension_semantics=("arbitrary", "arbitrary"), vmem_limit_bytes=VMEM_LIMIT,
            disable_bounds_checks=True),
        name="experts",
    )(tile_expert, n_tiles, src_tok, dst_row, u, wg, wu, wd)


def moe(h, g, w_router, wg, wu, wd, tm_route, tm_e, tf):
    m, d = h.shape
    u, top_w, top_i = router(h, g, w_router, tm_route)
    n_pairs = TOP_K * m
    flat_e = top_i[:, :TOP_K].reshape(-1)
    onehot = (flat_e[:, None] == jnp.arange(N_EXPERTS, dtype=jnp.int32)[None, :]).astype(jnp.int32)
    rank = jnp.sum((jnp.cumsum(onehot, axis=0) - 1) * onehot, axis=1)
    counts = jnp.sum(onehot, axis=0)
    tiles_per = (counts + tm_e - 1) // tm_e
    tile_end = jnp.cumsum(tiles_per)
    tile_start = tile_end - tiles_per
    grouped_row = tile_start[flat_e] * tm_e + rank
    n_rows = n_pairs + N_EXPERTS * tm_e
    n_row_tiles = n_rows // tm_e
    pair_ids = jnp.arange(n_pairs, dtype=jnp.int32)
    pair_at = jnp.full((n_rows,), -1, jnp.int32).at[grouped_row].set(pair_ids)
    spare = n_pairs + jnp.arange(n_rows, dtype=jnp.int32) % tm_e
    src_tok = jnp.where(pair_at >= 0, pair_at // TOP_K, 0)
    dst_row = jnp.where(pair_at >= 0, (pair_at % TOP_K) * m + pair_at // TOP_K, spare)
    tile_ids = jnp.arange(n_row_tiles, dtype=jnp.int32)
    tile_expert = jnp.minimum(jnp.sum((tile_ids[:, None] >= tile_end[None, :]).astype(jnp.int32), axis=1),
                              N_EXPERTS - 1).astype(jnp.int32)
    n_tiles = tile_end[-1:].astype(jnp.int32)
    last_e = tile_expert[jnp.maximum(n_tiles[0] - 1, 0)]
    tile_expert = jnp.where(tile_ids < n_tiles[0], tile_expert, last_e)
    y = experts(u, tile_expert, n_tiles, src_tok, dst_row, n_pairs + tm_e, wg, wu, wd, tm_e, tf)
    return y, top_w


def _ple_kernel(*refs, final, combine, n_first):
    if combine:
        h_ref, y0_ref, y1_ref, tw_ref, *refs = refs
        h = h_ref[...] + (tw_ref[:, 0:1] * y0_ref[...] + tw_ref[:, 1:2] * y1_ref[...])
    else:
        h_ref, *refs = refs
        h = h_ref[...]
    pp_ref, ps_ref, g_ref, wg_ref, wp_ref, gf_ref, *o_refs = refs
    first = pl.program_id(0) < n_first
    p = jnp.where(first, pp_ref[...], ps_ref[...])
    v = _rms(h, g_ref[...]).astype(BF16)
    pg = jax.nn.sigmoid(_dot(v, wg_ref[...]))
    e = _dot(p.astype(BF16), wp_ref[...])
    out = h + pg * e
    if not final:
        o_refs[0][...] = out
        return
    out = _rms(out, gf_ref[...])

    @pl.when(first)
    def _():
        o_refs[0][...] = out

    @pl.when(jnp.logical_not(first))
    def _():
        o_refs[1][...] = out


def ple(h, p_groups, g, w_gate, w_p, g_final, tm, final, expert_out=None):
    m, d = h.shape
    dp = p_groups[0].shape[1]
    n_first = p_groups[0].shape[0] // tm
    assert p_groups[0].shape[0] % tm == 0 and p_groups[1].shape[0] % tm == 0
    rows = lambda wd: pl.BlockSpec((tm, wd), lambda i: (i, 0))
    whole = lambda a, b: pl.BlockSpec((a, b), lambda i: (0, 0))
    extra, extra_specs = (), []
    if expert_out is not None:
        y, top_w = expert_out
        extra = (y, y, top_w)
        extra_specs = [rows(d), pl.BlockSpec((tm, d), lambda i: (i + m // tm, 0)), rows(LANES)]
    if final:
        out_specs = _group_specs(tm, d, n_first)
        out_shape = [jax.ShapeDtypeStruct((a.shape[0], d), F32) for a in p_groups]
    else:
        out_specs, out_shape = rows(d), jax.ShapeDtypeStruct((m, d), F32)
    return pl.pallas_call(
        functools.partial(_ple_kernel, final=final, combine=expert_out is not None, n_first=n_first),
        grid=(m // tm,),
        in_specs=([rows(d)] + extra_specs + _group_in_specs(p_groups, tm, n_first)
                  + [whole(1, d), whole(d, d), whole(dp, d), whole(1, d)]),
        out_specs=out_specs,
        out_shape=out_shape,
        compiler_params=pltpu.CompilerParams(
            dimension_semantics=("parallel",), vmem_limit_bytes=VMEM_LIMIT),
        name="ple",
    )(h, *extra, *[_operand(p) for p in p_groups], g.reshape(1, d), w_gate, w_p, g_final.reshape(1, d))


STATE_KEYS = ('gdn_conv', 'gdn', 'hgrn', 'ssd_conv', 'ssd', 'ml_c', 'ml_n', 'ml_m')
MIXER_KEYS = ('gdn_conv_w', 'gdn_a_log', 'gdn_dt_bias', 'gdn_norm', 'hgrn_norm',
              'ssd_conv_w', 'ssd_conv_b', 'ssd_a_log', 'ssd_dt_bias', 'ssd_d', 'ssd_norm',
              'ml_ig_b', 'ml_fg_b', 'ml_norm')


MIXER_SPECS = (gdn_spec, hgrn_spec, ssd_spec, mlstm_spec)
_gdn_kernel.n_stages = lambda t_len: 11 + 2 * (t_len.bit_length() - 2)
_hgrn_kernel.n_stages = lambda t_len: 1 + HG_H * (len(_hgrn_levels(t_len)) // 2 + 1)
_ssd_kernel.n_stages = lambda t_len: 8
_mlstm_kernel.n_stages = lambda t_len: 11


def _mixers(proj, row_blk0, st, w, lb, **grid):
    built = [f(st, w, lb, grid['n_seg']) for f in MIXER_SPECS]
    o, *new_states = _mixer_call([b[0] for b in built], proj, row_blk0, **grid)
    new, at = {}, 0
    for spec, finish in built:
        n = len(spec[2])
        new.update(finish(*new_states[at:at + n]))
        at += n
    return o, new


def kernel(x_prompt, x_sample, state_gdn_conv, state_gdn, state_hgrn, state_ssd_conv, state_ssd, state_mlstm_c, state_mlstm_n, state_mlstm_m, p_prompt, p_sample, g_mix, w_in, gdn_conv_w, gdn_a_log, gdn_dt_bias, gdn_norm, hgrn_lb, hgrn_norm, ssd_conv_w, ssd_conv_b, ssd_a_log, ssd_dt_bias, ssd_d, ssd_norm, ml_ig_b, ml_fg_b, ml_norm, w_branch, w_out, g_ffn, w_ff_gate, w_ff_up, w_ff_down, w_router, w_ex_gate, w_ex_up, w_ex_down, w_ple, w_ple_gate, g_ple, g_final):
    prm = {'gdn_conv_w': gdn_conv_w, 'gdn_a_log': gdn_a_log, 'gdn_dt_bias': gdn_dt_bias,
           'gdn_norm': gdn_norm, 'hgrn_norm': hgrn_norm, 'ssd_conv_w': ssd_conv_w,
           'ssd_conv_b': ssd_conv_b, 'ssd_a_log': ssd_a_log, 'ssd_dt_bias': ssd_dt_bias,
           'ssd_d': ssd_d, 'ssd_norm': ssd_norm, 'ml_ig_b': ml_ig_b, 'ml_fg_b': ml_fg_b,
           'ml_norm': ml_norm}
    bp, lp, d = x_prompt.shape
    bs, ls, _ = x_sample.shape
    mp, ms = bp * lp, bs * ls
    assert lp % CH == 0 and CH % ls == 0 and ms % CH == 0 and mp % CH == 0
    st_s = {'gdn_conv': state_gdn_conv, 'gdn': state_gdn, 'hgrn': state_hgrn, 'ssd_conv': state_ssd_conv,
            'ssd': state_ssd, 'ml_c': state_mlstm_c, 'ml_n': state_mlstm_n, 'ml_m': state_mlstm_m}
    st_p = {k: jnp.zeros((bp,) + v.shape[2:], F32) for k, v in st_s.items()}
    grid_p = dict(n_outer=bp, n_chunks=lp // CH, n_seg=1)
    grid_s = dict(n_outer=ms // CH, n_chunks=1, n_seg=CH // ls)

    sm = jax.nn.softmax(hgrn_lb, axis=0)
    lb_all = jnp.cumsum(sm, axis=0) - sm[0]

    h = jnp.concatenate([x_prompt.reshape(mp, d), x_sample.reshape(ms, d)], axis=0)
    new_p = {k: [] for k in STATE_KEYS}
    new_s = {k: [] for k in STATE_KEYS}
    for l in range(DEPTH):
        wl = {k: prm[k][l] for k in MIXER_KEYS}
        gates, mix = norm_matmul(h, g_mix[l], _permute_w_in(w_in, l), tm=512, tn=1024)
        br_p, np_ = _mixers(mix, 0, st_p, wl, lb_all[l], **grid_p)
        st_l = {k: (v[l] if k in ('ml_n', 'ml_m') else LayerOf(v, l)) for k, v in st_s.items()}
        br_s, ns_ = _mixers(mix, mp // CH, st_l, wl, lb_all[l], **grid_s)
        h = merge(h, gates, (br_p, br_s), w_branch[l].astype(BF16), w_out[l].astype(BF16), tm=512)
        j = l // 2
        expert_out = None
        if l % 2 == 0:
            h = ffn(h, g_ffn[l], w_ff_gate[j].astype(BF16), w_ff_up[j].astype(BF16),
                    w_ff_down[j].astype(BF16), tm=512, tf=D_FF)
        else:
            expert_out = moe(h, g_ffn[l], w_router[j], w_ex_gate[j].astype(BF16), w_ex_up[j].astype(BF16),
                             w_ex_down[j].astype(BF16), tm_route=512, tm_e=512, tf=D_FF // 2)
        p_groups = (LayerOf(p_prompt.reshape(DEPTH, mp, D_PLE), l), LayerOf(p_sample.reshape(DEPTH, ms, D_PLE), l))
        h = ple(h, p_groups, g_ple[l], w_ple_gate[l].astype(BF16), w_ple[l].astype(BF16), g_final,
                tm=512, final=(l == DEPTH - 1), expert_out=expert_out)
        for k in STATE_KEYS:
            new_p[k].append(np_[k])
            new_s[k].append(ns_[k])
    y_prompt = h[0].reshape(bp, lp, d)
    y_sample = h[1].reshape(bs, ls, d)
    sp = {k: jnp.stack(v) for k, v in new_p.items()}
    ss = {k: jnp.stack(v) for k, v in new_s.items()}
    return (y_prompt, y_sample,
            sp['gdn_conv'], sp['gdn'], sp['hgrn'], sp['ssd_conv'], sp['ssd'], sp['ml_c'], sp['ml_n'], sp['ml_m'],
            ss['gdn_conv'], ss['gdn'], ss['hgrn'], ss['ssd_conv'], ss['ssd'], ss['ml_c'], ss['ml_n'], ss['ml_m'])
```

```python
import functools

import numpy as np
import jax
import jax.numpy as jnp
from jax import lax
from jax.experimental import pallas as pl
from jax.experimental.pallas import tpu as pltpu

F32 = jnp.float32
BF16 = jnp.bfloat16
HI = lax.Precision.HIGHEST

D_MODEL = 1024
DEPTH = 2
D_PLE = 256
N_BRANCH = 4
BRANCH_W = D_MODEL // N_BRANCH
CONV_W = 4
EPS = 1e-6

GDN_H = 4
GDN_DK = 64
GDN_QK = GDN_H * GDN_DK
GDN_CONV_CH = 2 * GDN_QK + BRANCH_W
HG_H = 4
HG_QK = 256
SSD_H = 4
SSD_G = 2
SSD_N = 128
SSD_CONV_CH = BRANCH_W + 2 * SSD_G * SSD_N
ML_H = 4
ML_DK = 32
ML_QK = ML_H * ML_DK
D_FF = ((8 * D_MODEL // 3 + 255) // 256) * 256
N_EXPERTS = 8
TOP_K = 2

_REF_SPLITS = (('gdn_in', GDN_CONV_CH), ('gdn_b', GDN_H), ('gdn_a', GDN_H), ('gdn_z', BRANCH_W),
               ('hg_q', HG_QK), ('hg_f', HG_QK), ('hg_v', BRANCH_W), ('hg_g', BRANCH_W),
               ('ssd_z', BRANCH_W), ('ssd_in', SSD_CONV_CH), ('ssd_dt', SSD_H),
               ('ml_q', ML_QK), ('ml_k', ML_QK), ('ml_v', BRANCH_W), ('ml_i', ML_H), ('ml_f', ML_H),
               ('ml_o', BRANCH_W), ('gates', N_BRANCH * D_MODEL))
_MY_ORDER = ('gdn_in', 'ssd_in', 'gdn_z', 'ssd_z', 'hg_q', 'hg_f', 'hg_v', 'hg_g',
             'ml_v', 'ml_o', 'ml_q', 'ml_k', 'gdn_b', 'gdn_a', 'ssd_dt', 'ml_i', 'ml_f')
LANES = 128
MXU_TILE = 256
CH = 64
N_GATES = N_BRANCH * D_MODEL


def _layout():
    widths = dict(_REF_SPLITS)
    off, out = 0, {}
    for name in _MY_ORDER:
        out[name] = (off, widths[name])
        off += widths[name]
    return out, -(-off // MXU_TILE) * MXU_TILE


COLS, N_MIX = _layout()
GATE_COL0 = COLS['gdn_b'][0]
L_GDN_B, L_GDN_A, L_SSD_DT, L_ML_I, L_ML_F = (COLS[n][0] - GATE_COL0 for n in ('gdn_b', 'gdn_a', 'ssd_dt', 'ml_i', 'ml_f'))
VMEM_LIMIT = 56 * 1024 * 1024


def _ref_offsets():
    off, acc = {}, 0
    for name, wd in _REF_SPLITS:
        off[name] = acc
        acc += wd
    return off, acc


def _permute_kernel(w_ref, o_ref):
    ref_off, _ = _ref_offsets()
    tk = o_ref.shape[0]
    o_ref[:, :N_GATES] = w_ref[:, ref_off['gates']:ref_off['gates'] + N_GATES].astype(BF16)
    lane = lax.broadcasted_iota(jnp.int32, (tk, LANES), 1)
    gate_group = jnp.zeros((tk, LANES), F32)
    for name in _MY_ORDER:
        dst, wd = COLS[name]
        src = ref_off[name]
        if wd >= LANES:
            o_ref[:, N_GATES + dst:N_GATES + dst + wd] = w_ref[:, src:src + wd].astype(BF16)
        else:
            win0, at = src // LANES * LANES, src % LANES
            to = dst - GATE_COL0
            assert at + wd <= LANES and to + wd <= LANES
            window = pltpu.roll(w_ref[:, win0:win0 + LANES], (to - at) % LANES, axis=1)
            gate_group = jnp.where(jnp.logical_and(lane >= to, lane < to + wd), window, gate_group)
    o_ref[:, N_GATES + GATE_COL0:N_GATES + GATE_COL0 + LANES] = gate_group.astype(BF16)
    pad0 = N_GATES + GATE_COL0 + LANES
    o_ref[:, pad0:] = jnp.zeros((tk, N_GATES + N_MIX - pad0), BF16)


def _permute_w_in(w, tk=256):
    d, n_in = w.shape
    return pl.pallas_call(
        _permute_kernel, grid=(d // tk,),
        in_specs=[pl.BlockSpec((tk, n_in), lambda i: (i, 0))],
        out_specs=pl.BlockSpec((tk, N_GATES + N_MIX), lambda i: (i, 0)),
        out_shape=jax.ShapeDtypeStruct((d, N_GATES + N_MIX), BF16),
        compiler_params=pltpu.CompilerParams(dimension_semantics=("parallel",), vmem_limit_bytes=VMEM_LIMIT),
        name="permute_w_in",
    )(w)


def _rms(x, g):
    return x * lax.rsqrt(jnp.mean(x * x, axis=-1, keepdims=True) + EPS) * g


def _dot(a, b, **kw):
    return jnp.dot(a, b, preferred_element_type=F32, **kw)


def _dot_nt(a, b, **kw):
    return lax.dot_general(a, b, (((1,), (1,)), ((), ())), preferred_element_type=F32, **kw)


def _dot_tn(a, b, **kw):
    return lax.dot_general(a, b, (((0,), (0,)), ((), ())), preferred_element_type=F32, **kw)


def _norm_matmul_kernel(x_ref, g_ref, w_ref, gates_ref, mix_ref, *, tn):
    xn = _rms(x_ref[...], g_ref[...]).astype(BF16)
    for j in range(N_GATES // tn):
        gates_ref[:, j * tn:(j + 1) * tn] = _dot(xn, w_ref[:, j * tn:(j + 1) * tn]).astype(BF16)
    for j in range(N_MIX // tn):
        mix_ref[:, j * tn:(j + 1) * tn] = _dot(xn, w_ref[:, N_GATES + j * tn:N_GATES + (j + 1) * tn])


def norm_matmul(x, g, w, tm, tn):
    m, d = x.shape
    assert w.shape[1] == N_GATES + N_MIX and N_GATES % tn == 0 and N_MIX % tn == 0
    return pl.pallas_call(
        functools.partial(_norm_matmul_kernel, tn=tn),
        grid=(m // tm,),
        in_specs=[pl.BlockSpec((tm, d), lambda i: (i, 0)),
                  pl.BlockSpec((1, d), lambda i: (0, 0)),
                  pl.BlockSpec((d, N_GATES + N_MIX), lambda i: (0, 0), pipeline_mode=pl.Buffered(1))],
        out_specs=[pl.BlockSpec((tm, N_GATES), lambda i: (i, 0)),
                   pl.BlockSpec((tm, N_MIX), lambda i: (i, 0))],
        out_shape=[jax.ShapeDtypeStruct((m, N_GATES), BF16), jax.ShapeDtypeStruct((m, N_MIX), F32)],
        compiler_params=pltpu.CompilerParams(
            dimension_semantics=("parallel",), vmem_limit_bytes=VMEM_LIMIT),
        name="norm_matmul",
    )(x, g.reshape(1, d), w)


def _seg_masks(t_len):
    row = lax.broadcasted_iota(jnp.int32, (CH, CH), 0)
    col = lax.broadcasted_iota(jnp.int32, (CH, CH), 1)
    same = (row // t_len) == (col // t_len)
    tri = jnp.logical_and(same, col <= row)
    strict = jnp.logical_and(same, col < row)
    return same, tri, strict


def _row_forms(x, n_rows=24):
    r = lax.broadcasted_iota(jnp.int32, (n_rows, LANES), 0)
    l = lax.broadcasted_iota(jnp.int32, (n_rows, LANES), 1)
    return _dot_nt((r == l).astype(F32), x, precision=HI)


def _softplus(x):
    return jnp.maximum(x, 0.0) + jnp.log1p(jnp.exp(-jnp.abs(x)))


def _split2(x):
    hi = x.astype(BF16)
    return hi, (x - hi.astype(F32)).astype(BF16)


def _dot3(a, b):
    return _dot(a[0], b[0]) + (_dot(a[0], b[1]) + _dot(a[1], b[0]))


def _masked_exp(d, mask):
    return jnp.where(mask, jnp.exp(jnp.where(mask, d, 0.0)), 0.0)


def _conv_silu(x, ext_scr, cw_ref, bias, t_len, n_seg):
    w = x.shape[-1]
    ext_scr[:, 8:8 + t_len, :] = x.reshape(n_seg, t_len, w)
    y = cw_ref[3:4, :] * x
    for j in range(1, CONV_W):
        y = y + cw_ref[3 - j:4 - j, :] * ext_scr[:, 8 - j:8 - j + t_len, :].reshape(CH, w)
    if bias is not None:
        y = y + bias
    tail = ext_scr[:, 5 + t_len:8 + t_len, :]
    ext_scr[:, 5:8, :] = tail
    return jax.nn.silu(y), tail


def _halves(xp, lo):
    s_lo = jnp.sum(jnp.where(lo, xp, 0.0), axis=-1, keepdims=True)
    s_hi = jnp.sum(jnp.where(lo, 0.0, xp), axis=-1, keepdims=True)
    return jnp.where(lo, s_lo, s_hi)


def _head_rmsnorm(xp, lo):
    return xp * lax.rsqrt(_halves(xp * xp, lo) * (1.0 / 64) + EPS)


def _head_l2norm(xp, lo):
    return xp * lax.rsqrt(_halves(xp * xp, lo) + EPS)


def _seg_sum(parts, rowi, t_len):
    if len(parts) == 1:
        return parts[0]
    acc = jnp.where(rowi // t_len == 0, parts[0], 0.0)
    for s in range(1, len(parts)):
        acc = acc + jnp.where(rowi // t_len == s, parts[s], 0.0)
    return acc


def _seg_rows(x, rowi, t_len, s, n_seg):
    return x if n_seg == 1 else jnp.where(rowi // t_len == s, x, 0.0)


def _quarter_sel(idx, width, vals):
    out = vals[3]
    for h in (2, 1, 0):
        out = jnp.where(idx < (h + 1) * width, vals[h], out)
    return out


def _gate_rows(pairs):
    t = jnp.zeros((8, LANES), F32)
    for r, (off, v) in enumerate(pairs):
        t = t.at[r, off:off + v.shape[0]].set(v.astype(F32))
    return t


def _block_diag(blocks):
    n = len(blocks)
    z = jnp.zeros_like(blocks[0])
    return jnp.concatenate(
        [jnp.concatenate([blocks[i] if i == j else z for j in range(n)], axis=1) for i in range(n)], axis=0)


def _transpose64(x):
    r = lax.broadcasted_iota(jnp.int32, x.shape, 0)
    c = lax.broadcasted_iota(jnp.int32, x.shape, 1)
    eye = (r == c).astype(BF16)
    hi, mid, lo = _split3(x)
    return (_dot_nt(eye, hi) + _dot_nt(eye, mid)) + _dot_nt(eye, lo)


def _load_pair_states(s0_ref, s_scr, n_seg, transpose):
    prep = _transpose64 if transpose else (lambda t: t)
    for s in range(n_seg):
        for p in range(2):
            s_scr[s, p] = _block_diag([prep(s0_ref[s, 2 * p]), prep(s0_ref[s, 2 * p + 1])])


def _store_pair_states(s_scr, sn_ref, n_seg, transpose):
    prep = _transpose64 if transpose else (lambda t: t)
    for s in range(n_seg):
        for p in range(2):
            sn_ref[s, 2 * p] = prep(s_scr[s, p, 0:64, 0:64])
            sn_ref[s, 2 * p + 1] = prep(s_scr[s, p, 64:128, 64:128])


class LayerOf:
    def __init__(self, stack, layer):
        self.stack, self.layer = stack, layer
        self.shape, self.ndim = stack.shape[1:], stack.ndim - 1


def _operand(a):
    return a.stack if isinstance(a, LayerOf) else a


def _row_block_spec(a, block_rows, index_of_step):
    rest = (0,) * (a.ndim - 1)
    if isinstance(a, LayerOf):
        return pl.BlockSpec((None, block_rows) + a.shape[1:], lambda *ids: (a.layer, index_of_step(*ids)) + rest)
    return pl.BlockSpec((block_rows,) + a.shape[1:], lambda *ids: (index_of_step(*ids),) + rest)


def _interleave(gens, n_stages):
    results = [None] * len(gens)
    pos = [0] * len(gens)
    live = set(range(len(gens)))
    while live:
        k = min(live, key=lambda i: ((pos[i] + 1) / n_stages[i], i))
        try:
            next(gens[k])
            pos[k] += 1
        except StopIteration as stop:
            results[k] = stop.value
            live.remove(k)
    return results


def _fused_mixer_kernel(*refs, n_seg, parts):
    tot = [sum(p[j] for p in parts) for j in range(1, 5)]
    ins, rest = refs[:tot[0]], refs[tot[0]:]
    sts, rest = rest[:tot[1]], rest[tot[1]:]
    prs, rest = rest[:tot[2]], rest[tot[2]:]
    o_ref, rest = rest[0], rest[1:]
    outs, scr = rest[:tot[1]], rest[tot[1]:]
    at = [0, 0, 0, 0]
    calls = []
    for k, (body, n_in, n_st, n_pr, n_scr) in enumerate(parts):
        take = lambda seq, j, n: seq[at[j]:at[j] + n]
        calls.append(functools.partial(
            body, *take(ins, 0, n_in), *take(sts, 1, n_st), *take(prs, 2, n_pr),
            o_ref.at[:, k * BRANCH_W:(k + 1) * BRANCH_W],
            *take(outs, 1, n_st), *take(scr, 3, n_scr), n_seg=n_seg))
        for j, n in enumerate((n_in, n_st, n_pr, n_scr)):
            at[j] += n

    n_stages = [p[0].n_stages(CH // n_seg) for p in parts]

    @pl.when(pl.program_id(1) == 0)
    def _():
        _interleave([call(init=True) for call in calls], n_stages)

    finishers = _interleave([call(init=False) for call in calls], n_stages)

    @pl.when(pl.program_id(1) == pl.num_programs(1) - 1)
    def _():
        for fin in finishers:
            fin()


def _mixer_call(specs, proj, row_blk0, *, n_outer, n_chunks, n_seg):
    rows = n_outer * n_chunks * CH
    rmap = lambda blk: (lambda i, c: (row_blk0 + i * n_chunks + c, blk))
    full = lambda a: pl.BlockSpec(a.shape, lambda i, c: (0,) * a.ndim)
    sblk = lambda a: _row_block_spec(a, a.shape[0] // n_outer, lambda i, c: i)
    oblk = lambda a: pl.BlockSpec((a.shape[0] // n_outer,) + a.shape[1:], lambda i, c: (i,) + (0,) * (a.ndim - 1))
    in_blocks = [b for s in specs for b in s[1]]
    state_ins = [a for s in specs for a in s[2]]
    params = [a for s in specs for a in s[3]]
    scratch = [a for s in specs for a in s[4]]
    for off, wd in in_blocks:
        assert off % wd == 0
    parts = tuple((s[0], len(s[1]), len(s[2]), len(s[3]), len(s[4])) for s in specs)
    width = len(specs) * BRANCH_W
    return pl.pallas_call(
        functools.partial(_fused_mixer_kernel, n_seg=n_seg, parts=parts),
        grid=(n_outer, n_chunks),
        in_specs=([pl.BlockSpec((CH, wd), rmap(off // wd)) for off, wd in in_blocks]
                  + [sblk(a) for a in state_ins] + [full(a) for a in params]),
        out_specs=[pl.BlockSpec((CH, width), lambda i, c: (i * n_chunks + c, 0))] + [oblk(a) for a in state_ins],
        out_shape=([jax.ShapeDtypeStruct((rows, width), BF16)]
                   + [jax.ShapeDtypeStruct(a.shape, F32) for a in state_ins]),
        scratch_shapes=scratch,
        compiler_params=pltpu.CompilerParams(
            dimension_semantics=("parallel", "arbitrary"), vmem_limit_bytes=VMEM_LIMIT),
        name="token_mixers",
    )(*([proj] * len(in_blocks)), *[_operand(a) for a in state_ins], *params)


def _gdn_kernel(xin_ref, z_ref, sm_ref, conv0_ref, s0_ref, cw_ref, gp_ref, ng_ref,
                o_ref, convn_ref, sn_ref, ext_scr, s_scr, *, n_seg, init):
    t_len = CH // n_seg
    if init:
        _load_pair_states(s0_ref, s_scr, n_seg, transpose=False)
        ext_scr[:, 5:8, :] = conv0_ref[...]
        return None

    xc, tail = _conv_silu(xin_ref[...], ext_scr, cw_ref, None, t_len, n_seg)
    yield
    same, tri, strict = _seg_masks(t_len)
    lane = lax.broadcasted_iota(jnp.int32, (CH, LANES), 1)
    rowi = lax.broadcasted_iota(jnp.int32, (CH, 1), 0)
    lo = lane < 64
    sm = sm_ref[...]
    beta = jax.nn.sigmoid(sm)
    gl = jnp.logical_and(lane >= L_GDN_A, lane < L_GDN_A + GDN_H)
    g = jnp.where(gl, -jnp.exp(gp_ref[0:1, :]) * _softplus(sm + gp_ref[1:2, :]), 0.0)
    gam = _dot(tri.astype(F32), g, precision=HI)
    gam_end = _dot(same.astype(F32), g, precision=HI)
    gam_r = _row_forms(gam, 8)
    yield
    r128 = lax.broadcasted_iota(jnp.int32, (LANES, LANES), 0)
    l128 = lax.broadcasted_iota(jnp.int32, (LANES, LANES), 1)
    bd = (r128 < 64) == (l128 < 64)
    rsel = lax.broadcasted_iota(jnp.int32, (LANES, 1), 0) < 64
    qs_, ks_, atts_, gcs_, a_, x_ = [], [], [], [], [], []
    for p in range(2):
        q_p = _head_l2norm(xc[:, 128 * p:128 * (p + 1)], lo) * (GDN_DK ** -0.5)
        k_p = _head_l2norm(xc[:, GDN_QK + 128 * p:GDN_QK + 128 * (p + 1)], lo)
        v_p = xc[:, 2 * GDN_QK + 128 * p:2 * GDN_QK + 128 * (p + 1)]
        kb = k_p.astype(BF16)
        qs_.append(q_p)
        ks_.append(k_p)
        for j in range(2):
            h = 2 * p + j
            mj = lo if j == 0 else jnp.logical_not(lo)
            kk = _dot_nt(jnp.where(mj, k_p, 0.0).astype(BF16), kb)
            qk = _dot_nt(jnp.where(mj, q_p, 0.0).astype(BF16), kb)
            gc = gam[:, L_GDN_A + h:L_GDN_A + h + 1]
            bc = beta[:, L_GDN_B + h:L_GDN_B + h + 1]
            dec = _masked_exp(gc - gam_r[L_GDN_A + h:L_GDN_A + h + 1, :], tri)
            a_.append(jnp.where(strict, bc * kk * dec, 0.0))
            x_.append(jnp.concatenate([jnp.where(mj, bc * v_p, 0.0),
                                       jnp.where(mj, (bc * jnp.exp(gc)) * k_p, 0.0)], axis=-1))
            atts_.append(qk * dec)
            gcs_.append(gc)
            yield
    eye = jnp.logical_and(tri, jnp.logical_not(strict)).astype(F32)
    sa = [_split2(a) for a in a_]
    t_ = [eye - a for a in a_]
    n = 2
    while n < t_len:
        sa = [_split2(_dot3(s, s)) for s in sa]
        yield
        t_ = [t + _dot3(_split2(t), s) for t, s in zip(t_, sa)]
        yield
        n *= 2
    x_ = [_dot3(_split2(t), _split2(x)) for t, x in zip(t_, x_)]
    yield
    outs = []
    for p in range(2):
        q_p, k_p = qs_[p], ks_[p]
        solv = x_[2 * p][:, :LANES] + x_[2 * p + 1][:, :LANES]
        solk = x_[2 * p][:, LANES:] + x_[2 * p + 1][:, LANES:]
        atts, gcs = atts_[2 * p:2 * p + 2], gcs_[2 * p:2 * p + 2]
        solk_b = solk.astype(BF16)
        qb = q_p.astype(BF16)
        u = solv - _seg_sum([_dot(solk_b, s_scr[s, p].astype(BF16)) for s in range(n_seg)], rowi, t_len)
        qs = _seg_sum([_dot(qb, s_scr[s, p].astype(BF16)) for s in range(n_seg)], rowi, t_len)
        o = jnp.where(lo, jnp.exp(gcs[0]), jnp.exp(gcs[1])) * qs
        for j in range(2):
            mj = lo if j == 0 else jnp.logical_not(lo)
            o = o + _dot(atts[j].astype(BF16), jnp.where(mj, u, 0.0).astype(BF16))
        yield
        ge0 = gam_end[:, L_GDN_A + 2 * p:L_GDN_A + 2 * p + 1]
        ge1 = gam_end[:, L_GDN_A + 2 * p + 1:L_GDN_A + 2 * p + 2]
        kw = k_p * jnp.where(lo, jnp.exp(ge0 - gcs[0]), jnp.exp(ge1 - gcs[1]))
        ub = u.astype(BF16)
        for s in range(n_seg):
            r0 = s * t_len
            dec_s = jnp.where(rsel, jnp.exp(ge0[r0:r0 + 1, :]), jnp.exp(ge1[r0:r0 + 1, :]))
            upd = _dot_tn(_seg_rows(kw, rowi, t_len, s, n_seg).astype(BF16), ub)
            s_scr[s, p] = dec_s * s_scr[s, p] + jnp.where(bd, upd, 0.0)
        outs.append(_head_rmsnorm(o, lo))
        yield
    o_all = jnp.concatenate(outs, axis=-1) * ng_ref[...] * jax.nn.silu(z_ref[...])
    o_ref[...] = o_all.astype(BF16)

    def finish():
        convn_ref[...] = tail
        _store_pair_states(s_scr, sn_ref, n_seg, transpose=False)
    return finish


def gdn_spec(st, w, lb, n_seg):
    t_len = CH // n_seg
    gp = _gate_rows([(L_GDN_A, w['gdn_a_log']), (L_GDN_A, w['gdn_dt_bias'])])
    ng = jnp.tile(w['gdn_norm'], GDN_H).reshape(1, BRANCH_W)
    spec = (_gdn_kernel, [COLS['gdn_in'], COLS['gdn_z'], (GATE_COL0, LANES)],
            [st['gdn_conv'], st['gdn']], [w['gdn_conv_w'], gp, ng],
            [pltpu.VMEM((n_seg, 8 + t_len, GDN_CONV_CH), F32), pltpu.VMEM((n_seg, 2, 128, 128), F32)])
    return spec, lambda convn, sn: {'gdn_conv': convn, 'gdn': sn}


def _hgrn_levels(t_len):
    lv, n = [], t_len
    while n >= 2:
        lv.append(n)
        n //= 2
    return lv


def _hgrn_cmat(t_len):
    t = np.arange(CH)[:, None]
    j = np.arange(CH)[None, :]
    same = (t // t_len) == (j // t_len)
    mats = [same & (j <= t), same]
    for n in _hgrn_levels(t_len):
        mid = (t // n) * n + n // 2
        second = t % n >= n // 2
        mats.append((second & (j >= mid) & (j <= t)) | (~second & (j > t) & (j <= mid - 1)))
    return jnp.asarray(np.concatenate(mats, axis=0).astype(np.float32), dtype=BF16)


def _split3(x):
    hi = x.astype(BF16)
    r = x - hi.astype(F32)
    mid = r.astype(BF16)
    return hi, mid, (r - mid.astype(F32)).astype(BF16)


def _hgrn_kernel(x_ref, s0_ref, cm_ref, lb_ref, ng_ref, o_ref, sn_ref, s_scr, *, n_seg, init):
    t_len = CH // n_seg
    levels = _hgrn_levels(t_len)
    if init:
        _load_pair_states(s0_ref, s_scr, n_seg, transpose=True)
        return None

    lane = lax.broadcasted_iota(jnp.int32, (CH, LANES), 1)
    rowi = lax.broadcasted_iota(jnp.int32, (CH, 1), 0)
    row = lax.broadcasted_iota(jnp.int32, (CH, CH), 0)
    col = lax.broadcasted_iota(jnp.int32, (CH, CH), 1)
    lo = lane < 64
    r128 = lax.broadcasted_iota(jnp.int32, (LANES, LANES), 0)
    l128 = lax.broadcasted_iota(jnp.int32, (LANES, LANES), 1)
    bd = (r128 < 64) == (l128 < 64)

    lb = lb_ref[...]
    f_pre = x_ref[:, HG_QK:2 * HG_QK]
    log_f = jnp.log(lb + (1.0 - lb) * jax.nn.sigmoid(f_pre))
    k_in = (1.0 - lb) * jax.nn.sigmoid(-f_pre)
    cm = cm_ref[...]
    ex = None
    for part in _split3(log_f):
        t = _dot(cm, part)
        ex = t if ex is None else ex + t
    b = ex[0:CH]
    b_end = ex[CH:2 * CH]
    yield
    outs = []
    for p in range(2):
        ls = slice(128 * p, 128 * (p + 1))
        q_p = x_ref[:, ls]
        k_p = k_in[:, ls]
        v_p = x_ref[:, 2 * HG_QK + 128 * p:2 * HG_QK + 128 * (p + 1)]
        qk = q_p * k_p
        qe = (q_p * jnp.exp(b[:, ls])).astype(BF16)
        o = _seg_sum([_dot_nt(qe, s_scr[s, p].astype(BF16)) for s in range(n_seg)], rowi, t_len)
        scales = [jnp.exp(ex[(2 + li) * CH:(3 + li) * CH, ls]) for li in range(len(levels))]
        for j in range(2):
            mj = lo if j == 0 else jnp.logical_not(lo)
            diag = jnp.sum(jnp.where(mj, qk, 0.0), axis=-1, keepdims=True)
            att = jnp.where(row == col, diag, 0.0)
            for li, n in enumerate(levels):
                tq = (rowi % n) >= (n // 2)
                qt = jnp.where(jnp.logical_and(mj, tq), q_p * scales[li], 0.0)
                kt = jnp.where(tq, 0.0, k_p * scales[li])
                att = att + jnp.where((row // n) == (col // n), _dot_nt(qt.astype(BF16), kt.astype(BF16)), 0.0)
                if li % 2 == 1:
                    yield
            o = o + _dot(att.astype(BF16), jnp.where(mj, v_p, 0.0).astype(BF16))
            yield
        kw = (k_p * jnp.exp(b_end[:, ls] - b[:, ls])).astype(BF16)
        for s in range(n_seg):
            r0 = s * t_len
            upd = _dot_tn(_seg_rows(v_p, rowi, t_len, s, n_seg).astype(BF16), kw)
            s_scr[s, p] = jnp.exp(b_end[r0:r0 + 1, ls]) * s_scr[s, p] + jnp.where(bd, upd, 0.0)
        outs.append(_head_rmsnorm(o, lo))
    o_all = jnp.concatenate(outs, axis=-1) * ng_ref[...] * jax.nn.silu(x_ref[:, 3 * HG_QK:4 * HG_QK])
    o_ref[...] = o_all.astype(BF16)

    def finish():
        _store_pair_states(s_scr, sn_ref, n_seg, transpose=True)
    return finish


def hgrn_spec(st, w, lb, n_seg):
    t_len = CH // n_seg
    ng = jnp.tile(w['hgrn_norm'], HG_H).reshape(1, BRANCH_W)
    assert COLS['hg_f'][0] == COLS['hg_q'][0] + HG_QK and COLS['hg_g'][0] == COLS['hg_q'][0] + 3 * HG_QK
    spec = (_hgrn_kernel, [(COLS['hg_q'][0], 4 * HG_QK)], [st['hgrn']],
            [_hgrn_cmat(t_len), lb.reshape(1, HG_QK), ng], [pltpu.VMEM((n_seg, 2, 128, 128), F32)])
    return spec, lambda sn: {'hgrn': sn}


def _ssd_kernel(xin_ref, z_ref, sm_ref, conv0_ref, h0_ref, cw_ref, cb_ref, gp_ref, dvec_ref, ng_ref,
                o_ref, convn_ref, hn_ref, ext_scr, h_scr, *, n_seg, init):
    t_len = CH // n_seg
    if init:
        h_scr[...] = h0_ref[...].reshape(n_seg, 2, 128, SSD_N)
        ext_scr[:, 5:8, :] = conv0_ref[...]
        return None

    xc, tail = _conv_silu(xin_ref[...], ext_scr, cw_ref, cb_ref[...], t_len, n_seg)
    yield
    sx, bm, cm = xc[:, :256], xc[:, 256:512], xc[:, 512:768]
    same, tri, _ = _seg_masks(t_len)
    lane = lax.broadcasted_iota(jnp.int32, (CH, LANES), 1)
    rowi = lax.broadcasted_iota(jnp.int32, (CH, 1), 0)
    gl = jnp.logical_and(lane >= L_SSD_DT, lane < L_SSD_DT + SSD_H)
    dt = jnp.where(gl, _softplus(sm_ref[...] + gp_ref[1:2, :]), 0.0)
    da = -jnp.exp(gp_ref[0:1, :]) * dt
    cum = _dot(tri.astype(F32), da, precision=HI)
    cum_end = _dot(same.astype(F32), da, precision=HI)
    cum_r = _row_forms(cum, 16)
    dt_r = _row_forms(dt, 16)
    yield
    lo = lane < 64
    rsel = lax.broadcasted_iota(jnp.int32, (LANES, 1), 0) < 64
    ys = []
    for g in range(SSD_G):
        cg = cm[:, 128 * g:128 * (g + 1)].astype(BF16)
        bg = bm[:, 128 * g:128 * (g + 1)].astype(BF16)
        sxp = sx[:, 128 * g:128 * (g + 1)]
        cb = _dot_nt(cg, bg)
        yst = _seg_sum([_dot_nt(cg, h_scr[s, g].astype(BF16)) for s in range(n_seg)], rowi, t_len)
        yatt = jnp.zeros((CH, LANES), F32)
        cols = []
        for j in range(2):
            l = L_SSD_DT + 2 * g + j
            cc = cum[:, l:l + 1]
            dec = _masked_exp(cc - cum_r[l:l + 1, :], tri)
            att = cb * dec * dt_r[l:l + 1, :]
            xm = jnp.where(lo if j == 0 else jnp.logical_not(lo), sxp, 0.0)
            yatt = yatt + _dot(att.astype(BF16), xm.astype(BF16))
            cols.append((cc, dt[:, l:l + 1] * jnp.exp(cum_end[:, l:l + 1] - cc)))
            yield
        ys.append(jnp.where(lo, jnp.exp(cols[0][0]), jnp.exp(cols[1][0])) * yst + yatt)
        xw = sxp * jnp.where(lo, cols[0][1], cols[1][1])
        for s in range(n_seg):
            r0 = s * t_len
            l = L_SSD_DT + 2 * g
            e0 = jnp.exp(cum_end[r0:r0 + 1, l:l + 1])
            e1 = jnp.exp(cum_end[r0:r0 + 1, l + 1:l + 2])
            upd = _dot_tn(_seg_rows(xw, rowi, t_len, s, n_seg).astype(BF16), bg)
            h_scr[s, g] = jnp.where(rsel, e0, e1) * h_scr[s, g] + upd
        yield
    y_all = jnp.concatenate(ys, axis=-1) + dvec_ref[...] * sx
    o_ref[...] = _rms(y_all * jax.nn.silu(z_ref[...]), ng_ref[...]).astype(BF16)

    def finish():
        convn_ref[...] = tail
        hn_ref[...] = h_scr[...].reshape(n_seg, SSD_H, 64, SSD_N)
    return finish


def ssd_spec(st, w, lb, n_seg):
    t_len = CH // n_seg
    gp = _gate_rows([(L_SSD_DT, w['ssd_a_log']), (L_SSD_DT, w['ssd_dt_bias'])])
    dvec = jnp.repeat(w['ssd_d'], BRANCH_W // SSD_H).reshape(1, BRANCH_W)
    spec = (_ssd_kernel, [COLS['ssd_in'], COLS['ssd_z'], (GATE_COL0, LANES)],
            [st['ssd_conv'], st['ssd']],
            [w['ssd_conv_w'], w['ssd_conv_b'].reshape(1, SSD_CONV_CH), gp, dvec, w['ssd_norm'].reshape(1, BRANCH_W)],
            [pltpu.VMEM((n_seg, 8 + t_len, SSD_CONV_CH), F32), pltpu.VMEM((n_seg, 2, 128, SSD_N), F32)])
    return spec, lambda convn, hn: {'ssd_conv': convn, 'ssd': hn}


def _mlstm_kernel(vo_ref, qk_ref, sm_ref, c0_ref, n0_ref, m0_ref, gp_ref, ng_ref, o_ref, cn_ref, nn_ref, mn_ref,
                  c_scr, n_scr, m_scr, *, n_seg, init):
    t_len = CH // n_seg
    if init:
        for s in range(n_seg):
            c_scr[s] = jnp.concatenate(
                [jnp.concatenate([c0_ref[s, h] if g == h else jnp.zeros((ML_DK, 64), F32) for g in range(ML_H)], axis=1)
                 for h in range(ML_H)], axis=0)
        n_scr[...] = n0_ref[...]
        m_scr[...] = m0_ref[...]
        return None

    same, tri, _ = _seg_masks(t_len)
    lane = lax.broadcasted_iota(jnp.int32, (CH, LANES), 1)
    lane256 = lax.broadcasted_iota(jnp.int32, (CH, BRANCH_W), 1)
    rowi = lax.broadcasted_iota(jnp.int32, (CH, 1), 0)
    r128 = lax.broadcasted_iota(jnp.int32, (LANES, 1), 0)
    neg = jnp.float32(-jnp.inf)

    v_all = vo_ref[:, 0:BRANCH_W]
    q_all = qk_ref[:, 0:ML_QK]
    k_all = qk_ref[:, ML_QK:2 * ML_QK] * (ML_DK ** -0.5)
    sm = sm_ref[...]
    ig = sm + gp_ref[0:1, :]
    fl = jnp.logical_and(lane >= L_ML_F, lane < L_ML_F + ML_H)
    lf = jnp.where(fl, -_softplus(-(sm + gp_ref[1:2, :])), 0.0)
    b = _dot(tri.astype(F32), lf, precision=HI)
    b_end = _dot(same.astype(F32), lf, precision=HI)
    b_r = _row_forms(b)
    ig_r = _row_forms(ig)
    yield
    mm = m_scr[...]
    qb = q_all.astype(BF16)
    kb = k_all.astype(BF16)
    qn = _seg_sum([_dot(qb, n_scr[s].astype(BF16)) for s in range(n_seg)], rowi, t_len)
    qc = _seg_sum([_dot(qb, c_scr[s].astype(BF16)) for s in range(n_seg)], rowi, t_len)
    yield
    num_att = jnp.zeros((CH, BRANCH_W), F32)
    w_ins, dens, w_ends, a_ends, m_ends = [], [], [], [], []
    for h in range(ML_H):
        li, lf_ = L_ML_I + h, L_ML_F + h
        bc = b[:, lf_:lf_ + 1]
        bec = b_end[:, lf_:lf_ + 1]
        igc = ig[:, li:li + 1]
        mmc = mm[:, li:li + 1]
        br = b_r[lf_:lf_ + 1, :]
        igr = ig_r[li:li + 1, :]
        diff = igr - br
        cmx = jnp.max(jnp.where(tri, diff, neg), axis=-1, keepdims=True)
        smx = jnp.max(jnp.where(same, diff, neg), axis=-1, keepdims=True)
        m_c = bc + jnp.maximum(mmc, cmx)
        m_end = bec + jnp.maximum(mmc, smx)
        w_in = jnp.exp(bc + mmc - m_c)
        logw = bc - br + igr - m_c
        mq = jnp.logical_and(lane >= ML_DK * h, lane < ML_DK * (h + 1))
        qk = _dot_nt(jnp.where(mq, q_all, 0.0).astype(BF16), kb)
        wts = _masked_exp(logw, tri) * qk
        yield
        mv = jnp.logical_and(lane256 >= 64 * h, lane256 < 64 * (h + 1))
        num_att = num_att + _dot(wts.astype(BF16), jnp.where(mv, v_all, 0.0).astype(BF16))
        nq = w_in * qn[:, li:li + 1] + jnp.sum(wts, axis=-1, keepdims=True)
        w_ins.append(w_in)
        dens.append(jnp.maximum(jnp.abs(nq), jnp.exp(-m_c)))
        w_ends.append(jnp.exp(bec - bc + igc - m_end))
        a_ends.append(jnp.exp(bec + mmc - m_end))
        m_ends.append(m_end)
        yield
    num = _quarter_sel(lane256, 64, w_ins) * qc + num_att
    hout = num / _quarter_sel(lane256, 64, dens)
    outs = [_head_rmsnorm(hout[:, 128 * p:128 * (p + 1)], lane < 64) for p in range(2)]
    o_all = jnp.concatenate(outs, axis=-1) * ng_ref[...] * jax.nn.sigmoid(vo_ref[:, BRANCH_W:2 * BRANCH_W])
    o_ref[...] = o_all.astype(BF16)
    yield

    kw = k_all * _quarter_sel(lane, ML_DK, w_ends)
    wend_tile = jnp.zeros((CH, LANES), F32)
    m_tile = jnp.zeros((CH, LANES), F32)
    for h in range(ML_H):
        wend_tile = jnp.where(lane == L_ML_I + h, w_ends[h], wend_tile)
        m_tile = jnp.where(lane == L_ML_I + h, m_ends[h], m_tile)
    m_scr[...] = m_tile
    vb = v_all.astype(BF16)
    wb = wend_tile.astype(BF16)
    rc = lax.broadcasted_iota(jnp.int32, (LANES, BRANCH_W), 0)
    lc = lax.broadcasted_iota(jnp.int32, (LANES, BRANCH_W), 1)
    bd_c = (rc // ML_DK) == (lc // 64)
    rn = lax.broadcasted_iota(jnp.int32, (LANES, LANES), 0)
    ln = lax.broadcasted_iota(jnp.int32, (LANES, LANES), 1)
    bd_n = ln == (rn // ML_DK) + L_ML_I
    for s in range(n_seg):
        r0 = s * t_len
        a_sel = _quarter_sel(r128, ML_DK, [a[r0:r0 + 1, :] for a in a_ends])
        upd_c = _dot_tn(_seg_rows(kw, rowi, t_len, s, n_seg).astype(BF16), vb)
        upd_n = _dot_tn(_seg_rows(k_all, rowi, t_len, s, n_seg).astype(BF16), wb)
        c_scr[s] = a_sel * c_scr[s] + jnp.where(bd_c, upd_c, 0.0)
        n_scr[s] = a_sel * n_scr[s] + jnp.where(bd_n, upd_n, 0.0)

    def finish():
        for s in range(n_seg):
            for h in range(ML_H):
                cn_ref[s, h] = c_scr[s, ML_DK * h:ML_DK * (h + 1), 64 * h:64 * (h + 1)]
        nn_ref[...] = n_scr[...]
        mn_ref[...] = m_tile
    return finish


def mlstm_spec(st, w, lb, n_seg):
    t_len = CH // n_seg
    c0, n0, m0 = st['ml_c'], st['ml_n'], st['ml_m']
    bsz = c0.shape[0]
    gp = _gate_rows([(L_ML_I, w['ml_ig_b']), (L_ML_F, w['ml_fg_b'])])
    ng = jnp.tile(w['ml_norm'], ML_H).reshape(1, BRANCH_W)
    eye = jnp.eye(ML_H, dtype=F32)
    pad = ((0, 0), (L_ML_I, LANES - L_ML_I - ML_H))
    n_bd = jnp.pad(jnp.einsum('bhk,hg->bhkg', n0, eye).reshape(bsz, ML_QK, ML_H), ((0, 0),) + pad)
    m_exp = jnp.pad(jnp.repeat(m0, t_len, axis=0), pad)
    assert COLS['ml_o'][0] == COLS['ml_v'][0] + BRANCH_W and COLS['ml_k'][0] == COLS['ml_q'][0] + ML_QK
    spec = (_mlstm_kernel, [(COLS['ml_v'][0], 2 * BRANCH_W), (COLS['ml_q'][0], 2 * ML_QK), (GATE_COL0, LANES)],
            [c0, n_bd, m_exp], [gp, ng],
            [pltpu.VMEM((n_seg, ML_QK, BRANCH_W), F32), pltpu.VMEM((n_seg, ML_QK, LANES), F32),
             pltpu.VMEM((CH, LANES), F32)])

    def finish(c_new, nn, mn):
        n_new = jnp.einsum('bhkg,hg->bhk', nn[:, :, L_ML_I:L_ML_I + ML_H].reshape(bsz, ML_H, ML_DK, ML_H), eye)
        return {'ml_c': c_new, 'ml_n': n_new, 'ml_m': mn[::t_len, L_ML_I:L_ML_I + ML_H]}
    return spec, finish


def _group_specs(tm, width, n_first):
    return [pl.BlockSpec((tm, width), lambda i: (jnp.minimum(i, n_first - 1), 0)),
            pl.BlockSpec((tm, width), lambda i: (jnp.maximum(i - n_first, 0), 0))]


def _group_in_specs(groups, tm, n_first):
    return [_row_block_spec(groups[0], tm, lambda i: jnp.minimum(i, n_first - 1)),
            _row_block_spec(groups[1], tm, lambda i: jnp.maximum(i - n_first, 0))]


def _merge_kernel(h_ref, gates_ref, brp_ref, brs_ref, wb_ref, wo_ref, o_ref, *, n_first):
    br = jnp.where(pl.program_id(0) < n_first, brp_ref[...], brs_ref[...])
    merged = None
    for n in range(N_BRANCH):
        y = _dot(br[:, n * BRANCH_W:(n + 1) * BRANCH_W], wb_ref[n])
        z = gates_ref[:, n * D_MODEL:(n + 1) * D_MODEL].astype(F32)
        t = (0.5 * jnp.tanh(0.5 * z) + 0.5) * y
        merged = t if merged is None else merged + t
    o_ref[...] = h_ref[...] + _dot(merged.astype(BF16), wo_ref[...])


def merge(h, gates, br_groups, w_branch, w_out, tm):
    m, d = h.shape
    n_first = br_groups[0].shape[0] // tm
    assert br_groups[0].shape[0] % tm == 0 and br_groups[1].shape[0] % tm == 0
    return pl.pallas_call(
        functools.partial(_merge_kernel, n_first=n_first),
        grid=(m // tm,),
        in_specs=[pl.BlockSpec((tm, d), lambda i: (i, 0)),
                  pl.BlockSpec((tm, N_BRANCH * d), lambda i: (i, 0)),
                  *_group_specs(tm, N_BRANCH * BRANCH_W, n_first),
                  pl.BlockSpec((N_BRANCH, BRANCH_W, d), lambda i: (0, 0, 0)),
                  pl.BlockSpec((d, d), lambda i: (0, 0))],
        out_specs=pl.BlockSpec((tm, d), lambda i: (i, 0)),
        out_shape=jax.ShapeDtypeStruct((m, d), F32),
        compiler_params=pltpu.CompilerParams(
            dimension_semantics=("parallel",), vmem_limit_bytes=VMEM_LIMIT),
        name="merge",
    )(h, gates, *br_groups, w_branch, w_out)


def _ffn_kernel(h_ref, g_ref, wg_ref, wu_ref, wd_ref, o_ref, *, tf):
    h = h_ref[...]
    u = _rms(h, g_ref[...]).astype(BF16)
    y = h
    for c in range(wg_ref.shape[1] // tf):
        cols = slice(c * tf, (c + 1) * tf)
        a = jax.nn.silu(_dot(u, wg_ref[:, cols]))
        b = _dot(u, wu_ref[:, cols])
        y = y + _dot((a * b).astype(BF16), wd_ref[cols, :])
    o_ref[...] = y


def ffn(h, g, wg, wu, wd, tm, tf):
    m, d = h.shape
    ff = wg.shape[1]
    assert ff % tf == 0
    resident = lambda a: pl.BlockSpec(a.shape, lambda i: (0, 0), pipeline_mode=pl.Buffered(1))
    return pl.pallas_call(
        functools.partial(_ffn_kernel, tf=tf),
        grid=(m // tm,),
        in_specs=[pl.BlockSpec((tm, d), lambda i: (i, 0)),
                  pl.BlockSpec((1, d), lambda i: (0, 0)),
                  resident(wg), resident(wu), resident(wd)],
        out_specs=pl.BlockSpec((tm, d), lambda i: (i, 0)),
        out_shape=jax.ShapeDtypeStruct((m, d), F32),
        compiler_params=pltpu.CompilerParams(
            dimension_semantics=("parallel",), vmem_limit_bytes=VMEM_LIMIT),
        name="ffn",
    )(h, g.reshape(1, d), wg, wu, wd)


def _router_kernel(h_ref, g_ref, wr_ref, u_ref, w_ref, i_ref):
    u = _rms(h_ref[...], g_ref[...])
    u_ref[...] = u
    logits = _dot3(_split2(u), _split2(wr_ref[...]))
    lane = lax.broadcasted_iota(jnp.int32, logits.shape, 1)
    neg = jnp.float32(-jnp.inf)
    logits = jnp.where(lane < N_EXPERTS, logits, neg)
    m1 = jnp.max(logits, axis=-1, keepdims=True)
    i1 = jnp.min(jnp.where(logits == m1, lane, LANES), axis=-1, keepdims=True)
    rest = jnp.where(lane == i1, neg, logits)
    m2 = jnp.max(rest, axis=-1, keepdims=True)
    i2 = jnp.min(jnp.where(rest == m2, lane, LANES), axis=-1, keepdims=True)
    e = jnp.exp(m2 - m1)
    den = 1.0 + e
    w_ref[...] = jnp.where(lane == 0, 1.0 / den, jnp.where(lane == 1, e / den, 0.0))
    i_ref[...] = jnp.where(lane == 0, i1, jnp.where(lane == 1, i2, 0))


def router(h, g, w_router, tm):
    m, d = h.shape
    wr = jnp.pad(w_router, ((0, 0), (0, LANES - N_EXPERTS)))
    return pl.pallas_call(
        _router_kernel,
        grid=(m // tm,),
        in_specs=[pl.BlockSpec((tm, d), lambda i: (i, 0)),
                  pl.BlockSpec((1, d), lambda i: (0, 0)),
                  pl.BlockSpec((d, LANES), lambda i: (0, 0))],
        out_specs=[pl.BlockSpec((tm, d), lambda i: (i, 0)),
                   pl.BlockSpec((tm, LANES), lambda i: (i, 0)),
                   pl.BlockSpec((tm, LANES), lambda i: (i, 0))],
        out_shape=[jax.ShapeDtypeStruct((m, d), F32),
                   jax.ShapeDtypeStruct((m, LANES), F32),
                   jax.ShapeDtypeStruct((m, LANES), jnp.int32)],
        compiler_params=pltpu.CompilerParams(
            dimension_semantics=("parallel",), vmem_limit_bytes=VMEM_LIMIT),
        name="router",
    )(h, g.reshape(1, d), wr)


def _row_copies(src_hbm, dst_hbm, idx_ref, base, buf, sem, n_rows, gather, wait):
    def body(r, carry):
        row = idx_ref[base + r]
        if gather:
            cp = pltpu.make_async_copy(src_hbm.at[pl.ds(row, 1)], buf.at[pl.ds(r, 1)], sem)
        else:
            cp = pltpu.make_async_copy(buf.at[pl.ds(r, 1)], dst_hbm.at[pl.ds(row, 1)], sem)
        if wait:
            cp.wait()
        else:
            cp.start()
        return carry
    lax.fori_loop(0, n_rows, body, 0, unroll=8)


def _expert_kernel(te_ref, nt_ref, src_ref, dst_ref, u_hbm, wg_ref, wu_ref, wd_ref, y_hbm,
                   xbuf, xb_ref, acc_ref, obuf, gsem, ssem, *, tm):
    i = pl.program_id(0)
    f = pl.program_id(1)
    last_f = pl.num_programs(1) - 1
    nt = nt_ref[0]
    gather = functools.partial(_row_copies, u_hbm, None, src_ref, gather=True, n_rows=tm)
    scatter = functools.partial(_row_copies, None, y_hbm, dst_ref, buf=obuf, sem=ssem.at[0], gather=False, n_rows=tm)

    @pl.when(jnp.logical_and(i < nt, f == 0))
    def _():
        slot = i % 2

        @pl.when(i == 0)
        def _():
            gather(base=0, buf=xbuf.at[0], sem=gsem.at[0], wait=False)
            obuf[...] = jnp.zeros_like(obuf)
            fill = pltpu.make_async_copy(obuf, y_hbm.at[pl.ds(y_hbm.shape[0] - tm, tm)], ssem.at[0])
            fill.start()
            fill.wait()

        gather(base=i * tm, buf=xbuf.at[slot], sem=gsem.at[slot], wait=True)

        @pl.when(i + 1 < nt)
        def _():
            gather(base=(i + 1) * tm, buf=xbuf.at[1 - slot], sem=gsem.at[1 - slot], wait=False)

        xb_ref[...] = xbuf[slot].astype(BF16)

    @pl.when(i < nt)
    def _():
        x = xb_ref[...]
        a = jax.nn.silu(_dot(x, wg_ref[0]))
        b = _dot(x, wu_ref[0])
        y = _dot((a * b).astype(BF16), wd_ref[0])

        @pl.when(f == 0)
        def _():
            acc_ref[...] = y

        @pl.when(f != 0)
        def _():
            acc_ref[...] += y

        @pl.when(f == last_f)
        def _():
            @pl.when(i > 0)
            def _():
                scatter(base=(i - 1) * tm, wait=True)

            obuf[...] = acc_ref[...]
            scatter(base=i * tm, wait=False)

            @pl.when(i == nt - 1)
            def _():
                scatter(base=i * tm, wait=True)


def experts(u, tile_expert, n_tiles, src_tok, dst_row, n_out_rows, wg, wu, wd, tm, tf):
    d = u.shape[1]
    ff = wg.shape[2]
    n_row_tiles = tile_expert.shape[0]
    wmap = lambda i, f, te, nt, src, dst: (te[i], 0, f)
    wbuf = pl.Buffered(1) if tf == ff else None
    grid_spec = pltpu.PrefetchScalarGridSpec(
        num_scalar_prefetch=4,
        grid=(n_row_tiles, ff // tf),
        in_specs=[pl.BlockSpec(memory_space=pl.ANY),
                  pl.BlockSpec((1, d, tf), wmap, pipeline_mode=wbuf),
                  pl.BlockSpec((1, d, tf), wmap, pipeline_mode=wbuf),
                  pl.BlockSpec((1, tf, d), lambda i, f, te, nt, src, dst: (te[i], f, 0), pipeline_mode=wbuf)],
        out_specs=pl.BlockSpec(memory_space=pl.ANY),
        scratch_shapes=[pltpu.VMEM((2, tm, d), F32), pltpu.VMEM((tm, d), BF16), pltpu.VMEM((tm, d), F32),
                        pltpu.VMEM((tm, d), F32), pltpu.SemaphoreType.DMA((2,)), pltpu.SemaphoreType.DMA((1,))],
    )
    return pl.pallas_call(
        functools.partial(_expert_kernel, tm=tm),
        grid_spec=grid_spec,
        out_shape=jax.ShapeDtypeStruct((n_out_rows, d), F32),
        compiler_params=pltpu.CompilerParams(
            dimension_semantics=("arbitrary", "arbitrary"), vmem_limit_bytes=VMEM_LIMIT,
            disable_bounds_checks=True),
        name="experts",
    )(tile_expert, n_tiles, src_tok, dst_row, u, wg, wu, wd)


def moe(h, g, w_router, wg, wu, wd, tm_route, tm_e, tf):
    m, d = h.shape
    u, top_w, top_i = router(h, g, w_router, tm_route)
    n_pairs = TOP_K * m
    flat_e = top_i[:, :TOP_K].reshape(-1)
    onehot = (flat_e[:, None] == jnp.arange(N_EXPERTS, dtype=jnp.int32)[None, :]).astype(jnp.int32)
    rank = jnp.sum((jnp.cumsum(onehot, axis=0) - 1) * onehot, axis=1)
    counts = jnp.sum(onehot, axis=0)
    tiles_per = (counts + tm_e - 1) // tm_e
    tile_end = jnp.cumsum(tiles_per)
    tile_start = tile_end - tiles_per
    grouped_row = tile_start[flat_e] * tm_e + rank
    n_rows = n_pairs + N_EXPERTS * tm_e
    n_row_tiles = n_rows // tm_e
    pair_ids = jnp.arange(n_pairs, dtype=jnp.int32)
    pair_at = jnp.full((n_rows,), -1, jnp.int32).at[grouped_row].set(pair_ids)
    spare = n_pairs + jnp.arange(n_rows, dtype=jnp.int32) % tm_e
    src_tok = jnp.where(pair_at >= 0, pair_at // TOP_K, 0)
    dst_row = jnp.where(pair_at >= 0, (pair_at % TOP_K) * m + pair_at // TOP_K, spare)
    tile_ids = jnp.arange(n_row_tiles, dtype=jnp.int32)
    tile_expert = jnp.minimum(jnp.sum((tile_ids[:, None] >= tile_end[None, :]).astype(jnp.int32), axis=1),
                              N_EXPERTS - 1).astype(jnp.int32)
    n_tiles = tile_end[-1:].astype(jnp.int32)
    last_e = tile_expert[jnp.maximum(n_tiles[0] - 1, 0)]
    tile_expert = jnp.where(tile_ids < n_tiles[0], tile_expert, last_e)
    y = experts(u, tile_expert, n_tiles, src_tok, dst_row, n_pairs + tm_e, wg, wu, wd, tm_e, tf)
    return y, top_w


def _ple_kernel(*refs, final, combine, n_first):
    if combine:
        h_ref, y0_ref, y1_ref, tw_ref, *refs = refs
        h = h_ref[...] + (tw_ref[:, 0:1] * y0_ref[...] + tw_ref[:, 1:2] * y1_ref[...])
    else:
        h_ref, *refs = refs
        h = h_ref[...]
    pp_ref, ps_ref, g_ref, wg_ref, wp_ref, gf_ref, *o_refs = refs
    first = pl.program_id(0) < n_first
    p = jnp.where(first, pp_ref[...], ps_ref[...])
    v = _rms(h, g_ref[...]).astype(BF16)
    pg = jax.nn.sigmoid(_dot(v, wg_ref[...]))
    e = _dot(p.astype(BF16), wp_ref[...])
    out = h + pg * e
    if not final:
        o_refs[0][...] = out
        return
    out = _rms(out, gf_ref[...])

    @pl.when(first)
    def _():
        o_refs[0][...] = out

    @pl.when(jnp.logical_not(first))
    def _():
        o_refs[1][...] = out


def ple(h, p_groups, g, w_gate, w_p, g_final, tm, final, expert_out=None):
    m, d = h.shape
    dp = p_groups[0].shape[1]
    n_first = p_groups[0].shape[0] // tm
    assert p_groups[0].shape[0] % tm == 0 and p_groups[1].shape[0] % tm == 0
    rows = lambda wd: pl.BlockSpec((tm, wd), lambda i: (i, 0))
    whole = lambda a, b: pl.BlockSpec((a, b), lambda i: (0, 0))
    extra, extra_specs = (), []
    if expert_out is not None:
        y, top_w = expert_out
        extra = (y, y, top_w)
        extra_specs = [rows(d), pl.BlockSpec((tm, d), lambda i: (i + m // tm, 0)), rows(LANES)]
    if final:
        out_specs = _group_specs(tm, d, n_first)
        out_shape = [jax.ShapeDtypeStruct((a.shape[0], d), F32) for a in p_groups]
    else:
        out_specs, out_shape = rows(d), jax.ShapeDtypeStruct((m, d), F32)
    return pl.pallas_call(
        functools.partial(_ple_kernel, final=final, combine=expert_out is not None, n_first=n_first),
        grid=(m // tm,),
        in_specs=([rows(d)] + extra_specs + _group_in_specs(p_groups, tm, n_first)
                  + [whole(1, d), whole(d, d), whole(dp, d), whole(1, d)]),
        out_specs=out_specs,
        out_shape=out_shape,
        compiler_params=pltpu.CompilerParams(
            dimension_semantics=("parallel",), vmem_limit_bytes=VMEM_LIMIT),
        name="ple",
    )(h, *extra, *[_operand(p) for p in p_groups], g.reshape(1, d), w_gate, w_p, g_final.reshape(1, d))


STATE_KEYS = ('gdn_conv', 'gdn', 'hgrn', 'ssd_conv', 'ssd', 'ml_c', 'ml_n', 'ml_m')
MIXER_KEYS = ('gdn_conv_w', 'gdn_a_log', 'gdn_dt_bias', 'gdn_norm', 'hgrn_norm',
              'ssd_conv_w', 'ssd_conv_b', 'ssd_a_log', 'ssd_dt_bias', 'ssd_d', 'ssd_norm',
              'ml_ig_b', 'ml_fg_b', 'ml_norm')


MIXER_SPECS = (gdn_spec, hgrn_spec, ssd_spec, mlstm_spec)
_gdn_kernel.n_stages = lambda t_len: 11 + 2 * (t_len.bit_length() - 2)
_hgrn_kernel.n_stages = lambda t_len: 1 + HG_H * (len(_hgrn_levels(t_len)) // 2 + 1)
_ssd_kernel.n_stages = lambda t_len: 8
_mlstm_kernel.n_stages = lambda t_len: 11


def _mixers(proj, row_blk0, st, w, lb, **grid):
    built = [f(st, w, lb, grid['n_seg']) for f in MIXER_SPECS]
    o, *new_states = _mixer_call([b[0] for b in built], proj, row_blk0, **grid)
    new, at = {}, 0
    for spec, finish in built:
        n = len(spec[2])
        new.update(finish(*new_states[at:at + n]))
        at += n
    return o, new


def kernel(x_prompt, x_sample, state_gdn_conv, state_gdn, state_hgrn, state_ssd_conv, state_ssd, state_mlstm_c, state_mlstm_n, state_mlstm_m, p_prompt, p_sample, g_mix, w_in, gdn_conv_w, gdn_a_log, gdn_dt_bias, gdn_norm, hgrn_lb, hgrn_norm, ssd_conv_w, ssd_conv_b, ssd_a_log, ssd_dt_bias, ssd_d, ssd_norm, ml_ig_b, ml_fg_b, ml_norm, w_branch, w_out, g_ffn, w_ff_gate, w_ff_up, w_ff_down, w_router, w_ex_gate, w_ex_up, w_ex_down, w_ple, w_ple_gate, g_ple, g_final):
    prm = {'gdn_conv_w': gdn_conv_w, 'gdn_a_log': gdn_a_log, 'gdn_dt_bias': gdn_dt_bias,
           'gdn_norm': gdn_norm, 'hgrn_norm': hgrn_norm, 'ssd_conv_w': ssd_conv_w,
           'ssd_conv_b': ssd_conv_b, 'ssd_a_log': ssd_a_log, 'ssd_dt_bias': ssd_dt_bias,
           'ssd_d': ssd_d, 'ssd_norm': ssd_norm, 'ml_ig_b': ml_ig_b, 'ml_fg_b': ml_fg_b,
           'ml_norm': ml_norm}
    bp, lp, d = x_prompt.shape
    bs, ls, _ = x_sample.shape
    mp, ms = bp * lp, bs * ls
    assert lp % CH == 0 and CH % ls == 0 and ms % CH == 0 and mp % CH == 0
    st_s = {'gdn_conv': state_gdn_conv, 'gdn': state_gdn, 'hgrn': state_hgrn, 'ssd_conv': state_ssd_conv,
            'ssd': state_ssd, 'ml_c': state_mlstm_c, 'ml_n': state_mlstm_n, 'ml_m': state_mlstm_m}
    st_p = {k: jnp.zeros((bp,) + v.shape[2:], F32) for k, v in st_s.items()}
    grid_p = dict(n_outer=bp, n_chunks=lp // CH, n_seg=1)
    grid_s = dict(n_outer=ms // CH, n_chunks=1, n_seg=CH // ls)

    sm = jax.nn.softmax(hgrn_lb, axis=0)
    lb_all = jnp.cumsum(sm, axis=0) - sm[0]

    h = jnp.concatenate([x_prompt.reshape(mp, d), x_sample.reshape(ms, d)], axis=0)
    new_p = {k: [] for k in STATE_KEYS}
    new_s = {k: [] for k in STATE_KEYS}
    for l in range(DEPTH):
        wl = {k: prm[k][l] for k in MIXER_KEYS}
        gates, mix = norm_matmul(h, g_mix[l], _permute_w_in(w_in[l]), tm=512, tn=1024)
        br_p, np_ = _mixers(mix, 0, st_p, wl, lb_all[l], **grid_p)
        st_l = {k: (v[l] if k in ('ml_n', 'ml_m') else LayerOf(v, l)) for k, v in st_s.items()}
        br_s, ns_ = _mixers(mix, mp // CH, st_l, wl, lb_all[l], **grid_s)
        h = merge(h, gates, (br_p, br_s), w_branch[l].astype(BF16), w_out[l].astype(BF16), tm=512)
        j = l // 2
        expert_out = None
        if l % 2 == 0:
            h = ffn(h, g_ffn[l], w_ff_gate[j].astype(BF16), w_ff_up[j].astype(BF16),
                    w_ff_down[j].astype(BF16), tm=512, tf=D_FF)
        else:
            expert_out = moe(h, g_ffn[l], w_router[j], w_ex_gate[j].astype(BF16), w_ex_up[j].astype(BF16),
                             w_ex_down[j].astype(BF16), tm_route=512, tm_e=512, tf=D_FF)
        p_groups = (LayerOf(p_prompt.reshape(DEPTH, mp, D_PLE), l), LayerOf(p_sample.reshape(DEPTH, ms, D_PLE), l))
        h = ple(h, p_groups, g_ple[l], w_ple_gate[l].astype(BF16), w_ple[l].astype(BF16), g_final,
                tm=512, final=(l == DEPTH - 1), expert_out=expert_out)
        for k in STATE_KEYS:
            new_p[k].append(np_[k])
            new_s[k].append(ns_[k])
    y_prompt = h[0].reshape(bp, lp, d)
    y_sample = h[1].reshape(bs, ls, d)
    sp = {k: jnp.stack(v) for k, v in new_p.items()}
    ss = {k: jnp.stack(v) for k, v in new_s.items()}
    return (y_prompt, y_sample,
            sp['gdn_conv'], sp['gdn'], sp['hgrn'], sp['ssd_conv'], sp['ssd'], sp['ml_c'], sp['ml_n'], sp['ml_m'],
            ss['gdn_conv'], ss['gdn'], ss['hgrn'], ss['ssd_conv'], ss['ssd'], ss['ml_c'], ss['ml_n'], ss['ml_m'])
```

```python
import functools

import numpy as np
import jax
import jax.numpy as jnp
from jax import lax
from jax.experimental import pallas as pl
from jax.experimental.pallas import tpu as pltpu

F32 = jnp.float32
BF16 = jnp.bfloat16
HI = lax.Precision.HIGHEST

D_MODEL = 1024
DEPTH = 2
D_PLE = 256
N_BRANCH = 4
BRANCH_W = D_MODEL // N_BRANCH
CONV_W = 4
EPS = 1e-6

GDN_H = 4
GDN_DK = 64
GDN_QK = GDN_H * GDN_DK
GDN_CONV_CH = 2 * GDN_QK + BRANCH_W
HG_H = 4
HG_QK = 256
SSD_H = 4
SSD_G = 2
SSD_N = 128
SSD_CONV_CH = BRANCH_W + 2 * SSD_G * SSD_N
ML_H = 4
ML_DK = 32
ML_QK = ML_H * ML_DK
D_FF = ((8 * D_MODEL // 3 + 255) // 256) * 256
N_EXPERTS = 8
TOP_K = 2

_REF_SPLITS = (('gdn_in', GDN_CONV_CH), ('gdn_b', GDN_H), ('gdn_a', GDN_H), ('gdn_z', BRANCH_W),
               ('hg_q', HG_QK), ('hg_f', HG_QK), ('hg_v', BRANCH_W), ('hg_g', BRANCH_W),
               ('ssd_z', BRANCH_W), ('ssd_in', SSD_CONV_CH), ('ssd_dt', SSD_H),
               ('ml_q', ML_QK), ('ml_k', ML_QK), ('ml_v', BRANCH_W), ('ml_i', ML_H), ('ml_f', ML_H),
               ('ml_o', BRANCH_W), ('gates', N_BRANCH * D_MODEL))
_MY_ORDER = ('gdn_in', 'ssd_in', 'gdn_z', 'ssd_z', 'hg_q', 'hg_f', 'hg_v', 'hg_g',
             'ml_v', 'ml_o', 'ml_q', 'ml_k', 'gdn_b', 'gdn_a', 'ssd_dt', 'ml_i', 'ml_f')
LANES = 128
MXU_TILE = 256
CH = 64
N_GATES = N_BRANCH * D_MODEL


def _layout():
    widths = dict(_REF_SPLITS)
    off, out = 0, {}
    for name in _MY_ORDER:
        out[name] = (off, widths[name])
        off += widths[name]
    return out, -(-off // MXU_TILE) * MXU_TILE


COLS, N_MIX = _layout()
GATE_COL0 = COLS['gdn_b'][0]
L_GDN_B, L_GDN_A, L_SSD_DT, L_ML_I, L_ML_F = (COLS[n][0] - GATE_COL0 for n in ('gdn_b', 'gdn_a', 'ssd_dt', 'ml_i', 'ml_f'))
VMEM_LIMIT = 56 * 1024 * 1024
ROW_TILE = 512
PROJ_COL_TILE = 4 * MXU_TILE


def _ref_offsets():
    off, acc = {}, 0
    for name, wd in _REF_SPLITS:
        off[name] = acc
        acc += wd
    return off, acc


def _permute_kernel(w_ref, o_ref):
    ref_off, _ = _ref_offsets()
    tk = o_ref.shape[0]
    o_ref[:, :N_GATES] = w_ref[:, ref_off['gates']:ref_off['gates'] + N_GATES].astype(BF16)
    lane = lax.broadcasted_iota(jnp.int32, (tk, LANES), 1)
    gate_group = jnp.zeros((tk, LANES), F32)
    for name in _MY_ORDER:
        dst, wd = COLS[name]
        src = ref_off[name]
        if wd >= LANES:
            o_ref[:, N_GATES + dst:N_GATES + dst + wd] = w_ref[:, src:src + wd].astype(BF16)
        else:
            win0, at = src // LANES * LANES, src % LANES
            to = dst - GATE_COL0
            assert at + wd <= LANES and to + wd <= LANES
            window = pltpu.roll(w_ref[:, win0:win0 + LANES], (to - at) % LANES, axis=1)
            gate_group = jnp.where(jnp.logical_and(lane >= to, lane < to + wd), window, gate_group)
    o_ref[:, N_GATES + GATE_COL0:N_GATES + GATE_COL0 + LANES] = gate_group.astype(BF16)
    pad0 = N_GATES + GATE_COL0 + LANES
    o_ref[:, pad0:] = jnp.zeros((tk, N_GATES + N_MIX - pad0), BF16)


def _permute_w_in(w, tk=256):
    d, n_in = w.shape
    return pl.pallas_call(
        _permute_kernel, grid=(d // tk,),
        in_specs=[pl.BlockSpec((tk, n_in), lambda i: (i, 0))],
        out_specs=pl.BlockSpec((tk, N_GATES + N_MIX), lambda i: (i, 0)),
        out_shape=jax.ShapeDtypeStruct((d, N_GATES + N_MIX), BF16),
        compiler_params=pltpu.CompilerParams(dimension_semantics=("parallel",), vmem_limit_bytes=VMEM_LIMIT),
        name="permute_w_in",
    )(w)


def _rms(x, g):
    return x * lax.rsqrt(jnp.mean(x * x, axis=-1, keepdims=True) + EPS) * g


def _dot(a, b, **kw):
    return jnp.dot(a, b, preferred_element_type=F32, **kw)


def _dot_nt(a, b, **kw):
    return lax.dot_general(a, b, (((1,), (1,)), ((), ())), preferred_element_type=F32, **kw)


def _dot_tn(a, b, **kw):
    return lax.dot_general(a, b, (((0,), (0,)), ((), ())), preferred_element_type=F32, **kw)


def _rows_of(refs, n_first):
    if n_first is None:
        return refs[0][...]
    return jnp.where(pl.program_id(0) < n_first, refs[0][...], refs[1][...])


def _row_inputs(x, tm):
    if isinstance(x, tuple):
        assert x[0].shape[0] % tm == 0 and x[1].shape[0] % tm == 0
        n_first = x[0].shape[0] // tm
        return _group_in_specs(x, tm, n_first), list(x), n_first
    return [pl.BlockSpec((tm, x.shape[1]), lambda i: (i, 0))], [x], None


def _norm_matmul_kernel(*refs, tn, n_first):
    *x_refs, g_ref, w_ref, gates_ref, mix_ref = refs
    xn = _rms(_rows_of(x_refs, n_first), g_ref[...]).astype(BF16)
    for j in range(N_GATES // tn):
        gates_ref[:, j * tn:(j + 1) * tn] = _dot(xn, w_ref[:, j * tn:(j + 1) * tn]).astype(BF16)
    for j in range(N_MIX // tn):
        mix_ref[:, j * tn:(j + 1) * tn] = _dot(xn, w_ref[:, N_GATES + j * tn:N_GATES + (j + 1) * tn])


def norm_matmul(x, g, w, tm, tn):
    x_specs, x_ops, n_first = _row_inputs(x, tm)
    m, d = sum(a.shape[0] for a in x_ops), x_ops[0].shape[1]
    assert w.shape[1] == N_GATES + N_MIX and N_GATES % tn == 0 and N_MIX % tn == 0
    return pl.pallas_call(
        functools.partial(_norm_matmul_kernel, tn=tn, n_first=n_first),
        grid=(m // tm,),
        in_specs=[*x_specs,
                  pl.BlockSpec((1, d), lambda i: (0, 0)),
                  pl.BlockSpec((d, N_GATES + N_MIX), lambda i: (0, 0), pipeline_mode=pl.Buffered(1))],
        out_specs=[pl.BlockSpec((tm, N_GATES), lambda i: (i, 0)),
                   pl.BlockSpec((tm, N_MIX), lambda i: (i, 0))],
        out_shape=[jax.ShapeDtypeStruct((m, N_GATES), BF16), jax.ShapeDtypeStruct((m, N_MIX), F32)],
        compiler_params=pltpu.CompilerParams(
            dimension_semantics=("parallel",), vmem_limit_bytes=VMEM_LIMIT),
        name="norm_matmul",
    )(*x_ops, g.reshape(1, d), w)


def _seg_masks(t_len):
    row = lax.broadcasted_iota(jnp.int32, (CH, CH), 0)
    col = lax.broadcasted_iota(jnp.int32, (CH, CH), 1)
    same = (row // t_len) == (col // t_len)
    tri = jnp.logical_and(same, col <= row)
    strict = jnp.logical_and(same, col < row)
    return same, tri, strict


def _row_forms(x, n_rows=24):
    r = lax.broadcasted_iota(jnp.int32, (n_rows, LANES), 0)
    l = lax.broadcasted_iota(jnp.int32, (n_rows, LANES), 1)
    return _dot_nt((r == l).astype(F32), x, precision=HI)


def _softplus(x):
    return jnp.maximum(x, 0.0) + jnp.log1p(jnp.exp(-jnp.abs(x)))


def _split2(x):
    hi = x.astype(BF16)
    return hi, (x - hi.astype(F32)).astype(BF16)


def _dot3(a, b):
    return _dot(a[0], b[0]) + (_dot(a[0], b[1]) + _dot(a[1], b[0]))


def _masked_exp(d, mask):
    return jnp.where(mask, jnp.exp(jnp.where(mask, d, 0.0)), 0.0)


def _conv_silu(x, ext_scr, cw_ref, bias, t_len, n_seg):
    w = x.shape[-1]
    ext_scr[:, 8:8 + t_len, :] = x.reshape(n_seg, t_len, w)
    y = cw_ref[3:4, :] * x
    for j in range(1, CONV_W):
        y = y + cw_ref[3 - j:4 - j, :] * ext_scr[:, 8 - j:8 - j + t_len, :].reshape(CH, w)
    if bias is not None:
        y = y + bias
    tail = ext_scr[:, 5 + t_len:8 + t_len, :]
    ext_scr[:, 5:8, :] = tail
    return jax.nn.silu(y), tail


def _halves(xp, lo):
    s_lo = jnp.sum(jnp.where(lo, xp, 0.0), axis=-1, keepdims=True)
    s_hi = jnp.sum(jnp.where(lo, 0.0, xp), axis=-1, keepdims=True)
    return jnp.where(lo, s_lo, s_hi)


def _head_rmsnorm(xp, lo):
    return xp * lax.rsqrt(_halves(xp * xp, lo) * (1.0 / 64) + EPS)


def _head_l2norm(xp, lo):
    return xp * lax.rsqrt(_halves(xp * xp, lo) + EPS)


def _seg_sum(parts, rowi, t_len):
    if len(parts) == 1:
        return parts[0]
    acc = jnp.where(rowi // t_len == 0, parts[0], 0.0)
    for s in range(1, len(parts)):
        acc = acc + jnp.where(rowi // t_len == s, parts[s], 0.0)
    return acc


def _seg_rows(x, rowi, t_len, s, n_seg):
    return x if n_seg == 1 else jnp.where(rowi // t_len == s, x, 0.0)


def _quarter_sel(idx, width, vals):
    out = vals[3]
    for h in (2, 1, 0):
        out = jnp.where(idx < (h + 1) * width, vals[h], out)
    return out


def _gate_rows(pairs):
    t = jnp.zeros((8, LANES), F32)
    for r, (off, v) in enumerate(pairs):
        t = t.at[r, off:off + v.shape[0]].set(v.astype(F32))
    return t


def _block_diag(blocks):
    n = len(blocks)
    z = jnp.zeros_like(blocks[0])
    return jnp.concatenate(
        [jnp.concatenate([blocks[i] if i == j else z for j in range(n)], axis=1) for i in range(n)], axis=0)


def _transpose64(x):
    r = lax.broadcasted_iota(jnp.int32, x.shape, 0)
    c = lax.broadcasted_iota(jnp.int32, x.shape, 1)
    eye = (r == c).astype(BF16)
    hi, mid, lo = _split3(x)
    return (_dot_nt(eye, hi) + _dot_nt(eye, mid)) + _dot_nt(eye, lo)


def _load_pair_states(s0_ref, s_scr, n_seg, transpose):
    prep = _transpose64 if transpose else (lambda t: t)
    for s in range(n_seg):
        for p in range(2):
            s_scr[s, p] = _block_diag([prep(s0_ref[s, 2 * p]), prep(s0_ref[s, 2 * p + 1])])


def _store_pair_states(s_scr, sn_ref, n_seg, transpose):
    prep = _transpose64 if transpose else (lambda t: t)
    for s in range(n_seg):
        for p in range(2):
            sn_ref[s, 2 * p] = prep(s_scr[s, p, 0:64, 0:64])
            sn_ref[s, 2 * p + 1] = prep(s_scr[s, p, 64:128, 64:128])


class LayerOf:
    def __init__(self, stack, layer):
        self.stack, self.layer = stack, layer
        self.shape, self.ndim = stack.shape[1:], stack.ndim - 1


def _operand(a):
    return a.stack if isinstance(a, LayerOf) else a


def _row_block_spec(a, block_rows, index_of_step):
    rest = (0,) * (a.ndim - 1)
    if isinstance(a, LayerOf):
        return pl.BlockSpec((None, block_rows) + a.shape[1:], lambda *ids: (a.layer, index_of_step(*ids)) + rest)
    return pl.BlockSpec((block_rows,) + a.shape[1:], lambda *ids: (index_of_step(*ids),) + rest)


def _interleave(gens, n_stages):
    results = [None] * len(gens)
    pos = [0] * len(gens)
    live = set(range(len(gens)))
    while live:
        k = min(live, key=lambda i: ((pos[i] + 1) / n_stages[i], i))
        try:
            next(gens[k])
            pos[k] += 1
        except StopIteration as stop:
            results[k] = stop.value
            live.remove(k)
    return results


def _fused_mixer_kernel(*refs, n_seg, parts):
    tot = [sum(p[j] for p in parts) for j in range(1, 5)]
    ins, rest = refs[:tot[0]], refs[tot[0]:]
    sts, rest = rest[:tot[1]], rest[tot[1]:]
    prs, rest = rest[:tot[2]], rest[tot[2]:]
    o_ref, rest = rest[0], rest[1:]
    outs, scr = rest[:tot[1]], rest[tot[1]:]
    at = [0, 0, 0, 0]
    calls = []
    for k, (body, n_in, n_st, n_pr, n_scr) in enumerate(parts):
        take = lambda seq, j, n: seq[at[j]:at[j] + n]
        calls.append(functools.partial(
            body, *take(ins, 0, n_in), *take(sts, 1, n_st), *take(prs, 2, n_pr),
            o_ref.at[:, k * BRANCH_W:(k + 1) * BRANCH_W],
            *take(outs, 1, n_st), *take(scr, 3, n_scr), n_seg=n_seg))
        for j, n in enumerate((n_in, n_st, n_pr, n_scr)):
            at[j] += n

    n_stages = [p[0].n_stages(CH // n_seg) for p in parts]

    @pl.when(pl.program_id(1) == 0)
    def _():
        _interleave([call(init=True) for call in calls], n_stages)

    finishers = _interleave([call(init=False) for call in calls], n_stages)

    @pl.when(pl.program_id(1) == pl.num_programs(1) - 1)
    def _():
        for fin in finishers:
            fin()


def _mixer_call(specs, proj, row_blk0, *, n_outer, n_chunks, n_seg):
    rows = n_outer * n_chunks * CH
    rmap = lambda blk: (lambda i, c: (row_blk0 + i * n_chunks + c, blk))
    full = lambda a: pl.BlockSpec(a.shape, lambda i, c: (0,) * a.ndim)
    sblk = lambda a: _row_block_spec(a, a.shape[0] // n_outer, lambda i, c: i)
    oblk = lambda a: pl.BlockSpec((a.shape[0] // n_outer,) + a.shape[1:], lambda i, c: (i,) + (0,) * (a.ndim - 1))
    in_blocks = [b for s in specs for b in s[1]]
    state_ins = [a for s in specs for a in s[2]]
    params = [a for s in specs for a in s[3]]
    scratch = [a for s in specs for a in s[4]]
    for off, wd in in_blocks:
        assert off % wd == 0
    parts = tuple((s[0], len(s[1]), len(s[2]), len(s[3]), len(s[4])) for s in specs)
    width = len(specs) * BRANCH_W
    return pl.pallas_call(
        functools.partial(_fused_mixer_kernel, n_seg=n_seg, parts=parts),
        grid=(n_outer, n_chunks),
        in_specs=([pl.BlockSpec((CH, wd), rmap(off // wd)) for off, wd in in_blocks]
                  + [sblk(a) for a in state_ins] + [full(a) for a in params]),
        out_specs=[pl.BlockSpec((CH, width), lambda i, c: (i * n_chunks + c, 0))] + [oblk(a) for a in state_ins],
        out_shape=([jax.ShapeDtypeStruct((rows, width), BF16)]
                   + [jax.ShapeDtypeStruct(a.shape, F32) for a in state_ins]),
        scratch_shapes=scratch,
        compiler_params=pltpu.CompilerParams(
            dimension_semantics=("parallel", "arbitrary"), vmem_limit_bytes=VMEM_LIMIT),
        name="token_mixers",
    )(*([proj] * len(in_blocks)), *[_operand(a) for a in state_ins], *params)


def _gdn_kernel(xin_ref, z_ref, sm_ref, conv0_ref, s0_ref, cw_ref, gp_ref, ng_ref,
                o_ref, convn_ref, sn_ref, ext_scr, s_scr, *, n_seg, init):
    t_len = CH // n_seg
    if init:
        _load_pair_states(s0_ref, s_scr, n_seg, transpose=False)
        ext_scr[:, 5:8, :] = conv0_ref[...]
        return None

    xc, tail = _conv_silu(xin_ref[...], ext_scr, cw_ref, None, t_len, n_seg)
    yield
    same, tri, strict = _seg_masks(t_len)
    lane = lax.broadcasted_iota(jnp.int32, (CH, LANES), 1)
    rowi = lax.broadcasted_iota(jnp.int32, (CH, 1), 0)
    lo = lane < 64
    sm = sm_ref[...]
    beta = jax.nn.sigmoid(sm)
    gl = jnp.logical_and(lane >= L_GDN_A, lane < L_GDN_A + GDN_H)
    g = jnp.where(gl, -jnp.exp(gp_ref[0:1, :]) * _softplus(sm + gp_ref[1:2, :]), 0.0)
    gam = _dot(tri.astype(F32), g, precision=HI)
    gam_end = _dot(same.astype(F32), g, precision=HI)
    gam_r = _row_forms(gam, 8)
    yield
    r128 = lax.broadcasted_iota(jnp.int32, (LANES, LANES), 0)
    l128 = lax.broadcasted_iota(jnp.int32, (LANES, LANES), 1)
    bd = (r128 < 64) == (l128 < 64)
    rsel = lax.broadcasted_iota(jnp.int32, (LANES, 1), 0) < 64
    qs_, ks_, atts_, gcs_, a_, x_ = [], [], [], [], [], []
    for p in range(2):
        q_p = _head_l2norm(xc[:, 128 * p:128 * (p + 1)], lo) * (GDN_DK ** -0.5)
        k_p = _head_l2norm(xc[:, GDN_QK + 128 * p:GDN_QK + 128 * (p + 1)], lo)
        v_p = xc[:, 2 * GDN_QK + 128 * p:2 * GDN_QK + 128 * (p + 1)]
        kb = k_p.astype(BF16)
        qs_.append(q_p)
        ks_.append(k_p)
        for j in range(2):
            h = 2 * p + j
            mj = lo if j == 0 else jnp.logical_not(lo)
            kk = _dot_nt(jnp.where(mj, k_p, 0.0).astype(BF16), kb)
            qk = _dot_nt(jnp.where(mj, q_p, 0.0).astype(BF16), kb)
            gc = gam[:, L_GDN_A + h:L_GDN_A + h + 1]
            bc = beta[:, L_GDN_B + h:L_GDN_B + h + 1]
            dec = _masked_exp(gc - gam_r[L_GDN_A + h:L_GDN_A + h + 1, :], tri)
            a_.append(jnp.where(strict, bc * kk * dec, 0.0))
            x_.append(jnp.concatenate([jnp.where(mj, bc * v_p, 0.0),
                                       jnp.where(mj, (bc * jnp.exp(gc)) * k_p, 0.0)], axis=-1))
            atts_.append(qk * dec)
            gcs_.append(gc)
            yield
    eye = jnp.logical_and(tri, jnp.logical_not(strict)).astype(F32)
    sa = [_split2(a) for a in a_]
    t_ = [eye - a for a in a_]
    n = 2
    while n < t_len:
        sa = [_split2(_dot3(s, s)) for s in sa]
        yield
        t_ = [t + _dot3(_split2(t), s) for t, s in zip(t_, sa)]
        yield
        n *= 2
    x_ = [_dot3(_split2(t), _split2(x)) for t, x in zip(t_, x_)]
    yield
    outs = []
    for p in range(2):
        q_p, k_p = qs_[p], ks_[p]
        solv = x_[2 * p][:, :LANES] + x_[2 * p + 1][:, :LANES]
        solk = x_[2 * p][:, LANES:] + x_[2 * p + 1][:, LANES:]
        atts, gcs = atts_[2 * p:2 * p + 2], gcs_[2 * p:2 * p + 2]
        solk_b = solk.astype(BF16)
        qb = q_p.astype(BF16)
        u = solv - _seg_sum([_dot(solk_b, s_scr[s, p].astype(BF16)) for s in range(n_seg)], rowi, t_len)
        qs = _seg_sum([_dot(qb, s_scr[s, p].astype(BF16)) for s in range(n_seg)], rowi, t_len)
        o = jnp.where(lo, jnp.exp(gcs[0]), jnp.exp(gcs[1])) * qs
        for j in range(2):
            mj = lo if j == 0 else jnp.logical_not(lo)
            o = o + _dot(atts[j].astype(BF16), jnp.where(mj, u, 0.0).astype(BF16))
        yield
        ge0 = gam_end[:, L_GDN_A + 2 * p:L_GDN_A + 2 * p + 1]
        ge1 = gam_end[:, L_GDN_A + 2 * p + 1:L_GDN_A + 2 * p + 2]
        kw = k_p * jnp.where(lo, jnp.exp(ge0 - gcs[0]), jnp.exp(ge1 - gcs[1]))
        ub = u.astype(BF16)
        for s in range(n_seg):
            r0 = s * t_len
            dec_s = jnp.where(rsel, jnp.exp(ge0[r0:r0 + 1, :]), jnp.exp(ge1[r0:r0 + 1, :]))
            upd = _dot_tn(_seg_rows(kw, rowi, t_len, s, n_seg).astype(BF16), ub)
            s_scr[s, p] = dec_s * s_scr[s, p] + jnp.where(bd, upd, 0.0)
        outs.append(_head_rmsnorm(o, lo))
        yield
    o_all = jnp.concatenate(outs, axis=-1) * ng_ref[...] * jax.nn.silu(z_ref[...])
    o_ref[...] = o_all.astype(BF16)

    def finish():
        convn_ref[...] = tail
        _store_pair_states(s_scr, sn_ref, n_seg, transpose=False)
    return finish


def gdn_spec(st, w, lb, n_seg):
    t_len = CH // n_seg
    gp = _gate_rows([(L_GDN_A, w['gdn_a_log']), (L_GDN_A, w['gdn_dt_bias'])])
    ng = jnp.tile(w['gdn_norm'], GDN_H).reshape(1, BRANCH_W)
    spec = (_gdn_kernel, [COLS['gdn_in'], COLS['gdn_z'], (GATE_COL0, LANES)],
            [st['gdn_conv'], st['gdn']], [w['gdn_conv_w'], gp, ng],
            [pltpu.VMEM((n_seg, 8 + t_len, GDN_CONV_CH), F32), pltpu.VMEM((n_seg, 2, 128, 128), F32)])
    return spec, lambda convn, sn: {'gdn_conv': convn, 'gdn': sn}


def _hgrn_levels(t_len):
    lv, n = [], t_len
    while n >= 2:
        lv.append(n)
        n //= 2
    return lv


def _hgrn_cmat(t_len):
    t = np.arange(CH)[:, None]
    j = np.arange(CH)[None, :]
    same = (t // t_len) == (j // t_len)
    mats = [same & (j <= t), same]
    for n in _hgrn_levels(t_len):
        mid = (t // n) * n + n // 2
        second = t % n >= n // 2
        mats.append((second & (j >= mid) & (j <= t)) | (~second & (j > t) & (j <= mid - 1)))
    return jnp.asarray(np.concatenate(mats, axis=0).astype(np.float32), dtype=BF16)


def _split3(x):
    hi = x.astype(BF16)
    r = x - hi.astype(F32)
    mid = r.astype(BF16)
    return hi, mid, (r - mid.astype(F32)).astype(BF16)


def _hgrn_kernel(x_ref, s0_ref, cm_ref, lb_ref, ng_ref, o_ref, sn_ref, s_scr, *, n_seg, init):
    t_len = CH // n_seg
    levels = _hgrn_levels(t_len)
    if init:
        _load_pair_states(s0_ref, s_scr, n_seg, transpose=True)
        return None

    lane = lax.broadcasted_iota(jnp.int32, (CH, LANES), 1)
    rowi = lax.broadcasted_iota(jnp.int32, (CH, 1), 0)
    row = lax.broadcasted_iota(jnp.int32, (CH, CH), 0)
    col = lax.broadcasted_iota(jnp.int32, (CH, CH), 1)
    lo = lane < 64
    r128 = lax.broadcasted_iota(jnp.int32, (LANES, LANES), 0)
    l128 = lax.broadcasted_iota(jnp.int32, (LANES, LANES), 1)
    bd = (r128 < 64) == (l128 < 64)

    lb = lb_ref[...]
    f_pre = x_ref[:, HG_QK:2 * HG_QK]
    log_f = jnp.log(lb + (1.0 - lb) * jax.nn.sigmoid(f_pre))
    k_in = (1.0 - lb) * jax.nn.sigmoid(-f_pre)
    cm = cm_ref[...]
    ex = None
    for part in _split3(log_f):
        t = _dot(cm, part)
        ex = t if ex is None else ex + t
    b = ex[0:CH]
    b_end = ex[CH:2 * CH]
    yield
    outs = []
    for p in range(2):
        ls = slice(128 * p, 128 * (p + 1))
        q_p = x_ref[:, ls]
        k_p = k_in[:, ls]
        v_p = x_ref[:, 2 * HG_QK + 128 * p:2 * HG_QK + 128 * (p + 1)]
        qk = q_p * k_p
        qe = (q_p * jnp.exp(b[:, ls])).astype(BF16)
        o = _seg_sum([_dot_nt(qe, s_scr[s, p].astype(BF16)) for s in range(n_seg)], rowi, t_len)
        scales = [jnp.exp(ex[(2 + li) * CH:(3 + li) * CH, ls]) for li in range(len(levels))]
        for j in range(2):
            mj = lo if j == 0 else jnp.logical_not(lo)
            diag = jnp.sum(jnp.where(mj, qk, 0.0), axis=-1, keepdims=True)
            att = jnp.where(row == col, diag, 0.0)
            for li, n in enumerate(levels):
                tq = (rowi % n) >= (n // 2)
                qt = jnp.where(jnp.logical_and(mj, tq), q_p * scales[li], 0.0)
                kt = jnp.where(tq, 0.0, k_p * scales[li])
                att = att + jnp.where((row // n) == (col // n), _dot_nt(qt.astype(BF16), kt.astype(BF16)), 0.0)
                if li % 2 == 1:
                    yield
            o = o + _dot(att.astype(BF16), jnp.where(mj, v_p, 0.0).astype(BF16))
            yield
        kw = (k_p * jnp.exp(b_end[:, ls] - b[:, ls])).astype(BF16)
        for s in range(n_seg):
            r0 = s * t_len
            upd = _dot_tn(_seg_rows(v_p, rowi, t_len, s, n_seg).astype(BF16), kw)
            s_scr[s, p] = jnp.exp(b_end[r0:r0 + 1, ls]) * s_scr[s, p] + jnp.where(bd, upd, 0.0)
        outs.append(_head_rmsnorm(o, lo))
    o_all = jnp.concatenate(outs, axis=-1) * ng_ref[...] * jax.nn.silu(x_ref[:, 3 * HG_QK:4 * HG_QK])
    o_ref[...] = o_all.astype(BF16)

    def finish():
        _store_pair_states(s_scr, sn_ref, n_seg, transpose=True)
    return finish


def hgrn_spec(st, w, lb, n_seg):
    t_len = CH // n_seg
    ng = jnp.tile(w['hgrn_norm'], HG_H).reshape(1, BRANCH_W)
    assert COLS['hg_f'][0] == COLS['hg_q'][0] + HG_QK and COLS['hg_g'][0] == COLS['hg_q'][0] + 3 * HG_QK
    spec = (_hgrn_kernel, [(COLS['hg_q'][0], 4 * HG_QK)], [st['hgrn']],
            [_hgrn_cmat(t_len), lb.reshape(1, HG_QK), ng], [pltpu.VMEM((n_seg, 2, 128, 128), F32)])
    return spec, lambda sn: {'hgrn': sn}


def _ssd_kernel(xin_ref, z_ref, sm_ref, conv0_ref, h0_ref, cw_ref, cb_ref, gp_ref, dvec_ref, ng_ref,
                o_ref, convn_ref, hn_ref, ext_scr, h_scr, *, n_seg, init):
    t_len = CH // n_seg
    if init:
        h_scr[...] = h0_ref[...].reshape(n_seg, 2, 128, SSD_N)
        ext_scr[:, 5:8, :] = conv0_ref[...]
        return None

    xc, tail = _conv_silu(xin_ref[...], ext_scr, cw_ref, cb_ref[...], t_len, n_seg)
    yield
    sx, bm, cm = xc[:, :256], xc[:, 256:512], xc[:, 512:768]
    same, tri, _ = _seg_masks(t_len)
    lane = lax.broadcasted_iota(jnp.int32, (CH, LANES), 1)
    rowi = lax.broadcasted_iota(jnp.int32, (CH, 1), 0)
    gl = jnp.logical_and(lane >= L_SSD_DT, lane < L_SSD_DT + SSD_H)
    dt = jnp.where(gl, _softplus(sm_ref[...] + gp_ref[1:2, :]), 0.0)
    da = -jnp.exp(gp_ref[0:1, :]) * dt
    cum = _dot(tri.astype(F32), da, precision=HI)
    cum_end = _dot(same.astype(F32), da, precision=HI)
    cum_r = _row_forms(cum, 16)
    dt_r = _row_forms(dt, 16)
    yield
    lo = lane < 64
    rsel = lax.broadcasted_iota(jnp.int32, (LANES, 1), 0) < 64
    ys = []
    for g in range(SSD_G):
        cg = cm[:, 128 * g:128 * (g + 1)].astype(BF16)
        bg = bm[:, 128 * g:128 * (g + 1)].astype(BF16)
        sxp = sx[:, 128 * g:128 * (g + 1)]
        cb = _dot_nt(cg, bg)
        yst = _seg_sum([_dot_nt(cg, h_scr[s, g].astype(BF16)) for s in range(n_seg)], rowi, t_len)
        yatt = jnp.zeros((CH, LANES), F32)
        cols = []
        for j in range(2):
            l = L_SSD_DT + 2 * g + j
            cc = cum[:, l:l + 1]
            dec = _masked_exp(cc - cum_r[l:l + 1, :], tri)
            att = cb * dec * dt_r[l:l + 1, :]
            xm = jnp.where(lo if j == 0 else jnp.logical_not(lo), sxp, 0.0)
            yatt = yatt + _dot(att.astype(BF16), xm.astype(BF16))
            cols.append((cc, dt[:, l:l + 1] * jnp.exp(cum_end[:, l:l + 1] - cc)))
            yield
        ys.append(jnp.where(lo, jnp.exp(cols[0][0]), jnp.exp(cols[1][0])) * yst + yatt)
        xw = sxp * jnp.where(lo, cols[0][1], cols[1][1])
        for s in range(n_seg):
            r0 = s * t_len
            l = L_SSD_DT + 2 * g
            e0 = jnp.exp(cum_end[r0:r0 + 1, l:l + 1])
            e1 = jnp.exp(cum_end[r0:r0 + 1, l + 1:l + 2])
            upd = _dot_tn(_seg_rows(xw, rowi, t_len, s, n_seg).astype(BF16), bg)
            h_scr[s, g] = jnp.where(rsel, e0, e1) * h_scr[s, g] + upd
        yield
    y_all = jnp.concatenate(ys, axis=-1) + dvec_ref[...] * sx
    o_ref[...] = _rms(y_all * jax.nn.silu(z_ref[...]), ng_ref[...]).astype(BF16)

    def finish():
        convn_ref[...] = tail
        hn_ref[...] = h_scr[...].reshape(n_seg, SSD_H, 64, SSD_N)
    return finish


def ssd_spec(st, w, lb, n_seg):
    t_len = CH // n_seg
    gp = _gate_rows([(L_SSD_DT, w['ssd_a_log']), (L_SSD_DT, w['ssd_dt_bias'])])
    dvec = jnp.repeat(w['ssd_d'], BRANCH_W // SSD_H).reshape(1, BRANCH_W)
    spec = (_ssd_kernel, [COLS['ssd_in'], COLS['ssd_z'], (GATE_COL0, LANES)],
            [st['ssd_conv'], st['ssd']],
            [w['ssd_conv_w'], w['ssd_conv_b'].reshape(1, SSD_CONV_CH), gp, dvec, w['ssd_norm'].reshape(1, BRANCH_W)],
            [pltpu.VMEM((n_seg, 8 + t_len, SSD_CONV_CH), F32), pltpu.VMEM((n_seg, 2, 128, SSD_N), F32)])
    return spec, lambda convn, hn: {'ssd_conv': convn, 'ssd': hn}


def _mlstm_kernel(vo_ref, qk_ref, sm_ref, c0_ref, n0_ref, m0_ref, gp_ref, ng_ref, o_ref, cn_ref, nn_ref, mn_ref,
                  c_scr, n_scr, m_scr, *, n_seg, init):
    t_len = CH // n_seg
    if init:
        for s in range(n_seg):
            c_scr[s] = jnp.concatenate(
                [jnp.concatenate([c0_ref[s, h] if g == h else jnp.zeros((ML_DK, 64), F32) for g in range(ML_H)], axis=1)
                 for h in range(ML_H)], axis=0)
        n_scr[...] = n0_ref[...]
        m_scr[...] = m0_ref[...]
        return None

    same, tri, _ = _seg_masks(t_len)
    lane = lax.broadcasted_iota(jnp.int32, (CH, LANES), 1)
    lane256 = lax.broadcasted_iota(jnp.int32, (CH, BRANCH_W), 1)
    rowi = lax.broadcasted_iota(jnp.int32, (CH, 1), 0)
    r128 = lax.broadcasted_iota(jnp.int32, (LANES, 1), 0)
    neg = jnp.float32(-jnp.inf)

    v_all = vo_ref[:, 0:BRANCH_W]
    q_all = qk_ref[:, 0:ML_QK]
    k_all = qk_ref[:, ML_QK:2 * ML_QK] * (ML_DK ** -0.5)
    sm = sm_ref[...]
    ig = sm + gp_ref[0:1, :]
    fl = jnp.logical_and(lane >= L_ML_F, lane < L_ML_F + ML_H)
    lf = jnp.where(fl, -_softplus(-(sm + gp_ref[1:2, :])), 0.0)
    b = _dot(tri.astype(F32), lf, precision=HI)
    b_end = _dot(same.astype(F32), lf, precision=HI)
    b_r = _row_forms(b)
    ig_r = _row_forms(ig)
    yield
    mm = m_scr[...]
    qb = q_all.astype(BF16)
    kb = k_all.astype(BF16)
    qn = _seg_sum([_dot(qb, n_scr[s].astype(BF16)) for s in range(n_seg)], rowi, t_len)
    qc = _seg_sum([_dot(qb, c_scr[s].astype(BF16)) for s in range(n_seg)], rowi, t_len)
    yield
    num_att = jnp.zeros((CH, BRANCH_W), F32)
    w_ins, dens, w_ends, a_ends, m_ends = [], [], [], [], []
    for h in range(ML_H):
        li, lf_ = L_ML_I + h, L_ML_F + h
        bc = b[:, lf_:lf_ + 1]
        bec = b_end[:, lf_:lf_ + 1]
        igc = ig[:, li:li + 1]
        mmc = mm[:, li:li + 1]
        br = b_r[lf_:lf_ + 1, :]
        igr = ig_r[li:li + 1, :]
        diff = igr - br
        cmx = jnp.max(jnp.where(tri, diff, neg), axis=-1, keepdims=True)
        smx = jnp.max(jnp.where(same, diff, neg), axis=-1, keepdims=True)
        m_c = bc + jnp.maximum(mmc, cmx)
        m_end = bec + jnp.maximum(mmc, smx)
        w_in = jnp.exp(bc + mmc - m_c)
        logw = bc - br + igr - m_c
        mq = jnp.logical_and(lane >= ML_DK * h, lane < ML_DK * (h + 1))
        qk = _dot_nt(jnp.where(mq, q_all, 0.0).astype(BF16), kb)
        wts = _masked_exp(logw, tri) * qk
        yield
        mv = jnp.logical_and(lane256 >= 64 * h, lane256 < 64 * (h + 1))
        num_att = num_att + _dot(wts.astype(BF16), jnp.where(mv, v_all, 0.0).astype(BF16))
        nq = w_in * qn[:, li:li + 1] + jnp.sum(wts, axis=-1, keepdims=True)
        w_ins.append(w_in)
        dens.append(jnp.maximum(jnp.abs(nq), jnp.exp(-m_c)))
        w_ends.append(jnp.exp(bec - bc + igc - m_end))
        a_ends.append(jnp.exp(bec + mmc - m_end))
        m_ends.append(m_end)
        yield
    num = _quarter_sel(lane256, 64, w_ins) * qc + num_att
    hout = num / _quarter_sel(lane256, 64, dens)
    outs = [_head_rmsnorm(hout[:, 128 * p:128 * (p + 1)], lane < 64) for p in range(2)]
    o_all = jnp.concatenate(outs, axis=-1) * ng_ref[...] * jax.nn.sigmoid(vo_ref[:, BRANCH_W:2 * BRANCH_W])
    o_ref[...] = o_all.astype(BF16)
    yield

    kw = k_all * _quarter_sel(lane, ML_DK, w_ends)
    wend_tile = jnp.zeros((CH, LANES), F32)
    m_tile = jnp.zeros((CH, LANES), F32)
    for h in range(ML_H):
        wend_tile = jnp.where(lane == L_ML_I + h, w_ends[h], wend_tile)
        m_tile = jnp.where(lane == L_ML_I + h, m_ends[h], m_tile)
    m_scr[...] = m_tile
    vb = v_all.astype(BF16)
    wb = wend_tile.astype(BF16)
    rc = lax.broadcasted_iota(jnp.int32, (LANES, BRANCH_W), 0)
    lc = lax.broadcasted_iota(jnp.int32, (LANES, BRANCH_W), 1)
    bd_c = (rc // ML_DK) == (lc // 64)
    rn = lax.broadcasted_iota(jnp.int32, (LANES, LANES), 0)
    ln = lax.broadcasted_iota(jnp.int32, (LANES, LANES), 1)
    bd_n = ln == (rn // ML_DK) + L_ML_I
    for s in range(n_seg):
        r0 = s * t_len
        a_sel = _quarter_sel(r128, ML_DK, [a[r0:r0 + 1, :] for a in a_ends])
        upd_c = _dot_tn(_seg_rows(kw, rowi, t_len, s, n_seg).astype(BF16), vb)
        upd_n = _dot_tn(_seg_rows(k_all, rowi, t_len, s, n_seg).astype(BF16), wb)
        c_scr[s] = a_sel * c_scr[s] + jnp.where(bd_c, upd_c, 0.0)
        n_scr[s] = a_sel * n_scr[s] + jnp.where(bd_n, upd_n, 0.0)

    def finish():
        for s in range(n_seg):
            for h in range(ML_H):
                cn_ref[s, h] = c_scr[s, ML_DK * h:ML_DK * (h + 1), 64 * h:64 * (h + 1)]
        nn_ref[...] = n_scr[...]
        mn_ref[...] = m_tile
    return finish


def mlstm_spec(st, w, lb, n_seg):
    t_len = CH // n_seg
    c0, n0, m0 = st['ml_c'], st['ml_n'], st['ml_m']
    bsz = c0.shape[0]
    gp = _gate_rows([(L_ML_I, w['ml_ig_b']), (L_ML_F, w['ml_fg_b'])])
    ng = jnp.tile(w['ml_norm'], ML_H).reshape(1, BRANCH_W)
    eye = jnp.eye(ML_H, dtype=F32)
    pad = ((0, 0), (L_ML_I, LANES - L_ML_I - ML_H))
    n_bd = jnp.pad(jnp.einsum('bhk,hg->bhkg', n0, eye).reshape(bsz, ML_QK, ML_H), ((0, 0),) + pad)
    m_exp = jnp.pad(jnp.repeat(m0, t_len, axis=0), pad)
    assert COLS['ml_o'][0] == COLS['ml_v'][0] + BRANCH_W and COLS['ml_k'][0] == COLS['ml_q'][0] + ML_QK
    spec = (_mlstm_kernel, [(COLS['ml_v'][0], 2 * BRANCH_W), (COLS['ml_q'][0], 2 * ML_QK), (GATE_COL0, LANES)],
            [c0, n_bd, m_exp], [gp, ng],
            [pltpu.VMEM((n_seg, ML_QK, BRANCH_W), F32), pltpu.VMEM((n_seg, ML_QK, LANES), F32),
             pltpu.VMEM((CH, LANES), F32)])

    def finish(c_new, nn, mn):
        n_new = jnp.einsum('bhkg,hg->bhk', nn[:, :, L_ML_I:L_ML_I + ML_H].reshape(bsz, ML_H, ML_DK, ML_H), eye)
        return {'ml_c': c_new, 'ml_n': n_new, 'ml_m': mn[::t_len, L_ML_I:L_ML_I + ML_H]}
    return spec, finish


def _group_specs(tm, width, n_first):
    return [pl.BlockSpec((tm, width), lambda i: (jnp.minimum(i, n_first - 1), 0)),
            pl.BlockSpec((tm, width), lambda i: (jnp.maximum(i - n_first, 0), 0))]


def _group_in_specs(groups, tm, n_first):
    return [_row_block_spec(groups[0], tm, lambda i: jnp.minimum(i, n_first - 1)),
            _row_block_spec(groups[1], tm, lambda i: jnp.maximum(i - n_first, 0))]


def _merge_kernel(*refs, n_first, h_first):
    *h_refs, gates_ref, brp_ref, brs_ref, wb_ref, wo_ref, o_ref = refs
    br = _rows_of((brp_ref, brs_ref), n_first)
    merged = None
    for n in range(N_BRANCH):
        y = _dot(br[:, n * BRANCH_W:(n + 1) * BRANCH_W], wb_ref[n])
        z = gates_ref[:, n * D_MODEL:(n + 1) * D_MODEL].astype(F32)
        t = (0.5 * jnp.tanh(0.5 * z) + 0.5) * y
        merged = t if merged is None else merged + t
    o_ref[...] = _rows_of(h_refs, h_first) + _dot(merged.astype(BF16), wo_ref[...])


def merge(h, gates, br_groups, w_branch, w_out, tm):
    h_specs, h_ops, h_first = _row_inputs(h, tm)
    m, d = sum(a.shape[0] for a in h_ops), h_ops[0].shape[1]
    n_first = br_groups[0].shape[0] // tm
    assert br_groups[0].shape[0] % tm == 0 and br_groups[1].shape[0] % tm == 0
    return pl.pallas_call(
        functools.partial(_merge_kernel, n_first=n_first, h_first=h_first),
        grid=(m // tm,),
        in_specs=[*h_specs,
                  pl.BlockSpec((tm, N_BRANCH * d), lambda i: (i, 0)),
                  *_group_specs(tm, N_BRANCH * BRANCH_W, n_first),
                  pl.BlockSpec((N_BRANCH, BRANCH_W, d), lambda i: (0, 0, 0)),
                  pl.BlockSpec((d, d), lambda i: (0, 0))],
        out_specs=pl.BlockSpec((tm, d), lambda i: (i, 0)),
        out_shape=jax.ShapeDtypeStruct((m, d), F32),
        compiler_params=pltpu.CompilerParams(
            dimension_semantics=("parallel",), vmem_limit_bytes=VMEM_LIMIT),
        name="merge",
    )(*h_ops, gates, *br_groups, w_branch, w_out)


def _ffn_kernel(h_ref, g_ref, wg_ref, wu_ref, wd_ref, o_ref, *, tf):
    h = h_ref[...]
    u = _rms(h, g_ref[...]).astype(BF16)
    y = h
    for c in range(wg_ref.shape[1] // tf):
        cols = slice(c * tf, (c + 1) * tf)
        a = jax.nn.silu(_dot(u, wg_ref[:, cols]))
        b = _dot(u, wu_ref[:, cols])
        y = y + _dot((a * b).astype(BF16), wd_ref[cols, :])
    o_ref[...] = y


def ffn(h, g, wg, wu, wd, tm, tf):
    m, d = h.shape
    ff = wg.shape[1]
    assert ff % tf == 0
    resident = lambda a: pl.BlockSpec(a.shape, lambda i: (0, 0), pipeline_mode=pl.Buffered(1))
    return pl.pallas_call(
        functools.partial(_ffn_kernel, tf=tf),
        grid=(m // tm,),
        in_specs=[pl.BlockSpec((tm, d), lambda i: (i, 0)),
                  pl.BlockSpec((1, d), lambda i: (0, 0)),
                  resident(wg), resident(wu), resident(wd)],
        out_specs=pl.BlockSpec((tm, d), lambda i: (i, 0)),
        out_shape=jax.ShapeDtypeStruct((m, d), F32),
        compiler_params=pltpu.CompilerParams(
            dimension_semantics=("parallel",), vmem_limit_bytes=VMEM_LIMIT),
        name="ffn",
    )(h, g.reshape(1, d), wg, wu, wd)


def _router_kernel(h_ref, g_ref, wr_ref, u_ref, w_ref, i_ref):
    u = _rms(h_ref[...], g_ref[...])
    u_ref[...] = u
    logits = _dot3(_split2(u), _split2(wr_ref[...]))
    lane = lax.broadcasted_iota(jnp.int32, logits.shape, 1)
    neg = jnp.float32(-jnp.inf)
    logits = jnp.where(lane < N_EXPERTS, logits, neg)
    m1 = jnp.max(logits, axis=-1, keepdims=True)
    i1 = jnp.min(jnp.where(logits == m1, lane, LANES), axis=-1, keepdims=True)
    rest = jnp.where(lane == i1, neg, logits)
    m2 = jnp.max(rest, axis=-1, keepdims=True)
    i2 = jnp.min(jnp.where(rest == m2, lane, LANES), axis=-1, keepdims=True)
    e = jnp.exp(m2 - m1)
    den = 1.0 + e
    w_ref[...] = jnp.where(lane == 0, 1.0 / den, jnp.where(lane == 1, e / den, 0.0))
    i_ref[...] = jnp.where(lane == 0, i1, jnp.where(lane == 1, i2, 0))


def router(h, g, w_router, tm):
    m, d = h.shape
    wr = jnp.pad(w_router, ((0, 0), (0, LANES - N_EXPERTS)))
    return pl.pallas_call(
        _router_kernel,
        grid=(m // tm,),
        in_specs=[pl.BlockSpec((tm, d), lambda i: (i, 0)),
                  pl.BlockSpec((1, d), lambda i: (0, 0)),
                  pl.BlockSpec((d, LANES), lambda i: (0, 0))],
        out_specs=[pl.BlockSpec((tm, d), lambda i: (i, 0)),
                   pl.BlockSpec((tm, LANES), lambda i: (i, 0)),
                   pl.BlockSpec((tm, LANES), lambda i: (i, 0))],
        out_shape=[jax.ShapeDtypeStruct((m, d), F32),
                   jax.ShapeDtypeStruct((m, LANES), F32),
                   jax.ShapeDtypeStruct((m, LANES), jnp.int32)],
        compiler_params=pltpu.CompilerParams(
            dimension_semantics=("parallel",), vmem_limit_bytes=VMEM_LIMIT),
        name="router",
    )(h, g.reshape(1, d), wr)


def _row_copies(src_hbm, dst_hbm, idx_ref, base, buf, sem, n_rows, gather, wait):
    def body(r, carry):
        row = idx_ref[base + r]
        if gather:
            cp = pltpu.make_async_copy(src_hbm.at[pl.ds(row, 1)], buf.at[pl.ds(r, 1)], sem)
        else:
            cp = pltpu.make_async_copy(buf.at[pl.ds(r, 1)], dst_hbm.at[pl.ds(row, 1)], sem)
        if wait:
            cp.wait()
        else:
            cp.start()
        return carry
    lax.fori_loop(0, n_rows, body, 0, unroll=8)


def _expert_kernel(te_ref, nt_ref, src_ref, dst_ref, u_hbm, wg_ref, wu_ref, wd_ref, y_hbm,
                   xbuf, xb_ref, acc_ref, obuf, gsem, ssem, *, tm):
    i = pl.program_id(0)
    f = pl.program_id(1)
    last_f = pl.num_programs(1) - 1
    nt = nt_ref[0]
    gather = functools.partial(_row_copies, u_hbm, None, src_ref, gather=True, n_rows=tm)
    scatter = functools.partial(_row_copies, None, y_hbm, dst_ref, buf=obuf, sem=ssem.at[0], gather=False, n_rows=tm)

    @pl.when(jnp.logical_and(i < nt, f == 0))
    def _():
        slot = i % 2

        @pl.when(i == 0)
        def _():
            gather(base=0, buf=xbuf.at[0], sem=gsem.at[0], wait=False)
            obuf[...] = jnp.zeros_like(obuf)
            fill = pltpu.make_async_copy(obuf, y_hbm.at[pl.ds(y_hbm.shape[0] - tm, tm)], ssem.at[0])
            fill.start()
            fill.wait()

        gather(base=i * tm, buf=xbuf.at[slot], sem=gsem.at[slot], wait=True)

        @pl.when(i + 1 < nt)
        def _():
            gather(base=(i + 1) * tm, buf=xbuf.at[1 - slot], sem=gsem.at[1 - slot], wait=False)

        xb_ref[...] = xbuf[slot].astype(BF16)

    @pl.when(i < nt)
    def _():
        x = xb_ref[...]
        a = jax.nn.silu(_dot(x, wg_ref[0]))
        b = _dot(x, wu_ref[0])
        y = _dot((a * b).astype(BF16), wd_ref[0])

        @pl.when(f == 0)
        def _():
            acc_ref[...] = y

        @pl.when(f != 0)
        def _():
            acc_ref[...] += y

        @pl.when(f == last_f)
        def _():
            @pl.when(i > 0)
            def _():
                scatter(base=(i - 1) * tm, wait=True)

            obuf[...] = acc_ref[...]
            scatter(base=i * tm, wait=False)

            @pl.when(i == nt - 1)
            def _():
                scatter(base=i * tm, wait=True)


def experts(u, tile_expert, n_tiles, src_tok, dst_row, n_out_rows, wg, wu, wd, tm, tf):
    d = u.shape[1]
    ff = wg.shape[2]
    n_row_tiles = tile_expert.shape[0]
    wmap = lambda i, f, te, nt, src, dst: (te[i], 0, f)
    wbuf = pl.Buffered(1) if tf == ff else None
    grid_spec = pltpu.PrefetchScalarGridSpec(
        num_scalar_prefetch=4,
        grid=(n_row_tiles, ff // tf),
        in_specs=[pl.BlockSpec(memory_space=pl.ANY),
                  pl.BlockSpec((1, d, tf), wmap, pipeline_mode=wbuf),
                  pl.BlockSpec((1, d, tf), wmap, pipeline_mode=wbuf),
                  pl.BlockSpec((1, tf, d), lambda i, f, te, nt, src, dst: (te[i], f, 0), pipeline_mode=wbuf)],
        out_specs=pl.BlockSpec(memory_space=pl.ANY),
        scratch_shapes=[pltpu.VMEM((2, tm, d), F32), pltpu.VMEM((tm, d), BF16), pltpu.VMEM((tm, d), F32),
                        pltpu.VMEM((tm, d), F32), pltpu.SemaphoreType.DMA((2,)), pltpu.SemaphoreType.DMA((1,))],
    )
    return pl.pallas_call(
        functools.partial(_expert_kernel, tm=tm),
        grid_spec=grid_spec,
        out_shape=jax.ShapeDtypeStruct((n_out_rows, d), F32),
        compiler_params=pltpu.CompilerParams(
            dimension_semantics=("arbitrary", "arbitrary"), vmem_limit_bytes=VMEM_LIMIT,
            disable_bounds_checks=True),
        name="experts",
    )(tile_expert, n_tiles, src_tok, dst_row, u, wg, wu, wd)


def moe(h, g, w_router, wg, wu, wd, tm_route, tm_e, tf):
    m, d = h.shape
    u, top_w, top_i = router(h, g, w_router, tm_route)
    n_pairs = TOP_K * m
    flat_e = top_i[:, :TOP_K].reshape(-1)
    onehot = (flat_e[:, None] == jnp.arange(N_EXPERTS, dtype=jnp.int32)[None, :]).astype(jnp.int32)
    rank = jnp.sum((jnp.cumsum(onehot, axis=0) - 1) * onehot, axis=1)
    counts = jnp.sum(onehot, axis=0)
    tiles_per = (counts + tm_e - 1) // tm_e
    tile_end = jnp.cumsum(tiles_per)
    tile_start = tile_end - tiles_per
    grouped_row = tile_start[flat_e] * tm_e + rank
    n_rows = n_pairs + N_EXPERTS * tm_e
    n_row_tiles = n_rows // tm_e
    pair_ids = jnp.arange(n_pairs, dtype=jnp.int32)
    pair_at = jnp.full((n_rows,), -1, jnp.int32).at[grouped_row].set(pair_ids)
    spare = n_pairs + jnp.arange(n_rows, dtype=jnp.int32) % tm_e
    src_tok = jnp.where(pair_at >= 0, pair_at // TOP_K, 0)
    dst_row = jnp.where(pair_at >= 0, (pair_at % TOP_K) * m + pair_at // TOP_K, spare)
    tile_ids = jnp.arange(n_row_tiles, dtype=jnp.int32)
    tile_expert = jnp.minimum(jnp.sum((tile_ids[:, None] >= tile_end[None, :]).astype(jnp.int32), axis=1),
                              N_EXPERTS - 1).astype(jnp.int32)
    n_tiles = tile_end[-1:].astype(jnp.int32)
    last_e = tile_expert[jnp.maximum(n_tiles[0] - 1, 0)]
    tile_expert = jnp.where(tile_ids < n_tiles[0], tile_expert, last_e)
    y = experts(u, tile_expert, n_tiles, src_tok, dst_row, n_pairs + tm_e, wg, wu, wd, tm_e, tf)
    return y, top_w


def _ple_kernel(*refs, final, combine, n_first):
    if combine:
        h_ref, y0_ref, y1_ref, tw_ref, *refs = refs
        h = h_ref[...] + (tw_ref[:, 0:1] * y0_ref[...] + tw_ref[:, 1:2] * y1_ref[...])
    else:
        h_ref, *refs = refs
        h = h_ref[...]
    pp_ref, ps_ref, g_ref, wg_ref, wp_ref, gf_ref, *o_refs = refs
    first = pl.program_id(0) < n_first
    p = jnp.where(first, pp_ref[...], ps_ref[...])
    v = _rms(h, g_ref[...]).astype(BF16)
    pg = jax.nn.sigmoid(_dot(v, wg_ref[...]))
    e = _dot(p.astype(BF16), wp_ref[...])
    out = h + pg * e
    if not final:
        o_refs[0][...] = out
        return
    out = _rms(out, gf_ref[...])

    @pl.when(first)
    def _():
        o_refs[0][...] = out

    @pl.when(jnp.logical_not(first))
    def _():
        o_refs[1][...] = out


def ple(h, p_groups, g, w_gate, w_p, g_final, tm, final, expert_out=None):
    m, d = h.shape
    dp = p_groups[0].shape[1]
    n_first = p_groups[0].shape[0] // tm
    assert p_groups[0].shape[0] % tm == 0 and p_groups[1].shape[0] % tm == 0
    rows = lambda wd: pl.BlockSpec((tm, wd), lambda i: (i, 0))
    whole = lambda a, b: pl.BlockSpec((a, b), lambda i: (0, 0))
    extra, extra_specs = (), []
    if expert_out is not None:
        y, top_w = expert_out
        extra = (y, y, top_w)
        extra_specs = [rows(d), pl.BlockSpec((tm, d), lambda i: (i + m // tm, 0)), rows(LANES)]
    if final:
        out_specs = _group_specs(tm, d, n_first)
        out_shape = [jax.ShapeDtypeStruct((a.shape[0], d), F32) for a in p_groups]
    else:
        out_specs, out_shape = rows(d), jax.ShapeDtypeStruct((m, d), F32)
    return pl.pallas_call(
        functools.partial(_ple_kernel, final=final, combine=expert_out is not None, n_first=n_first),
        grid=(m // tm,),
        in_specs=([rows(d)] + extra_specs + _group_in_specs(p_groups, tm, n_first)
                  + [whole(1, d), whole(d, d), whole(dp, d), whole(1, d)]),
        out_specs=out_specs,
        out_shape=out_shape,
        compiler_params=pltpu.CompilerParams(
            dimension_semantics=("parallel",), vmem_limit_bytes=VMEM_LIMIT),
        name="ple",
    )(h, *extra, *[_operand(p) for p in p_groups], g.reshape(1, d), w_gate, w_p, g_final.reshape(1, d))


STATE_KEYS = ('gdn_conv', 'gdn', 'hgrn', 'ssd_conv', 'ssd', 'ml_c', 'ml_n', 'ml_m')
MIXER_KEYS = ('gdn_conv_w', 'gdn_a_log', 'gdn_dt_bias', 'gdn_norm', 'hgrn_norm',
              'ssd_conv_w', 'ssd_conv_b', 'ssd_a_log', 'ssd_dt_bias', 'ssd_d', 'ssd_norm',
              'ml_ig_b', 'ml_fg_b', 'ml_norm')


MIXER_SPECS = (gdn_spec, hgrn_spec, ssd_spec, mlstm_spec)
_gdn_kernel.n_stages = lambda t_len: 11 + 2 * (t_len.bit_length() - 2)
_hgrn_kernel.n_stages = lambda t_len: 1 + HG_H * (len(_hgrn_levels(t_len)) // 2 + 1)
_ssd_kernel.n_stages = lambda t_len: 8
_mlstm_kernel.n_stages = lambda t_len: 11


def _mixers(proj, row_blk0, st, w, lb, **grid):
    built = [f(st, w, lb, grid['n_seg']) for f in MIXER_SPECS]
    o, *new_states = _mixer_call([b[0] for b in built], proj, row_blk0, **grid)
    new, at = {}, 0
    for spec, finish in built:
        n = len(spec[2])
        new.update(finish(*new_states[at:at + n]))
        at += n
    return o, new


def kernel(x_prompt, x_sample, state_gdn_conv, state_gdn, state_hgrn, state_ssd_conv, state_ssd, state_mlstm_c, state_mlstm_n, state_mlstm_m, p_prompt, p_sample, g_mix, w_in, gdn_conv_w, gdn_a_log, gdn_dt_bias, gdn_norm, hgrn_lb, hgrn_norm, ssd_conv_w, ssd_conv_b, ssd_a_log, ssd_dt_bias, ssd_d, ssd_norm, ml_ig_b, ml_fg_b, ml_norm, w_branch, w_out, g_ffn, w_ff_gate, w_ff_up, w_ff_down, w_router, w_ex_gate, w_ex_up, w_ex_down, w_ple, w_ple_gate, g_ple, g_final):
    prm = {'gdn_conv_w': gdn_conv_w, 'gdn_a_log': gdn_a_log, 'gdn_dt_bias': gdn_dt_bias,
           'gdn_norm': gdn_norm, 'hgrn_norm': hgrn_norm, 'ssd_conv_w': ssd_conv_w,
           'ssd_conv_b': ssd_conv_b, 'ssd_a_log': ssd_a_log, 'ssd_dt_bias': ssd_dt_bias,
           'ssd_d': ssd_d, 'ssd_norm': ssd_norm, 'ml_ig_b': ml_ig_b, 'ml_fg_b': ml_fg_b,
           'ml_norm': ml_norm}
    bp, lp, d = x_prompt.shape
    bs, ls, _ = x_sample.shape
    mp, ms = bp * lp, bs * ls
    assert lp % CH == 0 and CH % ls == 0 and ms % CH == 0 and mp % CH == 0
    st_s = {'gdn_conv': state_gdn_conv, 'gdn': state_gdn, 'hgrn': state_hgrn, 'ssd_conv': state_ssd_conv,
            'ssd': state_ssd, 'ml_c': state_mlstm_c, 'ml_n': state_mlstm_n, 'ml_m': state_mlstm_m}
    st_p = {k: jnp.zeros((bp,) + v.shape[2:], F32) for k, v in st_s.items()}
    grid_p = dict(n_outer=bp, n_chunks=lp // CH, n_seg=1)
    grid_s = dict(n_outer=ms // CH, n_chunks=1, n_seg=CH // ls)

    sm = jax.nn.softmax(hgrn_lb, axis=0)
    lb_all = jnp.cumsum(sm, axis=0) - sm[0]

    h = (x_prompt.reshape(mp, d), x_sample.reshape(ms, d))
    new_p = {k: [] for k in STATE_KEYS}
    new_s = {k: [] for k in STATE_KEYS}
    for l in range(DEPTH):
        wl = {k: prm[k][l] for k in MIXER_KEYS}
        gates, mix = norm_matmul(h, g_mix[l], _permute_w_in(w_in[l]), tm=ROW_TILE, tn=PROJ_COL_TILE)
        br_p, np_ = _mixers(mix, 0, st_p, wl, lb_all[l], **grid_p)
        st_l = {k: (v[l] if k in ('ml_n', 'ml_m') else LayerOf(v, l)) for k, v in st_s.items()}
        br_s, ns_ = _mixers(mix, mp // CH, st_l, wl, lb_all[l], **grid_s)
        h = merge(h, gates, (br_p, br_s), w_branch[l].astype(BF16), w_out[l].astype(BF16), tm=ROW_TILE)
        j = l // 2
        expert_out = None
        if l % 2 == 0:
            h = ffn(h, g_ffn[l], w_ff_gate[j].astype(BF16), w_ff_up[j].astype(BF16),
                    w_ff_down[j].astype(BF16), tm=ROW_TILE, tf=D_FF)
        else:
            expert_out = moe(h, g_ffn[l], w_router[j], w_ex_gate[j].astype(BF16), w_ex_up[j].astype(BF16),
                             w_ex_down[j].astype(BF16), tm_route=ROW_TILE, tm_e=ROW_TILE, tf=D_FF)
        p_groups = (LayerOf(p_prompt.reshape(DEPTH, mp, D_PLE), l), LayerOf(p_sample.reshape(DEPTH, ms, D_PLE), l))
        h = ple(h, p_groups, g_ple[l], w_ple_gate[l].astype(BF16), w_ple[l].astype(BF16), g_final,
                tm=ROW_TILE, final=(l == DEPTH - 1), expert_out=expert_out)
        for k in STATE_KEYS:
            new_p[k].append(np_[k])
            new_s[k].append(ns_[k])
    y_prompt = h[0].reshape(bp, lp, d)
    y_sample = h[1].reshape(bs, ls, d)
    sp = {k: jnp.stack(v) for k, v in new_p.items()}
    ss = {k: jnp.stack(v) for k, v in new_s.items()}
    return (y_prompt, y_sample,
            sp['gdn_conv'], sp['gdn'], sp['hgrn'], sp['ssd_conv'], sp['ssd'], sp['ml_c'], sp['ml_n'], sp['ml_m'],
            ss['gdn_conv'], ss['gdn'], ss['hgrn'], ss['ssd_conv'], ss['ssd'], ss['ml_c'], ss['ml_n'], ss['ml_m'])
```

```python
import functools

import numpy as np
import jax
import jax.numpy as jnp
from jax import lax
from jax.experimental import pallas as pl
from jax.experimental.pallas import tpu as pltpu

F32 = jnp.float32
BF16 = jnp.bfloat16
HI = lax.Precision.HIGHEST

D_MODEL = 1024
DEPTH = 2
D_PLE = 256
N_BRANCH = 4
BRANCH_W = D_MODEL // N_BRANCH
CONV_W = 4
EPS = 1e-6

GDN_H = 4
GDN_DK = 64
GDN_QK = GDN_H * GDN_DK
GDN_CONV_CH = 2 * GDN_QK + BRANCH_W
HG_H = 4
HG_QK = 256
SSD_H = 4
SSD_G = 2
SSD_N = 128
SSD_CONV_CH = BRANCH_W + 2 * SSD_G * SSD_N
ML_H = 4
ML_DK = 32
ML_QK = ML_H * ML_DK
D_FF = ((8 * D_MODEL // 3 + 255) // 256) * 256
N_EXPERTS = 8
TOP_K = 2

_REF_SPLITS = (('gdn_in', GDN_CONV_CH), ('gdn_b', GDN_H), ('gdn_a', GDN_H), ('gdn_z', BRANCH_W),
               ('hg_q', HG_QK), ('hg_f', HG_QK), ('hg_v', BRANCH_W), ('hg_g', BRANCH_W),
               ('ssd_z', BRANCH_W), ('ssd_in', SSD_CONV_CH), ('ssd_dt', SSD_H),
               ('ml_q', ML_QK), ('ml_k', ML_QK), ('ml_v', BRANCH_W), ('ml_i', ML_H), ('ml_f', ML_H),
               ('ml_o', BRANCH_W), ('gates', N_BRANCH * D_MODEL))
_MY_ORDER = ('gdn_in', 'ssd_in', 'gdn_z', 'ssd_z', 'hg_q', 'hg_f', 'hg_v', 'hg_g',
             'ml_v', 'ml_o', 'ml_q', 'ml_k', 'gdn_b', 'gdn_a', 'ssd_dt', 'ml_i', 'ml_f')
LANES = 128
MXU_TILE = 256
CH = 64
N_GATES = N_BRANCH * D_MODEL


def _layout():
    widths = dict(_REF_SPLITS)
    off, out = 0, {}
    for name in _MY_ORDER:
        out[name] = (off, widths[name])
        off += widths[name]
    return out, -(-off // MXU_TILE) * MXU_TILE


COLS, N_MIX = _layout()
GATE_COL0 = COLS['gdn_b'][0]
L_GDN_B, L_GDN_A, L_SSD_DT, L_ML_I, L_ML_F = (COLS[n][0] - GATE_COL0 for n in ('gdn_b', 'gdn_a', 'ssd_dt', 'ml_i', 'ml_f'))
VMEM_LIMIT = 56 * 1024 * 1024
ROW_TILE = 512
PROJ_COL_TILE = 4 * MXU_TILE


def _ref_offsets():
    off, acc = {}, 0
    for name, wd in _REF_SPLITS:
        off[name] = acc
        acc += wd
    return off, acc


def _permute_kernel(w_ref, o_ref):
    ref_off, _ = _ref_offsets()
    tk = o_ref.shape[0]
    o_ref[:, :N_GATES] = w_ref[:, ref_off['gates']:ref_off['gates'] + N_GATES].astype(BF16)
    lane = lax.broadcasted_iota(jnp.int32, (tk, LANES), 1)
    gate_group = jnp.zeros((tk, LANES), F32)
    for name in _MY_ORDER:
        dst, wd = COLS[name]
        src = ref_off[name]
        if wd >= LANES:
            o_ref[:, N_GATES + dst:N_GATES + dst + wd] = w_ref[:, src:src + wd].astype(BF16)
        else:
            win0, at = src // LANES * LANES, src % LANES
            to = dst - GATE_COL0
            assert at + wd <= LANES and to + wd <= LANES
            window = pltpu.roll(w_ref[:, win0:win0 + LANES], (to - at) % LANES, axis=1)
            gate_group = jnp.where(jnp.logical_and(lane >= to, lane < to + wd), window, gate_group)
    o_ref[:, N_GATES + GATE_COL0:N_GATES + GATE_COL0 + LANES] = gate_group.astype(BF16)
    pad0 = N_GATES + GATE_COL0 + LANES
    o_ref[:, pad0:] = jnp.zeros((tk, N_GATES + N_MIX - pad0), BF16)


def _permute_w_in(w, tk=256):
    d, n_in = w.shape
    return pl.pallas_call(
        _permute_kernel, grid=(d // tk,),
        in_specs=[pl.BlockSpec((tk, n_in), lambda i: (i, 0))],
        out_specs=pl.BlockSpec((tk, N_GATES + N_MIX), lambda i: (i, 0)),
        out_shape=jax.ShapeDtypeStruct((d, N_GATES + N_MIX), BF16),
        compiler_params=pltpu.CompilerParams(dimension_semantics=("parallel",), vmem_limit_bytes=VMEM_LIMIT),
        name="permute_w_in",
    )(w)


def _rms(x, g):
    return x * lax.rsqrt(jnp.mean(x * x, axis=-1, keepdims=True) + EPS) * g


def _dot(a, b, **kw):
    return jnp.dot(a, b, preferred_element_type=F32, **kw)


def _dot_nt(a, b, **kw):
    return lax.dot_general(a, b, (((1,), (1,)), ((), ())), preferred_element_type=F32, **kw)


def _dot_tn(a, b, **kw):
    return lax.dot_general(a, b, (((0,), (0,)), ((), ())), preferred_element_type=F32, **kw)


def _rows_of(refs, n_first):
    if n_first is None:
        return refs[0][...]
    return jnp.where(pl.program_id(0) < n_first, refs[0][...], refs[1][...])


def _row_inputs(x, tm):
    if isinstance(x, tuple):
        assert x[0].shape[0] % tm == 0 and x[1].shape[0] % tm == 0
        n_first = x[0].shape[0] // tm
        return _group_in_specs(x, tm, n_first), list(x), n_first
    return [pl.BlockSpec((tm, x.shape[1]), lambda i: (i, 0))], [x], None


def _norm_matmul_kernel(*refs, tn, n_first):
    *x_refs, g_ref, w_ref, gates_ref, mix_ref = refs
    xn = _rms(_rows_of(x_refs, n_first), g_ref[...]).astype(BF16)
    for j in range(N_GATES // tn):
        gates_ref[:, j * tn:(j + 1) * tn] = _dot(xn, w_ref[:, j * tn:(j + 1) * tn]).astype(BF16)
    for j in range(N_MIX // tn):
        mix_ref[:, j * tn:(j + 1) * tn] = _dot(xn, w_ref[:, N_GATES + j * tn:N_GATES + (j + 1) * tn])


def norm_matmul(x, g, w, tm, tn):
    x_specs, x_ops, n_first = _row_inputs(x, tm)
    m, d = sum(a.shape[0] for a in x_ops), x_ops[0].shape[1]
    assert w.shape[1] == N_GATES + N_MIX and N_GATES % tn == 0 and N_MIX % tn == 0
    return pl.pallas_call(
        functools.partial(_norm_matmul_kernel, tn=tn, n_first=n_first),
        grid=(m // tm,),
        in_specs=[*x_specs,
                  pl.BlockSpec((1, d), lambda i: (0, 0)),
                  pl.BlockSpec((d, N_GATES + N_MIX), lambda i: (0, 0), pipeline_mode=pl.Buffered(1))],
        out_specs=[pl.BlockSpec((tm, N_GATES), lambda i: (i, 0)),
                   pl.BlockSpec((tm, N_MIX), lambda i: (i, 0))],
        out_shape=[jax.ShapeDtypeStruct((m, N_GATES), BF16), jax.ShapeDtypeStruct((m, N_MIX), F32)],
        compiler_params=pltpu.CompilerParams(
            dimension_semantics=("parallel",), vmem_limit_bytes=VMEM_LIMIT),
        name="norm_matmul",
    )(*x_ops, g.reshape(1, d), w)


def _seg_masks(t_len):
    row = lax.broadcasted_iota(jnp.int32, (CH, CH), 0)
    col = lax.broadcasted_iota(jnp.int32, (CH, CH), 1)
    same = (row // t_len) == (col // t_len)
    tri = jnp.logical_and(same, col <= row)
    strict = jnp.logical_and(same, col < row)
    return same, tri, strict


def _row_forms(x, n_rows=24):
    r = lax.broadcasted_iota(jnp.int32, (n_rows, LANES), 0)
    l = lax.broadcasted_iota(jnp.int32, (n_rows, LANES), 1)
    return _dot_nt((r == l).astype(F32), x, precision=HI)


def _softplus(x):
    return jnp.maximum(x, 0.0) + jnp.log1p(jnp.exp(-jnp.abs(x)))


def _split2(x):
    hi = x.astype(BF16)
    return hi, (x - hi.astype(F32)).astype(BF16)


def _dot3(a, b):
    return _dot(a[0], b[0]) + (_dot(a[0], b[1]) + _dot(a[1], b[0]))


def _masked_exp(d, mask):
    return jnp.where(mask, jnp.exp(jnp.where(mask, d, 0.0)), 0.0)


def _conv_silu(x, ext_scr, cw_ref, bias, t_len, n_seg):
    w = x.shape[-1]
    ext_scr[:, 8:8 + t_len, :] = x.reshape(n_seg, t_len, w)
    y = cw_ref[3:4, :] * x
    for j in range(1, CONV_W):
        y = y + cw_ref[3 - j:4 - j, :] * ext_scr[:, 8 - j:8 - j + t_len, :].reshape(CH, w)
    if bias is not None:
        y = y + bias
    tail = ext_scr[:, 5 + t_len:8 + t_len, :]
    ext_scr[:, 5:8, :] = tail
    return jax.nn.silu(y), tail


def _halves(xp, lo):
    s_lo = jnp.sum(jnp.where(lo, xp, 0.0), axis=-1, keepdims=True)
    s_hi = jnp.sum(jnp.where(lo, 0.0, xp), axis=-1, keepdims=True)
    return jnp.where(lo, s_lo, s_hi)


def _head_rmsnorm(xp, lo):
    return xp * lax.rsqrt(_halves(xp * xp, lo) * (1.0 / 64) + EPS)


def _head_l2norm(xp, lo):
    return xp * lax.rsqrt(_halves(xp * xp, lo) + EPS)


def _seg_sum(parts, rowi, t_len):
    if len(parts) == 1:
        return parts[0]
    acc = jnp.where(rowi // t_len == 0, parts[0], 0.0)
    for s in range(1, len(parts)):
        acc = acc + jnp.where(rowi // t_len == s, parts[s], 0.0)
    return acc


def _seg_rows(x, rowi, t_len, s, n_seg):
    return x if n_seg == 1 else jnp.where(rowi // t_len == s, x, 0.0)


def _quarter_sel(idx, width, vals):
    out = vals[3]
    for h in (2, 1, 0):
        out = jnp.where(idx < (h + 1) * width, vals[h], out)
    return out


def _gate_rows(pairs):
    t = jnp.zeros((8, LANES), F32)
    for r, (off, v) in enumerate(pairs):
        t = t.at[r, off:off + v.shape[0]].set(v.astype(F32))
    return t


def _block_diag(blocks):
    n = len(blocks)
    z = jnp.zeros_like(blocks[0])
    return jnp.concatenate(
        [jnp.concatenate([blocks[i] if i == j else z for j in range(n)], axis=1) for i in range(n)], axis=0)


def _transpose64(x):
    r = lax.broadcasted_iota(jnp.int32, x.shape, 0)
    c = lax.broadcasted_iota(jnp.int32, x.shape, 1)
    eye = (r == c).astype(BF16)
    hi, mid, lo = _split3(x)
    return (_dot_nt(eye, hi) + _dot_nt(eye, mid)) + _dot_nt(eye, lo)


def _load_pair_states(s0_ref, s_scr, n_seg, transpose):
    prep = _transpose64 if transpose else (lambda t: t)
    for s in range(n_seg):
        for p in range(2):
            s_scr[s, p] = _block_diag([prep(s0_ref[s, 2 * p]), prep(s0_ref[s, 2 * p + 1])])


def _store_pair_states(s_scr, sn_ref, n_seg, transpose):
    prep = _transpose64 if transpose else (lambda t: t)
    for s in range(n_seg):
        for p in range(2):
            sn_ref[s, 2 * p] = prep(s_scr[s, p, 0:64, 0:64])
            sn_ref[s, 2 * p + 1] = prep(s_scr[s, p, 64:128, 64:128])


class LayerOf:
    def __init__(self, stack, layer):
        self.stack, self.layer = stack, layer
        self.shape, self.ndim = stack.shape[1:], stack.ndim - 1


def _operand(a):
    return a.stack if isinstance(a, LayerOf) else a


def _row_block_spec(a, block_rows, index_of_step):
    rest = (0,) * (a.ndim - 1)
    if isinstance(a, LayerOf):
        return pl.BlockSpec((None, block_rows) + a.shape[1:], lambda *ids: (a.layer, index_of_step(*ids)) + rest)
    return pl.BlockSpec((block_rows,) + a.shape[1:], lambda *ids: (index_of_step(*ids),) + rest)


def _interleave(gens, n_stages):
    results = [None] * len(gens)
    pos = [0] * len(gens)
    live = set(range(len(gens)))
    while live:
        k = min(live, key=lambda i: ((pos[i] + 1) / n_stages[i], i))
        try:
            next(gens[k])
            pos[k] += 1
        except StopIteration as stop:
            results[k] = stop.value
            live.remove(k)
    return results


def _fused_mixer_kernel(*refs, n_seg, parts):
    tot = [sum(p[j] for p in parts) for j in range(1, 5)]
    ins, rest = refs[:tot[0]], refs[tot[0]:]
    sts, rest = rest[:tot[1]], rest[tot[1]:]
    prs, rest = rest[:tot[2]], rest[tot[2]:]
    o_ref, rest = rest[0], rest[1:]
    outs, scr = rest[:tot[1]], rest[tot[1]:]
    at = [0, 0, 0, 0]
    calls = []
    for k, (body, n_in, n_st, n_pr, n_scr) in enumerate(parts):
        take = lambda seq, j, n: seq[at[j]:at[j] + n]
        calls.append(functools.partial(
            body, *take(ins, 0, n_in), *take(sts, 1, n_st), *take(prs, 2, n_pr),
            o_ref.at[:, k * BRANCH_W:(k + 1) * BRANCH_W],
            *take(outs, 1, n_st), *take(scr, 3, n_scr), n_seg=n_seg))
        for j, n in enumerate((n_in, n_st, n_pr, n_scr)):
            at[j] += n

    n_stages = [p[0].n_stages(CH // n_seg) for p in parts]

    @pl.when(pl.program_id(1) == 0)
    def _():
        _interleave([call(init=True) for call in calls], n_stages)

    finishers = _interleave([call(init=False) for call in calls], n_stages)

    @pl.when(pl.program_id(1) == pl.num_programs(1) - 1)
    def _():
        for fin in finishers:
            fin()


def _mixer_call(specs, proj, row_blk0, *, n_outer, n_chunks, n_seg):
    rows = n_outer * n_chunks * CH
    rmap = lambda blk: (lambda i, c: (row_blk0 + i * n_chunks + c, blk))
    full = lambda a: pl.BlockSpec(a.shape, lambda i, c: (0,) * a.ndim)
    sblk = lambda a: _row_block_spec(a, a.shape[0] // n_outer, lambda i, c: i)
    oblk = lambda a: pl.BlockSpec((a.shape[0] // n_outer,) + a.shape[1:], lambda i, c: (i,) + (0,) * (a.ndim - 1))
    in_blocks = [b for s in specs for b in s[1]]
    state_ins = [a for s in specs for a in s[2]]
    params = [a for s in specs for a in s[3]]
    scratch = [a for s in specs for a in s[4]]
    for off, wd in in_blocks:
        assert off % wd == 0
    parts = tuple((s[0], len(s[1]), len(s[2]), len(s[3]), len(s[4])) for s in specs)
    width = len(specs) * BRANCH_W
    return pl.pallas_call(
        functools.partial(_fused_mixer_kernel, n_seg=n_seg, parts=parts),
        grid=(n_outer, n_chunks),
        in_specs=([pl.BlockSpec((CH, wd), rmap(off // wd)) for off, wd in in_blocks]
                  + [sblk(a) for a in state_ins] + [full(a) for a in params]),
        out_specs=[pl.BlockSpec((CH, width), lambda i, c: (i * n_chunks + c, 0))] + [oblk(a) for a in state_ins],
        out_shape=([jax.ShapeDtypeStruct((rows, width), BF16)]
                   + [jax.ShapeDtypeStruct(a.shape, F32) for a in state_ins]),
        scratch_shapes=scratch,
        compiler_params=pltpu.CompilerParams(
            dimension_semantics=("parallel", "arbitrary"), vmem_limit_bytes=VMEM_LIMIT),
        name="token_mixers",
    )(*([proj] * len(in_blocks)), *[_operand(a) for a in state_ins], *params)


def _gdn_kernel(xin_ref, z_ref, sm_ref, conv0_ref, s0_ref, cw_ref, gp_ref, ng_ref,
                o_ref, convn_ref, sn_ref, ext_scr, s_scr, *, n_seg, init):
    t_len = CH // n_seg
    if init:
        _load_pair_states(s0_ref, s_scr, n_seg, transpose=False)
        ext_scr[:, 5:8, :] = conv0_ref[...]
        return None

    xc, tail = _conv_silu(xin_ref[...], ext_scr, cw_ref, None, t_len, n_seg)
    yield
    same, tri, strict = _seg_masks(t_len)
    lane = lax.broadcasted_iota(jnp.int32, (CH, LANES), 1)
    rowi = lax.broadcasted_iota(jnp.int32, (CH, 1), 0)
    lo = lane < 64
    sm = sm_ref[...]
    beta = jax.nn.sigmoid(sm)
    gl = jnp.logical_and(lane >= L_GDN_A, lane < L_GDN_A + GDN_H)
    g = jnp.where(gl, -jnp.exp(gp_ref[0:1, :]) * _softplus(sm + gp_ref[1:2, :]), 0.0)
    gam = _dot(tri.astype(F32), g, precision=HI)
    gam_end = _dot(same.astype(F32), g, precision=HI)
    gam_r = _row_forms(gam, 8)
    yield
    r128 = lax.broadcasted_iota(jnp.int32, (LANES, LANES), 0)
    l128 = lax.broadcasted_iota(jnp.int32, (LANES, LANES), 1)
    bd = (r128 < 64) == (l128 < 64)
    rsel = lax.broadcasted_iota(jnp.int32, (LANES, 1), 0) < 64
    qs_, ks_, atts_, gcs_, a_, x_ = [], [], [], [], [], []
    for p in range(2):
        q_p = _head_l2norm(xc[:, 128 * p:128 * (p + 1)], lo) * (GDN_DK ** -0.5)
        k_p = _head_l2norm(xc[:, GDN_QK + 128 * p:GDN_QK + 128 * (p + 1)], lo)
        v_p = xc[:, 2 * GDN_QK + 128 * p:2 * GDN_QK + 128 * (p + 1)]
        kb = k_p.astype(BF16)
        qs_.append(q_p)
        ks_.append(k_p)
        for j in range(2):
            h = 2 * p + j
            mj = lo if j == 0 else jnp.logical_not(lo)
            kk = _dot_nt(jnp.where(mj, k_p, 0.0).astype(BF16), kb)
            qk = _dot_nt(jnp.where(mj, q_p, 0.0).astype(BF16), kb)
            gc = gam[:, L_GDN_A + h:L_GDN_A + h + 1]
            bc = beta[:, L_GDN_B + h:L_GDN_B + h + 1]
            dec = _masked_exp(gc - gam_r[L_GDN_A + h:L_GDN_A + h + 1, :], tri)
            a_.append(jnp.where(strict, bc * kk * dec, 0.0))
            x_.append(jnp.concatenate([jnp.where(mj, bc * v_p, 0.0),
                                       jnp.where(mj, (bc * jnp.exp(gc)) * k_p, 0.0)], axis=-1))
            atts_.append(qk * dec)
            gcs_.append(gc)
            yield
    eye = jnp.logical_and(tri, jnp.logical_not(strict)).astype(F32)
    sa = [_split2(a) for a in a_]
    t_ = [eye - a for a in a_]
    n = 2
    while n < t_len:
        sa = [_split2(_dot3(s, s)) for s in sa]
        yield
        t_ = [t + _dot3(_split2(t), s) for t, s in zip(t_, sa)]
        yield
        n *= 2
    x_ = [_dot3(_split2(t), _split2(x)) for t, x in zip(t_, x_)]
    yield
    outs = []
    for p in range(2):
        q_p, k_p = qs_[p], ks_[p]
        solv = x_[2 * p][:, :LANES] + x_[2 * p + 1][:, :LANES]
        solk = x_[2 * p][:, LANES:] + x_[2 * p + 1][:, LANES:]
        atts, gcs = atts_[2 * p:2 * p + 2], gcs_[2 * p:2 * p + 2]
        solk_b = solk.astype(BF16)
        qb = q_p.astype(BF16)
        u = solv - _seg_sum([_dot(solk_b, s_scr[s, p].astype(BF16)) for s in range(n_seg)], rowi, t_len)
        qs = _seg_sum([_dot(qb, s_scr[s, p].astype(BF16)) for s in range(n_seg)], rowi, t_len)
        o = jnp.where(lo, jnp.exp(gcs[0]), jnp.exp(gcs[1])) * qs
        for j in range(2):
            mj = lo if j == 0 else jnp.logical_not(lo)
            o = o + _dot(atts[j].astype(BF16), jnp.where(mj, u, 0.0).astype(BF16))
        yield
        ge0 = gam_end[:, L_GDN_A + 2 * p:L_GDN_A + 2 * p + 1]
        ge1 = gam_end[:, L_GDN_A + 2 * p + 1:L_GDN_A + 2 * p + 2]
        kw = k_p * jnp.where(lo, jnp.exp(ge0 - gcs[0]), jnp.exp(ge1 - gcs[1]))
        ub = u.astype(BF16)
        for s in range(n_seg):
            r0 = s * t_len
            dec_s = jnp.where(rsel, jnp.exp(ge0[r0:r0 + 1, :]), jnp.exp(ge1[r0:r0 + 1, :]))
            upd = _dot_tn(_seg_rows(kw, rowi, t_len, s, n_seg).astype(BF16), ub)
            s_scr[s, p] = dec_s * s_scr[s, p] + jnp.where(bd, upd, 0.0)
        outs.append(_head_rmsnorm(o, lo))
        yield
    o_all = jnp.concatenate(outs, axis=-1) * ng_ref[...] * jax.nn.silu(z_ref[...])
    o_ref[...] = o_all.astype(BF16)

    def finish():
        convn_ref[...] = tail
        _store_pair_states(s_scr, sn_ref, n_seg, transpose=False)
    return finish


def gdn_spec(st, w, lb, n_seg):
    t_len = CH // n_seg
    gp = _gate_rows([(L_GDN_A, w['gdn_a_log']), (L_GDN_A, w['gdn_dt_bias'])])
    ng = jnp.tile(w['gdn_norm'], GDN_H).reshape(1, BRANCH_W)
    spec = (_gdn_kernel, [COLS['gdn_in'], COLS['gdn_z'], (GATE_COL0, LANES)],
            [st['gdn_conv'], st['gdn']], [w['gdn_conv_w'], gp, ng],
            [pltpu.VMEM((n_seg, 8 + t_len, GDN_CONV_CH), F32), pltpu.VMEM((n_seg, 2, 128, 128), F32)])
    return spec, lambda convn, sn: {'gdn_conv': convn, 'gdn': sn}


def _hgrn_levels(t_len):
    lv, n = [], t_len
    while n >= 2:
        lv.append(n)
        n //= 2
    return lv


def _hgrn_cmat(t_len):
    t = np.arange(CH)[:, None]
    j = np.arange(CH)[None, :]
    same = (t // t_len) == (j // t_len)
    mats = [same & (j <= t), same]
    for n in _hgrn_levels(t_len):
        mid = (t // n) * n + n // 2
        second = t % n >= n // 2
        mats.append((second & (j >= mid) & (j <= t)) | (~second & (j > t) & (j <= mid - 1)))
    return jnp.asarray(np.concatenate(mats, axis=0).astype(np.float32), dtype=BF16)


def _split3(x):
    hi = x.astype(BF16)
    r = x - hi.astype(F32)
    mid = r.astype(BF16)
    return hi, mid, (r - mid.astype(F32)).astype(BF16)


def _hgrn_kernel(x_ref, s0_ref, cm_ref, lb_ref, ng_ref, o_ref, sn_ref, s_scr, *, n_seg, init):
    t_len = CH // n_seg
    levels = _hgrn_levels(t_len)
    if init:
        _load_pair_states(s0_ref, s_scr, n_seg, transpose=True)
        return None

    lane = lax.broadcasted_iota(jnp.int32, (CH, LANES), 1)
    rowi = lax.broadcasted_iota(jnp.int32, (CH, 1), 0)
    row = lax.broadcasted_iota(jnp.int32, (CH, CH), 0)
    col = lax.broadcasted_iota(jnp.int32, (CH, CH), 1)
    lo = lane < 64
    r128 = lax.broadcasted_iota(jnp.int32, (LANES, LANES), 0)
    l128 = lax.broadcasted_iota(jnp.int32, (LANES, LANES), 1)
    bd = (r128 < 64) == (l128 < 64)

    lb = lb_ref[...]
    f_pre = x_ref[:, HG_QK:2 * HG_QK]
    log_f = jnp.log(lb + (1.0 - lb) * jax.nn.sigmoid(f_pre))
    k_in = (1.0 - lb) * jax.nn.sigmoid(-f_pre)
    cm = cm_ref[...]
    ex = None
    for part in _split3(log_f):
        t = _dot(cm, part)
        ex = t if ex is None else ex + t
    b = ex[0:CH]
    b_end = ex[CH:2 * CH]
    yield
    outs = []
    for p in range(2):
        ls = slice(128 * p, 128 * (p + 1))
        q_p = x_ref[:, ls]
        k_p = k_in[:, ls]
        v_p = x_ref[:, 2 * HG_QK + 128 * p:2 * HG_QK + 128 * (p + 1)]
        qk = q_p * k_p
        qe = (q_p * jnp.exp(b[:, ls])).astype(BF16)
        o = _seg_sum([_dot_nt(qe, s_scr[s, p].astype(BF16)) for s in range(n_seg)], rowi, t_len)
        scales = [jnp.exp(ex[(2 + li) * CH:(3 + li) * CH, ls]) for li in range(len(levels))]
        for j in range(2):
            mj = lo if j == 0 else jnp.logical_not(lo)
            diag = jnp.sum(jnp.where(mj, qk, 0.0), axis=-1, keepdims=True)
            att = jnp.where(row == col, diag, 0.0)
            for li, n in enumerate(levels):
                tq = (rowi % n) >= (n // 2)
                qt = jnp.where(jnp.logical_and(mj, tq), q_p * scales[li], 0.0)
                kt = jnp.where(tq, 0.0, k_p * scales[li])
                att = att + jnp.where((row // n) == (col // n), _dot_nt(qt.astype(BF16), kt.astype(BF16)), 0.0)
                if li % 2 == 1:
                    yield
            o = o + _dot(att.astype(BF16), jnp.where(mj, v_p, 0.0).astype(BF16))
            yield
        kw = (k_p * jnp.exp(b_end[:, ls] - b[:, ls])).astype(BF16)
        for s in range(n_seg):
            r0 = s * t_len
            upd = _dot_tn(_seg_rows(v_p, rowi, t_len, s, n_seg).astype(BF16), kw)
            s_scr[s, p] = jnp.exp(b_end[r0:r0 + 1, ls]) * s_scr[s, p] + jnp.where(bd, upd, 0.0)
        outs.append(_head_rmsnorm(o, lo))
    o_all = jnp.concatenate(outs, axis=-1) * ng_ref[...] * jax.nn.silu(x_ref[:, 3 * HG_QK:4 * HG_QK])
    o_ref[...] = o_all.astype(BF16)

    def finish():
        _store_pair_states(s_scr, sn_ref, n_seg, transpose=True)
    return finish


def hgrn_spec(st, w, lb, n_seg):
    t_len = CH // n_seg
    ng = jnp.tile(w['hgrn_norm'], HG_H).reshape(1, BRANCH_W)
    assert COLS['hg_f'][0] == COLS['hg_q'][0] + HG_QK and COLS['hg_g'][0] == COLS['hg_q'][0] + 3 * HG_QK
    spec = (_hgrn_kernel, [(COLS['hg_q'][0], 4 * HG_QK)], [st['hgrn']],
            [_hgrn_cmat(t_len), lb.reshape(1, HG_QK), ng], [pltpu.VMEM((n_seg, 2, 128, 128), F32)])
    return spec, lambda sn: {'hgrn': sn}


def _ssd_kernel(xin_ref, z_ref, sm_ref, conv0_ref, h0_ref, cw_ref, cb_ref, gp_ref, dvec_ref, ng_ref,
                o_ref, convn_ref, hn_ref, ext_scr, h_scr, *, n_seg, init):
    t_len = CH // n_seg
    if init:
        h_scr[...] = h0_ref[...].reshape(n_seg, 2, 128, SSD_N)
        ext_scr[:, 5:8, :] = conv0_ref[...]
        return None

    xc, tail = _conv_silu(xin_ref[...], ext_scr, cw_ref, cb_ref[...], t_len, n_seg)
    yield
    sx, bm, cm = xc[:, :256], xc[:, 256:512], xc[:, 512:768]
    same, tri, _ = _seg_masks(t_len)
    lane = lax.broadcasted_iota(jnp.int32, (CH, LANES), 1)
    rowi = lax.broadcasted_iota(jnp.int32, (CH, 1), 0)
    gl = jnp.logical_and(lane >= L_SSD_DT, lane < L_SSD_DT + SSD_H)
    dt = jnp.where(gl, _softplus(sm_ref[...] + gp_ref[1:2, :]), 0.0)
    da = -jnp.exp(gp_ref[0:1, :]) * dt
    cum = _dot(tri.astype(F32), da, precision=HI)
    cum_end = _dot(same.astype(F32), da, precision=HI)
    cum_r = _row_forms(cum, 16)
    dt_r = _row_forms(dt, 16)
    yield
    lo = lane < 64
    rsel = lax.broadcasted_iota(jnp.int32, (LANES, 1), 0) < 64
    ys = []
    for g in range(SSD_G):
        cg = cm[:, 128 * g:128 * (g + 1)].astype(BF16)
        bg = bm[:, 128 * g:128 * (g + 1)].astype(BF16)
        sxp = sx[:, 128 * g:128 * (g + 1)]
        cb = _dot_nt(cg, bg)
        yst = _seg_sum([_dot_nt(cg, h_scr[s, g].astype(BF16)) for s in range(n_seg)], rowi, t_len)
        yatt = jnp.zeros((CH, LANES), F32)
        cols = []
        for j in range(2):
            l = L_SSD_DT + 2 * g + j
            cc = cum[:, l:l + 1]
            dec = _masked_exp(cc - cum_r[l:l + 1, :], tri)
            att = cb * dec * dt_r[l:l + 1, :]
            xm = jnp.where(lo if j == 0 else jnp.logical_not(lo), sxp, 0.0)
            yatt = yatt + _dot(att.astype(BF16), xm.astype(BF16))
            cols.append((cc, dt[:, l:l + 1] * jnp.exp(cum_end[:, l:l + 1] - cc)))
            yield
        ys.append(jnp.where(lo, jnp.exp(cols[0][0]), jnp.exp(cols[1][0])) * yst + yatt)
        xw = sxp * jnp.where(lo, cols[0][1], cols[1][1])
        for s in range(n_seg):
            r0 = s * t_len
            l = L_SSD_DT + 2 * g
            e0 = jnp.exp(cum_end[r0:r0 + 1, l:l + 1])
            e1 = jnp.exp(cum_end[r0:r0 + 1, l + 1:l + 2])
            upd = _dot_tn(_seg_rows(xw, rowi, t_len, s, n_seg).astype(BF16), bg)
            h_scr[s, g] = jnp.where(rsel, e0, e1) * h_scr[s, g] + upd
        yield
    y_all = jnp.concatenate(ys, axis=-1) + dvec_ref[...] * sx
    o_ref[...] = _rms(y_all * jax.nn.silu(z_ref[...]), ng_ref[...]).astype(BF16)

    def finish():
        convn_ref[...] = tail
        hn_ref[...] = h_scr[...].reshape(n_seg, SSD_H, 64, SSD_N)
    return finish


def ssd_spec(st, w, lb, n_seg):
    t_len = CH // n_seg
    gp = _gate_rows([(L_SSD_DT, w['ssd_a_log']), (L_SSD_DT, w['ssd_dt_bias'])])
    dvec = jnp.repeat(w['ssd_d'], BRANCH_W // SSD_H).reshape(1, BRANCH_W)
    spec = (_ssd_kernel, [COLS['ssd_in'], COLS['ssd_z'], (GATE_COL0, LANES)],
            [st['ssd_conv'], st['ssd']],
            [w['ssd_conv_w'], w['ssd_conv_b'].reshape(1, SSD_CONV_CH), gp, dvec, w['ssd_norm'].reshape(1, BRANCH_W)],
            [pltpu.VMEM((n_seg, 8 + t_len, SSD_CONV_CH), F32), pltpu.VMEM((n_seg, 2, 128, SSD_N), F32)])
    return spec, lambda convn, hn: {'ssd_conv': convn, 'ssd': hn}


def _mlstm_kernel(vo_ref, qk_ref, sm_ref, c0_ref, n0_ref, m0_ref, gp_ref, ng_ref, o_ref, cn_ref, nn_ref, mn_ref,
                  c_scr, n_scr, m_scr, *, n_seg, init):
    t_len = CH // n_seg
    if init:
        for s in range(n_seg):
            c_scr[s] = jnp.concatenate(
                [jnp.concatenate([c0_ref[s, h] if g == h else jnp.zeros((ML_DK, 64), F32) for g in range(ML_H)], axis=1)
                 for h in range(ML_H)], axis=0)
        n_scr[...] = n0_ref[...]
        m_scr[...] = m0_ref[...]
        return None

    same, tri, _ = _seg_masks(t_len)
    lane = lax.broadcasted_iota(jnp.int32, (CH, LANES), 1)
    lane256 = lax.broadcasted_iota(jnp.int32, (CH, BRANCH_W), 1)
    rowi = lax.broadcasted_iota(jnp.int32, (CH, 1), 0)
    r128 = lax.broadcasted_iota(jnp.int32, (LANES, 1), 0)
    neg = jnp.float32(-jnp.inf)

    v_all = vo_ref[:, 0:BRANCH_W]
    q_all = qk_ref[:, 0:ML_QK]
    k_all = qk_ref[:, ML_QK:2 * ML_QK] * (ML_DK ** -0.5)
    sm = sm_ref[...]
    ig = sm + gp_ref[0:1, :]
    fl = jnp.logical_and(lane >= L_ML_F, lane < L_ML_F + ML_H)
    lf = jnp.where(fl, -_softplus(-(sm + gp_ref[1:2, :])), 0.0)
    b = _dot(tri.astype(F32), lf, precision=HI)
    b_end = _dot(same.astype(F32), lf, precision=HI)
    b_r = _row_forms(b)
    ig_r = _row_forms(ig)
    yield
    mm = m_scr[...]
    qb = q_all.astype(BF16)
    kb = k_all.astype(BF16)
    qn = _seg_sum([_dot(qb, n_scr[s].astype(BF16)) for s in range(n_seg)], rowi, t_len)
    qc = _seg_sum([_dot(qb, c_scr[s].astype(BF16)) for s in range(n_seg)], rowi, t_len)
    yield
    num_att = jnp.zeros((CH, BRANCH_W), F32)
    w_ins, dens, w_ends, a_ends, m_ends = [], [], [], [], []
    for h in range(ML_H):
        li, lf_ = L_ML_I + h, L_ML_F + h
        bc = b[:, lf_:lf_ + 1]
        bec = b_end[:, lf_:lf_ + 1]
        igc = ig[:, li:li + 1]
        mmc = mm[:, li:li + 1]
        br = b_r[lf_:lf_ + 1, :]
        igr = ig_r[li:li + 1, :]
        diff = igr - br
        cmx = jnp.max(jnp.where(tri, diff, neg), axis=-1, keepdims=True)
        smx = jnp.max(jnp.where(same, diff, neg), axis=-1, keepdims=True)
        m_c = bc + jnp.maximum(mmc, cmx)
        m_end = bec + jnp.maximum(mmc, smx)
        w_in = jnp.exp(bc + mmc - m_c)
        logw = bc - br + igr - m_c
        mq = jnp.logical_and(lane >= ML_DK * h, lane < ML_DK * (h + 1))
        qk = _dot_nt(jnp.where(mq, q_all, 0.0).astype(BF16), kb)
        wts = _masked_exp(logw, tri) * qk
        yield
        mv = jnp.logical_and(lane256 >= 64 * h, lane256 < 64 * (h + 1))
        num_att = num_att + _dot(wts.astype(BF16), jnp.where(mv, v_all, 0.0).astype(BF16))
        nq = w_in * qn[:, li:li + 1] + jnp.sum(wts, axis=-1, keepdims=True)
        w_ins.append(w_in)
        dens.append(jnp.maximum(jnp.abs(nq), jnp.exp(-m_c)))
        w_ends.append(jnp.exp(bec - bc + igc - m_end))
        a_ends.append(jnp.exp(bec + mmc - m_end))
        m_ends.append(m_end)
        yield
    num = _quarter_sel(lane256, 64, w_ins) * qc + num_att
    hout = num / _quarter_sel(lane256, 64, dens)
    outs = [_head_rmsnorm(hout[:, 128 * p:128 * (p + 1)], lane < 64) for p in range(2)]
    o_all = jnp.concatenate(outs, axis=-1) * ng_ref[...] * jax.nn.sigmoid(vo_ref[:, BRANCH_W:2 * BRANCH_W])
    o_ref[...] = o_all.astype(BF16)
    yield

    kw = k_all * _quarter_sel(lane, ML_DK, w_ends)
    wend_tile = jnp.zeros((CH, LANES), F32)
    m_tile = jnp.zeros((CH, LANES), F32)
    for h in range(ML_H):
        wend_tile = jnp.where(lane == L_ML_I + h, w_ends[h], wend_tile)
        m_tile = jnp.where(lane == L_ML_I + h, m_ends[h], m_tile)
    m_scr[...] = m_tile
    vb = v_all.astype(BF16)
    wb = wend_tile.astype(BF16)
    rc = lax.broadcasted_iota(jnp.int32, (LANES, BRANCH_W), 0)
    lc = lax.broadcasted_iota(jnp.int32, (LANES, BRANCH_W), 1)
    bd_c = (rc // ML_DK) == (lc // 64)
    rn = lax.broadcasted_iota(jnp.int32, (LANES, LANES), 0)
    ln = lax.broadcasted_iota(jnp.int32, (LANES, LANES), 1)
    bd_n = ln == (rn // ML_DK) + L_ML_I
    for s in range(n_seg):
        r0 = s * t_len
        a_sel = _quarter_sel(r128, ML_DK, [a[r0:r0 + 1, :] for a in a_ends])
        upd_c = _dot_tn(_seg_rows(kw, rowi, t_len, s, n_seg).astype(BF16), vb)
        upd_n = _dot_tn(_seg_rows(k_all, rowi, t_len, s, n_seg).astype(BF16), wb)
        c_scr[s] = a_sel * c_scr[s] + jnp.where(bd_c, upd_c, 0.0)
        n_scr[s] = a_sel * n_scr[s] + jnp.where(bd_n, upd_n, 0.0)

    def finish():
        for s in range(n_seg):
            for h in range(ML_H):
                cn_ref[s, h] = c_scr[s, ML_DK * h:ML_DK * (h + 1), 64 * h:64 * (h + 1)]
        nn_ref[...] = n_scr[...]
        mn_ref[...] = m_tile
    return finish


def mlstm_spec(st, w, lb, n_seg):
    t_len = CH // n_seg
    c0, n0, m0 = st['ml_c'], st['ml_n'], st['ml_m']
    bsz = c0.shape[0]
    gp = _gate_rows([(L_ML_I, w['ml_ig_b']), (L_ML_F, w['ml_fg_b'])])
    ng = jnp.tile(w['ml_norm'], ML_H).reshape(1, BRANCH_W)
    eye = jnp.eye(ML_H, dtype=F32)
    pad = ((0, 0), (L_ML_I, LANES - L_ML_I - ML_H))
    n_bd = jnp.pad(jnp.einsum('bhk,hg->bhkg', n0, eye).reshape(bsz, ML_QK, ML_H), ((0, 0),) + pad)
    m_exp = jnp.pad(jnp.repeat(m0, t_len, axis=0), pad)
    assert COLS['ml_o'][0] == COLS['ml_v'][0] + BRANCH_W and COLS['ml_k'][0] == COLS['ml_q'][0] + ML_QK
    spec = (_mlstm_kernel, [(COLS['ml_v'][0], 2 * BRANCH_W), (COLS['ml_q'][0], 2 * ML_QK), (GATE_COL0, LANES)],
            [c0, n_bd, m_exp], [gp, ng],
            [pltpu.VMEM((n_seg, ML_QK, BRANCH_W), F32), pltpu.VMEM((n_seg, ML_QK, LANES), F32),
             pltpu.VMEM((CH, LANES), F32)])

    def finish(c_new, nn, mn):
        n_new = jnp.einsum('bhkg,hg->bhk', nn[:, :, L_ML_I:L_ML_I + ML_H].reshape(bsz, ML_H, ML_DK, ML_H), eye)
        return {'ml_c': c_new, 'ml_n': n_new, 'ml_m': mn[::t_len, L_ML_I:L_ML_I + ML_H]}
    return spec, finish


def _group_specs(tm, width, n_first):
    return [pl.BlockSpec((tm, width), lambda i: (jnp.minimum(i, n_first - 1), 0)),
            pl.BlockSpec((tm, width), lambda i: (jnp.maximum(i - n_first, 0), 0))]


def _group_in_specs(groups, tm, n_first):
    return [_row_block_spec(groups[0], tm, lambda i: jnp.minimum(i, n_first - 1)),
            _row_block_spec(groups[1], tm, lambda i: jnp.maximum(i - n_first, 0))]


def _merge_kernel(*refs, n_first, h_first):
    *h_refs, gates_ref, brp_ref, brs_ref, wb_ref, wo_ref, o_ref = refs
    br = _rows_of((brp_ref, brs_ref), n_first)
    merged = None
    for n in range(N_BRANCH):
        y = _dot(br[:, n * BRANCH_W:(n + 1) * BRANCH_W], wb_ref[n])
        z = gates_ref[:, n * D_MODEL:(n + 1) * D_MODEL].astype(F32)
        t = (0.5 * jnp.tanh(0.5 * z) + 0.5) * y
        merged = t if merged is None else merged + t
    o_ref[...] = _rows_of(h_refs, h_first) + _dot(merged.astype(BF16), wo_ref[...])


def merge(h, gates, br_groups, w_branch, w_out, tm):
    h_specs, h_ops, h_first = _row_inputs(h, tm)
    m, d = sum(a.shape[0] for a in h_ops), h_ops[0].shape[1]
    n_first = br_groups[0].shape[0] // tm
    assert br_groups[0].shape[0] % tm == 0 and br_groups[1].shape[0] % tm == 0
    return pl.pallas_call(
        functools.partial(_merge_kernel, n_first=n_first, h_first=h_first),
        grid=(m // tm,),
        in_specs=[*h_specs,
                  pl.BlockSpec((tm, N_BRANCH * d), lambda i: (i, 0)),
                  *_group_specs(tm, N_BRANCH * BRANCH_W, n_first),
                  pl.BlockSpec((N_BRANCH, BRANCH_W, d), lambda i: (0, 0, 0)),
                  pl.BlockSpec((d, d), lambda i: (0, 0))],
        out_specs=pl.BlockSpec((tm, d), lambda i: (i, 0)),
        out_shape=jax.ShapeDtypeStruct((m, d), F32),
        compiler_params=pltpu.CompilerParams(
            dimension_semantics=("parallel",), vmem_limit_bytes=VMEM_LIMIT),
        name="merge",
    )(*h_ops, gates, *br_groups, w_branch, w_out)


def _ffn_kernel(h_ref, g_ref, wg_ref, wu_ref, wd_ref, *rest, tf, n_first):
    h = h_ref[...]
    u = _rms(h, g_ref[...]).astype(BF16)
    y = h
    for c in range(wg_ref.shape[1] // tf):
        cols = slice(c * tf, (c + 1) * tf)
        a = jax.nn.silu(_dot(u, wg_ref[:, cols]))
        b = _dot(u, wu_ref[:, cols])
        y = y + _dot((a * b).astype(BF16), wd_ref[cols, :])
    if n_first is None:
        rest[0][...] = y
        return
    pp_ref, ps_ref, gp_ref, wpg_ref, wpp_ref, o_ref = rest
    p = _rows_of((pp_ref, ps_ref), n_first)
    v = _rms(y, gp_ref[...]).astype(BF16)
    pg = jax.nn.sigmoid(_dot(v, wpg_ref[...]))
    o_ref[...] = y + pg * _dot(p.astype(BF16), wpp_ref[...])


def ffn(h, g, wg, wu, wd, tm, tf, ple_term=None):
    m, d = h.shape
    ff = wg.shape[1]
    assert ff % tf == 0
    resident = lambda a: pl.BlockSpec(a.shape, lambda i: (0, 0), pipeline_mode=pl.Buffered(1))
    extra_specs, extra_ops, n_first = [], [], None
    if ple_term is not None:
        p_groups, g_ple, w_gate, w_p = ple_term
        n_first = p_groups[0].shape[0] // tm
        assert p_groups[0].shape[0] % tm == 0 and p_groups[1].shape[0] % tm == 0
        extra_specs = [*_group_in_specs(p_groups, tm, n_first), pl.BlockSpec((1, d), lambda i: (0, 0)),
                       resident(w_gate), resident(w_p)]
        extra_ops = [*[_operand(p) for p in p_groups], g_ple.reshape(1, d), w_gate, w_p]
    return pl.pallas_call(
        functools.partial(_ffn_kernel, tf=tf, n_first=n_first),
        grid=(m // tm,),
        in_specs=[pl.BlockSpec((tm, d), lambda i: (i, 0)),
                  pl.BlockSpec((1, d), lambda i: (0, 0)),
                  resident(wg), resident(wu), resident(wd), *extra_specs],
        out_specs=pl.BlockSpec((tm, d), lambda i: (i, 0)),
        out_shape=jax.ShapeDtypeStruct((m, d), F32),
        compiler_params=pltpu.CompilerParams(
            dimension_semantics=("parallel",), vmem_limit_bytes=VMEM_LIMIT),
        name="ffn",
    )(h, g.reshape(1, d), wg, wu, wd, *extra_ops)


def _router_kernel(h_ref, g_ref, wr_ref, u_ref, w_ref, i_ref):
    u = _rms(h_ref[...], g_ref[...])
    u_ref[...] = u
    logits = _dot3(_split2(u), _split2(wr_ref[...]))
    lane = lax.broadcasted_iota(jnp.int32, logits.shape, 1)
    neg = jnp.float32(-jnp.inf)
    logits = jnp.where(lane < N_EXPERTS, logits, neg)
    m1 = jnp.max(logits, axis=-1, keepdims=True)
    i1 = jnp.min(jnp.where(logits == m1, lane, LANES), axis=-1, keepdims=True)
    rest = jnp.where(lane == i1, neg, logits)
    m2 = jnp.max(rest, axis=-1, keepdims=True)
    i2 = jnp.min(jnp.where(rest == m2, lane, LANES), axis=-1, keepdims=True)
    e = jnp.exp(m2 - m1)
    den = 1.0 + e
    w_ref[...] = jnp.where(lane == 0, 1.0 / den, jnp.where(lane == 1, e / den, 0.0))
    i_ref[...] = jnp.where(lane == 0, i1, jnp.where(lane == 1, i2, 0))


def router(h, g, w_router, tm):
    m, d = h.shape
    wr = jnp.pad(w_router, ((0, 0), (0, LANES - N_EXPERTS)))
    return pl.pallas_call(
        _router_kernel,
        grid=(m // tm,),
        in_specs=[pl.BlockSpec((tm, d), lambda i: (i, 0)),
                  pl.BlockSpec((1, d), lambda i: (0, 0)),
                  pl.BlockSpec((d, LANES), lambda i: (0, 0))],
        out_specs=[pl.BlockSpec((tm, d), lambda i: (i, 0)),
                   pl.BlockSpec((tm, LANES), lambda i: (i, 0)),
                   pl.BlockSpec((tm, LANES), lambda i: (i, 0))],
        out_shape=[jax.ShapeDtypeStruct((m, d), F32),
                   jax.ShapeDtypeStruct((m, LANES), F32),
                   jax.ShapeDtypeStruct((m, LANES), jnp.int32)],
        compiler_params=pltpu.CompilerParams(
            dimension_semantics=("parallel",), vmem_limit_bytes=VMEM_LIMIT),
        name="router",
    )(h, g.reshape(1, d), wr)


def _row_copies(src_hbm, dst_hbm, idx_ref, base, buf, sem, n_rows, gather, wait):
    def body(r, carry):
        row = idx_ref[base + r]
        if gather:
            cp = pltpu.make_async_copy(src_hbm.at[pl.ds(row, 1)], buf.at[pl.ds(r, 1)], sem)
        else:
            cp = pltpu.make_async_copy(buf.at[pl.ds(r, 1)], dst_hbm.at[pl.ds(row, 1)], sem)
        if wait:
            cp.wait()
        else:
            cp.start()
        return carry
    lax.fori_loop(0, n_rows, body, 0, unroll=8)


def _expert_kernel(te_ref, nt_ref, src_ref, dst_ref, u_hbm, wg_ref, wu_ref, wd_ref, y_hbm,
                   xbuf, xb_ref, acc_ref, obuf, gsem, ssem, *, tm):
    i = pl.program_id(0)
    f = pl.program_id(1)
    last_f = pl.num_programs(1) - 1
    nt = nt_ref[0]
    gather = functools.partial(_row_copies, u_hbm, None, src_ref, gather=True, n_rows=tm)
    scatter = functools.partial(_row_copies, None, y_hbm, dst_ref, buf=obuf, sem=ssem.at[0], gather=False, n_rows=tm)

    @pl.when(jnp.logical_and(i < nt, f == 0))
    def _():
        slot = i % 2

        @pl.when(i == 0)
        def _():
            gather(base=0, buf=xbuf.at[0], sem=gsem.at[0], wait=False)
            obuf[...] = jnp.zeros_like(obuf)
            fill = pltpu.make_async_copy(obuf, y_hbm.at[pl.ds(y_hbm.shape[0] - tm, tm)], ssem.at[0])
            fill.start()
            fill.wait()

        gather(base=i * tm, buf=xbuf.at[slot], sem=gsem.at[slot], wait=True)

        @pl.when(i + 1 < nt)
        def _():
            gather(base=(i + 1) * tm, buf=xbuf.at[1 - slot], sem=gsem.at[1 - slot], wait=False)

        xb_ref[...] = xbuf[slot].astype(BF16)

    @pl.when(i < nt)
    def _():
        x = xb_ref[...]
        a = jax.nn.silu(_dot(x, wg_ref[0]))
        b = _dot(x, wu_ref[0])
        y = _dot((a * b).astype(BF16), wd_ref[0])

        @pl.when(f == 0)
        def _():
            acc_ref[...] = y

        @pl.when(f != 0)
        def _():
            acc_ref[...] += y

        @pl.when(f == last_f)
        def _():
            @pl.when(i > 0)
            def _():
                scatter(base=(i - 1) * tm, wait=True)

            obuf[...] = acc_ref[...]
            scatter(base=i * tm, wait=False)

            @pl.when(i == nt - 1)
            def _():
                scatter(base=i * tm, wait=True)


def experts(u, tile_expert, n_tiles, src_tok, dst_row, n_out_rows, wg, wu, wd, tm, tf):
    d = u.shape[1]
    ff = wg.shape[2]
    n_row_tiles = tile_expert.shape[0]
    wmap = lambda i, f, te, nt, src, dst: (te[i], 0, f)
    wbuf = pl.Buffered(1) if tf == ff else None
    grid_spec = pltpu.PrefetchScalarGridSpec(
        num_scalar_prefetch=4,
        grid=(n_row_tiles, ff // tf),
        in_specs=[pl.BlockSpec(memory_space=pl.ANY),
                  pl.BlockSpec((1, d, tf), wmap, pipeline_mode=wbuf),
                  pl.BlockSpec((1, d, tf), wmap, pipeline_mode=wbuf),
                  pl.BlockSpec((1, tf, d), lambda i, f, te, nt, src, dst: (te[i], f, 0), pipeline_mode=wbuf)],
        out_specs=pl.BlockSpec(memory_space=pl.ANY),
        scratch_shapes=[pltpu.VMEM((2, tm, d), F32), pltpu.VMEM((tm, d), BF16), pltpu.VMEM((tm, d), F32),
                        pltpu.VMEM((tm, d), F32), pltpu.SemaphoreType.DMA((2,)), pltpu.SemaphoreType.DMA((1,))],
    )
    return pl.pallas_call(
        functools.partial(_expert_kernel, tm=tm),
        grid_spec=grid_spec,
        out_shape=jax.ShapeDtypeStruct((n_out_rows, d), F32),
        compiler_params=pltpu.CompilerParams(
            dimension_semantics=("arbitrary", "arbitrary"), vmem_limit_bytes=VMEM_LIMIT,
            disable_bounds_checks=True),
        name="experts",
    )(tile_expert, n_tiles, src_tok, dst_row, u, wg, wu, wd)


def moe(h, g, w_router, wg, wu, wd, tm_route, tm_e, tf):
    m, d = h.shape
    u, top_w, top_i = router(h, g, w_router, tm_route)
    n_pairs = TOP_K * m
    flat_e = top_i[:, :TOP_K].reshape(-1)
    onehot = (flat_e[:, None] == jnp.arange(N_EXPERTS, dtype=jnp.int32)[None, :]).astype(jnp.int32)
    rank = jnp.sum((jnp.cumsum(onehot, axis=0) - 1) * onehot, axis=1)
    counts = jnp.sum(onehot, axis=0)
    tiles_per = (counts + tm_e - 1) // tm_e
    tile_end = jnp.cumsum(tiles_per)
    tile_start = tile_end - tiles_per
    grouped_row = tile_start[flat_e] * tm_e + rank
    n_rows = n_pairs + N_EXPERTS * tm_e
    n_row_tiles = n_rows // tm_e
    pair_ids = jnp.arange(n_pairs, dtype=jnp.int32)
    pair_at = jnp.full((n_rows,), -1, jnp.int32).at[grouped_row].set(pair_ids)
    spare = n_pairs + jnp.arange(n_rows, dtype=jnp.int32) % tm_e
    src_tok = jnp.where(pair_at >= 0, pair_at // TOP_K, 0)
    dst_row = jnp.where(pair_at >= 0, (pair_at % TOP_K) * m + pair_at // TOP_K, spare)
    tile_ids = jnp.arange(n_row_tiles, dtype=jnp.int32)
    tile_expert = jnp.minimum(jnp.sum((tile_ids[:, None] >= tile_end[None, :]).astype(jnp.int32), axis=1),
                              N_EXPERTS - 1).astype(jnp.int32)
    n_tiles = tile_end[-1:].astype(jnp.int32)
    last_e = tile_expert[jnp.maximum(n_tiles[0] - 1, 0)]
    tile_expert = jnp.where(tile_ids < n_tiles[0], tile_expert, last_e)
    y = experts(u, tile_expert, n_tiles, src_tok, dst_row, n_pairs + tm_e, wg, wu, wd, tm_e, tf)
    return y, top_w


def _ple_kernel(*refs, final, combine, n_first):
    if combine:
        h_ref, y0_ref, y1_ref, tw_ref, *refs = refs
        h = h_ref[...] + (tw_ref[:, 0:1] * y0_ref[...] + tw_ref[:, 1:2] * y1_ref[...])
    else:
        h_ref, *refs = refs
        h = h_ref[...]
    pp_ref, ps_ref, g_ref, wg_ref, wp_ref, gf_ref, *o_refs = refs
    first = pl.program_id(0) < n_first
    p = jnp.where(first, pp_ref[...], ps_ref[...])
    v = _rms(h, g_ref[...]).astype(BF16)
    pg = jax.nn.sigmoid(_dot(v, wg_ref[...]))
    e = _dot(p.astype(BF16), wp_ref[...])
    out = h + pg * e
    if not final:
        o_refs[0][...] = out
        return
    out = _rms(out, gf_ref[...])

    @pl.when(first)
    def _():
        o_refs[0][...] = out

    @pl.when(jnp.logical_not(first))
    def _():
        o_refs[1][...] = out


def ple(h, p_groups, g, w_gate, w_p, g_final, tm, final, expert_out=None):
    m, d = h.shape
    dp = p_groups[0].shape[1]
    n_first = p_groups[0].shape[0] // tm
    assert p_groups[0].shape[0] % tm == 0 and p_groups[1].shape[0] % tm == 0
    rows = lambda wd: pl.BlockSpec((tm, wd), lambda i: (i, 0))
    whole = lambda a, b: pl.BlockSpec((a, b), lambda i: (0, 0))
    extra, extra_specs = (), []
    if expert_out is not None:
        y, top_w = expert_out
        extra = (y, y, top_w)
        extra_specs = [rows(d), pl.BlockSpec((tm, d), lambda i: (i + m // tm, 0)), rows(LANES)]
    if final:
        out_specs = _group_specs(tm, d, n_first)
        out_shape = [jax.ShapeDtypeStruct((a.shape[0], d), F32) for a in p_groups]
    else:
        out_specs, out_shape = rows(d), jax.ShapeDtypeStruct((m, d), F32)
    return pl.pallas_call(
        functools.partial(_ple_kernel, final=final, combine=expert_out is not None, n_first=n_first),
        grid=(m // tm,),
        in_specs=([rows(d)] + extra_specs + _group_in_specs(p_groups, tm, n_first)
                  + [whole(1, d), whole(d, d), whole(dp, d), whole(1, d)]),
        out_specs=out_specs,
        out_shape=out_shape,
        compiler_params=pltpu.CompilerParams(
            dimension_semantics=("parallel",), vmem_limit_bytes=VMEM_LIMIT),
        name="ple",
    )(h, *extra, *[_operand(p) for p in p_groups], g.reshape(1, d), w_gate, w_p, g_final.reshape(1, d))


STATE_KEYS = ('gdn_conv', 'gdn', 'hgrn', 'ssd_conv', 'ssd', 'ml_c', 'ml_n', 'ml_m')
MIXER_KEYS = ('gdn_conv_w', 'gdn_a_log', 'gdn_dt_bias', 'gdn_norm', 'hgrn_norm',
              'ssd_conv_w', 'ssd_conv_b', 'ssd_a_log', 'ssd_dt_bias', 'ssd_d', 'ssd_norm',
              'ml_ig_b', 'ml_fg_b', 'ml_norm')


MIXER_SPECS = (gdn_spec, hgrn_spec, ssd_spec, mlstm_spec)
_gdn_kernel.n_stages = lambda t_len: 11 + 2 * (t_len.bit_length() - 2)
_hgrn_kernel.n_stages = lambda t_len: 1 + HG_H * (len(_hgrn_levels(t_len)) // 2 + 1)
_ssd_kernel.n_stages = lambda t_len: 8
_mlstm_kernel.n_stages = lambda t_len: 11


def _mixers(proj, row_blk0, st, w, lb, **grid):
    built = [f(st, w, lb, grid['n_seg']) for f in MIXER_SPECS]
    o, *new_states = _mixer_call([b[0] for b in built], proj, row_blk0, **grid)
    new, at = {}, 0
    for spec, finish in built:
        n = len(spec[2])
        new.update(finish(*new_states[at:at + n]))
        at += n
    return o, new


def kernel(x_prompt, x_sample, state_gdn_conv, state_gdn, state_hgrn, state_ssd_conv, state_ssd, state_mlstm_c, state_mlstm_n, state_mlstm_m, p_prompt, p_sample, g_mix, w_in, gdn_conv_w, gdn_a_log, gdn_dt_bias, gdn_norm, hgrn_lb, hgrn_norm, ssd_conv_w, ssd_conv_b, ssd_a_log, ssd_dt_bias, ssd_d, ssd_norm, ml_ig_b, ml_fg_b, ml_norm, w_branch, w_out, g_ffn, w_ff_gate, w_ff_up, w_ff_down, w_router, w_ex_gate, w_ex_up, w_ex_down, w_ple, w_ple_gate, g_ple, g_final):
    prm = {'gdn_conv_w': gdn_conv_w, 'gdn_a_log': gdn_a_log, 'gdn_dt_bias': gdn_dt_bias,
           'gdn_norm': gdn_norm, 'hgrn_norm': hgrn_norm, 'ssd_conv_w': ssd_conv_w,
           'ssd_conv_b': ssd_conv_b, 'ssd_a_log': ssd_a_log, 'ssd_dt_bias': ssd_dt_bias,
           'ssd_d': ssd_d, 'ssd_norm': ssd_norm, 'ml_ig_b': ml_ig_b, 'ml_fg_b': ml_fg_b,
           'ml_norm': ml_norm}
    bp, lp, d = x_prompt.shape
    bs, ls, _ = x_sample.shape
    mp, ms = bp * lp, bs * ls
    assert lp % CH == 0 and CH % ls == 0 and ms % CH == 0 and mp % CH == 0
    st_s = {'gdn_conv': state_gdn_conv, 'gdn': state_gdn, 'hgrn': state_hgrn, 'ssd_conv': state_ssd_conv,
            'ssd': state_ssd, 'ml_c': state_mlstm_c, 'ml_n': state_mlstm_n, 'ml_m': state_mlstm_m}
    st_p = {k: jnp.zeros((bp,) + v.shape[2:], F32) for k, v in st_s.items()}
    grid_p = dict(n_outer=bp, n_chunks=lp // CH, n_seg=1)
    grid_s = dict(n_outer=ms // CH, n_chunks=1, n_seg=CH // ls)

    sm = jax.nn.softmax(hgrn_lb, axis=0)
    lb_all = jnp.cumsum(sm, axis=0) - sm[0]

    h = (x_prompt.reshape(mp, d), x_sample.reshape(ms, d))
    new_p = {k: [] for k in STATE_KEYS}
    new_s = {k: [] for k in STATE_KEYS}
    for l in range(DEPTH):
        wl = {k: prm[k][l] for k in MIXER_KEYS}
        gates, mix = norm_matmul(h, g_mix[l], _permute_w_in(w_in[l]), tm=ROW_TILE, tn=PROJ_COL_TILE)
        br_p, np_ = _mixers(mix, 0, st_p, wl, lb_all[l], **grid_p)
        st_l = {k: (v[l] if k in ('ml_n', 'ml_m') else LayerOf(v, l)) for k, v in st_s.items()}
        br_s, ns_ = _mixers(mix, mp // CH, st_l, wl, lb_all[l], **grid_s)
        h = merge(h, gates, (br_p, br_s), w_branch[l].astype(BF16), w_out[l].astype(BF16), tm=ROW_TILE)
        j = l // 2
        final = l == DEPTH - 1
        p_groups = (LayerOf(p_prompt.reshape(DEPTH, mp, D_PLE), l), LayerOf(p_sample.reshape(DEPTH, ms, D_PLE), l))
        ple_w = (w_ple_gate[l].astype(BF16), w_ple[l].astype(BF16))
        if l % 2 == 0 and not final:
            h = ffn(h, g_ffn[l], w_ff_gate[j].astype(BF16), w_ff_up[j].astype(BF16),
                    w_ff_down[j].astype(BF16), tm=ROW_TILE, tf=D_FF, ple_term=(p_groups, g_ple[l], *ple_w))
        else:
            expert_out = None
            if l % 2 == 0:
                h = ffn(h, g_ffn[l], w_ff_gate[j].astype(BF16), w_ff_up[j].astype(BF16),
                        w_ff_down[j].astype(BF16), tm=ROW_TILE, tf=D_FF)
            else:
                expert_out = moe(h, g_ffn[l], w_router[j], w_ex_gate[j].astype(BF16), w_ex_up[j].astype(BF16),
                                 w_ex_down[j].astype(BF16), tm_route=ROW_TILE, tm_e=ROW_TILE, tf=D_FF)
            h = ple(h, p_groups, g_ple[l], *ple_w, g_final, tm=ROW_TILE, final=final, expert_out=expert_out)
        for k in STATE_KEYS:
            new_p[k].append(np_[k])
            new_s[k].append(ns_[k])
    y_prompt = h[0].reshape(bp, lp, d)
    y_sample = h[1].reshape(bs, ls, d)
    sp = {k: jnp.stack(v) for k, v in new_p.items()}
    ss = {k: jnp.stack(v) for k, v in new_s.items()}
    return (y_prompt, y_sample,
            sp['gdn_conv'], sp['gdn'], sp['hgrn'], sp['ssd_conv'], sp['ssd'], sp['ml_c'], sp['ml_n'], sp['ml_m'],
            ss['gdn_conv'], ss['gdn'], ss['hgrn'], ss['ssd_conv'], ss['ssd'], ss['ml_c'], ss['ml_n'], ss['ml_m'])
```

```python
import functools

import numpy as np
import jax
import jax.numpy as jnp
from jax import lax
from jax.experimental import pallas as pl
from jax.experimental.pallas import tpu as pltpu

F32 = jnp.float32
BF16 = jnp.bfloat16
HI = lax.Precision.HIGHEST

D_MODEL = 1024
DEPTH = 2
D_PLE = 256
N_BRANCH = 4
BRANCH_W = D_MODEL // N_BRANCH
CONV_W = 4
EPS = 1e-6

GDN_H = 4
GDN_DK = 64
GDN_QK = GDN_H * GDN_DK
GDN_CONV_CH = 2 * GDN_QK + BRANCH_W
HG_H = 4
HG_QK = 256
SSD_H = 4
SSD_G = 2
SSD_N = 128
SSD_CONV_CH = BRANCH_W + 2 * SSD_G * SSD_N
ML_H = 4
ML_DK = 32
ML_QK = ML_H * ML_DK
D_FF = ((8 * D_MODEL // 3 + 255) // 256) * 256
N_EXPERTS = 8
TOP_K = 2

_REF_SPLITS = (('gdn_in', GDN_CONV_CH), ('gdn_b', GDN_H), ('gdn_a', GDN_H), ('gdn_z', BRANCH_W),
               ('hg_q', HG_QK), ('hg_f', HG_QK), ('hg_v', BRANCH_W), ('hg_g', BRANCH_W),
               ('ssd_z', BRANCH_W), ('ssd_in', SSD_CONV_CH), ('ssd_dt', SSD_H),
               ('ml_q', ML_QK), ('ml_k', ML_QK), ('ml_v', BRANCH_W), ('ml_i', ML_H), ('ml_f', ML_H),
               ('ml_o', BRANCH_W), ('gates', N_BRANCH * D_MODEL))
_MY_ORDER = ('gdn_in', 'ssd_in', 'gdn_z', 'ssd_z', 'hg_q', 'hg_f', 'hg_v', 'hg_g',
             'ml_v', 'ml_o', 'ml_q', 'ml_k', 'gdn_b', 'gdn_a', 'ssd_dt', 'ml_i', 'ml_f')
LANES = 128
MXU_TILE = 256
CH = 64
N_GATES = N_BRANCH * D_MODEL


def _layout():
    widths = dict(_REF_SPLITS)
    off, out = 0, {}
    for name in _MY_ORDER:
        out[name] = (off, widths[name])
        off += widths[name]
    return out, -(-off // MXU_TILE) * MXU_TILE


COLS, N_MIX = _layout()
GATE_COL0 = COLS['gdn_b'][0]
L_GDN_B, L_GDN_A, L_SSD_DT, L_ML_I, L_ML_F = (COLS[n][0] - GATE_COL0 for n in ('gdn_b', 'gdn_a', 'ssd_dt', 'ml_i', 'ml_f'))
VMEM_LIMIT = 56 * 1024 * 1024
ROW_TILE = 512
PROJ_COL_TILE = 4 * MXU_TILE


def _ref_offsets():
    off, acc = {}, 0
    for name, wd in _REF_SPLITS:
        off[name] = acc
        acc += wd
    return off, acc


def _permute_kernel(w_ref, o_ref):
    ref_off, _ = _ref_offsets()
    tk = o_ref.shape[0]
    o_ref[:, :N_GATES] = w_ref[:, ref_off['gates']:ref_off['gates'] + N_GATES].astype(BF16)
    lane = lax.broadcasted_iota(jnp.int32, (tk, LANES), 1)
    gate_group = jnp.zeros((tk, LANES), F32)
    for name in _MY_ORDER:
        dst, wd = COLS[name]
        src = ref_off[name]
        if wd >= LANES:
            o_ref[:, N_GATES + dst:N_GATES + dst + wd] = w_ref[:, src:src + wd].astype(BF16)
        else:
            win0, at = src // LANES * LANES, src % LANES
            to = dst - GATE_COL0
            assert at + wd <= LANES and to + wd <= LANES
            window = pltpu.roll(w_ref[:, win0:win0 + LANES], (to - at) % LANES, axis=1)
            gate_group = jnp.where(jnp.logical_and(lane >= to, lane < to + wd), window, gate_group)
    o_ref[:, N_GATES + GATE_COL0:N_GATES + GATE_COL0 + LANES] = gate_group.astype(BF16)
    pad0 = N_GATES + GATE_COL0 + LANES
    o_ref[:, pad0:] = jnp.zeros((tk, N_GATES + N_MIX - pad0), BF16)


def _permute_w_in(w, tk=256):
    d, n_in = w.shape
    return pl.pallas_call(
        _permute_kernel, grid=(d // tk,),
        in_specs=[pl.BlockSpec((tk, n_in), lambda i: (i, 0))],
        out_specs=pl.BlockSpec((tk, N_GATES + N_MIX), lambda i: (i, 0)),
        out_shape=jax.ShapeDtypeStruct((d, N_GATES + N_MIX), BF16),
        compiler_params=pltpu.CompilerParams(dimension_semantics=("parallel",), vmem_limit_bytes=VMEM_LIMIT),
        name="permute_w_in",
    )(w)


def _rms(x, g):
    return x * lax.rsqrt(jnp.mean(x * x, axis=-1, keepdims=True) + EPS) * g


def _dot(a, b, **kw):
    return jnp.dot(a, b, preferred_element_type=F32, **kw)


def _dot_nt(a, b, **kw):
    return lax.dot_general(a, b, (((1,), (1,)), ((), ())), preferred_element_type=F32, **kw)


def _dot_tn(a, b, **kw):
    return lax.dot_general(a, b, (((0,), (0,)), ((), ())), preferred_element_type=F32, **kw)


def _rows_of(refs, n_first):
    if n_first is None:
        return refs[0][...]
    return jnp.where(pl.program_id(0) < n_first, refs[0][...], refs[1][...])


def _row_inputs(x, tm):
    if isinstance(x, tuple):
        assert x[0].shape[0] % tm == 0 and x[1].shape[0] % tm == 0
        n_first = x[0].shape[0] // tm
        return _group_in_specs(x, tm, n_first), list(x), n_first
    return [pl.BlockSpec((tm, x.shape[1]), lambda i: (i, 0))], [x], None


def _norm_matmul_kernel(*refs, tn, n_first):
    *x_refs, g_ref, w_ref, gates_ref, mix_ref = refs
    xn = _rms(_rows_of(x_refs, n_first), g_ref[...]).astype(BF16)
    for j in range(N_GATES // tn):
        gates_ref[:, j * tn:(j + 1) * tn] = _dot(xn, w_ref[:, j * tn:(j + 1) * tn]).astype(BF16)
    for j in range(N_MIX // tn):
        mix_ref[:, j * tn:(j + 1) * tn] = _dot(xn, w_ref[:, N_GATES + j * tn:N_GATES + (j + 1) * tn])


def norm_matmul(x, g, w, tm, tn):
    x_specs, x_ops, n_first = _row_inputs(x, tm)
    m, d = sum(a.shape[0] for a in x_ops), x_ops[0].shape[1]
    assert w.shape[1] == N_GATES + N_MIX and N_GATES % tn == 0 and N_MIX % tn == 0
    return pl.pallas_call(
        functools.partial(_norm_matmul_kernel, tn=tn, n_first=n_first),
        grid=(m // tm,),
        in_specs=[*x_specs,
                  pl.BlockSpec((1, d), lambda i: (0, 0)),
                  pl.BlockSpec((d, N_GATES + N_MIX), lambda i: (0, 0), pipeline_mode=pl.Buffered(1))],
        out_specs=[pl.BlockSpec((tm, N_GATES), lambda i: (i, 0)),
                   pl.BlockSpec((tm, N_MIX), lambda i: (i, 0))],
        out_shape=[jax.ShapeDtypeStruct((m, N_GATES), BF16), jax.ShapeDtypeStruct((m, N_MIX), F32)],
        compiler_params=pltpu.CompilerParams(
            dimension_semantics=("parallel",), vmem_limit_bytes=VMEM_LIMIT),
        name="norm_matmul",
    )(*x_ops, g.reshape(1, d), w)


def _seg_masks(t_len):
    row = lax.broadcasted_iota(jnp.int32, (CH, CH), 0)
    col = lax.broadcasted_iota(jnp.int32, (CH, CH), 1)
    same = (row // t_len) == (col // t_len)
    tri = jnp.logical_and(same, col <= row)
    strict = jnp.logical_and(same, col < row)
    return same, tri, strict


def _row_forms(x, n_rows=24):
    r = lax.broadcasted_iota(jnp.int32, (n_rows, LANES), 0)
    l = lax.broadcasted_iota(jnp.int32, (n_rows, LANES), 1)
    return _dot_nt((r == l).astype(F32), x, precision=HI)


def _softplus(x):
    return jnp.maximum(x, 0.0) + jnp.log1p(jnp.exp(-jnp.abs(x)))


def _split2(x):
    hi = x.astype(BF16)
    return hi, (x - hi.astype(F32)).astype(BF16)


def _dot3(a, b):
    return _dot(a[0], b[0]) + (_dot(a[0], b[1]) + _dot(a[1], b[0]))


def _masked_exp(d, mask):
    return jnp.where(mask, jnp.exp(jnp.where(mask, d, 0.0)), 0.0)


def _conv_silu(x, ext_scr, cw_ref, bias, t_len, n_seg):
    w = x.shape[-1]
    ext_scr[:, 8:8 + t_len, :] = x.reshape(n_seg, t_len, w)
    y = cw_ref[3:4, :] * x
    for j in range(1, CONV_W):
        y = y + cw_ref[3 - j:4 - j, :] * ext_scr[:, 8 - j:8 - j + t_len, :].reshape(CH, w)
    if bias is not None:
        y = y + bias
    tail = ext_scr[:, 5 + t_len:8 + t_len, :]
    ext_scr[:, 5:8, :] = tail
    return jax.nn.silu(y), tail


def _halves(xp, lo):
    s_lo = jnp.sum(jnp.where(lo, xp, 0.0), axis=-1, keepdims=True)
    s_hi = jnp.sum(jnp.where(lo, 0.0, xp), axis=-1, keepdims=True)
    return jnp.where(lo, s_lo, s_hi)


def _head_rmsnorm(xp, lo):
    return xp * lax.rsqrt(_halves(xp * xp, lo) * (1.0 / 64) + EPS)


def _head_l2norm(xp, lo):
    return xp * lax.rsqrt(_halves(xp * xp, lo) + EPS)


def _seg_sum(parts, rowi, t_len):
    if len(parts) == 1:
        return parts[0]
    acc = jnp.where(rowi // t_len == 0, parts[0], 0.0)
    for s in range(1, len(parts)):
        acc = acc + jnp.where(rowi // t_len == s, parts[s], 0.0)
    return acc


def _seg_rows(x, rowi, t_len, s, n_seg):
    return x if n_seg == 1 else jnp.where(rowi // t_len == s, x, 0.0)


def _quarter_sel(idx, width, vals):
    out = vals[3]
    for h in (2, 1, 0):
        out = jnp.where(idx < (h + 1) * width, vals[h], out)
    return out


def _gate_rows(pairs):
    t = jnp.zeros((8, LANES), F32)
    for r, (off, v) in enumerate(pairs):
        t = t.at[r, off:off + v.shape[0]].set(v.astype(F32))
    return t


def _block_diag(blocks):
    n = len(blocks)
    z = jnp.zeros_like(blocks[0])
    return jnp.concatenate(
        [jnp.concatenate([blocks[i] if i == j else z for j in range(n)], axis=1) for i in range(n)], axis=0)


def _transpose64(x):
    r = lax.broadcasted_iota(jnp.int32, x.shape, 0)
    c = lax.broadcasted_iota(jnp.int32, x.shape, 1)
    eye = (r == c).astype(BF16)
    hi, mid, lo = _split3(x)
    return (_dot_nt(eye, hi) + _dot_nt(eye, mid)) + _dot_nt(eye, lo)


def _load_pair_states(s0_ref, s_scr, n_seg, transpose):
    prep = _transpose64 if transpose else (lambda t: t)
    for s in range(n_seg):
        for p in range(2):
            s_scr[s, p] = _block_diag([prep(s0_ref[s, 2 * p]), prep(s0_ref[s, 2 * p + 1])])


def _store_pair_states(s_scr, sn_ref, n_seg, transpose):
    prep = _transpose64 if transpose else (lambda t: t)
    for s in range(n_seg):
        for p in range(2):
            sn_ref[s, 2 * p] = prep(s_scr[s, p, 0:64, 0:64])
            sn_ref[s, 2 * p + 1] = prep(s_scr[s, p, 64:128, 64:128])


class LayerOf:
    def __init__(self, stack, layer):
        self.stack, self.layer = stack, layer
        self.shape, self.ndim = stack.shape[1:], stack.ndim - 1


def _operand(a):
    return a.stack if isinstance(a, LayerOf) else a


def _row_block_spec(a, block_rows, index_of_step):
    rest = (0,) * (a.ndim - 1)
    if isinstance(a, LayerOf):
        return pl.BlockSpec((None, block_rows) + a.shape[1:], lambda *ids: (a.layer, index_of_step(*ids)) + rest)
    return pl.BlockSpec((block_rows,) + a.shape[1:], lambda *ids: (index_of_step(*ids),) + rest)


def _interleave(gens, n_stages):
    results = [None] * len(gens)
    pos = [0] * len(gens)
    live = set(range(len(gens)))
    while live:
        k = min(live, key=lambda i: ((pos[i] + 1) / n_stages[i], i))
        try:
            next(gens[k])
            pos[k] += 1
        except StopIteration as stop:
            results[k] = stop.value
            live.remove(k)
    return results


def _fused_mixer_kernel(*refs, n_seg, parts):
    tot = [sum(p[j] for p in parts) for j in range(1, 5)]
    ins, rest = refs[:tot[0]], refs[tot[0]:]
    sts, rest = rest[:tot[1]], rest[tot[1]:]
    prs, rest = rest[:tot[2]], rest[tot[2]:]
    o_ref, rest = rest[0], rest[1:]
    outs, scr = rest[:tot[1]], rest[tot[1]:]
    at = [0, 0, 0, 0]
    calls = []
    for k, (body, n_in, n_st, n_pr, n_scr) in enumerate(parts):
        take = lambda seq, j, n: seq[at[j]:at[j] + n]
        calls.append(functools.partial(
            body, *take(ins, 0, n_in), *take(sts, 1, n_st), *take(prs, 2, n_pr),
            o_ref.at[:, k * BRANCH_W:(k + 1) * BRANCH_W],
            *take(outs, 1, n_st), *take(scr, 3, n_scr), n_seg=n_seg))
        for j, n in enumerate((n_in, n_st, n_pr, n_scr)):
            at[j] += n

    n_stages = [p[0].n_stages(CH // n_seg) for p in parts]

    @pl.when(pl.program_id(1) == 0)
    def _():
        _interleave([call(init=True) for call in calls], n_stages)

    finishers = _interleave([call(init=False) for call in calls], n_stages)

    @pl.when(pl.program_id(1) == pl.num_programs(1) - 1)
    def _():
        for fin in finishers:
            fin()


def _mixer_call(specs, proj, row_blk0, *, n_outer, n_chunks, n_seg):
    rows = n_outer * n_chunks * CH
    rmap = lambda blk: (lambda i, c: (row_blk0 + i * n_chunks + c, blk))
    full = lambda a: pl.BlockSpec(a.shape, lambda i, c: (0,) * a.ndim)
    sblk = lambda a: _row_block_spec(a, a.shape[0] // n_outer, lambda i, c: i)
    oblk = lambda a: pl.BlockSpec((a.shape[0] // n_outer,) + a.shape[1:], lambda i, c: (i,) + (0,) * (a.ndim - 1))
    in_blocks = [b for s in specs for b in s[1]]
    state_ins = [a for s in specs for a in s[2]]
    params = [a for s in specs for a in s[3]]
    scratch = [a for s in specs for a in s[4]]
    for off, wd in in_blocks:
        assert off % wd == 0
    parts = tuple((s[0], len(s[1]), len(s[2]), len(s[3]), len(s[4])) for s in specs)
    width = len(specs) * BRANCH_W
    return pl.pallas_call(
        functools.partial(_fused_mixer_kernel, n_seg=n_seg, parts=parts),
        grid=(n_outer, n_chunks),
        in_specs=([pl.BlockSpec((CH, wd), rmap(off // wd)) for off, wd in in_blocks]
                  + [sblk(a) for a in state_ins] + [full(a) for a in params]),
        out_specs=[pl.BlockSpec((CH, width), lambda i, c: (i * n_chunks + c, 0))] + [oblk(a) for a in state_ins],
        out_shape=([jax.ShapeDtypeStruct((rows, width), BF16)]
                   + [jax.ShapeDtypeStruct(a.shape, F32) for a in state_ins]),
        scratch_shapes=scratch,
        compiler_params=pltpu.CompilerParams(
            dimension_semantics=("parallel", "arbitrary"), vmem_limit_bytes=VMEM_LIMIT),
        name="token_mixers",
    )(*([proj] * len(in_blocks)), *[_operand(a) for a in state_ins], *params)


def _gdn_kernel(xin_ref, z_ref, sm_ref, conv0_ref, s0_ref, cw_ref, gp_ref, ng_ref,
                o_ref, convn_ref, sn_ref, ext_scr, s_scr, *, n_seg, init):
    t_len = CH // n_seg
    if init:
        _load_pair_states(s0_ref, s_scr, n_seg, transpose=False)
        ext_scr[:, 5:8, :] = conv0_ref[...]
        return None

    xc, tail = _conv_silu(xin_ref[...], ext_scr, cw_ref, None, t_len, n_seg)
    yield
    same, tri, strict = _seg_masks(t_len)
    lane = lax.broadcasted_iota(jnp.int32, (CH, LANES), 1)
    rowi = lax.broadcasted_iota(jnp.int32, (CH, 1), 0)
    lo = lane < 64
    sm = sm_ref[...]
    beta = jax.nn.sigmoid(sm)
    gl = jnp.logical_and(lane >= L_GDN_A, lane < L_GDN_A + GDN_H)
    g = jnp.where(gl, -jnp.exp(gp_ref[0:1, :]) * _softplus(sm + gp_ref[1:2, :]), 0.0)
    gam = _dot(tri.astype(F32), g, precision=HI)
    gam_end = _dot(same.astype(F32), g, precision=HI)
    gam_r = _row_forms(gam, 8)
    yield
    r128 = lax.broadcasted_iota(jnp.int32, (LANES, LANES), 0)
    l128 = lax.broadcasted_iota(jnp.int32, (LANES, LANES), 1)
    bd = (r128 < 64) == (l128 < 64)
    rsel = lax.broadcasted_iota(jnp.int32, (LANES, 1), 0) < 64
    qs_, ks_, atts_, gcs_, a_, x_ = [], [], [], [], [], []
    for p in range(2):
        q_p = _head_l2norm(xc[:, 128 * p:128 * (p + 1)], lo) * (GDN_DK ** -0.5)
        k_p = _head_l2norm(xc[:, GDN_QK + 128 * p:GDN_QK + 128 * (p + 1)], lo)
        v_p = xc[:, 2 * GDN_QK + 128 * p:2 * GDN_QK + 128 * (p + 1)]
        kb = k_p.astype(BF16)
        qs_.append(q_p)
        ks_.append(k_p)
        for j in range(2):
            h = 2 * p + j
            mj = lo if j == 0 else jnp.logical_not(lo)
            kk = _dot_nt(jnp.where(mj, k_p, 0.0).astype(BF16), kb)
            qk = _dot_nt(jnp.where(mj, q_p, 0.0).astype(BF16), kb)
            gc = gam[:, L_GDN_A + h:L_GDN_A + h + 1]
            bc = beta[:, L_GDN_B + h:L_GDN_B + h + 1]
            dec = _masked_exp(gc - gam_r[L_GDN_A + h:L_GDN_A + h + 1, :], tri)
            a_.append(jnp.where(strict, bc * kk * dec, 0.0))
            x_.append(jnp.concatenate([jnp.where(mj, bc * v_p, 0.0),
                                       jnp.where(mj, (bc * jnp.exp(gc)) * k_p, 0.0)], axis=-1))
            atts_.append(qk * dec)
            gcs_.append(gc)
            yield
    eye = jnp.logical_and(tri, jnp.logical_not(strict)).astype(F32)
    sa = [_split2(a) for a in a_]
    t_ = [eye - a for a in a_]
    n = 2
    while n < t_len:
        sa = [_split2(_dot3(s, s)) for s in sa]
        yield
        t_ = [t + _dot3(_split2(t), s) for t, s in zip(t_, sa)]
        yield
        n *= 2
    x_ = [_dot3(_split2(t), _split2(x)) for t, x in zip(t_, x_)]
    yield
    outs = []
    for p in range(2):
        q_p, k_p = qs_[p], ks_[p]
        solv = x_[2 * p][:, :LANES] + x_[2 * p + 1][:, :LANES]
        solk = x_[2 * p][:, LANES:] + x_[2 * p + 1][:, LANES:]
        atts, gcs = atts_[2 * p:2 * p + 2], gcs_[2 * p:2 * p + 2]
        solk_b = solk.astype(BF16)
        qb = q_p.astype(BF16)
        u = solv - _seg_sum([_dot(solk_b, s_scr[s, p].astype(BF16)) for s in range(n_seg)], rowi, t_len)
        qs = _seg_sum([_dot(qb, s_scr[s, p].astype(BF16)) for s in range(n_seg)], rowi, t_len)
        o = jnp.where(lo, jnp.exp(gcs[0]), jnp.exp(gcs[1])) * qs
        for j in range(2):
            mj = lo if j == 0 else jnp.logical_not(lo)
            o = o + _dot(atts[j].astype(BF16), jnp.where(mj, u, 0.0).astype(BF16))
        yield
        ge0 = gam_end[:, L_GDN_A + 2 * p:L_GDN_A + 2 * p + 1]
        ge1 = gam_end[:, L_GDN_A + 2 * p + 1:L_GDN_A + 2 * p + 2]
        kw = k_p * jnp.where(lo, jnp.exp(ge0 - gcs[0]), jnp.exp(ge1 - gcs[1]))
        ub = u.astype(BF16)
        for s in range(n_seg):
            r0 = s * t_len
            dec_s = jnp.where(rsel, jnp.exp(ge0[r0:r0 + 1, :]), jnp.exp(ge1[r0:r0 + 1, :]))
            upd = _dot_tn(_seg_rows(kw, rowi, t_len, s, n_seg).astype(BF16), ub)
            s_scr[s, p] = dec_s * s_scr[s, p] + jnp.where(bd, upd, 0.0)
        outs.append(_head_rmsnorm(o, lo))
        yield
    o_all = jnp.concatenate(outs, axis=-1) * ng_ref[...] * jax.nn.silu(z_ref[...])
    o_ref[...] = o_all.astype(BF16)

    def finish():
        convn_ref[...] = tail
        _store_pair_states(s_scr, sn_ref, n_seg, transpose=False)
    return finish


def gdn_spec(st, w, lb, n_seg):
    t_len = CH // n_seg
    gp = _gate_rows([(L_GDN_A, w['gdn_a_log']), (L_GDN_A, w['gdn_dt_bias'])])
    ng = jnp.tile(w['gdn_norm'], GDN_H).reshape(1, BRANCH_W)
    spec = (_gdn_kernel, [COLS['gdn_in'], COLS['gdn_z'], (GATE_COL0, LANES)],
            [st['gdn_conv'], st['gdn']], [w['gdn_conv_w'], gp, ng],
            [pltpu.VMEM((n_seg, 8 + t_len, GDN_CONV_CH), F32), pltpu.VMEM((n_seg, 2, 128, 128), F32)])
    return spec, lambda convn, sn: {'gdn_conv': convn, 'gdn': sn}


def _hgrn_levels(t_len):
    lv, n = [], t_len
    while n >= 2:
        lv.append(n)
        n //= 2
    return lv


def _hgrn_cmat(t_len):
    t = np.arange(CH)[:, None]
    j = np.arange(CH)[None, :]
    same = (t // t_len) == (j // t_len)
    mats = [same & (j <= t), same]
    for n in _hgrn_levels(t_len):
        mid = (t // n) * n + n // 2
        second = t % n >= n // 2
        mats.append((second & (j >= mid) & (j <= t)) | (~second & (j > t) & (j <= mid - 1)))
    return jnp.asarray(np.concatenate(mats, axis=0).astype(np.float32), dtype=BF16)


def _split3(x):
    hi = x.astype(BF16)
    r = x - hi.astype(F32)
    mid = r.astype(BF16)
    return hi, mid, (r - mid.astype(F32)).astype(BF16)


def _hgrn_kernel(x_ref, s0_ref, cm_ref, lb_ref, ng_ref, o_ref, sn_ref, s_scr, *, n_seg, init):
    t_len = CH // n_seg
    levels = _hgrn_levels(t_len)
    if init:
        _load_pair_states(s0_ref, s_scr, n_seg, transpose=True)
        return None

    lane = lax.broadcasted_iota(jnp.int32, (CH, LANES), 1)
    rowi = lax.broadcasted_iota(jnp.int32, (CH, 1), 0)
    row = lax.broadcasted_iota(jnp.int32, (CH, CH), 0)
    col = lax.broadcasted_iota(jnp.int32, (CH, CH), 1)
    lo = lane < 64
    r128 = lax.broadcasted_iota(jnp.int32, (LANES, LANES), 0)
    l128 = lax.broadcasted_iota(jnp.int32, (LANES, LANES), 1)
    bd = (r128 < 64) == (l128 < 64)

    lb = lb_ref[...]
    f_pre = x_ref[:, HG_QK:2 * HG_QK]
    log_f = jnp.log(lb + (1.0 - lb) * jax.nn.sigmoid(f_pre))
    k_in = (1.0 - lb) * jax.nn.sigmoid(-f_pre)
    cm = cm_ref[...]
    ex = None
    for part in _split3(log_f):
        t = _dot(cm, part)
        ex = t if ex is None else ex + t
    b = ex[0:CH]
    b_end = ex[CH:2 * CH]
    yield
    outs = []
    for p in range(2):
        ls = slice(128 * p, 128 * (p + 1))
        q_p = x_ref[:, ls]
        k_p = k_in[:, ls]
        v_p = x_ref[:, 2 * HG_QK + 128 * p:2 * HG_QK + 128 * (p + 1)]
        qk = q_p * k_p
        qe = (q_p * jnp.exp(b[:, ls])).astype(BF16)
        o = _seg_sum([_dot_nt(qe, s_scr[s, p].astype(BF16)) for s in range(n_seg)], rowi, t_len)
        scales = [jnp.exp(ex[(2 + li) * CH:(3 + li) * CH, ls]) for li in range(len(levels))]
        for j in range(2):
            mj = lo if j == 0 else jnp.logical_not(lo)
            diag = jnp.sum(jnp.where(mj, qk, 0.0), axis=-1, keepdims=True)
            att = jnp.where(row == col, diag, 0.0)
            for li, n in enumerate(levels):
                tq = (rowi % n) >= (n // 2)
                qt = jnp.where(jnp.logical_and(mj, tq), q_p * scales[li], 0.0)
                kt = jnp.where(tq, 0.0, k_p * scales[li])
                att = att + jnp.where((row // n) == (col // n), _dot_nt(qt.astype(BF16), kt.astype(BF16)), 0.0)
                if li % 2 == 1:
                    yield
            o = o + _dot(att.astype(BF16), jnp.where(mj, v_p, 0.0).astype(BF16))
            yield
        kw = (k_p * jnp.exp(b_end[:, ls] - b[:, ls])).astype(BF16)
        for s in range(n_seg):
            r0 = s * t_len
            upd = _dot_tn(_seg_rows(v_p, rowi, t_len, s, n_seg).astype(BF16), kw)
            s_scr[s, p] = jnp.exp(b_end[r0:r0 + 1, ls]) * s_scr[s, p] + jnp.where(bd, upd, 0.0)
        outs.append(_head_rmsnorm(o, lo))
    o_all = jnp.concatenate(outs, axis=-1) * ng_ref[...] * jax.nn.silu(x_ref[:, 3 * HG_QK:4 * HG_QK])
    o_ref[...] = o_all.astype(BF16)

    def finish():
        _store_pair_states(s_scr, sn_ref, n_seg, transpose=True)
    return finish


def hgrn_spec(st, w, lb, n_seg):
    t_len = CH // n_seg
    ng = jnp.tile(w['hgrn_norm'], HG_H).reshape(1, BRANCH_W)
    assert COLS['hg_f'][0] == COLS['hg_q'][0] + HG_QK and COLS['hg_g'][0] == COLS['hg_q'][0] + 3 * HG_QK
    spec = (_hgrn_kernel, [(COLS['hg_q'][0], 4 * HG_QK)], [st['hgrn']],
            [_hgrn_cmat(t_len), lb.reshape(1, HG_QK), ng], [pltpu.VMEM((n_seg, 2, 128, 128), F32)])
    return spec, lambda sn: {'hgrn': sn}


def _ssd_kernel(xin_ref, z_ref, sm_ref, conv0_ref, h0_ref, cw_ref, cb_ref, gp_ref, dvec_ref, ng_ref,
                o_ref, convn_ref, hn_ref, ext_scr, h_scr, *, n_seg, init):
    t_len = CH // n_seg
    if init:
        h_scr[...] = h0_ref[...].reshape(n_seg, 2, 128, SSD_N)
        ext_scr[:, 5:8, :] = conv0_ref[...]
        return None

    xc, tail = _conv_silu(xin_ref[...], ext_scr, cw_ref, cb_ref[...], t_len, n_seg)
    yield
    sx, bm, cm = xc[:, :256], xc[:, 256:512], xc[:, 512:768]
    same, tri, _ = _seg_masks(t_len)
    lane = lax.broadcasted_iota(jnp.int32, (CH, LANES), 1)
    rowi = lax.broadcasted_iota(jnp.int32, (CH, 1), 0)
    gl = jnp.logical_and(lane >= L_SSD_DT, lane < L_SSD_DT + SSD_H)
    dt = jnp.where(gl, _softplus(sm_ref[...] + gp_ref[1:2, :]), 0.0)
    da = -jnp.exp(gp_ref[0:1, :]) * dt
    cum = _dot(tri.astype(F32), da, precision=HI)
    cum_end = _dot(same.astype(F32), da, precision=HI)
    cum_r = _row_forms(cum, 16)
    dt_r = _row_forms(dt, 16)
    yield
    lo = lane < 64
    rsel = lax.broadcasted_iota(jnp.int32, (LANES, 1), 0) < 64
    ys = []
    for g in range(SSD_G):
        cg = cm[:, 128 * g:128 * (g + 1)].astype(BF16)
        bg = bm[:, 128 * g:128 * (g + 1)].astype(BF16)
        sxp = sx[:, 128 * g:128 * (g + 1)]
        cb = _dot_nt(cg, bg)
        yst = _seg_sum([_dot_nt(cg, h_scr[s, g].astype(BF16)) for s in range(n_seg)], rowi, t_len)
        yatt = jnp.zeros((CH, LANES), F32)
        cols = []
        for j in range(2):
            l = L_SSD_DT + 2 * g + j
            cc = cum[:, l:l + 1]
            dec = _masked_exp(cc - cum_r[l:l + 1, :], tri)
            att = cb * dec * dt_r[l:l + 1, :]
            xm = jnp.where(lo if j == 0 else jnp.logical_not(lo), sxp, 0.0)
            yatt = yatt + _dot(att.astype(BF16), xm.astype(BF16))
            cols.append((cc, dt[:, l:l + 1] * jnp.exp(cum_end[:, l:l + 1] - cc)))
            yield
        ys.append(jnp.where(lo, jnp.exp(cols[0][0]), jnp.exp(cols[1][0])) * yst + yatt)
        xw = sxp * jnp.where(lo, cols[0][1], cols[1][1])
        for s in range(n_seg):
            r0 = s * t_len
            l = L_SSD_DT + 2 * g
            e0 = jnp.exp(cum_end[r0:r0 + 1, l:l + 1])
            e1 = jnp.exp(cum_end[r0:r0 + 1, l + 1:l + 2])
            upd = _dot_tn(_seg_rows(xw, rowi, t_len, s, n_seg).astype(BF16), bg)
            h_scr[s, g] = jnp.where(rsel, e0, e1) * h_scr[s, g] + upd
        yield
    y_all = jnp.concatenate(ys, axis=-1) + dvec_ref[...] * sx
    o_ref[...] = _rms(y_all * jax.nn.silu(z_ref[...]), ng_ref[...]).astype(BF16)

    def finish():
        convn_ref[...] = tail
        hn_ref[...] = h_scr[...].reshape(n_seg, SSD_H, 64, SSD_N)
    return finish


def ssd_spec(st, w, lb, n_seg):
    t_len = CH // n_seg
    gp = _gate_rows([(L_SSD_DT, w['ssd_a_log']), (L_SSD_DT, w['ssd_dt_bias'])])
    dvec = jnp.repeat(w['ssd_d'], BRANCH_W // SSD_H).reshape(1, BRANCH_W)
    spec = (_ssd_kernel, [COLS['ssd_in'], COLS['ssd_z'], (GATE_COL0, LANES)],
            [st['ssd_conv'], st['ssd']],
            [w['ssd_conv_w'], w['ssd_conv_b'].reshape(1, SSD_CONV_CH), gp, dvec, w['ssd_norm'].reshape(1, BRANCH_W)],
            [pltpu.VMEM((n_seg, 8 + t_len, SSD_CONV_CH), F32), pltpu.VMEM((n_seg, 2, 128, SSD_N), F32)])
    return spec, lambda convn, hn: {'ssd_conv': convn, 'ssd': hn}


def _mlstm_kernel(vo_ref, qk_ref, sm_ref, c0_ref, n0_ref, m0_ref, gp_ref, ng_ref, o_ref, cn_ref, nn_ref, mn_ref,
                  c_scr, n_scr, m_scr, *, n_seg, init):
    t_len = CH // n_seg
    if init:
        for s in range(n_seg):
            c_scr[s] = jnp.concatenate(
                [jnp.concatenate([c0_ref[s, h] if g == h else jnp.zeros((ML_DK, 64), F32) for g in range(ML_H)], axis=1)
                 for h in range(ML_H)], axis=0)
        n_scr[...] = n0_ref[...]
        m_scr[...] = m0_ref[...]
        return None

    same, tri, _ = _seg_masks(t_len)
    lane = lax.broadcasted_iota(jnp.int32, (CH, LANES), 1)
    lane256 = lax.broadcasted_iota(jnp.int32, (CH, BRANCH_W), 1)
    rowi = lax.broadcasted_iota(jnp.int32, (CH, 1), 0)
    r128 = lax.broadcasted_iota(jnp.int32, (LANES, 1), 0)
    neg = jnp.float32(-jnp.inf)

    v_all = vo_ref[:, 0:BRANCH_W]
    q_all = qk_ref[:, 0:ML_QK]
    k_all = qk_ref[:, ML_QK:2 * ML_QK] * (ML_DK ** -0.5)
    sm = sm_ref[...]
    ig = sm + gp_ref[0:1, :]
    fl = jnp.logical_and(lane >= L_ML_F, lane < L_ML_F + ML_H)
    lf = jnp.where(fl, -_softplus(-(sm + gp_ref[1:2, :])), 0.0)
    b = _dot(tri.astype(F32), lf, precision=HI)
    b_end = _dot(same.astype(F32), lf, precision=HI)
    b_r = _row_forms(b)
    ig_r = _row_forms(ig)
    yield
    mm = m_scr[...]
    qb = q_all.astype(BF16)
    kb = k_all.astype(BF16)
    qn = _seg_sum([_dot(qb, n_scr[s].astype(BF16)) for s in range(n_seg)], rowi, t_len)
    qc = _seg_sum([_dot(qb, c_scr[s].astype(BF16)) for s in range(n_seg)], rowi, t_len)
    yield
    num_att = jnp.zeros((CH, BRANCH_W), F32)
    w_ins, dens, w_ends, a_ends, m_ends = [], [], [], [], []
    for h in range(ML_H):
        li, lf_ = L_ML_I + h, L_ML_F + h
        bc = b[:, lf_:lf_ + 1]
        bec = b_end[:, lf_:lf_ + 1]
        igc = ig[:, li:li + 1]
        mmc = mm[:, li:li + 1]
        br = b_r[lf_:lf_ + 1, :]
        igr = ig_r[li:li + 1, :]
        diff = igr - br
        cmx = jnp.max(jnp.where(tri, diff, neg), axis=-1, keepdims=True)
        smx = jnp.max(jnp.where(same, diff, neg), axis=-1, keepdims=True)
        m_c = bc + jnp.maximum(mmc, cmx)
        m_end = bec + jnp.maximum(mmc, smx)
        w_in = jnp.exp(bc + mmc - m_c)
        logw = bc - br + igr - m_c
        mq = jnp.logical_and(lane >= ML_DK * h, lane < ML_DK * (h + 1))
        qk = _dot_nt(jnp.where(mq, q_all, 0.0).astype(BF16), kb)
        wts = _masked_exp(logw, tri) * qk
        yield
        mv = jnp.logical_and(lane256 >= 64 * h, lane256 < 64 * (h + 1))
        num_att = num_att + _dot(wts.astype(BF16), jnp.where(mv, v_all, 0.0).astype(BF16))
        nq = w_in * qn[:, li:li + 1] + jnp.sum(wts, axis=-1, keepdims=True)
        w_ins.append(w_in)
        dens.append(jnp.maximum(jnp.abs(nq), jnp.exp(-m_c)))
        w_ends.append(jnp.exp(bec - bc + igc - m_end))
        a_ends.append(jnp.exp(bec + mmc - m_end))
        m_ends.append(m_end)
        yield
    num = _quarter_sel(lane256, 64, w_ins) * qc + num_att
    hout = num / _quarter_sel(lane256, 64, dens)
    outs = [_head_rmsnorm(hout[:, 128 * p:128 * (p + 1)], lane < 64) for p in range(2)]
    o_all = jnp.concatenate(outs, axis=-1) * ng_ref[...] * jax.nn.sigmoid(vo_ref[:, BRANCH_W:2 * BRANCH_W])
    o_ref[...] = o_all.astype(BF16)
    yield

    kw = k_all * _quarter_sel(lane, ML_DK, w_ends)
    wend_tile = jnp.zeros((CH, LANES), F32)
    m_tile = jnp.zeros((CH, LANES), F32)
    for h in range(ML_H):
        wend_tile = jnp.where(lane == L_ML_I + h, w_ends[h], wend_tile)
        m_tile = jnp.where(lane == L_ML_I + h, m_ends[h], m_tile)
    m_scr[...] = m_tile
    vb = v_all.astype(BF16)
    wb = wend_tile.astype(BF16)
    rc = lax.broadcasted_iota(jnp.int32, (LANES, BRANCH_W), 0)
    lc = lax.broadcasted_iota(jnp.int32, (LANES, BRANCH_W), 1)
    bd_c = (rc // ML_DK) == (lc // 64)
    rn = lax.broadcasted_iota(jnp.int32, (LANES, LANES), 0)
    ln = lax.broadcasted_iota(jnp.int32, (LANES, LANES), 1)
    bd_n = ln == (rn // ML_DK) + L_ML_I
    for s in range(n_seg):
        r0 = s * t_len
        a_sel = _quarter_sel(r128, ML_DK, [a[r0:r0 + 1, :] for a in a_ends])
        upd_c = _dot_tn(_seg_rows(kw, rowi, t_len, s, n_seg).astype(BF16), vb)
        upd_n = _dot_tn(_seg_rows(k_all, rowi, t_len, s, n_seg).astype(BF16), wb)
        c_scr[s] = a_sel * c_scr[s] + jnp.where(bd_c, upd_c, 0.0)
        n_scr[s] = a_sel * n_scr[s] + jnp.where(bd_n, upd_n, 0.0)

    def finish():
        for s in range(n_seg):
            for h in range(ML_H):
                cn_ref[s, h] = c_scr[s, ML_DK * h:ML_DK * (h + 1), 64 * h:64 * (h + 1)]
        nn_ref[...] = n_scr[...]
        mn_ref[...] = m_tile
    return finish


def mlstm_spec(st, w, lb, n_seg):
    t_len = CH // n_seg
    c0, n0, m0 = st['ml_c'], st['ml_n'], st['ml_m']
    bsz = c0.shape[0]
    gp = _gate_rows([(L_ML_I, w['ml_ig_b']), (L_ML_F, w['ml_fg_b'])])
    ng = jnp.tile(w['ml_norm'], ML_H).reshape(1, BRANCH_W)
    eye = jnp.eye(ML_H, dtype=F32)
    pad = ((0, 0), (L_ML_I, LANES - L_ML_I - ML_H))
    n_bd = jnp.pad(jnp.einsum('bhk,hg->bhkg', n0, eye).reshape(bsz, ML_QK, ML_H), ((0, 0),) + pad)
    m_exp = jnp.pad(jnp.repeat(m0, t_len, axis=0), pad)
    assert COLS['ml_o'][0] == COLS['ml_v'][0] + BRANCH_W and COLS['ml_k'][0] == COLS['ml_q'][0] + ML_QK
    spec = (_mlstm_kernel, [(COLS['ml_v'][0], 2 * BRANCH_W), (COLS['ml_q'][0], 2 * ML_QK), (GATE_COL0, LANES)],
            [c0, n_bd, m_exp], [gp, ng],
            [pltpu.VMEM((n_seg, ML_QK, BRANCH_W), F32), pltpu.VMEM((n_seg, ML_QK, LANES), F32),
             pltpu.VMEM((CH, LANES), F32)])

    def finish(c_new, nn, mn):
        n_new = jnp.einsum('bhkg,hg->bhk', nn[:, :, L_ML_I:L_ML_I + ML_H].reshape(bsz, ML_H, ML_DK, ML_H), eye)
        return {'ml_c': c_new, 'ml_n': n_new, 'ml_m': mn[::t_len, L_ML_I:L_ML_I + ML_H]}
    return spec, finish


def _group_specs(tm, width, n_first):
    return [pl.BlockSpec((tm, width), lambda i: (jnp.minimum(i, n_first - 1), 0)),
            pl.BlockSpec((tm, width), lambda i: (jnp.maximum(i - n_first, 0), 0))]


def _group_in_specs(groups, tm, n_first):
    return [_row_block_spec(groups[0], tm, lambda i: jnp.minimum(i, n_first - 1)),
            _row_block_spec(groups[1], tm, lambda i: jnp.maximum(i - n_first, 0))]


def _merge_kernel(*refs, n_first, h_first):
    *h_refs, gates_ref, brp_ref, brs_ref, wb_ref, wo_ref, o_ref = refs
    br = _rows_of((brp_ref, brs_ref), n_first)
    merged = None
    for n in range(N_BRANCH):
        y = _dot(br[:, n * BRANCH_W:(n + 1) * BRANCH_W], wb_ref[n])
        z = gates_ref[:, n * D_MODEL:(n + 1) * D_MODEL].astype(F32)
        t = (0.5 * jnp.tanh(0.5 * z) + 0.5) * y
        merged = t if merged is None else merged + t
    o_ref[...] = _rows_of(h_refs, h_first) + _dot(merged.astype(BF16), wo_ref[...])


def merge(h, gates, br_groups, w_branch, w_out, tm):
    h_specs, h_ops, h_first = _row_inputs(h, tm)
    m, d = sum(a.shape[0] for a in h_ops), h_ops[0].shape[1]
    n_first = br_groups[0].shape[0] // tm
    assert br_groups[0].shape[0] % tm == 0 and br_groups[1].shape[0] % tm == 0
    return pl.pallas_call(
        functools.partial(_merge_kernel, n_first=n_first, h_first=h_first),
        grid=(m // tm,),
        in_specs=[*h_specs,
                  pl.BlockSpec((tm, N_BRANCH * d), lambda i: (i, 0)),
                  *_group_specs(tm, N_BRANCH * BRANCH_W, n_first),
                  pl.BlockSpec((N_BRANCH, BRANCH_W, d), lambda i: (0, 0, 0)),
                  pl.BlockSpec((d, d), lambda i: (0, 0))],
        out_specs=pl.BlockSpec((tm, d), lambda i: (i, 0)),
        out_shape=jax.ShapeDtypeStruct((m, d), F32),
        compiler_params=pltpu.CompilerParams(
            dimension_semantics=("parallel",), vmem_limit_bytes=VMEM_LIMIT),
        name="merge",
    )(*h_ops, gates, *br_groups, w_branch, w_out)


def _ffn_kernel(h_ref, g_ref, wg_ref, wu_ref, wd_ref, *rest, tf, n_first):
    h = h_ref[...]
    u = _rms(h, g_ref[...]).astype(BF16)
    y = h
    for c in range(wg_ref.shape[1] // tf):
        cols = slice(c * tf, (c + 1) * tf)
        a = jax.nn.silu(_dot(u, wg_ref[:, cols]))
        b = _dot(u, wu_ref[:, cols])
        y = y + _dot((a * b).astype(BF16), wd_ref[cols, :])
    if n_first is None:
        rest[0][...] = y
        return
    pp_ref, ps_ref, gp_ref, wpg_ref, wpp_ref, o_ref = rest
    p = _rows_of((pp_ref, ps_ref), n_first)
    v = _rms(y, gp_ref[...]).astype(BF16)
    pg = jax.nn.sigmoid(_dot(v, wpg_ref[...]))
    o_ref[...] = y + pg * _dot(p.astype(BF16), wpp_ref[...])


def ffn(h, g, wg, wu, wd, tm, tf, ple_term=None):
    m, d = h.shape
    ff = wg.shape[1]
    assert ff % tf == 0
    resident = lambda a: pl.BlockSpec(a.shape, lambda i: (0, 0), pipeline_mode=pl.Buffered(1))
    extra_specs, extra_ops, n_first = [], [], None
    if ple_term is not None:
        p_groups, g_ple, w_gate, w_p = ple_term
        n_first = p_groups[0].shape[0] // tm
        assert p_groups[0].shape[0] % tm == 0 and p_groups[1].shape[0] % tm == 0
        extra_specs = [*_group_in_specs(p_groups, tm, n_first), pl.BlockSpec((1, d), lambda i: (0, 0)),
                       resident(w_gate), resident(w_p)]
        extra_ops = [*[_operand(p) for p in p_groups], g_ple.reshape(1, d), w_gate, w_p]
    return pl.pallas_call(
        functools.partial(_ffn_kernel, tf=tf, n_first=n_first),
        grid=(m // tm,),
        in_specs=[pl.BlockSpec((tm, d), lambda i: (i, 0)),
                  pl.BlockSpec((1, d), lambda i: (0, 0)),
                  resident(wg), resident(wu), resident(wd), *extra_specs],
        out_specs=pl.BlockSpec((tm, d), lambda i: (i, 0)),
        out_shape=jax.ShapeDtypeStruct((m, d), F32),
        compiler_params=pltpu.CompilerParams(
            dimension_semantics=("parallel",), vmem_limit_bytes=VMEM_LIMIT),
        name="ffn",
    )(h, g.reshape(1, d), wg, wu, wd, *extra_ops)


def _router_kernel(h_ref, g_ref, wr_ref, u_ref, w_ref, i_ref):
    u = _rms(h_ref[...], g_ref[...])
    u_ref[...] = u
    logits = _dot3(_split2(u), _split2(wr_ref[...]))
    lane = lax.broadcasted_iota(jnp.int32, logits.shape, 1)
    neg = jnp.float32(-jnp.inf)
    logits = jnp.where(lane < N_EXPERTS, logits, neg)
    m1 = jnp.max(logits, axis=-1, keepdims=True)
    i1 = jnp.min(jnp.where(logits == m1, lane, LANES), axis=-1, keepdims=True)
    rest = jnp.where(lane == i1, neg, logits)
    m2 = jnp.max(rest, axis=-1, keepdims=True)
    i2 = jnp.min(jnp.where(rest == m2, lane, LANES), axis=-1, keepdims=True)
    e = jnp.exp(m2 - m1)
    den = 1.0 + e
    w_ref[...] = jnp.where(lane == 0, 1.0 / den, jnp.where(lane == 1, e / den, 0.0))
    i_ref[...] = jnp.where(lane == 0, i1, jnp.where(lane == 1, i2, 0))


def router(h, g, w_router, tm):
    m, d = h.shape
    wr = jnp.pad(w_router, ((0, 0), (0, LANES - N_EXPERTS)))
    return pl.pallas_call(
        _router_kernel,
        grid=(m // tm,),
        in_specs=[pl.BlockSpec((tm, d), lambda i: (i, 0)),
                  pl.BlockSpec((1, d), lambda i: (0, 0)),
                  pl.BlockSpec((d, LANES), lambda i: (0, 0))],
        out_specs=[pl.BlockSpec((tm, d), lambda i: (i, 0)),
                   pl.BlockSpec((tm, LANES), lambda i: (i, 0)),
                   pl.BlockSpec((tm, LANES), lambda i: (i, 0))],
        out_shape=[jax.ShapeDtypeStruct((m, d), F32),
                   jax.ShapeDtypeStruct((m, LANES), F32),
                   jax.ShapeDtypeStruct((m, LANES), jnp.int32)],
        compiler_params=pltpu.CompilerParams(
            dimension_semantics=("parallel",), vmem_limit_bytes=VMEM_LIMIT),
        name="router",
    )(h, g.reshape(1, d), wr)


def _row_copies(src_hbm, dst_hbm, idx_ref, base, buf, sem, n_rows, gather, wait):
    assert n_rows % 2 == 0

    def body(r2, carry):
        for prio in range(2):
            r = 2 * r2 + prio
            row = idx_ref[base + r]
            if gather:
                cp = pltpu.make_async_copy(src_hbm.at[pl.ds(row, 1)], buf.at[pl.ds(r, 1)], sem)
            else:
                cp = pltpu.make_async_copy(buf.at[pl.ds(r, 1)], dst_hbm.at[pl.ds(row, 1)], sem)
            if wait:
                cp.wait()
            else:
                cp.start(priority=prio)
        return carry
    lax.fori_loop(0, n_rows // 2, body, 0, unroll=4)


def _expert_kernel(te_ref, nt_ref, src_ref, dst_ref, u_hbm, wg_ref, wu_ref, wd_ref, y_hbm,
                   xbuf, xb_ref, acc_ref, obuf, gsem, ssem, *, tm):
    i = pl.program_id(0)
    f = pl.program_id(1)
    last_f = pl.num_programs(1) - 1
    nt = nt_ref[0]
    gather = functools.partial(_row_copies, u_hbm, None, src_ref, gather=True, n_rows=tm)
    scatter = functools.partial(_row_copies, None, y_hbm, dst_ref, buf=obuf, sem=ssem.at[0], gather=False, n_rows=tm)

    @pl.when(jnp.logical_and(i < nt, f == 0))
    def _():
        slot = i % 2

        @pl.when(i == 0)
        def _():
            gather(base=0, buf=xbuf.at[0], sem=gsem.at[0], wait=False)
            obuf[...] = jnp.zeros_like(obuf)
            fill = pltpu.make_async_copy(obuf, y_hbm.at[pl.ds(y_hbm.shape[0] - tm, tm)], ssem.at[0])
            fill.start()
            fill.wait()

        gather(base=i * tm, buf=xbuf.at[slot], sem=gsem.at[slot], wait=True)

        @pl.when(i + 1 < nt)
        def _():
            gather(base=(i + 1) * tm, buf=xbuf.at[1 - slot], sem=gsem.at[1 - slot], wait=False)

        xb_ref[...] = xbuf[slot].astype(BF16)

    @pl.when(i < nt)
    def _():
        x = xb_ref[...]
        a = jax.nn.silu(_dot(x, wg_ref[0]))
        b = _dot(x, wu_ref[0])
        y = _dot((a * b).astype(BF16), wd_ref[0])

        @pl.when(f == 0)
        def _():
            acc_ref[...] = y

        @pl.when(f != 0)
        def _():
            acc_ref[...] += y

        @pl.when(f == last_f)
        def _():
            @pl.when(i > 0)
            def _():
                scatter(base=(i - 1) * tm, wait=True)

            obuf[...] = acc_ref[...]
            scatter(base=i * tm, wait=False)

            @pl.when(i == nt - 1)
            def _():
                scatter(base=i * tm, wait=True)


def experts(u, tile_expert, n_tiles, src_tok, dst_row, n_out_rows, wg, wu, wd, tm, tf):
    d = u.shape[1]
    ff = wg.shape[2]
    n_row_tiles = tile_expert.shape[0]
    wmap = lambda i, f, te, nt, src, dst: (te[i], 0, f)
    wbuf = pl.Buffered(1) if tf == ff else None
    grid_spec = pltpu.PrefetchScalarGridSpec(
        num_scalar_prefetch=4,
        grid=(n_row_tiles, ff // tf),
        in_specs=[pl.BlockSpec(memory_space=pl.ANY),
                  pl.BlockSpec((1, d, tf), wmap, pipeline_mode=wbuf),
                  pl.BlockSpec((1, d, tf), wmap, pipeline_mode=wbuf),
                  pl.BlockSpec((1, tf, d), lambda i, f, te, nt, src, dst: (te[i], f, 0), pipeline_mode=wbuf)],
        out_specs=pl.BlockSpec(memory_space=pl.ANY),
        scratch_shapes=[pltpu.VMEM((2, tm, d), F32), pltpu.VMEM((tm, d), BF16), pltpu.VMEM((tm, d), F32),
                        pltpu.VMEM((tm, d), F32), pltpu.SemaphoreType.DMA((2,)), pltpu.SemaphoreType.DMA((1,))],
    )
    return pl.pallas_call(
        functools.partial(_expert_kernel, tm=tm),
        grid_spec=grid_spec,
        out_shape=jax.ShapeDtypeStruct((n_out_rows, d), F32),
        compiler_params=pltpu.CompilerParams(
            dimension_semantics=("arbitrary", "arbitrary"), vmem_limit_bytes=VMEM_LIMIT,
            disable_bounds_checks=True),
        name="experts",
    )(tile_expert, n_tiles, src_tok, dst_row, u, wg, wu, wd)


def moe(h, g, w_router, wg, wu, wd, tm_route, tm_e, tf):
    m, d = h.shape
    u, top_w, top_i = router(h, g, w_router, tm_route)
    n_pairs = TOP_K * m
    flat_e = top_i[:, :TOP_K].reshape(-1)
    onehot = (flat_e[:, None] == jnp.arange(N_EXPERTS, dtype=jnp.int32)[None, :]).astype(jnp.int32)
    rank = jnp.sum((jnp.cumsum(onehot, axis=0) - 1) * onehot, axis=1)
    counts = jnp.sum(onehot, axis=0)
    tiles_per = (counts + tm_e - 1) // tm_e
    tile_end = jnp.cumsum(tiles_per)
    tile_start = tile_end - tiles_per
    grouped_row = tile_start[flat_e] * tm_e + rank
    n_rows = n_pairs + N_EXPERTS * tm_e
    n_row_tiles = n_rows // tm_e
    pair_ids = jnp.arange(n_pairs, dtype=jnp.int32)
    pair_at = jnp.full((n_rows,), -1, jnp.int32).at[grouped_row].set(pair_ids)
    spare = n_pairs + jnp.arange(n_rows, dtype=jnp.int32) % tm_e
    src_tok = jnp.where(pair_at >= 0, pair_at // TOP_K, 0)
    dst_row = jnp.where(pair_at >= 0, (pair_at % TOP_K) * m + pair_at // TOP_K, spare)
    tile_ids = jnp.arange(n_row_tiles, dtype=jnp.int32)
    tile_expert = jnp.minimum(jnp.sum((tile_ids[:, None] >= tile_end[None, :]).astype(jnp.int32), axis=1),
                              N_EXPERTS - 1).astype(jnp.int32)
    n_tiles = tile_end[-1:].astype(jnp.int32)
    last_e = tile_expert[jnp.maximum(n_tiles[0] - 1, 0)]
    tile_expert = jnp.where(tile_ids < n_tiles[0], tile_expert, last_e)
    y = experts(u, tile_expert, n_tiles, src_tok, dst_row, n_pairs + tm_e, wg, wu, wd, tm_e, tf)
    return y, top_w


def _ple_kernel(*refs, final, combine, n_first):
    if combine:
        h_ref, y0_ref, y1_ref, tw_ref, *refs = refs
        h = h_ref[...] + (tw_ref[:, 0:1] * y0_ref[...] + tw_ref[:, 1:2] * y1_ref[...])
    else:
        h_ref, *refs = refs
        h = h_ref[...]
    pp_ref, ps_ref, g_ref, wg_ref, wp_ref, gf_ref, *o_refs = refs
    first = pl.program_id(0) < n_first
    p = jnp.where(first, pp_ref[...], ps_ref[...])
    v = _rms(h, g_ref[...]).astype(BF16)
    pg = jax.nn.sigmoid(_dot(v, wg_ref[...]))
    e = _dot(p.astype(BF16), wp_ref[...])
    out = h + pg * e
    if not final:
        o_refs[0][...] = out
        return
    out = _rms(out, gf_ref[...])

    @pl.when(first)
    def _():
        o_refs[0][...] = out

    @pl.when(jnp.logical_not(first))
    def _():
        o_refs[1][...] = out


def ple(h, p_groups, g, w_gate, w_p, g_final, tm, final, expert_out=None):
    m, d = h.shape
    dp = p_groups[0].shape[1]
    n_first = p_groups[0].shape[0] // tm
    assert p_groups[0].shape[0] % tm == 0 and p_groups[1].shape[0] % tm == 0
    rows = lambda wd: pl.BlockSpec((tm, wd), lambda i: (i, 0))
    whole = lambda a, b: pl.BlockSpec((a, b), lambda i: (0, 0))
    extra, extra_specs = (), []
    if expert_out is not None:
        y, top_w = expert_out
        extra = (y, y, top_w)
        extra_specs = [rows(d), pl.BlockSpec((tm, d), lambda i: (i + m // tm, 0)), rows(LANES)]
    if final:
        out_specs = _group_specs(tm, d, n_first)
        out_shape = [jax.ShapeDtypeStruct((a.shape[0], d), F32) for a in p_groups]
    else:
        out_specs, out_shape = rows(d), jax.ShapeDtypeStruct((m, d), F32)
    return pl.pallas_call(
        functools.partial(_ple_kernel, final=final, combine=expert_out is not None, n_first=n_first),
        grid=(m // tm,),
        in_specs=([rows(d)] + extra_specs + _group_in_specs(p_groups, tm, n_first)
                  + [whole(1, d), whole(d, d), whole(dp, d), whole(1, d)]),
        out_specs=out_specs,
        out_shape=out_shape,
        compiler_params=pltpu.CompilerParams(
            dimension_semantics=("parallel",), vmem_limit_bytes=VMEM_LIMIT),
        name="ple",
    )(h, *extra, *[_operand(p) for p in p_groups], g.reshape(1, d), w_gate, w_p, g_final.reshape(1, d))


STATE_KEYS = ('gdn_conv', 'gdn', 'hgrn', 'ssd_conv', 'ssd', 'ml_c', 'ml_n', 'ml_m')
MIXER_KEYS = ('gdn_conv_w', 'gdn_a_log', 'gdn_dt_bias', 'gdn_norm', 'hgrn_norm',
              'ssd_conv_w', 'ssd_conv_b', 'ssd_a_log', 'ssd_dt_bias', 'ssd_d', 'ssd_norm',
              'ml_ig_b', 'ml_fg_b', 'ml_norm')


MIXER_SPECS = (gdn_spec, hgrn_spec, ssd_spec, mlstm_spec)
_gdn_kernel.n_stages = lambda t_len: 11 + 2 * (t_len.bit_length() - 2)
_hgrn_kernel.n_stages = lambda t_len: 1 + HG_H * (len(_hgrn_levels(t_len)) // 2 + 1)
_ssd_kernel.n_stages = lambda t_len: 8
_mlstm_kernel.n_stages = lambda t_len: 11


def _mixers(proj, row_blk0, st, w, lb, **grid):
    built = [f(st, w, lb, grid['n_seg']) for f in MIXER_SPECS]
    o, *new_states = _mixer_call([b[0] for b in built], proj, row_blk0, **grid)
    new, at = {}, 0
    for spec, finish in built:
        n = len(spec[2])
        new.update(finish(*new_states[at:at + n]))
        at += n
    return o, new


def kernel(x_prompt, x_sample, state_gdn_conv, state_gdn, state_hgrn, state_ssd_conv, state_ssd, state_mlstm_c, state_mlstm_n, state_mlstm_m, p_prompt, p_sample, g_mix, w_in, gdn_conv_w, gdn_a_log, gdn_dt_bias, gdn_norm, hgrn_lb, hgrn_norm, ssd_conv_w, ssd_conv_b, ssd_a_log, ssd_dt_bias, ssd_d, ssd_norm, ml_ig_b, ml_fg_b, ml_norm, w_branch, w_out, g_ffn, w_ff_gate, w_ff_up, w_ff_down, w_router, w_ex_gate, w_ex_up, w_ex_down, w_ple, w_ple_gate, g_ple, g_final):
    prm = {'gdn_conv_w': gdn_conv_w, 'gdn_a_log': gdn_a_log, 'gdn_dt_bias': gdn_dt_bias,
           'gdn_norm': gdn_norm, 'hgrn_norm': hgrn_norm, 'ssd_conv_w': ssd_conv_w,
           'ssd_conv_b': ssd_conv_b, 'ssd_a_log': ssd_a_log, 'ssd_dt_bias': ssd_dt_bias,
           'ssd_d': ssd_d, 'ssd_norm': ssd_norm, 'ml_ig_b': ml_ig_b, 'ml_fg_b': ml_fg_b,
           'ml_norm': ml_norm}
    bp, lp, d = x_prompt.shape
    bs, ls, _ = x_sample.shape
    mp, ms = bp * lp, bs * ls
    assert lp % CH == 0 and CH % ls == 0 and ms % CH == 0 and mp % CH == 0
    st_s = {'gdn_conv': state_gdn_conv, 'gdn': state_gdn, 'hgrn': state_hgrn, 'ssd_conv': state_ssd_conv,
            'ssd': state_ssd, 'ml_c': state_mlstm_c, 'ml_n': state_mlstm_n, 'ml_m': state_mlstm_m}
    st_p = {k: jnp.zeros((bp,) + v.shape[2:], F32) for k, v in st_s.items()}
    grid_p = dict(n_outer=bp, n_chunks=lp // CH, n_seg=1)
    grid_s = dict(n_outer=ms // CH, n_chunks=1, n_seg=CH // ls)

    sm = jax.nn.softmax(hgrn_lb, axis=0)
    lb_all = jnp.cumsum(sm, axis=0) - sm[0]

    h = (x_prompt.reshape(mp, d), x_sample.reshape(ms, d))
    new_p = {k: [] for k in STATE_KEYS}
    new_s = {k: [] for k in STATE_KEYS}
    for l in range(DEPTH):
        wl = {k: prm[k][l] for k in MIXER_KEYS}
        gates, mix = norm_matmul(h, g_mix[l], _permute_w_in(w_in[l]), tm=ROW_TILE, tn=PROJ_COL_TILE)
        br_p, np_ = _mixers(mix, 0, st_p, wl, lb_all[l], **grid_p)
        st_l = {k: (v[l] if k in ('ml_n', 'ml_m') else LayerOf(v, l)) for k, v in st_s.items()}
        br_s, ns_ = _mixers(mix, mp // CH, st_l, wl, lb_all[l], **grid_s)
        h = merge(h, gates, (br_p, br_s), w_branch[l].astype(BF16), w_out[l].astype(BF16), tm=ROW_TILE)
        j = l // 2
        final = l == DEPTH - 1
        p_groups = (LayerOf(p_prompt.reshape(DEPTH, mp, D_PLE), l), LayerOf(p_sample.reshape(DEPTH, ms, D_PLE), l))
        ple_w = (w_ple_gate[l].astype(BF16), w_ple[l].astype(BF16))
        if l % 2 == 0 and not final:
            h = ffn(h, g_ffn[l], w_ff_gate[j].astype(BF16), w_ff_up[j].astype(BF16),
                    w_ff_down[j].astype(BF16), tm=ROW_TILE, tf=D_FF, ple_term=(p_groups, g_ple[l], *ple_w))
        else:
            expert_out = None
            if l % 2 == 0:
                h = ffn(h, g_ffn[l], w_ff_gate[j].astype(BF16), w_ff_up[j].astype(BF16),
                        w_ff_down[j].astype(BF16), tm=ROW_TILE, tf=D_FF)
            else:
                expert_out = moe(h, g_ffn[l], w_router[j], w_ex_gate[j].astype(BF16), w_ex_up[j].astype(BF16),
                                 w_ex_down[j].astype(BF16), tm_route=ROW_TILE, tm_e=ROW_TILE, tf=D_FF)
            h = ple(h, p_groups, g_ple[l], *ple_w, g_final, tm=ROW_TILE, final=final, expert_out=expert_out)
        for k in STATE_KEYS:
            new_p[k].append(np_[k])
            new_s[k].append(ns_[k])
    y_prompt = h[0].reshape(bp, lp, d)
    y_sample = h[1].reshape(bs, ls, d)
    sp = {k: jnp.stack(v) for k, v in new_p.items()}
    ss = {k: jnp.stack(v) for k, v in new_s.items()}
    return (y_prompt, y_sample,
            sp['gdn_conv'], sp['gdn'], sp['hgrn'], sp['ssd_conv'], sp['ssd'], sp['ml_c'], sp['ml_n'], sp['ml_m'],
            ss['gdn_conv'], ss['gdn'], ss['hgrn'], ss['ssd_conv'], ss['ssd'], ss['ml_c'], ss['ml_n'], ss['ml_m'])
```
